```python
import jax, jax.numpy as jnp
from jax import lax
import numpy as np

D_MODEL = 1024
BATCH = 8
SEQ = 2048
DEPTH = 4

N_MIXERS = 2
N_DN_LAYERS = (DEPTH + 1) // 2
N_SB_LAYERS = DEPTH // 2
DN_HEADS = 8
DN_HEAD_DIM = D_MODEL // DN_HEADS
DN_CONV = 4
DN_CHUNK = 64
SB_HEADS = 16
SB_HEAD_DIM = D_MODEL // SB_HEADS
SB_BLOCK = 128
D_FF = -(-8 * D_MODEL // (3 * 256)) * 256
N_MOD = 6
EPS = 1e-6
ADA_INIT = 0.25

kernel_name = 'hybrid_deltanet_stickbreaking_adaln_trunk'


def rms_norm(x, g):
    xf = x.astype(jnp.float32)
    y = xf * lax.rsqrt(jnp.mean(xf * xf, axis=-1, keepdims=True) + EPS)
    return (y * g.astype(jnp.float32)).astype(x.dtype)


def l2_norm(x):
    return x * lax.rsqrt(jnp.sum(x * x, axis=-1, keepdims=True) + EPS)


def causal_dwconv(x, w):
    width = w.shape[0]
    return lax.conv_general_dilated(
        x, w[:, None, :].astype(x.dtype), window_strides=(1,), padding=[(width - 1, 0)],
        dimension_numbers=('NWC', 'WIO', 'NWC'), feature_group_count=x.shape[-1])


def chunk_gated_delta_rule(q, k, v, g, beta):
    B, T, H, dk = q.shape
    dv = v.shape[-1]
    C = DN_CHUNK
    N = T // C
    to_c = lambda t: t.transpose(0, 2, 1, 3).reshape(B, H, N, C, t.shape[-1])
    q = to_c(q) * (dk ** -0.5)
    k = to_c(k)
    v = to_c(v)
    g = g.transpose(0, 2, 1).reshape(B, H, N, C)
    beta = beta.transpose(0, 2, 1).reshape(B, H, N, C)
    G = jnp.cumsum(g, axis=-1)
    idx = jnp.arange(C)
    incl = idx[:, None] >= idx[None, :]
    strict = idx[:, None] > idx[None, :]
    decay = jnp.exp(jnp.where(incl, G[..., :, None] - G[..., None, :], -jnp.inf))
    k_beta = k * beta[..., None]
    v_beta = v * beta[..., None]
    A = jnp.where(strict, jnp.einsum('bhnid,bhnjd->bhnij', k_beta, k) * decay, 0.0)
    tri = jnp.eye(C, dtype=A.dtype) + A
    U = lax.linalg.triangular_solve(tri, v_beta, left_side=True, lower=True, unit_diagonal=True)
    W = lax.linalg.triangular_solve(tri, k_beta * jnp.exp(G)[..., None], left_side=True,
                                    lower=True, unit_diagonal=True)
    attn_intra = jnp.einsum('bhnid,bhnjd->bhnij', q, k) * decay
    q_dec = q * jnp.exp(G)[..., None]
    k_tail = k * jnp.exp(G[..., -1:] - G)[..., None]
    g_last = jnp.exp(G[..., -1])
    xs = tuple(jnp.moveaxis(t, 2, 0) for t in (q_dec, k_tail, U, W, attn_intra, g_last))

    def step(S, inp):
        qd, kt, u, w, a, gl = inp
        v_new = u - jnp.einsum('bhck,bhkv->bhcv', w, S)
        o = jnp.einsum('bhck,bhkv->bhcv', qd, S) + jnp.einsum('bhij,bhjv->bhiv', a, v_new)
        S = S * gl[..., None, None] + jnp.einsum('bhck,bhcv->bhkv', kt, v_new)
        return S, o

    S0 = jnp.zeros((B, H, dk, dv), jnp.float32)
    _, o = lax.scan(step, S0, xs)
    return jnp.moveaxis(o, 0, 2).reshape(B, H, T, dv).transpose(0, 2, 1, 3)


def gated_deltanet_mixer(h, w_in, conv_w, a_log, dt_bias, onorm_g, w_out):
    B, T, _ = h.shape
    H, d = DN_HEADS, DN_HEAD_DIM
    proj = h @ w_in
    qkv, z, a, b = jnp.split(proj, [3 * H * d, 4 * H * d, 4 * H * d + H], axis=-1)
    qkv = jax.nn.silu(causal_dwconv(qkv, conv_w)).astype(jnp.float32)
    q, k, v = [t.reshape(B, T, H, d) for t in jnp.split(qkv, 3, axis=-1)]
    q = l2_norm(q)
    k = l2_norm(k)
    beta = jax.nn.sigmoid(b.astype(jnp.float32))
    g = -jnp.exp(a_log.astype(jnp.float32)) * jax.nn.softplus(
        a.astype(jnp.float32) + dt_bias.astype(jnp.float32))
    o = chunk_gated_delta_rule(q, k, v, g, beta)
    o = rms_norm(o, onorm_g) * jax.nn.silu(z.reshape(B, T, H, d).astype(jnp.float32))
    return o.reshape(B, T, H * d).astype(h.dtype) @ w_out


def stick_breaking_mixer(h, w_qkv, q_norm_g, k_norm_g, w_out):
    B, T, _ = h.shape
    H, d = SB_HEADS, SB_HEAD_DIM
    qkv = (h @ w_qkv).reshape(B, T, 3, H, d)
    q = rms_norm(qkv[:, :, 0], q_norm_g).astype(jnp.float32).transpose(0, 2, 1, 3)
    k = rms_norm(qkv[:, :, 1], k_norm_g).astype(jnp.float32).transpose(0, 2, 1, 3)
    v = qkv[:, :, 2].astype(jnp.float32).transpose(0, 2, 1, 3)
    scale = d ** -0.5
    outs = []
    for blk in range(T // SB_BLOCK):
        q0 = blk * SB_BLOCK
        kv_len = q0 + SB_BLOCK
        z = jnp.einsum('bhqd,bhkd->bhqk', q[:, :, q0:kv_len], k[:, :, :kv_len]) * scale
        t_pos = q0 + jnp.arange(SB_BLOCK)
        s_pos = jnp.arange(kv_len)
        causal = s_pos[None, :] < t_pos[:, None]
        log_1m = jnp.where(causal, jax.nn.log_sigmoid(-z), 0.0)
        log_stick = lax.cumsum(log_1m, axis=3, reverse=True) - log_1m
        a = jnp.where(causal, jnp.exp(jax.nn.log_sigmoid(z) + log_stick), 0.0)
        outs.append(jnp.einsum('bhqk,bhkd->bhqd', a, v[:, :, :kv_len]))
    o = jnp.concatenate(outs, axis=2).transpose(0, 2, 1, 3).reshape(B, T, H * d)
    return o.astype(h.dtype) @ w_out


def swiglu(h, w_in, w_out):
    gate, up = jnp.split(h @ w_in, 2, axis=-1)
    return (jax.nn.silu(gate) * up) @ w_out


def _fwd_setup_inputs(seed: int = 0) -> dict:
    key = jax.random.key(seed)
    ks = jax.random.split(key, 20)
    D, H, d = D_MODEL, DN_HEADS, DN_HEAD_DIM
    nrm = lambda k, shape, s: jax.random.normal(k, shape, jnp.float32) * s
    dn_in_cols = 4 * H * d + 2 * H
    dt = jnp.exp(jax.random.uniform(ks[8], (N_DN_LAYERS, H), jnp.float32, np.log(1e-3), np.log(1e-1)))
    return {
        'x': nrm(ks[0], (BATCH, SEQ, D), 1.0),
        'c': nrm(ks[1], (BATCH, D), 1.0),
        'ada_w': nrm(ks[2], (DEPTH, D, N_MOD * D), ADA_INIT * D ** -0.5),
        'ada_b': nrm(ks[3], (DEPTH, N_MOD * D), 0.01),
        'norm1_g': 1.0 + nrm(ks[4], (DEPTH, D), 0.02),
        'norm2_g': 1.0 + nrm(ks[5], (DEPTH, D), 0.02),
        'dn_w_in': nrm(ks[6], (N_DN_LAYERS, D, dn_in_cols), D ** -0.5),
        'dn_conv_w': nrm(ks[7], (N_DN_LAYERS, DN_CONV, 3 * H * d), DN_CONV ** -0.5),
        'dn_a_log': jnp.log(jax.random.uniform(ks[9], (N_DN_LAYERS, H), jnp.float32, 1.0, 16.0)),
        'dn_dt_bias': dt + jnp.log(-jnp.expm1(-dt)),
        'dn_onorm_g': 1.0 + nrm(ks[10], (N_DN_LAYERS, d), 0.02),
        'dn_w_out': nrm(ks[11], (N_DN_LAYERS, H * d, D), (H * d) ** -0.5),
        'sb_w_qkv': nrm(ks[12], (N_SB_LAYERS, D, 3 * SB_HEADS * SB_HEAD_DIM), D ** -0.5),
        'sb_q_norm_g': 1.0 + nrm(ks[13], (N_SB_LAYERS, SB_HEAD_DIM), 0.02),
        'sb_k_norm_g': 1.0 + nrm(ks[14], (N_SB_LAYERS, SB_HEAD_DIM), 0.02),
        'sb_w_out': nrm(ks[15], (N_SB_LAYERS, SB_HEADS * SB_HEAD_DIM, D), (SB_HEADS * SB_HEAD_DIM) ** -0.5),
        'ffn_w_in': nrm(ks[16], (DEPTH, D, 2 * D_FF), D ** -0.5),
        'ffn_w_out': nrm(ks[17], (DEPTH, D_FF, D), D_FF ** -0.5),
    }


def _fwd_reference(x, c, ada_w, ada_b, norm1_g, norm2_g, dn_w_in, dn_conv_w, dn_a_log, dn_dt_bias,
              dn_onorm_g, dn_w_out, sb_w_qkv, sb_q_norm_g, sb_k_norm_g, sb_w_out, ffn_w_in, ffn_w_out):
    cond = jax.nn.silu(c)
    for i in range(DEPTH):
        mod = (cond @ ada_w[i] + ada_b[i])[:, None, :]
        sh1, sc1, gt1, sh2, sc2, gt2 = jnp.split(mod, N_MOD, axis=-1)
        h = rms_norm(x, norm1_g[i]) * (1.0 + sc1) + sh1
        j = i // N_MIXERS
        if i % N_MIXERS == 0:
            y = gated_deltanet_mixer(h, dn_w_in[j], dn_conv_w[j], dn_a_log[j], dn_dt_bias[j],
                                     dn_onorm_g[j], dn_w_out[j])
        else:
            y = stick_breaking_mixer(h, sb_w_qkv[j], sb_q_norm_g[j], sb_k_norm_g[j], sb_w_out[j])
        x = x + gt1 * y
        h = rms_norm(x, norm2_g[i]) * (1.0 + sc2) + sh2
        x = x + gt2 * swiglu(h, ffn_w_in[i], ffn_w_out[i])
    return x


import jax as _jax
import jax.numpy as _jnp

TWIN_FORMAT = 'train_step'
FWD_PARAMS = ['x', 'c', 'ada_w', 'ada_b', 'norm1_g', 'norm2_g', 'dn_w_in', 'dn_conv_w', 'dn_a_log', 'dn_dt_bias', 'dn_onorm_g', 'dn_w_out', 'sb_w_qkv', 'sb_q_norm_g', 'sb_k_norm_g', 'sb_w_out', 'ffn_w_in', 'ffn_w_out']
TWIN_WEIGHTS = ['ada_w', 'ada_b', 'norm1_g', 'norm2_g', 'dn_w_in', 'dn_conv_w', 'dn_a_log', 'dn_dt_bias', 'dn_onorm_g', 'dn_w_out', 'sb_w_qkv', 'sb_q_norm_g', 'sb_k_norm_g', 'sb_w_out', 'ffn_w_in', 'ffn_w_out']
TWIN_DIFF_INPUT = 'x'
TWIN_INPUTS = ['x', 'c', 'ada_w', 'ada_b', 'norm1_g', 'norm2_g', 'dn_w_in', 'dn_conv_w', 'dn_a_log', 'dn_dt_bias', 'dn_onorm_g', 'dn_w_out', 'sb_w_qkv', 'sb_q_norm_g', 'sb_k_norm_g', 'sb_w_out', 'ffn_w_in', 'ffn_w_out', 'loss_target', 'm_ada_w', 'm_ada_b', 'm_norm1_g', 'm_norm2_g', 'm_dn_w_in', 'm_dn_conv_w', 'm_dn_a_log', 'm_dn_dt_bias', 'm_dn_onorm_g', 'm_dn_w_out', 'm_sb_w_qkv', 'm_sb_q_norm_g', 'm_sb_k_norm_g', 'm_sb_w_out', 'm_ffn_w_in', 'm_ffn_w_out', 'v_ada_w', 'v_ada_b', 'v_norm1_g', 'v_norm2_g', 'v_dn_w_in', 'v_dn_conv_w', 'v_dn_a_log', 'v_dn_dt_bias', 'v_dn_onorm_g', 'v_dn_w_out', 'v_sb_w_qkv', 'v_sb_q_norm_g', 'v_sb_k_norm_g', 'v_sb_w_out', 'v_ffn_w_in', 'v_ffn_w_out']
TWIN_OUTPUTS = ['loss', 'grad_x', 'grad_ada_w', 'grad_ada_b', 'grad_norm1_g', 'grad_norm2_g', 'grad_dn_w_in', 'grad_dn_conv_w', 'grad_dn_a_log', 'grad_dn_dt_bias', 'grad_dn_onorm_g', 'grad_dn_w_out', 'grad_sb_w_qkv', 'grad_sb_q_norm_g', 'grad_sb_k_norm_g', 'grad_sb_w_out', 'grad_ffn_w_in', 'grad_ffn_w_out', 'delta_ada_w', 'delta_ada_b', 'delta_norm1_g', 'delta_norm2_g', 'delta_dn_w_in', 'delta_dn_conv_w', 'delta_dn_a_log', 'delta_dn_dt_bias', 'delta_dn_onorm_g', 'delta_dn_w_out', 'delta_sb_w_qkv', 'delta_sb_q_norm_g', 'delta_sb_k_norm_g', 'delta_sb_w_out', 'delta_ffn_w_in', 'delta_ffn_w_out', 'new_m_ada_w', 'new_m_ada_b', 'new_m_norm1_g', 'new_m_norm2_g', 'new_m_dn_w_in', 'new_m_dn_conv_w', 'new_m_dn_a_log', 'new_m_dn_dt_bias', 'new_m_dn_onorm_g', 'new_m_dn_w_out', 'new_m_sb_w_qkv', 'new_m_sb_q_norm_g', 'new_m_sb_k_norm_g', 'new_m_sb_w_out', 'new_m_ffn_w_in', 'new_m_ffn_w_out', 'new_v_ada_w', 'new_v_ada_b', 'new_v_norm1_g', 'new_v_norm2_g', 'new_v_dn_w_in', 'new_v_dn_conv_w', 'new_v_dn_a_log', 'new_v_dn_dt_bias', 'new_v_dn_onorm_g', 'new_v_dn_w_out', 'new_v_sb_w_qkv', 'new_v_sb_q_norm_g', 'new_v_sb_k_norm_g', 'new_v_sb_w_out', 'new_v_ffn_w_in', 'new_v_ffn_w_out']
TWIN_LEAF_KINDS = {'loss': 'loss', 'grad_x': 'grad_x', 'grad_ada_w': 'grad_w', 'grad_ada_b': 'grad_w', 'grad_norm1_g': 'grad_w', 'grad_norm2_g': 'grad_w', 'grad_dn_w_in': 'grad_w', 'grad_dn_conv_w': 'grad_w', 'grad_dn_a_log': 'grad_w', 'grad_dn_dt_bias': 'grad_w', 'grad_dn_onorm_g': 'grad_w', 'grad_dn_w_out': 'grad_w', 'grad_sb_w_qkv': 'grad_w', 'grad_sb_q_norm_g': 'grad_w', 'grad_sb_k_norm_g': 'grad_w', 'grad_sb_w_out': 'grad_w', 'grad_ffn_w_in': 'grad_w', 'grad_ffn_w_out': 'grad_w', 'delta_ada_w': 'delta_w', 'delta_ada_b': 'delta_w', 'delta_norm1_g': 'delta_w', 'delta_norm2_g': 'delta_w', 'delta_dn_w_in': 'delta_w', 'delta_dn_conv_w': 'delta_w', 'delta_dn_a_log': 'delta_w', 'delta_dn_dt_bias': 'delta_w', 'delta_dn_onorm_g': 'delta_w', 'delta_dn_w_out': 'delta_w', 'delta_sb_w_qkv': 'delta_w', 'delta_sb_q_norm_g': 'delta_w', 'delta_sb_k_norm_g': 'delta_w', 'delta_sb_w_out': 'delta_w', 'delta_ffn_w_in': 'delta_w', 'delta_ffn_w_out': 'delta_w', 'new_m_ada_w': 'new_m', 'new_m_ada_b': 'new_m', 'new_m_norm1_g': 'new_m', 'new_m_norm2_g': 'new_m', 'new_m_dn_w_in': 'new_m', 'new_m_dn_conv_w': 'new_m', 'new_m_dn_a_log': 'new_m', 'new_m_dn_dt_bias': 'new_m', 'new_m_dn_onorm_g': 'new_m', 'new_m_dn_w_out': 'new_m', 'new_m_sb_w_qkv': 'new_m', 'new_m_sb_q_norm_g': 'new_m', 'new_m_sb_k_norm_g': 'new_m', 'new_m_sb_w_out': 'new_m', 'new_m_ffn_w_in': 'new_m', 'new_m_ffn_w_out': 'new_m', 'new_v_ada_w': 'new_v', 'new_v_ada_b': 'new_v', 'new_v_norm1_g': 'new_v', 'new_v_norm2_g': 'new_v', 'new_v_dn_w_in': 'new_v', 'new_v_dn_conv_w': 'new_v', 'new_v_dn_a_log': 'new_v', 'new_v_dn_dt_bias': 'new_v', 'new_v_dn_onorm_g': 'new_v', 'new_v_dn_w_out': 'new_v', 'new_v_sb_w_qkv': 'new_v', 'new_v_sb_q_norm_g': 'new_v', 'new_v_sb_k_norm_g': 'new_v', 'new_v_sb_w_out': 'new_v', 'new_v_ffn_w_in': 'new_v', 'new_v_ffn_w_out': 'new_v'}


def _forward(args):
    return _fwd_reference(*[args[k] for k in FWD_PARAMS])


def _output_shape():
    out = _jax.eval_shape(lambda: _forward(_fwd_setup_inputs(0)))
    return out.shape, out.dtype

N_MICROBATCH = 1
ADAM_LR = 0.001
ADAM_B1 = 0.9
ADAM_B2 = 0.999
ADAM_EPS = 1e-08
ADAM_WD = 0.01
ADAM_STEP = 10
PER_EXAMPLE_BATCH_AXIS = {'x': 0, 'c': 0, 'loss_target': 0}
SHARED_INPUTS = []
_WEIGHT_DTYPES = {'ada_w': _jnp.float32, 'ada_b': _jnp.float32, 'norm1_g': _jnp.float32, 'norm2_g': _jnp.float32, 'dn_w_in': _jnp.float32, 'dn_conv_w': _jnp.float32, 'dn_a_log': _jnp.float32, 'dn_dt_bias': _jnp.float32, 'dn_onorm_g': _jnp.float32, 'dn_w_out': _jnp.float32, 'sb_w_qkv': _jnp.float32, 'sb_q_norm_g': _jnp.float32, 'sb_k_norm_g': _jnp.float32, 'sb_w_out': _jnp.float32, 'ffn_w_in': _jnp.float32, 'ffn_w_out': _jnp.float32}
MOMENT_SCALE = {'ada_w': 1.753481e-01, 'ada_b': 3.515143e-01, 'norm1_g': 1.602181e-01, 'norm2_g': 3.178280e-01, 'dn_w_in': 1.747388e-02, 'dn_conv_w': 1.689731e-02, 'dn_a_log': 2.622069e-01, 'dn_dt_bias': 2.491016e-01, 'dn_onorm_g': 1.181583e+00, 'dn_w_out': 2.317761e-02, 'sb_w_qkv': 1.509088e-02, 'sb_q_norm_g': 3.984054e-01, 'sb_k_norm_g': 3.979770e-01, 'sb_w_out': 2.238193e-02, 'ffn_w_in': 1.198699e-02, 'ffn_w_out': 1.990587e-02}


def _to_microbatches(a, axis):
    t = _jnp.moveaxis(a, axis, 0)
    t = t.reshape((N_MICROBATCH, t.shape[0] // N_MICROBATCH) + t.shape[1:])
    return _jnp.moveaxis(t, 1, axis + 1)


def setup_inputs(seed: int = 0) -> dict:
    inp = _fwd_setup_inputs(seed)
    key = _jax.random.fold_in(_jax.random.key(seed), 7919)
    shape, _ = _output_shape()
    out = dict(inp)
    out["loss_target"] = _jax.random.normal(_jax.random.fold_in(key, 0), shape, _jnp.float32)
    for i, name in enumerate(TWIN_WEIGHTS):
        w = inp[name].astype(_jnp.float32)
        if MOMENT_SCALE is None:
            s = _jnp.sqrt(_jnp.mean(_jnp.square(w)) + 1e-30)
        else:
            s = MOMENT_SCALE[name]
        km, kv = _jax.random.split(_jax.random.fold_in(key, i + 1))
        out[name] = w
        out["m_" + name] = s * _jax.random.normal(km, w.shape, _jnp.float32)
        out["v_" + name] = (s * s) * _jax.random.uniform(kv, w.shape, _jnp.float32, 0.5, 1.5)
    if N_MICROBATCH > 1:
        for name, axis in PER_EXAMPLE_BATCH_AXIS.items():
            out[name] = _to_microbatches(out[name], axis)
    return {'x': out['x'], 'c': out['c'], 'ada_w': out['ada_w'], 'ada_b': out['ada_b'], 'norm1_g': out['norm1_g'], 'norm2_g': out['norm2_g'], 'dn_w_in': out['dn_w_in'], 'dn_conv_w': out['dn_conv_w'], 'dn_a_log': out['dn_a_log'], 'dn_dt_bias': out['dn_dt_bias'], 'dn_onorm_g': out['dn_onorm_g'], 'dn_w_out': out['dn_w_out'], 'sb_w_qkv': out['sb_w_qkv'], 'sb_q_norm_g': out['sb_q_norm_g'], 'sb_k_norm_g': out['sb_k_norm_g'], 'sb_w_out': out['sb_w_out'], 'ffn_w_in': out['ffn_w_in'], 'ffn_w_out': out['ffn_w_out'], 'loss_target': out['loss_target'], 'm_ada_w': out['m_ada_w'], 'm_ada_b': out['m_ada_b'], 'm_norm1_g': out['m_norm1_g'], 'm_norm2_g': out['m_norm2_g'], 'm_dn_w_in': out['m_dn_w_in'], 'm_dn_conv_w': out['m_dn_conv_w'], 'm_dn_a_log': out['m_dn_a_log'], 'm_dn_dt_bias': out['m_dn_dt_bias'], 'm_dn_onorm_g': out['m_dn_onorm_g'], 'm_dn_w_out': out['m_dn_w_out'], 'm_sb_w_qkv': out['m_sb_w_qkv'], 'm_sb_q_norm_g': out['m_sb_q_norm_g'], 'm_sb_k_norm_g': out['m_sb_k_norm_g'], 'm_sb_w_out': out['m_sb_w_out'], 'm_ffn_w_in': out['m_ffn_w_in'], 'm_ffn_w_out': out['m_ffn_w_out'], 'v_ada_w': out['v_ada_w'], 'v_ada_b': out['v_ada_b'], 'v_norm1_g': out['v_norm1_g'], 'v_norm2_g': out['v_norm2_g'], 'v_dn_w_in': out['v_dn_w_in'], 'v_dn_conv_w': out['v_dn_conv_w'], 'v_dn_a_log': out['v_dn_a_log'], 'v_dn_dt_bias': out['v_dn_dt_bias'], 'v_dn_onorm_g': out['v_dn_onorm_g'], 'v_dn_w_out': out['v_dn_w_out'], 'v_sb_w_qkv': out['v_sb_w_qkv'], 'v_sb_q_norm_g': out['v_sb_q_norm_g'], 'v_sb_k_norm_g': out['v_sb_k_norm_g'], 'v_sb_w_out': out['v_sb_w_out'], 'v_ffn_w_in': out['v_ffn_w_in'], 'v_ffn_w_out': out['v_ffn_w_out']}


def _loss(weights, diff, rest, loss_target):
    with _jax.named_scope("forward"):
        args = {**rest, TWIN_DIFF_INPUT: diff, **{k: w.astype(_WEIGHT_DTYPES[k]) for k, w in weights.items()}}
        y = _forward(args)
    with _jax.named_scope("loss_head"):
        err = _jnp.square(y.astype(_jnp.float32) - loss_target)
        return 0.5 * _jnp.sum(_jnp.mean(err, axis=-1)) if err.ndim else 0.5 * err


def _adamw(w, g, m, v):
    m = ADAM_B1 * m + (1.0 - ADAM_B1) * g
    v = ADAM_B2 * v + (1.0 - ADAM_B2) * _jnp.square(g)
    m_hat = m / (1.0 - ADAM_B1 ** ADAM_STEP)
    v_hat = v / (1.0 - ADAM_B2 ** ADAM_STEP)
    delta = -ADAM_LR * (m_hat / (_jnp.sqrt(v_hat) + ADAM_EPS) + ADAM_WD * w)
    return delta, m, v


def reference(x, c, ada_w, ada_b, norm1_g, norm2_g, dn_w_in, dn_conv_w, dn_a_log, dn_dt_bias, dn_onorm_g, dn_w_out, sb_w_qkv, sb_q_norm_g, sb_k_norm_g, sb_w_out, ffn_w_in, ffn_w_out, loss_target, m_ada_w, m_ada_b, m_norm1_g, m_norm2_g, m_dn_w_in, m_dn_conv_w, m_dn_a_log, m_dn_dt_bias, m_dn_onorm_g, m_dn_w_out, m_sb_w_qkv, m_sb_q_norm_g, m_sb_k_norm_g, m_sb_w_out, m_ffn_w_in, m_ffn_w_out, v_ada_w, v_ada_b, v_norm1_g, v_norm2_g, v_dn_w_in, v_dn_conv_w, v_dn_a_log, v_dn_dt_bias, v_dn_onorm_g, v_dn_w_out, v_sb_w_qkv, v_sb_q_norm_g, v_sb_k_norm_g, v_sb_w_out, v_ffn_w_in, v_ffn_w_out):
    given = dict(x=x, c=c, ada_w=ada_w, ada_b=ada_b, norm1_g=norm1_g, norm2_g=norm2_g, dn_w_in=dn_w_in, dn_conv_w=dn_conv_w, dn_a_log=dn_a_log, dn_dt_bias=dn_dt_bias, dn_onorm_g=dn_onorm_g, dn_w_out=dn_w_out, sb_w_qkv=sb_w_qkv, sb_q_norm_g=sb_q_norm_g, sb_k_norm_g=sb_k_norm_g, sb_w_out=sb_w_out, ffn_w_in=ffn_w_in, ffn_w_out=ffn_w_out, loss_target=loss_target, m_ada_w=m_ada_w, m_ada_b=m_ada_b, m_norm1_g=m_norm1_g, m_norm2_g=m_norm2_g, m_dn_w_in=m_dn_w_in, m_dn_conv_w=m_dn_conv_w, m_dn_a_log=m_dn_a_log, m_dn_dt_bias=m_dn_dt_bias, m_dn_onorm_g=m_dn_onorm_g, m_dn_w_out=m_dn_w_out, m_sb_w_qkv=m_sb_w_qkv, m_sb_q_norm_g=m_sb_q_norm_g, m_sb_k_norm_g=m_sb_k_norm_g, m_sb_w_out=m_sb_w_out, m_ffn_w_in=m_ffn_w_in, m_ffn_w_out=m_ffn_w_out, v_ada_w=v_ada_w, v_ada_b=v_ada_b, v_norm1_g=v_norm1_g, v_norm2_g=v_norm2_g, v_dn_w_in=v_dn_w_in, v_dn_conv_w=v_dn_conv_w, v_dn_a_log=v_dn_a_log, v_dn_dt_bias=v_dn_dt_bias, v_dn_onorm_g=v_dn_onorm_g, v_dn_w_out=v_dn_w_out, v_sb_w_qkv=v_sb_w_qkv, v_sb_q_norm_g=v_sb_q_norm_g, v_sb_k_norm_g=v_sb_k_norm_g, v_sb_w_out=v_sb_w_out, v_ffn_w_in=v_ffn_w_in, v_ffn_w_out=v_ffn_w_out)
    weights = {n: given[n] for n in TWIN_WEIGHTS}
    shared = {n: given[n] for n in SHARED_INPUTS}
    per_example = {n: given[n] for n in ['x', 'c']}
    grad_fn = _jax.value_and_grad(_loss, argnums=(0, 1))

    def one_microbatch(ex, loss_target):
        ex = dict(ex)
        diff = ex.pop(TWIN_DIFF_INPUT)
        return grad_fn(weights, diff, {**shared, **ex}, loss_target)

    if N_MICROBATCH == 1:
        loss, (grad_w, grad_x) = one_microbatch(per_example, given["loss_target"])
    else:
        def body(carry, xs):
            loss_sum, grad_sum = carry
            l_k, (gw_k, gx_k) = one_microbatch(xs[0], xs[1])
            with _jax.named_scope("update"):
                return (loss_sum + l_k, _jax.tree.map(_jnp.add, grad_sum, gw_k)), gx_k

        init = (_jnp.zeros((), _jnp.float32), _jax.tree.map(_jnp.zeros_like, weights))
        (loss, grad_w), grad_x = _jax.lax.scan(body, init, (per_example, given["loss_target"]))
    with _jax.named_scope("update"):
        delta_w, new_m, new_v = {}, {}, {}
        for n in TWIN_WEIGHTS:
            delta_w[n], new_m[n], new_v[n] = _adamw(weights[n], grad_w[n], given["m_" + n], given["v_" + n])
    return (loss, grad_x, *[grad_w[n] for n in TWIN_WEIGHTS], *[delta_w[n] for n in TWIN_WEIGHTS],
            *[new_m[n] for n in TWIN_WEIGHTS], *[new_v[n] for n in TWIN_WEIGHTS])
```

```python
import functools

import jax
import jax.numpy as jnp
from jax import lax
from jax.experimental import pallas as pl
from jax.experimental.pallas import tpu as pltpu

F32, BF16 = jnp.float32, jnp.bfloat16

D = 1024
DEPTH = 4
N_MOD = 6
DN_H, DN_D, DN_C, DN_CONV = 8, 128, 64, 4
DN_COLS = 4 * DN_H * DN_D + 2 * DN_H
DN_COLS_PAD = 33 * 128
SB_H, SB_D, SB_BLK = 16, 64, 128
DFF = 2816
EPS = 1e-6
NDEV = 8
LANES = 128
VMEM_LIMIT = 48 * 1024 * 1024

ADAM_LR, ADAM_B1, ADAM_B2, ADAM_EPS, ADAM_WD, ADAM_STEP = 0.001, 0.9, 0.999, 1e-08, 0.01, 10

NN = ((1,), (0,))
NT = ((1,), (1,))
TN = ((0,), (0,))
MESH = pl.DeviceIdType.MESH


def _dot(a, b, dims=NN):
    return lax.dot_general(a.astype(BF16), b.astype(BF16), (dims, ((), ())), preferred_element_type=F32)


def _split(a):
    hi = a.astype(BF16)
    lo = (a - hi.astype(F32)).astype(BF16)
    return hi, lo


def _dot3(a, b, dims=NN):
    ah, al = _split(a)
    bh, bl = _split(b)
    return _dot(ah, bh, dims) + (_dot(al, bh, dims) + _dot(ah, bl, dims))


def _dot2r(a, m, dims=NN):
    ah, al = _split(a)
    return _dot(ah, m, dims) + _dot(al, m, dims)


def _dot2l(m, b, dims=NN):
    bh, bl = _split(b)
    return _dot(m, bh, dims) + _dot(m, bl, dims)


def _sigmoid(x):
    return 1.0 / (1.0 + jnp.exp(-x))


def _silu(x):
    return x * _sigmoid(x)


def _dsilu(x):
    s = _sigmoid(x)
    return s * (1.0 + x * (1.0 - s))


def _softplus(x):
    return jnp.maximum(x, 0.0) + jnp.log(1.0 + jnp.exp(-jnp.abs(x)))


def _iota(shape, dim):
    return lax.broadcasted_iota(jnp.int32, shape, dim)


def _rowsum(x):
    return jnp.sum(x, axis=1, keepdims=True)


def _colsum(x):
    return jnp.sum(x, axis=0, keepdims=True)


def _tile(n, pref):
    if n <= pref:
        return n
    best = None
    for t in range(LANES, pref + 1, LANES):
        if n % t == 0:
            best = t
    assert best is not None, (n, pref)
    return best


def _rtile(r, pref=512):
    best = None
    for t in range(8, min(r, pref) + 1, 8):
        if r % t == 0:
            best = t
    return best if best is not None else r


def _params(sem):
    return pltpu.CompilerParams(dimension_semantics=sem, vmem_limit_bytes=VMEM_LIMIT)


def _mm(a, b, mode, out_dtype, name, tm=512, tn=512, tk=1024):
    if mode == "nn":
        (M, K), (K2, N) = a.shape, b.shape
        dims = NN
    elif mode == "nt":
        (M, K), (N, K2) = a.shape, b.shape
        dims = NT
    else:
        (K, M), (K2, N) = a.shape, b.shape
        dims = TN
    assert K == K2, (a.shape, b.shape, mode)
    tm, tn, tk = _tile(M, tm), _tile(N, tn), _tile(K, tk)
    nk = K // tk

    def body(a_ref, b_ref, o_ref, acc_ref):
        k = pl.program_id(2)
        p = _dot(a_ref[...], b_ref[...], dims)

        @pl.when(k == 0)
        def _():
            acc_ref[...] = p

        @pl.when(k > 0)
        def _():
            acc_ref[...] += p

        @pl.when(k == nk - 1)
        def _():
            o_ref[...] = acc_ref[...].astype(o_ref.dtype)

    if mode == "nn":
        a_spec = pl.BlockSpec((tm, tk), lambda i, j, k: (i, k))
        b_spec = pl.BlockSpec((tk, tn), lambda i, j, k: (k, j))
    elif mode == "nt":
        a_spec = pl.BlockSpec((tm, tk), lambda i, j, k: (i, k))
        b_spec = pl.BlockSpec((tn, tk), lambda i, j, k: (j, k))
    else:
        a_spec = pl.BlockSpec((tk, tm), lambda i, j, k: (k, i))
        b_spec = pl.BlockSpec((tk, tn), lambda i, j, k: (k, j))
    return pl.pallas_call(
        body, grid=(M // tm, N // tn, nk), in_specs=[a_spec, b_spec],
        out_specs=pl.BlockSpec((tm, tn), lambda i, j, k: (i, j)),
        out_shape=jax.ShapeDtypeStruct((M, N), out_dtype),
        scratch_shapes=[pltpu.VMEM((tm, tn), F32)], name=name,
        compiler_params=_params(("parallel", "parallel", "arbitrary")),
    )(a, b)


def _rowwise(fn, name, rows, bcasts, out_rows, out_reds=(), tile=256):
    T = rows[0].shape[0]
    tile = min(tile, T)
    nr, nb, no = len(rows), len(bcasts), len(out_rows)

    def body(*refs):
        rv = [r[...] for r in refs[:nr]]
        bv = [r[...] for r in refs[nr:nr + nb]]
        outs, reds = fn(rv, bv)
        for r, o in zip(refs[nr + nb:nr + nb + no], outs):
            r[...] = o.astype(r.dtype)
        red_refs = refs[nr + nb + no:]
        if red_refs:
            @pl.when(pl.program_id(0) == 0)
            def _():
                for r in red_refs:
                    r[...] = jnp.zeros(r.shape, F32)

            for r, v in zip(red_refs, reds):
                r[...] += v

    in_specs = [pl.BlockSpec((tile, a.shape[1]), lambda i: (i, 0)) for a in rows]
    in_specs += [pl.BlockSpec(b.shape, lambda i: (0, 0)) for b in bcasts]
    out_specs = [pl.BlockSpec((tile, c), lambda i: (i, 0)) for c, _ in out_rows]
    out_specs += [pl.BlockSpec(s, lambda i: (0, 0)) for s in out_reds]
    out_shape = [jax.ShapeDtypeStruct((T, c), dt) for c, dt in out_rows]
    out_shape += [jax.ShapeDtypeStruct(s, F32) for s in out_reds]
    return pl.pallas_call(
        body, grid=(T // tile,), in_specs=in_specs, out_specs=out_specs, out_shape=out_shape, name=name,
        compiler_params=_params(("arbitrary",)),
    )(*rows, *bcasts)


def _norm_mod(x, g, sc, sh, name):
    def fn(rv, bv):
        (xv,), (gv, scv, shv) = rv, bv
        r = lax.rsqrt(jnp.mean(xv * xv, axis=1, keepdims=True) + EPS)
        return [(xv * r * gv) * (1.0 + scv) + shv], []
    return _rowwise(fn, name, [x], [g, sc, sh], [(D, BF16)])[0]


def _norm_mod_bwd(dh, x, dres, g, sc, name):
    def fn(rv, bv):
        (dhv, xv, drv), (gv, scv) = rv, bv
        r = lax.rsqrt(jnp.mean(xv * xv, axis=1, keepdims=True) + EPS)
        xhat = xv * r
        dxhat = dhv * (gv * (1.0 + scv))
        dx = r * (dxhat - xhat * jnp.mean(dxhat * xhat, axis=1, keepdims=True)) + drv
        return [dx], [_colsum(dhv * xhat), _colsum(dhv)]
    return _rowwise(fn, name, [dh, x, dres], [g, sc], [(D, F32)], [(1, D), (1, D)])


def _gate_res(x, y, gt, name):
    def fn(rv, bv):
        return [rv[0] + bv[0] * rv[1]], []
    return _rowwise(fn, name, [x, y], [gt], [(D, F32)])[0]


def _gate_res_bwd(dxn, y, gt, name):
    def fn(rv, bv):
        return [rv[0] * bv[0]], [_colsum(rv[0] * rv[1])]
    return _rowwise(fn, name, [dxn, y], [gt], [(D, BF16)], [(1, D)])


def _swiglu_act(u, name):
    def fn(rv, bv):
        uv = rv[0]
        return [_silu(uv[:, :DFF]) * uv[:, DFF:]], []
    return _rowwise(fn, name, [u], [], [(DFF, BF16)])[0]


def _swiglu_act_bwd(da, u, name):
    def fn(rv, bv):
        dav, uv = rv
        gate, up = uv[:, :DFF], uv[:, DFF:]
        return [jnp.concatenate([dav * up * _dsilu(gate), dav * _silu(gate)], axis=1)], []
    return _rowwise(fn, name, [da, u], [], [(2 * DFF, BF16)])[0]


def _loss_head(y, target, name):
    def fn(rv, bv):
        err = rv[0] - rv[1]
        return [err * (1.0 / D)], [_colsum(_rowsum(err * err))]
    return _rowwise(fn, name, [y, target], [], [(D, F32)], [(1, 1)])


def _shift_rows(x, s):
    if s == 0:
        return x
    T = x.shape[0]
    r = pltpu.roll(x, s % T, axis=0)
    t = _iota(x.shape, 0)
    keep = (t >= s) if s > 0 else (t < T + s)
    return jnp.where(keep, r, 0.0)


def _dn_prep(pq, pk, pv, pab, cq_ref, ck_ref, cv_ref, alog, dtb, h):
    lane = _iota(pab.shape, 1)
    a_col = _rowsum(jnp.where(lane == h, pab, 0.0))
    b_col = _rowsum(jnp.where(lane == DN_H + h, pab, 0.0))
    lane1 = _iota(alog.shape, 1)
    alog_h = _rowsum(jnp.where(lane1 == h, alog, 0.0))
    dtb_h = _rowsum(jnp.where(lane1 == h, dtb, 0.0))
    pre = a_col + dtb_h
    neg_ea = -jnp.exp(alog_h)
    g = neg_ea * _softplus(pre)
    beta = _sigmoid(b_col)

    def conv(x, w_ref):
        acc = x * w_ref[DN_CONV - 1:DN_CONV, :]
        for i in range(DN_CONV - 1):
            acc = acc + _shift_rows(x, DN_CONV - 1 - i) * w_ref[i:i + 1, :]
        return acc

    xq, xk, xv = conv(pq, cq_ref), conv(pk, ck_ref), conv(pv, cv_ref)
    sq, sk, v = _silu(xq), _silu(xk), _silu(xv)
    rq = lax.rsqrt(_rowsum(sq * sq) + EPS)
    rk = lax.rsqrt(_rowsum(sk * sk) + EPS)
    return dict(g=g, beta=beta, pre=pre, neg_ea=neg_ea, xq=xq, xk=xk, xv=xv, rq=rq, rk=rk,
                qn=sq * rq, kn=sk * rk, v=v)


def _dn_masks():
    C = DN_C
    r, c = _iota((C, C), 0), _iota((C, C), 1)
    incl = r >= c
    strict = r > c
    blk16 = jnp.right_shift(r, 4) == jnp.right_shift(c, 4)
    blk32 = jnp.right_shift(r, 5) == jnp.right_shift(c, 5)
    return dict(incl=incl, strict=strict, upper=r <= c, blk16=blk16, blk32=blk32,
                tri=incl.astype(BF16), triT=(r <= c).astype(BF16), ones=jnp.ones((C, C), BF16),
                eye=(r == c).astype(F32), last=_iota((C, 1), 0) == C - 1)


def _tri_inverse(A, mk):
    P = -jnp.where(mk["blk16"], A, 0.0)
    X = mk["eye"] + P
    for _ in range(3):
        P = _dot3(P, P)
        X = X + _dot3(X, P)
    off1 = jnp.where(mk["blk32"] & (~mk["blk16"]), A, 0.0)
    X = X - _dot3(_dot3(X, off1), X)
    off2 = jnp.where(mk["blk32"], 0.0, A)
    X = X - _dot3(_dot3(X, off2), X)
    return X


def _dn_chunk_fwd(qc, kc, vc, gc, bc, S, mk):
    C = DN_C
    gm = jnp.broadcast_to(gc, (C, C))
    Gc = _dot2l(mk["tri"], gm)
    Gr = _dot2l(mk["ones"], jnp.where(mk["upper"], gm, 0.0))
    Dm = jnp.where(mk["incl"], jnp.exp(jnp.where(mk["incl"], Gc - Gr, 0.0)), 0.0)
    Gcol = jnp.max(Gc, axis=1, keepdims=True)
    Gl = _colsum(jnp.where(mk["last"], Gcol, 0.0))
    eG = jnp.exp(Gcol)
    eT = jnp.exp(Gl - Gcol)
    gl = jnp.exp(Gl)
    kb = kc * bc
    vb = vc * bc
    KK = _dot(kb, kc, NT)
    A = jnp.where(mk["strict"], KK * Dm, 0.0)
    Tinv = _tri_inverse(A, mk)
    KBE = kb * eG
    U = _dot(Tinv, vb)
    W = _dot(Tinv, KBE)
    QK = _dot(qc, kc, NT)
    attn = QK * Dm
    vnew = U - _dot(W, S)
    QD = qc * eG
    KT = kc * eT
    o = _dot(QD, S) + _dot(attn, vnew)
    S_new = S * gl + _dot(KT, vnew, TN)
    return o, S_new, dict(Dm=Dm, eG=eG, eT=eT, gl=gl, kb=kb, vb=vb, KK=KK, Tinv=Tinv, KBE=KBE, W=W, QK=QK,
                          attn=attn, vnew=vnew, QD=QD, KT=KT)


def _dn_chunk_bwd(qc, kc, vc, gc, bc, S, do, dS, mk):
    C = DN_C
    _, _, f = _dn_chunk_fwd(qc, kc, vc, gc, bc, S, mk)
    Dm, eG, eT, gl, kb, vb, KK, Tinv, KBE, W, QK, attn, vnew, QD, KT = (
        f[n] for n in ("Dm", "eG", "eT", "gl", "kb", "vb", "KK", "Tinv", "KBE", "W", "QK", "attn", "vnew", "QD", "KT"))
    dvnew = _dot(KT, dS)
    dKT = _dot(vnew, dS, NT)
    dgl = _colsum(_rowsum(dS * S))
    dS_prev = dS * gl
    dQD = _dot(do, S, NT)
    dS_prev = dS_prev + _dot(QD, do, TN)
    dattn = _dot(do, vnew, NT)
    dvnew = dvnew + _dot(attn, do, TN)
    dU = dvnew
    dW = -_dot(dvnew, S, NT)
    dS_prev = dS_prev - _dot(W, dvnew, TN)
    dQK = dattn * Dm
    dD = dattn * QK
    dq = _dot(dQK, kc)
    dk = _dot(dQK, qc, TN)
    dTinv = _dot(dU, vb, NT) + _dot(dW, KBE, NT)
    dvb = _dot(Tinv, dU, TN)
    dKBE = _dot(Tinv, dW, TN)
    dA = -_dot3(_dot3(Tinv, dTinv, TN), Tinv, NT)
    dA = jnp.where(mk["strict"], dA, 0.0)
    dKK = dA * Dm
    dD = dD + dA * KK
    dkb = _dot(dKK, kc) + dKBE * eG
    dk = dk + _dot(dKK, kb, TN)
    deG = _rowsum(dKBE * kb)
    dk = dk + dkb * bc
    dbeta = _rowsum(dkb * kc) + _rowsum(dvb * vc)
    dv = dvb * bc
    dq = dq + dQD * eG
    deG = deG + _rowsum(dQD * qc)
    dk = dk + dKT * eT
    deT = _rowsum(dKT * kc)
    dGcol = deG * eG - deT * eT
    dGl = _colsum(deT * eT) + dgl * gl
    Y = dD * Dm
    ycol = jnp.max(_dot2r(Y, mk["ones"], TN), axis=1, keepdims=True)
    dGcol = dGcol + _rowsum(Y) - ycol
    dGcol = dGcol + jnp.where(mk["last"], dGl, 0.0)
    dg = jnp.max(_dot2l(mk["triT"], jnp.broadcast_to(dGcol, (C, C))), axis=1, keepdims=True)
    return dq, dk, dv, dg, dbeta, dS_prev


def _dn_core_fwd(proj, conv_w, alog, dtb, og, name):
    T = proj.shape[0]
    N = T // DN_C

    def body(pq_ref, pk_ref, pv_ref, pz_ref, pab_ref, cq_ref, ck_ref, cv_ref, alog_ref, dtb_ref, og_ref,
             out_ref, o_ref, st_ref, q_s, k_s, v_s, g_s, b_s, S_s):
        h = pl.program_id(0)
        p = _dn_prep(pq_ref[...], pk_ref[...], pv_ref[...], pab_ref[...], cq_ref, ck_ref, cv_ref,
                     alog_ref[...], dtb_ref[...], h)
        q_s[...] = p["qn"] * (DN_D ** -0.5)
        k_s[...] = p["kn"]
        v_s[...] = p["v"]
        g_s[...] = p["g"]
        b_s[...] = p["beta"]
        S_s[...] = jnp.zeros(S_s.shape, F32)
        mk = _dn_masks()

        def step(c, carry):
            rows = pl.ds(pl.multiple_of(c * DN_C, DN_C), DN_C)
            S = S_s[...]
            st_ref[0, c] = S
            o, S_new, _ = _dn_chunk_fwd(q_s[rows, :], k_s[rows, :], v_s[rows, :], g_s[rows, :], b_s[rows, :], S, mk)
            o_ref[rows, :] = o
            S_s[...] = S_new
            return carry

        lax.fori_loop(0, N, step, 0)
        o = o_ref[...]
        ro = lax.rsqrt(jnp.mean(o * o, axis=1, keepdims=True) + EPS)
        out_ref[...] = ((o * ro * og_ref[...]) * _silu(pz_ref[...])).astype(out_ref.dtype)

    col = lambda k: pl.BlockSpec((T, DN_D), lambda h: (0, k * DN_H + h))
    cw = lambda k: pl.BlockSpec((DN_CONV, DN_D), lambda h: (0, k * DN_H + h))
    small = pl.BlockSpec((1, LANES), lambda h: (0, 0))
    return pl.pallas_call(
        body, grid=(DN_H,),
        in_specs=[col(0), col(1), col(2), col(3), pl.BlockSpec((T, LANES), lambda h: (0, 4 * DN_H)),
                  cw(0), cw(1), cw(2), small, small, small],
        out_specs=[pl.BlockSpec((T, DN_D), lambda h: (0, h)), pl.BlockSpec((T, DN_D), lambda h: (0, h)),
                   pl.BlockSpec((1, N, DN_D, DN_D), lambda h: (h, 0, 0, 0))],
        out_shape=[jax.ShapeDtypeStruct((T, D), BF16), jax.ShapeDtypeStruct((T, D), F32),
                   jax.ShapeDtypeStruct((DN_H, N, DN_D, DN_D), F32)],
        scratch_shapes=[pltpu.VMEM((T, DN_D), F32)] * 3 + [pltpu.VMEM((T, 1), F32)] * 2 + [pltpu.VMEM((DN_D, DN_D), F32)],
        name=name, compiler_params=_params(("arbitrary",)),
    )(proj, proj, proj, proj, proj, conv_w, conv_w, conv_w, alog, dtb, og)


def _dn_core_bwd(proj, conv_w, alog, dtb, og, o, states, dout, name):
    T = proj.shape[0]
    N = T // DN_C

    def body(pq_ref, pk_ref, pv_ref, pz_ref, pab_ref, cq_ref, ck_ref, cv_ref, alog_ref, dtb_ref, og_ref,
             o_ref, st_ref, dout_ref,
             dpq_ref, dpk_ref, dpv_ref, dpz_ref, dpab_ref, dcq_ref, dck_ref, dcv_ref, dalog_ref, ddtb_ref, dog_ref,
             q_s, k_s, v_s, g_s, b_s, do_s, dS_s):
        h = pl.program_id(0)
        scale = DN_D ** -0.5

        def prep():
            return _dn_prep(pq_ref[...], pk_ref[...], pv_ref[...], pab_ref[...], cq_ref, ck_ref, cv_ref,
                            alog_ref[...], dtb_ref[...], h)

        p = prep()
        q_s[...] = p["qn"] * scale
        k_s[...] = p["kn"]
        v_s[...] = p["v"]
        g_s[...] = p["g"]
        b_s[...] = p["beta"]
        del p

        o = o_ref[...]
        z = pz_ref[...]
        dout = dout_ref[...]
        ogv = og_ref[...]
        ro = lax.rsqrt(jnp.mean(o * o, axis=1, keepdims=True) + EPS)
        on = o * ro
        dy = dout * _silu(z)
        dpz_ref[...] = (dout * (on * ogv) * _dsilu(z)).astype(dpz_ref.dtype)
        dyg = dy * ogv
        do_s[...] = ro * (dyg - on * jnp.mean(dyg * on, axis=1, keepdims=True))
        dog_h = _colsum(dy * on)

        dS_s[...] = jnp.zeros(dS_s.shape, F32)
        mk = _dn_masks()

        def step(i, carry):
            c = N - 1 - i
            rows = pl.ds(pl.multiple_of(c * DN_C, DN_C), DN_C)
            dq, dk, dv, dg, db, dS_prev = _dn_chunk_bwd(
                q_s[rows, :], k_s[rows, :], v_s[rows, :], g_s[rows, :], b_s[rows, :], st_ref[0, c],
                do_s[rows, :], dS_s[...], mk)
            q_s[rows, :] = dq
            k_s[rows, :] = dk
            v_s[rows, :] = dv
            g_s[rows, :] = dg
            b_s[rows, :] = db
            dS_s[...] = dS_prev
            return carry

        lax.fori_loop(0, N, step, 0)

        p = prep()
        pq, pk, pv = pq_ref[...], pk_ref[...], pv_ref[...]
        dqn = q_s[...] * scale
        dkn = k_s[...]
        qn, kn = p["qn"], p["kn"]
        dsq = p["rq"] * (dqn - qn * _rowsum(dqn * qn))
        dsk = p["rk"] * (dkn - kn * _rowsum(dkn * kn))
        dxq = dsq * _dsilu(p["xq"])
        dxk = dsk * _dsilu(p["xk"])
        dxv = v_s[...] * _dsilu(p["xv"])

        def conv_bwd(dx, x, w_ref, dp_ref, dc_ref):
            acc = dx * w_ref[DN_CONV - 1:DN_CONV, :]
            dc_ref[DN_CONV - 1:DN_CONV, :] = _colsum(dx * x)
            for i in range(DN_CONV - 1):
                s = DN_CONV - 1 - i
                acc = acc + _shift_rows(dx, -s) * w_ref[i:i + 1, :]
                dc_ref[i:i + 1, :] = _colsum(dx * _shift_rows(x, s))
            dp_ref[...] = acc.astype(dp_ref.dtype)

        conv_bwd(dxq, pq, cq_ref, dpq_ref, dcq_ref)
        conv_bwd(dxk, pk, ck_ref, dpk_ref, dck_ref)
        conv_bwd(dxv, pv, cv_ref, dpv_ref, dcv_ref)

        dg = g_s[...]
        beta = p["beta"]
        da_raw = dg * p["neg_ea"] * _sigmoid(p["pre"])
        db_raw = b_s[...] * beta * (1.0 - beta)
        lane = _iota((T, LANES), 1)
        contrib = jnp.where(lane == h, da_raw, 0.0) + jnp.where(lane == DN_H + h, db_raw, 0.0)
        lane1 = _iota((1, LANES), 1)
        dalog_h = jnp.where(lane1 == h, _colsum(dg * p["g"]), 0.0)
        ddtb_h = jnp.where(lane1 == h, _colsum(da_raw), 0.0)

        @pl.when(h == 0)
        def _():
            dpab_ref[...] = contrib.astype(dpab_ref.dtype)
            dalog_ref[...] = dalog_h
            ddtb_ref[...] = ddtb_h
            dog_ref[...] = dog_h

        @pl.when(h > 0)
        def _():
            dpab_ref[...] += contrib.astype(dpab_ref.dtype)
            dalog_ref[...] += dalog_h
            ddtb_ref[...] += ddtb_h
            dog_ref[...] += dog_h

    col = lambda k: pl.BlockSpec((T, DN_D), lambda h: (0, k * DN_H + h))
    cw = lambda k: pl.BlockSpec((DN_CONV, DN_D), lambda h: (0, k * DN_H + h))
    small = pl.BlockSpec((1, LANES), lambda h: (0, 0))
    ab = pl.BlockSpec((T, LANES), lambda h: (0, 4 * DN_H))
    hcol = pl.BlockSpec((T, DN_D), lambda h: (0, h))
    outs = pl.pallas_call(
        body, grid=(DN_H,),
        in_specs=[col(0), col(1), col(2), col(3), ab, cw(0), cw(1), cw(2), small, small, small,
                  hcol, pl.BlockSpec((1, N, DN_D, DN_D), lambda h: (h, 0, 0, 0)), hcol],
        out_specs=[hcol, hcol, hcol, hcol, pl.BlockSpec((T, LANES), lambda h: (0, 0)),
                   pl.BlockSpec((DN_CONV, DN_D), lambda h: (0, h)), pl.BlockSpec((DN_CONV, DN_D), lambda h: (0, h)),
                   pl.BlockSpec((DN_CONV, DN_D), lambda h: (0, h)), small, small, small],
        out_shape=[jax.ShapeDtypeStruct((T, D), BF16)] * 4 + [jax.ShapeDtypeStruct((T, LANES), BF16)]
                  + [jax.ShapeDtypeStruct((DN_CONV, D), F32)] * 3 + [jax.ShapeDtypeStruct((1, LANES), F32)] * 3,
        scratch_shapes=[pltpu.VMEM((T, DN_D), F32)] * 3 + [pltpu.VMEM((T, 1), F32)] * 2
                       + [pltpu.VMEM((T, DN_D), F32), pltpu.VMEM((DN_D, DN_D), F32)],
        name=name, compiler_params=_params(("arbitrary",)),
    )(proj, proj, proj, proj, proj, conv_w, conv_w, conv_w, alog, dtb, og, o, states, dout)
    dpq, dpk, dpv, dpz, dpab, dcq, dck, dcv, dalog, ddtb, dog = outs
    dproj = jnp.concatenate([dpq, dpk, dpv, dpz, dpab], axis=1)
    dconv = jnp.concatenate([dcq, dck, dcv], axis=1)
    return dproj, dconv, dalog, ddtb, dog


def _sb_head_norm(x, g, h0):
    xx = x * x
    s0 = _rowsum(jnp.where(h0, xx, 0.0))
    s1 = _rowsum(jnp.where(h0, 0.0, xx))
    r = lax.rsqrt(jnp.where(h0, s0, s1) * (1.0 / SB_D) + EPS)
    return x * r, r


def _sb_fill(q_ref, k_ref, v_ref, qg_ref, kg_ref, qm_s, kn_s, vm_s):
    T = q_ref.shape[0]
    h0 = _iota((T, LANES), 1) < SB_D
    qh, rq = _sb_head_norm(q_ref[...], None, h0)
    kh, rk = _sb_head_norm(k_ref[...], None, h0)
    qs = qh * qg_ref[...] * (SB_D ** -0.5)
    qm_s[0] = jnp.where(h0, qs, 0.0).astype(BF16)
    qm_s[1] = jnp.where(h0, 0.0, qs).astype(BF16)
    kn_s[...] = (kh * kg_ref[...]).astype(BF16)
    v = v_ref[...]
    vm_s[0] = jnp.where(h0, v, 0.0).astype(BF16)
    vm_s[1] = jnp.where(h0, 0.0, v).astype(BF16)
    return h0, qh, rq, kh, rk


def _sb_core_fwd(qkv, qg2, kg2, name):
    T = qkv.shape[0]
    B = SB_BLK
    NB = T // B
    NP = SB_H // 2

    def body(q_ref, k_ref, v_ref, qg_ref, kg_ref, o_ref, tot_ref, qm_s, kn_s, vm_s):
        _sb_fill(q_ref, k_ref, v_ref, qg_ref, kg_ref, qm_s, kn_s, vm_s)
        r, c = _iota((B, B), 0), _iota((B, B), 1)
        causal = c < r
        m_after = (r > c).astype(BF16)
        h0b = _iota((B, LANES), 1) < SB_D

        def tile(qb, kj, vj, R, acc, diag):
            z = _dot(qb, kj, NT)
            sp = _softplus(z)
            ls = z - sp
            lm = jnp.where(causal, -sp, 0.0) if diag else -sp
            cs = _dot2r(lm, m_after) + R
            a = jnp.exp(ls + cs)
            if diag:
                a = jnp.where(causal, a, 0.0)
            return R + _rowsum(lm), acc + _dot(a, vj)

        def qblock(i, carry):
            rows_i = pl.ds(pl.multiple_of(i * B, B), B)
            kd = kn_s[rows_i, :]
            res = []
            for hh in range(2):
                qb = qm_s[hh, rows_i, :]
                R, acc = tile(qb, kd, vm_s[hh, rows_i, :], jnp.zeros((B, 1), F32), jnp.zeros((B, LANES), F32), True)

                def inner(s, rc, qb=qb, hh=hh):
                    rows_j = pl.ds(pl.multiple_of((i - 1 - s) * B, B), B)
                    return tile(qb, kn_s[rows_j, :], vm_s[hh, rows_j, :], rc[0], rc[1], False)

                res.append(lax.fori_loop(0, i, inner, (R, acc)))
            o_ref[rows_i, :] = (res[0][1] + res[1][1]).astype(o_ref.dtype)
            tot_ref[rows_i, :] = jnp.where(h0b, res[0][0], res[1][0])
            return carry

        lax.fori_loop(0, NB, qblock, 0)

    blk = lambda k: pl.BlockSpec((T, LANES), lambda p: (0, k * NP + p))
    small = pl.BlockSpec((1, LANES), lambda p: (0, 0))
    return pl.pallas_call(
        body, grid=(NP,), in_specs=[blk(0), blk(1), blk(2), small, small],
        out_specs=[pl.BlockSpec((T, LANES), lambda p: (0, p))] * 2,
        out_shape=[jax.ShapeDtypeStruct((T, D), BF16), jax.ShapeDtypeStruct((T, D), F32)],
        scratch_shapes=[pltpu.VMEM((2, T, LANES), BF16), pltpu.VMEM((T, LANES), BF16), pltpu.VMEM((2, T, LANES), BF16)],
        name=name, compiler_params=_params(("arbitrary",)),
    )(qkv, qkv, qkv, qg2, kg2)


def _sb_core_bwd(qkv, qg2, kg2, tot, dout, name):
    T = qkv.shape[0]
    B = SB_BLK
    NB = T // B
    NP = SB_H // 2

    def body(q_ref, k_ref, v_ref, qg_ref, kg_ref, tot_ref, do_ref, dq_ref, dk_ref, dv_ref, dqg_ref, dkg_ref,
             qm_s, kn_s, vm_s, dom_s, dqn_s, dkn_s, dvv_s):
        p = pl.program_id(0)
        h0, qh, rq, kh, rk = _sb_fill(q_ref, k_ref, v_ref, qg_ref, kg_ref, qm_s, kn_s, vm_s)
        dov = do_ref[...]
        dom_s[0] = jnp.where(h0, dov, 0.0).astype(BF16)
        dom_s[1] = jnp.where(h0, 0.0, dov).astype(BF16)
        dkn_s[...] = jnp.zeros(dkn_s.shape, F32)
        dvv_s[...] = jnp.zeros(dvv_s.shape, F32)
        r, c = _iota((B, B), 0), _iota((B, B), 1)
        causal = c < r
        m_upto = (r <= c).astype(BF16)
        m_before = (r < c).astype(BF16)
        laneb = _iota((B, LANES), 1)

        def tile(qb, dob, tot_h, rows_j, hh, PL, P, dq, diag):
            kj = kn_s[rows_j, :]
            vj = vm_s[hh, rows_j, :]
            z = _dot(qb, kj, NT)
            sp = _softplus(z)
            ls = z - sp
            lm = jnp.where(causal, -sp, 0.0) if diag else -sp
            cs = tot_h - PL - _dot2r(lm, m_upto)
            a = jnp.exp(ls + cs)
            if diag:
                a = jnp.where(causal, a, 0.0)
            e = _dot(dob, vj, NT) * a
            E = _dot2r(e, m_before) + P
            sig = jnp.exp(ls)
            dz = e * (1.0 - sig) - E * sig
            if diag:
                dz = jnp.where(causal, dz, 0.0)
            dkn_s[rows_j, :] += _dot(dz, qb, TN)
            dvv_s[rows_j, :] += _dot(a, dob, TN)
            return PL + _rowsum(lm), P + _rowsum(e), dq + _dot(dz, kj)

        def qblock(i, carry):
            rows_i = pl.ds(pl.multiple_of(i * B, B), B)
            totb = tot_ref[rows_i, :]
            dqs = []
            for hh in range(2):
                qb = qm_s[hh, rows_i, :]
                dob = dom_s[hh, rows_i, :]
                tot_h = _rowsum(jnp.where(laneb == hh * SB_D, totb, 0.0))

                def inner(j, st, qb=qb, dob=dob, tot_h=tot_h, hh=hh):
                    rows_j = pl.ds(pl.multiple_of(j * B, B), B)
                    return tile(qb, dob, tot_h, rows_j, hh, st[0], st[1], st[2], False)

                st = lax.fori_loop(0, i, inner, (jnp.zeros((B, 1), F32), jnp.zeros((B, 1), F32), jnp.zeros((B, LANES), F32)))
                st = tile(qb, dob, tot_h, rows_i, hh, st[0], st[1], st[2], True)
                dqs.append(st[2])
            dqn_s[rows_i, :] = jnp.where(laneb < SB_D, dqs[0], dqs[1]) * (SB_D ** -0.5)
            return carry

        lax.fori_loop(0, NB, qblock, 0)

        def norm_bwd(dn, xh, rr, g):
            dg = _colsum(dn * xh)
            t = dn * g
            tx = t * xh
            m0 = _rowsum(jnp.where(h0, tx, 0.0))
            m1 = _rowsum(jnp.where(h0, 0.0, tx))
            return rr * (t - xh * (jnp.where(h0, m0, m1) * (1.0 / SB_D))), dg

        dq, dqg = norm_bwd(dqn_s[...], qh, rq, qg_ref[...])
        dk, dkg = norm_bwd(dkn_s[...], kh, rk, kg_ref[...])
        dq_ref[...] = dq.astype(dq_ref.dtype)
        dk_ref[...] = dk.astype(dk_ref.dtype)
        dv_ref[...] = dvv_s[...].astype(dv_ref.dtype)

        @pl.when(p == 0)
        def _():
            dqg_ref[...] = dqg
            dkg_ref[...] = dkg

        @pl.when(p > 0)
        def _():
            dqg_ref[...] += dqg
            dkg_ref[...] += dkg

    blk = lambda k: pl.BlockSpec((T, LANES), lambda p: (0, k * NP + p))
    small = pl.BlockSpec((1, LANES), lambda p: (0, 0))
    own = pl.BlockSpec((T, LANES), lambda p: (0, p))
    dq, dk, dv, dqg, dkg = pl.pallas_call(
        body, grid=(NP,), in_specs=[blk(0), blk(1), blk(2), small, small, own, own],
        out_specs=[own, own, own, small, small],
        out_shape=[jax.ShapeDtypeStruct((T, D), BF16)] * 3 + [jax.ShapeDtypeStruct((1, LANES), F32)] * 2,
        scratch_shapes=[pltpu.VMEM((2, T, LANES), BF16), pltpu.VMEM((T, LANES), BF16), pltpu.VMEM((2, T, LANES), BF16),
                        pltpu.VMEM((2, T, LANES), BF16)] + [pltpu.VMEM((T, LANES), F32)] * 3,
        name=name, compiler_params=_params(("arbitrary",)),
    )(qkv, qkv, qkv, qg2, kg2, tot, dout)
    return jnp.concatenate([dq, dk, dv], axis=1), dqg, dkg


def _ada_fwd(c16, ada_w, ada_b_cols, name):
    L, _, cols = ada_w.shape

    def body(c_ref, w_ref, b_ref, o_ref):
        o_ref[0] = _dot(_silu(c_ref[...]), w_ref[0]) + b_ref[0]

    return pl.pallas_call(
        body, grid=(L,),
        in_specs=[pl.BlockSpec((16, D), lambda i: (0, 0)), pl.BlockSpec((1, D, cols), lambda i: (i, 0, 0)),
                  pl.BlockSpec((1, 1, cols), lambda i: (i, 0, 0))],
        out_specs=pl.BlockSpec((1, 16, cols), lambda i: (i, 0, 0)),
        out_shape=jax.ShapeDtypeStruct((L, 16, cols), F32), name=name, compiler_params=_params(("arbitrary",)),
    )(c16, ada_w, ada_b_cols)


def _ada_bwd(c16, dmod16, name):
    L, _, cols = dmod16.shape

    def body(c_ref, d_ref, o_ref):
        o_ref[0] = _dot(_silu(c_ref[...]), d_ref[0], TN)

    return pl.pallas_call(
        body, grid=(L,),
        in_specs=[pl.BlockSpec((16, D), lambda i: (0, 0)), pl.BlockSpec((1, 16, cols), lambda i: (i, 0, 0))],
        out_specs=pl.BlockSpec((1, D, cols), lambda i: (i, 0, 0)),
        out_shape=jax.ShapeDtypeStruct((L, D, cols), F32), name=name, compiler_params=_params(("arbitrary",)),
    )(c16, dmod16)


def _sum_sources(x, name, tile=512):
    n, R, C = x.shape
    tile = min(tile, R)

    def body(x_ref, o_ref):
        acc = x_ref[0].astype(F32)
        for k in range(1, n):
            acc = acc + x_ref[k].astype(F32)
        o_ref[...] = acc

    return pl.pallas_call(
        body, grid=(R // tile,), in_specs=[pl.BlockSpec((n, tile, C), lambda i: (0, i, 0))],
        out_specs=pl.BlockSpec((tile, C), lambda i: (i, 0)), out_shape=jax.ShapeDtypeStruct((R, C), F32),
        name=name, compiler_params=_params(("parallel",)),
    )(x)


def _adamw(w, g, m, v, name):
    shape = w.shape
    C = shape[-1]
    R = w.size // C
    w2, g2, m2, v2 = (a.reshape(R, C) for a in (w, g, m, v))
    tile = _rtile(R)
    c1 = 1.0 / (1.0 - ADAM_B1 ** ADAM_STEP)
    c2 = 1.0 / (1.0 - ADAM_B2 ** ADAM_STEP)

    def body(w_ref, g_ref, m_ref, v_ref, d_ref, nm_ref, nv_ref):
        gv = g_ref[...]
        nm = ADAM_B1 * m_ref[...] + (1.0 - ADAM_B1) * gv
        nv = ADAM_B2 * v_ref[...] + (1.0 - ADAM_B2) * (gv * gv)
        d_ref[...] = -ADAM_LR * ((nm * c1) / (jnp.sqrt(nv * c2) + ADAM_EPS) + ADAM_WD * w_ref[...])
        nm_ref[...] = nm
        nv_ref[...] = nv

    spec = pl.BlockSpec((tile, C), lambda i: (i, 0))
    outs = pl.pallas_call(
        body, grid=(R // tile,), in_specs=[spec] * 4, out_specs=[spec] * 3,
        out_shape=[jax.ShapeDtypeStruct((R, C), F32)] * 3, name=name, compiler_params=_params(("parallel",)),
    )(w2, g2, m2, v2)
    return tuple(o.reshape(shape) for o in outs)


def _all_gather(x, name, in_vmem):
    R, C = x.shape

    def body(x_ref, out_ref, send_sems, recv_sems, local_sem):
        mx, my, mc = lax.axis_index("x"), lax.axis_index("y"), lax.axis_index("c")
        me, sibling = (mx, my, mc), (mx, my, 1 - mc)
        chips = [(1 - mx, my), (mx, 1 - my), (1 - mx, 1 - my)]

        def slot(px, py, pc):
            return out_ref.at[4 * px + 2 * py + pc]

        def copy(k, block, to, src=None):
            return pltpu.make_async_remote_copy(
                src_ref=slot(*block) if src is None else src, dst_ref=slot(*block),
                send_sem=send_sems.at[k], recv_sem=recv_sems.at[k], device_id=to, device_id_type=MESH)

        mine = pltpu.make_async_copy(x_ref, slot(*me), local_sem)
        mine.start()
        first = [copy(0, me, sibling, src=x_ref)]
        first += [copy(1 + j, me, (*chip, mc), src=x_ref) for j, chip in enumerate(chips)]
        for cp in first:
            cp.start()
        passed = [copy(4 + j, (*chip, mc), sibling) for j, chip in enumerate(chips)]
        for j, chip in enumerate(chips):
            copy(1 + j, (*chip, mc), me).wait_recv()
            passed[j].start()
        copy(0, sibling, me).wait_recv()
        for j, chip in enumerate(chips):
            copy(4 + j, (*chip, 1 - mc), me).wait_recv()
        for cp in first + passed:
            cp.wait_send()
        mine.wait()

    space = pltpu.VMEM if in_vmem else pl.ANY
    return pl.pallas_call(
        body, out_shape=jax.ShapeDtypeStruct((NDEV, R, C), x.dtype),
        in_specs=[pl.BlockSpec(memory_space=space)], out_specs=pl.BlockSpec(memory_space=space),
        scratch_shapes=[pltpu.SemaphoreType.DMA((7,)), pltpu.SemaphoreType.DMA((7,)), pltpu.SemaphoreType.DMA],
        name=name, compiler_params=pltpu.CompilerParams(vmem_limit_bytes=VMEM_LIMIT),
    )(x)


def _all_to_all(x, name):
    n, R, C = x.shape

    def body(x_ref, out_ref, send_sems, recv_sems, local_sem):
        mx, my, mc = lax.axis_index("x"), lax.axis_index("y"), lax.axis_index("c")
        me = 4 * mx + 2 * my + mc
        mine = pltpu.make_async_copy(x_ref.at[me], out_ref.at[me], local_sem)
        mine.start()
        copies = []
        for k in range(1, NDEV):
            px, py, pc = mx ^ (k >> 2), my ^ ((k >> 1) & 1), mc ^ (k & 1)
            peer = 4 * px + 2 * py + pc
            copies.append(pltpu.make_async_remote_copy(
                src_ref=x_ref.at[peer], dst_ref=out_ref.at[me], send_sem=send_sems.at[k - 1],
                recv_sem=recv_sems.at[k - 1], device_id=(px, py, pc), device_id_type=MESH))
        for cp in copies:
            cp.start()
        for cp in copies:
            cp.wait_recv()
        for cp in copies:
            cp.wait_send()
        mine.wait()

    return pl.pallas_call(
        body, out_shape=jax.ShapeDtypeStruct((n, R, C), x.dtype),
        in_specs=[pl.BlockSpec(memory_space=pl.ANY)], out_specs=pl.BlockSpec(memory_space=pl.ANY),
        scratch_shapes=[pltpu.SemaphoreType.DMA((7,)), pltpu.SemaphoreType.DMA((7,)), pltpu.SemaphoreType.DMA],
        name=name, compiler_params=pltpu.CompilerParams(vmem_limit_bytes=VMEM_LIMIT),
    )(x)


def _local_step(x, target, mod, norm1_g, norm2_g, dn_w_in, dn_conv_w, dn_a_log, dn_dt_bias, dn_onorm_g, dn_w_out,
                sb_w_qkv, sb_q_norm_g, sb_k_norm_g, sb_w_out, ffn_w_in, ffn_w_out):
    row = lambda v: v.reshape(1, -1)
    pad128 = lambda v: jnp.pad(v.reshape(1, -1), ((0, 0), (0, LANES - v.size)))
    saved = []
    for i in range(DEPTH):
        j = i // 2
        m = [row(mod[i, k * D:(k + 1) * D]) for k in range(N_MOD)]
        sh1, sc1, gt1, sh2, sc2, gt2 = m
        g1, g2 = row(norm1_g[i]), row(norm2_g[i])
        h1 = _norm_mod(x, g1, sc1, sh1, f"norm1_{i}")
        if i % 2 == 0:
            proj = _mm(h1, dn_w_in[j], "nn", F32, f"dn_proj_{i}", tn=384)
            alog, dtb, og = pad128(dn_a_log[j]), pad128(dn_dt_bias[j]), row(dn_onorm_g[j])
            om, o_pre, states = _dn_core_fwd(proj, dn_conv_w[j], alog, dtb, og, f"dn_core_{i}")
            y1 = _mm(om, dn_w_out[j], "nn", F32, f"dn_out_{i}")
            mix = (proj, alog, dtb, og, o_pre, states, om)
        else:
            qkv = _mm(h1, sb_w_qkv[j], "nn", F32, f"sb_qkv_{i}")
            qg2 = jnp.tile(row(sb_q_norm_g[j]), (1, 2))
            kg2 = jnp.tile(row(sb_k_norm_g[j]), (1, 2))
            om, tot = _sb_core_fwd(qkv, qg2, kg2, f"sb_core_{i}")
            y1 = _mm(om, sb_w_out[j], "nn", F32, f"sb_out_{i}")
            mix = (qkv, qg2, kg2, tot, om)
        x_mid = _gate_res(x, y1, gt1, f"res1_{i}")
        h2 = _norm_mod(x_mid, g2, sc2, sh2, f"norm2_{i}")
        u = _mm(h2, ffn_w_in[i], "nn", F32, f"ffn_in_{i}")
        a = _swiglu_act(u, f"ffn_act_{i}")
        y2 = _mm(a, ffn_w_out[i], "nn", F32, f"ffn_out_{i}", tk=1408)
        x_out = _gate_res(x_mid, y2, gt2, f"res2_{i}")
        saved.append((x, h1, mix, y1, x_mid, h2, u, a, y2, m, g1, g2))
        x = x_out

    dx, sq = _loss_head(x, target, "loss_head")

    gr = dict(mod=[None] * DEPTH, norm1_g=[None] * DEPTH, norm2_g=[None] * DEPTH, ffn_w_in=[None] * DEPTH,
              ffn_w_out=[None] * DEPTH, dn_w_in=[None] * 2, dn_w_out=[None] * 2, dn_conv_w=[None] * 2,
              dn_a_log=[None] * 2, dn_dt_bias=[None] * 2, dn_onorm_g=[None] * 2, sb_w_qkv=[None] * 2,
              sb_w_out=[None] * 2, sb_q_norm_g=[None] * 2, sb_k_norm_g=[None] * 2)
    for i in reversed(range(DEPTH)):
        j = i // 2
        x_in, h1, mix, y1, x_mid, h2, u, a, y2, m, g1, g2 = saved[i]
        sh1, sc1, gt1, sh2, sc2, gt2 = m
        dy2, dgt2 = _gate_res_bwd(dx, y2, gt2, f"res2_bwd_{i}")
        da = _mm(dy2, ffn_w_out[i], "nt", F32, f"ffn_out_bwd_{i}", tn=256)
        gr["ffn_w_out"][i] = _mm(a, dy2, "tn", F32, f"ffn_out_wg_{i}", tm=256, tk=512)
        du = _swiglu_act_bwd(da, u, f"ffn_act_bwd_{i}")
        dh2 = _mm(du, ffn_w_in[i], "nt", F32, f"ffn_in_bwd_{i}", tk=1408)
        gr["ffn_w_in"][i] = _mm(h2, du, "tn", F32, f"ffn_in_wg_{i}", tk=512)
        dx_mid, s2, dsh2 = _norm_mod_bwd(dh2, x_mid, dx, g2, sc2, f"norm2_bwd_{i}")
        dy1, dgt1 = _gate_res_bwd(dx_mid, y1, gt1, f"res1_bwd_{i}")
        if i % 2 == 0:
            proj, alog, dtb, og, o_pre, states, om = mix
            dom = _mm(dy1, dn_w_out[j], "nt", F32, f"dn_out_bwd_{i}")
            gr["dn_w_out"][j] = _mm(om, dy1, "tn", F32, f"dn_out_wg_{i}", tk=512)
            dproj, dconv, dalog, ddtb, dog = _dn_core_bwd(proj, dn_conv_w[j], alog, dtb, og, o_pre, states, dom,
                                                          f"dn_core_bwd_{i}")
            dh1 = _mm(dproj, dn_w_in[j], "nt", F32, f"dn_proj_bwd_{i}", tk=384)
            gr["dn_w_in"][j] = _mm(h1, dproj, "tn", F32, f"dn_proj_wg_{i}", tn=384, tk=512)[:, :DN_COLS]
            gr["dn_conv_w"][j] = dconv
            gr["dn_a_log"][j] = dalog[0, :DN_H]
            gr["dn_dt_bias"][j] = ddtb[0, :DN_H]
            gr["dn_onorm_g"][j] = dog[0]
        else:
            qkv, qg2, kg2, tot, om = mix
            dom = _mm(dy1, sb_w_out[j], "nt", F32, f"sb_out_bwd_{i}")
            gr["sb_w_out"][j] = _mm(om, dy1, "tn", F32, f"sb_out_wg_{i}", tk=512)
            dqkv, dqg, dkg = _sb_core_bwd(qkv, qg2, kg2, tot, dom, f"sb_core_bwd_{i}")
            dh1 = _mm(dqkv, sb_w_qkv[j], "nt", F32, f"sb_qkv_bwd_{i}")
            gr["sb_w_qkv"][j] = _mm(h1, dqkv, "tn", F32, f"sb_qkv_wg_{i}", tk=512)
            gr["sb_q_norm_g"][j] = dqg[0, :SB_D] + dqg[0, SB_D:]
            gr["sb_k_norm_g"][j] = dkg[0, :SB_D] + dkg[0, SB_D:]
        dx, s1, dsh1 = _norm_mod_bwd(dh1, x_in, dx_mid, g1, sc1, f"norm1_bwd_{i}")
        gr["mod"][i] = jnp.concatenate([dsh1, s1 * g1, dgt1, dsh2, s2 * g2, dgt2], axis=1)[0]
        gr["norm1_g"][i] = (s1 * (1.0 + sc1))[0]
        gr["norm2_g"][i] = (s2 * (1.0 + sc2))[0]
    return sq, dx, {k: jnp.stack(v) for k, v in gr.items()}


_BIG = (("dn_w_in", (2, 1024, 514)), ("dn_w_out", (2, 128, 1024)), ("sb_w_qkv", (2, 1024, 384)),
        ("sb_w_out", (2, 128, 1024)), ("ffn_w_in", (4, 1024, 704)), ("ffn_w_out", (4, 352, 1024)))
_BIG_COL_SHARDED = {"dn_w_in", "sb_w_qkv", "ffn_w_in"}
_BIG_ROWS_PAD = 6656


def _big_layout():
    out, r = [], 0
    for i, (n, s) in enumerate(_BIG):
        k = s[0] * s[1] * s[2] // 1024
        kp = -(-k // 16) * 16 if i < len(_BIG) - 1 else _BIG_ROWS_PAD - r
        out.append((n, s, k, kp, r))
        r += kp
    assert r == _BIG_ROWS_PAD
    return out


def _pack_big(parts, dtype):
    return jnp.concatenate([jnp.pad(parts[n].astype(dtype).reshape(k, 1024), ((0, kp - k), (0, 0)))
                            for n, _, k, kp, _ in _big_layout()], axis=0)


def _unpack_big(buf):
    return {n: buf[r:r + k].reshape(s) for n, s, k, _, r in _big_layout()}


def _full_from_gathered(g):
    out = {}
    for n, s, k, _, r in _big_layout():
        a = g[:, r:r + k].reshape((NDEV,) + s)
        if n in _BIG_COL_SHARDED:
            out[n] = jnp.transpose(a, (1, 2, 0, 3)).reshape(s[0], s[1], NDEV * s[2])
        else:
            out[n] = jnp.transpose(a, (1, 0, 2, 3)).reshape(s[0], NDEV * s[1], s[2])
    return out


def _shards_from_full(full):
    parts = []
    for n, s, k, kp, _ in _big_layout():
        a = full[n]
        if n in _BIG_COL_SHARDED:
            a = jnp.transpose(a.reshape(s[0], s[1], NDEV, s[2]), (2, 0, 1, 3))
        else:
            a = jnp.transpose(a.reshape(s[0], NDEV, s[1], s[2]), (1, 0, 2, 3))
        parts.append(jnp.pad(a.astype(BF16).reshape(NDEV, k, 1024), ((0, 0), (0, kp - k), (0, 0))))
    return jnp.concatenate(parts, axis=1)


def _device_index():
    return 4 * lax.axis_index("x") + 2 * lax.axis_index("y") + lax.axis_index("c")


def _gather_phase(w, c):
    me = _device_index()
    ada_cols = w["ada_w"].shape[-1]
    conv_cols = w["dn_conv_w"].shape[-1]
    blk = jnp.concatenate([c.reshape(1, D), w["dn_conv_w"].reshape(-1, D)], axis=0)
    g1 = _all_gather(blk, "gather_cond", True)
    c16 = jnp.pad(g1[:, 0, :], ((0, 8), (0, 0)))
    conv_full = jnp.transpose(g1[:, 1:, :].reshape(NDEV, 2, DN_CONV, conv_cols), (1, 2, 0, 3)).reshape(2, DN_CONV, -1)
    b_cols = lax.dynamic_slice_in_dim(w["ada_b"], me * ada_cols, ada_cols, axis=1).reshape(DEPTH, 1, ada_cols)
    mod_part = _ada_fwd(c16, w["ada_w"], b_cols, "ada_fwd")[:, :NDEV, :]
    g2 = _all_gather(mod_part.reshape(DEPTH * NDEV, ada_cols), "gather_mod", True)
    g2 = g2.reshape(NDEV, DEPTH, NDEV, ada_cols)
    mod = lax.dynamic_index_in_dim(g2, me, axis=2, keepdims=False)
    mod = jnp.transpose(mod, (1, 0, 2)).reshape(DEPTH, N_MOD * D)
    big = _all_gather(_pack_big(w, BF16), "gather_weights", False)
    full = _full_from_gathered(big)
    full["dn_w_in"] = jnp.pad(full["dn_w_in"], ((0, 0), (0, 0), (0, DN_COLS_PAD - DN_COLS)))
    return c16, conv_full, mod, full


def _reduce_phase(gr, c16):
    me = _device_index()
    ada_cols = N_MOD * D // NDEV
    conv_cols = 3 * DN_H * DN_D // NDEV
    recv = _all_to_all(_shards_from_full(gr), "exchange_grads")
    grads = _unpack_big(_sum_sources(recv, "sum_grads"))
    small = jnp.concatenate([gr["dn_a_log"].reshape(-1), gr["dn_dt_bias"].reshape(-1), gr["dn_onorm_g"].reshape(-1),
                             gr["sb_q_norm_g"].reshape(-1), gr["sb_k_norm_g"].reshape(-1)])
    small = jnp.pad(small, (0, D - small.size)).reshape(1, D)
    blk3 = jnp.concatenate([gr["mod"].reshape(-1, D), gr["norm1_g"], gr["norm2_g"], gr["dn_conv_w"].reshape(-1, D),
                            small], axis=0)
    blk3 = jnp.pad(blk3, ((0, 64 - blk3.shape[0]), (0, 0)))
    g3 = _all_gather(blk3, "gather_small_grads", True)
    tot = _sum_sources(g3, "sum_small_grads", tile=64)
    grads["ada_b"] = tot[:24].reshape(DEPTH, N_MOD * D)
    grads["norm1_g"] = tot[24:28]
    grads["norm2_g"] = tot[28:32]
    conv_g = tot[32:56].reshape(2, DN_CONV, NDEV * conv_cols)
    grads["dn_conv_w"] = lax.dynamic_slice_in_dim(conv_g, me * conv_cols, conv_cols, axis=2)
    sm = tot[56]
    grads["dn_a_log"] = sm[0:16].reshape(2, DN_H)
    grads["dn_dt_bias"] = sm[16:32].reshape(2, DN_H)
    grads["dn_onorm_g"] = sm[32:288].reshape(2, DN_D)
    grads["sb_q_norm_g"] = sm[288:416].reshape(2, SB_D)
    grads["sb_k_norm_g"] = sm[416:544].reshape(2, SB_D)
    dmod_all = g3[:, :24, :].reshape(NDEV, DEPTH, N_MOD * D)
    dmod_cols = lax.dynamic_slice_in_dim(dmod_all, me * ada_cols, ada_cols, axis=2)
    dmod16 = jnp.pad(jnp.transpose(dmod_cols, (1, 0, 2)), ((0, 0), (0, 8), (0, 0)))
    grads["ada_w"] = _ada_bwd(c16, dmod16, "ada_bwd")
    return grads


def kernel(x, c, ada_w, ada_b, norm1_g, norm2_g, dn_w_in, dn_conv_w, dn_a_log, dn_dt_bias, dn_onorm_g, dn_w_out, sb_w_qkv, sb_q_norm_g, sb_k_norm_g, sb_w_out, ffn_w_in, ffn_w_out, loss_target, m_ada_w, m_ada_b, m_norm1_g, m_norm2_g, m_dn_w_in, m_dn_conv_w, m_dn_a_log, m_dn_dt_bias, m_dn_onorm_g, m_dn_w_out, m_sb_w_qkv, m_sb_q_norm_g, m_sb_k_norm_g, m_sb_w_out, m_ffn_w_in, m_ffn_w_out, v_ada_w, v_ada_b, v_norm1_g, v_norm2_g, v_dn_w_in, v_dn_conv_w, v_dn_a_log, v_dn_dt_bias, v_dn_onorm_g, v_dn_w_out, v_sb_w_qkv, v_sb_q_norm_g, v_sb_k_norm_g, v_sb_w_out, v_ffn_w_in, v_ffn_w_out):
    w = dict(ada_w=ada_w, ada_b=ada_b, norm1_g=norm1_g, norm2_g=norm2_g, dn_w_in=dn_w_in, dn_conv_w=dn_conv_w,
             dn_a_log=dn_a_log, dn_dt_bias=dn_dt_bias, dn_onorm_g=dn_onorm_g, dn_w_out=dn_w_out, sb_w_qkv=sb_w_qkv,
             sb_q_norm_g=sb_q_norm_g, sb_k_norm_g=sb_k_norm_g, sb_w_out=sb_w_out, ffn_w_in=ffn_w_in, ffn_w_out=ffn_w_out)
    mom = dict(ada_w=m_ada_w, ada_b=m_ada_b, norm1_g=m_norm1_g, norm2_g=m_norm2_g, dn_w_in=m_dn_w_in,
               dn_conv_w=m_dn_conv_w, dn_a_log=m_dn_a_log, dn_dt_bias=m_dn_dt_bias, dn_onorm_g=m_dn_onorm_g,
               dn_w_out=m_dn_w_out, sb_w_qkv=m_sb_w_qkv, sb_q_norm_g=m_sb_q_norm_g, sb_k_norm_g=m_sb_k_norm_g,
               sb_w_out=m_sb_w_out, ffn_w_in=m_ffn_w_in, ffn_w_out=m_ffn_w_out)
    var = dict(ada_w=v_ada_w, ada_b=v_ada_b, norm1_g=v_norm1_g, norm2_g=v_norm2_g, dn_w_in=v_dn_w_in,
               dn_conv_w=v_dn_conv_w, dn_a_log=v_dn_a_log, dn_dt_bias=v_dn_dt_bias, dn_onorm_g=v_dn_onorm_g,
               dn_w_out=v_dn_w_out, sb_w_qkv=v_sb_w_qkv, sb_q_norm_g=v_sb_q_norm_g, sb_k_norm_g=v_sb_k_norm_g,
               sb_w_out=v_sb_w_out, ffn_w_in=v_ffn_w_in, ffn_w_out=v_ffn_w_out)
    names = list(w)
    c16, conv_full, mod, full = _gather_phase(w, c)
    sq, grad_x, gr = _local_step(
        x[0], loss_target[0], mod, norm1_g, norm2_g, full["dn_w_in"], conv_full, dn_a_log, dn_dt_bias, dn_onorm_g,
        full["dn_w_out"], full["sb_w_qkv"], sb_q_norm_g, sb_k_norm_g, full["sb_w_out"], full["ffn_w_in"],
        full["ffn_w_out"])
    loss = lax.psum(sq[0, 0] * (0.5 / D), ("x", "y", "c"))
    grads = _reduce_phase(gr, c16)
    delta, new_m, new_v = {}, {}, {}
    for n in names:
        delta[n], new_m[n], new_v[n] = _adamw(w[n], grads[n], mom[n], var[n], f"adamw_{n}")
    return (loss, grad_x[None], *[grads[n] for n in names], *[delta[n] for n in names],
            *[new_m[n] for n in names], *[new_v[n] for n in names])
```

```python
import functools

import jax
import jax.numpy as jnp
from jax import lax
from jax.experimental import pallas as pl
from jax.experimental.pallas import tpu as pltpu

F32, BF16 = jnp.float32, jnp.bfloat16

D = 1024
DEPTH = 4
N_MOD = 6
DN_H, DN_D, DN_C, DN_CONV = 8, 128, 64, 4
DN_UNROLL_FWD, DN_UNROLL_BWD = 16, 4
DN_COLS = 4 * DN_H * DN_D + 2 * DN_H
DN_COLS_PAD = 33 * 128
SB_H, SB_D = 16, 64
SB_TILE = 256
DFF = 2816
EPS = 1e-6
NDEV = 8
LANES = 128
VMEM_LIMIT = 48 * 1024 * 1024

ADAM_LR, ADAM_B1, ADAM_B2, ADAM_EPS, ADAM_WD, ADAM_STEP = 0.001, 0.9, 0.999, 1e-08, 0.01, 10

NN = ((1,), (0,))
NT = ((1,), (1,))
TN = ((0,), (0,))
MESH = pl.DeviceIdType.MESH


def _dot(a, b, dims=NN):
    a, b = a.astype(BF16), b.astype(BF16)
    if a.ndim == 2 and b.ndim == 2:
        return lax.dot_general(a, b, (dims, ((), ())), preferred_element_type=F32)
    n = a.shape[0] if a.ndim == 3 else b.shape[0]
    if a.ndim == 2:
        a = jnp.broadcast_to(a, (n,) + a.shape)
    if b.ndim == 2:
        b = jnp.broadcast_to(b, (n,) + b.shape)
    (ca,), (cb,) = dims
    return lax.dot_general(a, b, (((ca + 1,), (cb + 1,)), ((0,), (0,))), preferred_element_type=F32)


def _split(a):
    hi = a.astype(BF16)
    lo = (a - hi.astype(F32)).astype(BF16)
    return hi, lo


def _dot3(a, b, dims=NN):
    ah, al = _split(a)
    bh, bl = _split(b)
    return _dot(ah, bh, dims) + (_dot(al, bh, dims) + _dot(ah, bl, dims))


def _dot2r(a, m, dims=NN):
    ah, al = _split(a)
    return _dot(ah, m, dims) + _dot(al, m, dims)


def _dot2l(m, b, dims=NN):
    bh, bl = _split(b)
    return _dot(m, bh, dims) + _dot(m, bl, dims)


def _sigmoid(x):
    return 1.0 / (1.0 + jnp.exp(-x))


def _silu(x):
    return x * _sigmoid(x)


def _dsilu(x):
    s = _sigmoid(x)
    return s * (1.0 + x * (1.0 - s))


def _softplus(x):
    return jnp.maximum(x, 0.0) + jnp.log(1.0 + jnp.exp(-jnp.abs(x)))


def _iota(shape, dim):
    return lax.broadcasted_iota(jnp.int32, shape, dim)


def _rowsum(x):
    return jnp.sum(x, axis=-1, keepdims=True)


def _colsum(x):
    return jnp.sum(x, axis=-2, keepdims=True)


def _tile(n, pref):
    if n <= pref:
        return n
    best = None
    for t in range(LANES, pref + 1, LANES):
        if n % t == 0:
            best = t
    assert best is not None, (n, pref)
    return best


def _rtile(r, pref=512):
    best = None
    for t in range(8, min(r, pref) + 1, 8):
        if r % t == 0:
            best = t
    return best if best is not None else r


def _params(sem):
    return pltpu.CompilerParams(dimension_semantics=sem, vmem_limit_bytes=VMEM_LIMIT)


MM_TILE = 1408


def _mm(a, b, mode, out_dtype, name):
    if mode == "nn":
        (M, K), (K2, N) = a.shape, b.shape
        dims = NN
    elif mode == "nt":
        (M, K), (N, K2) = a.shape, b.shape
        dims = NT
    else:
        (K, M), (K2, N) = a.shape, b.shape
        dims = TN
    assert K == K2, (a.shape, b.shape, mode)
    tm, tn, tk = _tile(M, MM_TILE), _tile(N, MM_TILE), _tile(K, MM_TILE)
    nk = K // tk

    def body_single(a_ref, b_ref, o_ref):
        o_ref[...] = _dot(a_ref[...], b_ref[...], dims).astype(o_ref.dtype)

    def body_acc(a_ref, b_ref, o_ref, acc_ref):
        k = pl.program_id(2)
        p = _dot(a_ref[...], b_ref[...], dims)

        @pl.when(k == 0)
        def _():
            acc_ref[...] = p

        @pl.when(k > 0)
        def _():
            acc_ref[...] += p

        @pl.when(k == nk - 1)
        def _():
            o_ref[...] = acc_ref[...].astype(o_ref.dtype)

    if mode == "nn":
        a_spec = pl.BlockSpec((tm, tk), lambda i, j, k: (i, k))
        b_spec = pl.BlockSpec((tk, tn), lambda i, j, k: (k, j))
    elif mode == "nt":
        a_spec = pl.BlockSpec((tm, tk), lambda i, j, k: (i, k))
        b_spec = pl.BlockSpec((tn, tk), lambda i, j, k: (j, k))
    else:
        a_spec = pl.BlockSpec((tk, tm), lambda i, j, k: (k, i))
        b_spec = pl.BlockSpec((tk, tn), lambda i, j, k: (k, j))
    return pl.pallas_call(
        body_single if nk == 1 else body_acc, grid=(M // tm, N // tn, nk), in_specs=[a_spec, b_spec],
        out_specs=pl.BlockSpec((tm, tn), lambda i, j, k: (i, j)),
        out_shape=jax.ShapeDtypeStruct((M, N), out_dtype),
        scratch_shapes=[] if nk == 1 else [pltpu.VMEM((tm, tn), F32)], name=name,
        compiler_params=_params(("parallel", "parallel", "arbitrary")),
    )(a, b)


def _rowwise(fn, name, rows, bcasts, out_rows, out_reds=(), tile=256):
    T = rows[0].shape[0]
    tile = min(tile, T)
    nr, nb, no = len(rows), len(bcasts), len(out_rows)

    def body(*refs):
        rv = [r[...] for r in refs[:nr]]
        bv = [r[...] for r in refs[nr:nr + nb]]
        outs, reds = fn(rv, bv)
        for r, o in zip(refs[nr + nb:nr + nb + no], outs):
            r[...] = o.astype(r.dtype)
        red_refs = refs[nr + nb + no:]
        if red_refs:
            @pl.when(pl.program_id(0) == 0)
            def _():
                for r in red_refs:
                    r[...] = jnp.zeros(r.shape, F32)

            for r, v in zip(red_refs, reds):
                r[...] += v

    in_specs = [pl.BlockSpec((tile, a.shape[1]), lambda i: (i, 0)) for a in rows]
    in_specs += [pl.BlockSpec(b.shape, lambda i: (0, 0)) for b in bcasts]
    out_specs = [pl.BlockSpec((tile, c), lambda i: (i, 0)) for c, _ in out_rows]
    out_specs += [pl.BlockSpec(s, lambda i: (0, 0)) for s in out_reds]
    out_shape = [jax.ShapeDtypeStruct((T, c), dt) for c, dt in out_rows]
    out_shape += [jax.ShapeDtypeStruct(s, F32) for s in out_reds]
    return pl.pallas_call(
        body, grid=(T // tile,), in_specs=in_specs, out_specs=out_specs, out_shape=out_shape, name=name,
        compiler_params=_params(("arbitrary",)),
    )(*rows, *bcasts)


def _norm_mod(x, g, sc, sh, name):
    def fn(rv, bv):
        (xv,), (gv, scv, shv) = rv, bv
        r = lax.rsqrt(jnp.mean(xv * xv, axis=1, keepdims=True) + EPS)
        return [(xv * r * gv) * (1.0 + scv) + shv], []
    return _rowwise(fn, name, [x], [g, sc, sh], [(D, BF16)])[0]


def _norm_mod_bwd(dh, x, dres, g, sc, name):
    def fn(rv, bv):
        (dhv, xv, drv), (gv, scv) = rv, bv
        r = lax.rsqrt(jnp.mean(xv * xv, axis=1, keepdims=True) + EPS)
        xhat = xv * r
        dxhat = dhv * (gv * (1.0 + scv))
        dx = r * (dxhat - xhat * jnp.mean(dxhat * xhat, axis=1, keepdims=True)) + drv
        return [dx], [_colsum(dhv * xhat), _colsum(dhv)]
    return _rowwise(fn, name, [dh, x, dres], [g, sc], [(D, F32)], [(1, D), (1, D)])


def _gate_res(x, y, gt, name):
    def fn(rv, bv):
        return [rv[0] + bv[0] * rv[1]], []
    return _rowwise(fn, name, [x, y], [gt], [(D, F32)])[0]


def _gate_res_bwd(dxn, y, gt, name):
    def fn(rv, bv):
        return [rv[0] * bv[0]], [_colsum(rv[0] * rv[1])]
    return _rowwise(fn, name, [dxn, y], [gt], [(D, BF16)], [(1, D)])


def _swiglu_act(u, name):
    def fn(rv, bv):
        uv = rv[0]
        return [_silu(uv[:, :DFF]) * uv[:, DFF:]], []
    return _rowwise(fn, name, [u], [], [(DFF, BF16)])[0]


def _swiglu_act_bwd(da, u, name):
    def fn(rv, bv):
        dav, uv = rv
        gate, up = uv[:, :DFF], uv[:, DFF:]
        return [jnp.concatenate([dav * up * _dsilu(gate), dav * _silu(gate)], axis=1)], []
    return _rowwise(fn, name, [da, u], [], [(2 * DFF, BF16)])[0]


def _loss_head(y, target, name):
    def fn(rv, bv):
        err = rv[0] - rv[1]
        return [err * (1.0 / D)], [_colsum(_rowsum(err * err))]
    return _rowwise(fn, name, [y, target], [], [(D, F32)], [(1, 1)])


def _shift_rows(x, s):
    if s == 0:
        return x
    T = x.shape[0]
    r = pltpu.roll(x, s % T, axis=0)
    t = _iota(x.shape, 0)
    keep = (t >= s) if s > 0 else (t < T + s)
    return jnp.where(keep, r, 0.0)


def _dn_prep(pq, pk, pv, pab, cq_ref, ck_ref, cv_ref, alog, dtb, h):
    lane = _iota(pab.shape, 1)
    a_col = _rowsum(jnp.where(lane == h, pab, 0.0))
    b_col = _rowsum(jnp.where(lane == DN_H + h, pab, 0.0))
    lane1 = _iota(alog.shape, 1)
    alog_h = _rowsum(jnp.where(lane1 == h, alog, 0.0))
    dtb_h = _rowsum(jnp.where(lane1 == h, dtb, 0.0))
    pre = a_col + dtb_h
    neg_ea = -jnp.exp(alog_h)
    g = neg_ea * _softplus(pre)
    beta = _sigmoid(b_col)

    def conv(x, w_ref):
        acc = x * w_ref[DN_CONV - 1:DN_CONV, :]
        for i in range(DN_CONV - 1):
            acc = acc + _shift_rows(x, DN_CONV - 1 - i) * w_ref[i:i + 1, :]
        return acc

    xq, xk, xv = conv(pq, cq_ref), conv(pk, ck_ref), conv(pv, cv_ref)
    sq, sk, v = _silu(xq), _silu(xk), _silu(xv)
    rq = lax.rsqrt(_rowsum(sq * sq) + EPS)
    rk = lax.rsqrt(_rowsum(sk * sk) + EPS)
    return dict(g=g, beta=beta, pre=pre, neg_ea=neg_ea, xq=xq, xk=xk, xv=xv, rq=rq, rk=rk,
                qn=sq * rq, kn=sk * rk, v=v)


def _dn_masks():
    C = DN_C
    r, c = _iota((C, C), 0), _iota((C, C), 1)
    incl = r >= c
    strict = r > c
    blk16 = jnp.right_shift(r, 4) == jnp.right_shift(c, 4)
    blk32 = jnp.right_shift(r, 5) == jnp.right_shift(c, 5)
    return dict(incl=incl, strict=strict, upper=r <= c, blk16=blk16, blk32=blk32,
                tri=incl.astype(BF16), triT=(r <= c).astype(BF16), ones=jnp.ones((C, C), BF16),
                eye=(r == c).astype(F32), last=_iota((C, 1), 0) == C - 1)


def _tri_inverse(A, mk):
    P = -jnp.where(mk["blk16"], A, 0.0)
    X = mk["eye"] + P
    for _ in range(3):
        P = _dot3(P, P)
        X = X + _dot3(X, P)
    off1 = jnp.where(mk["blk32"] & (~mk["blk16"]), A, 0.0)
    X = X - _dot3(_dot3(X, off1), X)
    off2 = jnp.where(mk["blk32"], 0.0, A)
    X = X - _dot3(_dot3(X, off2), X)
    return X


def _dn_local(qc, kc, vc, gc, bc, mk):
    C = DN_C
    gm = jnp.broadcast_to(gc, gc.shape[:-1] + (C,))
    Gc = _dot2l(mk["tri"], gm)
    Gr = _dot2l(mk["ones"], jnp.where(mk["upper"], gm, 0.0))
    Dm = jnp.where(mk["incl"], jnp.exp(jnp.where(mk["incl"], Gc - Gr, 0.0)), 0.0)
    Gcol = jnp.max(Gc, axis=-1, keepdims=True)
    Gl = _colsum(jnp.where(mk["last"], Gcol, 0.0))
    eG = jnp.exp(Gcol)
    eT = jnp.exp(Gl - Gcol)
    gl = jnp.exp(Gl)
    kb = kc * bc
    vb = vc * bc
    KK = _dot(kb, kc, NT)
    Tinv = _tri_inverse(jnp.where(mk["strict"], KK * Dm, 0.0), mk)
    KBE = kb * eG
    QK = _dot(qc, kc, NT)
    return dict(Dm=Dm, eG=eG, eT=eT, gl=gl, kb=kb, vb=vb, KK=KK, Tinv=Tinv, KBE=KBE, U=_dot(Tinv, vb),
                W=_dot(Tinv, KBE), QK=QK, attn=QK * Dm, QD=qc * eG, KT=kc * eT)


def _dn_recur(f, S):
    vnew = f["U"] - _dot(f["W"], S)
    o = _dot(f["QD"], S) + _dot(f["attn"], vnew)
    return o, S * f["gl"] + _dot(f["KT"], vnew, TN), vnew


def _dn_bwd_chain(f, do, dS):
    dvnew = _dot(f["KT"], dS) + _dot(f["attn"], do, TN)
    return dvnew, dS * f["gl"] + _dot(f["QD"], do, TN) - _dot(f["W"], dvnew, TN)


def _dn_bwd_rest(qc, kc, vc, bc, f, S, vnew, do, dS, dvnew, mk):
    C = DN_C
    Dm, eG, eT, gl, kb, vb, KK, Tinv, KBE, QK = (
        f[n] for n in ("Dm", "eG", "eT", "gl", "kb", "vb", "KK", "Tinv", "KBE", "QK"))
    dKT = _dot(vnew, dS, NT)
    dgl = _colsum(_rowsum(dS * S))
    dQD = _dot(do, S, NT)
    dattn = _dot(do, vnew, NT)
    dU = dvnew
    dW = -_dot(dvnew, S, NT)
    dQK = dattn * Dm
    dD = dattn * QK
    dq = _dot(dQK, kc)
    dk = _dot(dQK, qc, TN)
    dTinv = _dot(dU, vb, NT) + _dot(dW, KBE, NT)
    dvb = _dot(Tinv, dU, TN)
    dKBE = _dot(Tinv, dW, TN)
    dA = -_dot3(_dot3(Tinv, dTinv, TN), Tinv, NT)
    dA = jnp.where(mk["strict"], dA, 0.0)
    dKK = dA * Dm
    dD = dD + dA * KK
    dkb = _dot(dKK, kc) + dKBE * eG
    dk = dk + _dot(dKK, kb, TN)
    deG = _rowsum(dKBE * kb)
    dk = dk + dkb * bc
    dbeta = _rowsum(dkb * kc) + _rowsum(dvb * vc)
    dv = dvb * bc
    dq = dq + dQD * eG
    deG = deG + _rowsum(dQD * qc)
    dk = dk + dKT * eT
    deT = _rowsum(dKT * kc)
    dGcol = deG * eG - deT * eT
    dGl = _colsum(deT * eT) + dgl * gl
    Y = dD * Dm
    ycol = jnp.max(_dot2r(Y, mk["ones"], TN), axis=-1, keepdims=True)
    dGcol = dGcol + _rowsum(Y) - ycol
    dGcol = dGcol + jnp.where(mk["last"], dGl, 0.0)
    dg = jnp.max(_dot2l(mk["triT"], jnp.broadcast_to(dGcol, dGcol.shape[:-1] + (C,))), axis=-1, keepdims=True)
    return dq, dk, dv, dg, dbeta


def _dn_core_fwd(proj, conv_w, alog, dtb, og, name):
    T = proj.shape[0]
    N = T // DN_C

    def body(pq_ref, pk_ref, pv_ref, pz_ref, pab_ref, cq_ref, ck_ref, cv_ref, alog_ref, dtb_ref, og_ref,
             out_ref, o_ref, st_ref, q_s, k_s, v_s, g_s, b_s, S_s):
        h = pl.program_id(0)
        p = _dn_prep(pq_ref[...], pk_ref[...], pv_ref[...], pab_ref[...], cq_ref, ck_ref, cv_ref,
                     alog_ref[...], dtb_ref[...], h)
        q_s[...] = p["qn"] * (DN_D ** -0.5)
        k_s[...] = p["kn"]
        v_s[...] = p["v"]
        g_s[...] = p["g"]
        b_s[...] = p["beta"]
        S_s[...] = jnp.zeros(S_s.shape, F32)
        mk = _dn_masks()

        nu = min(DN_UNROLL_FWD, N)

        def step(it, carry):
            rows = pl.ds(pl.multiple_of(it * (nu * DN_C), nu * DN_C), nu * DN_C)
            loc = _dn_local(*(r[rows, :].reshape(nu, DN_C, r.shape[1]) for r in (q_s, k_s, v_s, g_s, b_s)), mk)
            S = S_s[...]
            outs = []
            for u in range(nu):
                st_ref[0, it * nu + u] = S
                o, S, _ = _dn_recur({n: v[u] for n, v in loc.items()}, S)
                outs.append(o)
            o_ref[rows, :] = jnp.concatenate(outs, axis=0)
            S_s[...] = S
            return carry

        lax.fori_loop(0, N // nu, step, 0)
        o = o_ref[...]
        ro = lax.rsqrt(jnp.mean(o * o, axis=1, keepdims=True) + EPS)
        out_ref[...] = ((o * ro * og_ref[...]) * _silu(pz_ref[...])).astype(out_ref.dtype)

    col = lambda k: pl.BlockSpec((T, DN_D), lambda h: (0, k * DN_H + h))
    cw = lambda k: pl.BlockSpec((DN_CONV, DN_D), lambda h: (0, k * DN_H + h))
    small = pl.BlockSpec((1, LANES), lambda h: (0, 0))
    return pl.pallas_call(
        body, grid=(DN_H,),
        in_specs=[col(0), col(1), col(2), col(3), pl.BlockSpec((T, LANES), lambda h: (0, 4 * DN_H)),
                  cw(0), cw(1), cw(2), small, small, small],
        out_specs=[pl.BlockSpec((T, DN_D), lambda h: (0, h)), pl.BlockSpec((T, DN_D), lambda h: (0, h)),
                   pl.BlockSpec((1, N, DN_D, DN_D), lambda h: (h, 0, 0, 0))],
        out_shape=[jax.ShapeDtypeStruct((T, D), BF16), jax.ShapeDtypeStruct((T, D), F32),
                   jax.ShapeDtypeStruct((DN_H, N, DN_D, DN_D), F32)],
        scratch_shapes=[pltpu.VMEM((T, DN_D), F32)] * 3 + [pltpu.VMEM((T, 1), F32)] * 2 + [pltpu.VMEM((DN_D, DN_D), F32)],
        name=name, compiler_params=_params(("arbitrary",)),
    )(proj, proj, proj, proj, proj, conv_w, conv_w, conv_w, alog, dtb, og)


def _dn_core_bwd(proj, conv_w, alog, dtb, og, o, states, dout, name):
    T = proj.shape[0]
    N = T // DN_C

    def body(pq_ref, pk_ref, pv_ref, pz_ref, pab_ref, cq_ref, ck_ref, cv_ref, alog_ref, dtb_ref, og_ref,
             o_ref, st_ref, dout_ref,
             dpq_ref, dpk_ref, dpv_ref, dpz_ref, dpab_ref, dcq_ref, dck_ref, dcv_ref, dalog_ref, ddtb_ref, dog_ref,
             q_s, k_s, v_s, g_s, b_s, do_s, dS_s):
        h = pl.program_id(0)
        scale = DN_D ** -0.5

        def prep():
            return _dn_prep(pq_ref[...], pk_ref[...], pv_ref[...], pab_ref[...], cq_ref, ck_ref, cv_ref,
                            alog_ref[...], dtb_ref[...], h)

        p = prep()
        q_s[...] = p["qn"] * scale
        k_s[...] = p["kn"]
        v_s[...] = p["v"]
        g_s[...] = p["g"]
        b_s[...] = p["beta"]
        del p

        o = o_ref[...]
        z = pz_ref[...]
        dout = dout_ref[...]
        ogv = og_ref[...]
        ro = lax.rsqrt(jnp.mean(o * o, axis=1, keepdims=True) + EPS)
        on = o * ro
        dy = dout * _silu(z)
        dpz_ref[...] = (dout * (on * ogv) * _dsilu(z)).astype(dpz_ref.dtype)
        dyg = dy * ogv
        do_s[...] = ro * (dyg - on * jnp.mean(dyg * on, axis=1, keepdims=True))
        dog_h = _colsum(dy * on)

        dS_s[...] = jnp.zeros(dS_s.shape, F32)
        mk = _dn_masks()

        nu = min(DN_UNROLL_BWD, N)

        def step(it, carry):
            c0 = (N // nu - 1 - it) * nu
            rows = pl.ds(pl.multiple_of(c0 * DN_C, nu * DN_C), nu * DN_C)
            q, k, v, g, b, do = (r[rows, :].reshape(nu, DN_C, r.shape[1]) for r in (q_s, k_s, v_s, g_s, b_s, do_s))
            loc = _dn_local(q, k, v, g, b, mk)
            Ss = st_ref[0, pl.ds(c0, nu)]
            vnew = loc["U"] - _dot(loc["W"], Ss)
            dS = dS_s[...]
            dS_in, dvnew = [None] * nu, [None] * nu
            for u in reversed(range(nu)):
                dS_in[u] = dS
                dvnew[u], dS = _dn_bwd_chain({n: x[u] for n, x in loc.items()}, do[u], dS)
            dS_s[...] = dS
            grads = _dn_bwd_rest(q, k, v, b, loc, Ss, vnew, do, jnp.stack(dS_in), jnp.stack(dvnew), mk)
            for r, d in zip((q_s, k_s, v_s, g_s, b_s), grads):
                r[rows, :] = d.reshape(nu * DN_C, r.shape[1])
            return carry

        lax.fori_loop(0, N // nu, step, 0)

        p = prep()
        pq, pk, pv = pq_ref[...], pk_ref[...], pv_ref[...]
        dqn = q_s[...] * scale
        dkn = k_s[...]
        qn, kn = p["qn"], p["kn"]
        dsq = p["rq"] * (dqn - qn * _rowsum(dqn * qn))
        dsk = p["rk"] * (dkn - kn * _rowsum(dkn * kn))
        dxq = dsq * _dsilu(p["xq"])
        dxk = dsk * _dsilu(p["xk"])
        dxv = v_s[...] * _dsilu(p["xv"])

        def conv_bwd(dx, x, w_ref, dp_ref, dc_ref):
            acc = dx * w_ref[DN_CONV - 1:DN_CONV, :]
            dc_ref[DN_CONV - 1:DN_CONV, :] = _colsum(dx * x)
            for i in range(DN_CONV - 1):
                s = DN_CONV - 1 - i
                acc = acc + _shift_rows(dx, -s) * w_ref[i:i + 1, :]
                dc_ref[i:i + 1, :] = _colsum(dx * _shift_rows(x, s))
            dp_ref[...] = acc.astype(dp_ref.dtype)

        conv_bwd(dxq, pq, cq_ref, dpq_ref, dcq_ref)
        conv_bwd(dxk, pk, ck_ref, dpk_ref, dck_ref)
        conv_bwd(dxv, pv, cv_ref, dpv_ref, dcv_ref)

        dg = g_s[...]
        beta = p["beta"]
        da_raw = dg * p["neg_ea"] * _sigmoid(p["pre"])
        db_raw = b_s[...] * beta * (1.0 - beta)
        lane = _iota((T, LANES), 1)
        contrib = jnp.where(lane == h, da_raw, 0.0) + jnp.where(lane == DN_H + h, db_raw, 0.0)
        lane1 = _iota((1, LANES), 1)
        dalog_h = jnp.where(lane1 == h, _colsum(dg * p["g"]), 0.0)
        ddtb_h = jnp.where(lane1 == h, _colsum(da_raw), 0.0)

        @pl.when(h == 0)
        def _():
            dpab_ref[...] = contrib.astype(dpab_ref.dtype)
            dalog_ref[...] = dalog_h
            ddtb_ref[...] = ddtb_h
            dog_ref[...] = dog_h

        @pl.when(h > 0)
        def _():
            dpab_ref[...] += contrib.astype(dpab_ref.dtype)
            dalog_ref[...] += dalog_h
            ddtb_ref[...] += ddtb_h
            dog_ref[...] += dog_h

    col = lambda k: pl.BlockSpec((T, DN_D), lambda h: (0, k * DN_H + h))
    cw = lambda k: pl.BlockSpec((DN_CONV, DN_D), lambda h: (0, k * DN_H + h))
    small = pl.BlockSpec((1, LANES), lambda h: (0, 0))
    ab = pl.BlockSpec((T, LANES), lambda h: (0, 4 * DN_H))
    hcol = pl.BlockSpec((T, DN_D), lambda h: (0, h))
    outs = pl.pallas_call(
        body, grid=(DN_H,),
        in_specs=[col(0), col(1), col(2), col(3), ab, cw(0), cw(1), cw(2), small, small, small,
                  hcol, pl.BlockSpec((1, N, DN_D, DN_D), lambda h: (h, 0, 0, 0)), hcol],
        out_specs=[hcol, hcol, hcol, hcol, pl.BlockSpec((T, LANES), lambda h: (0, 0)),
                   pl.BlockSpec((DN_CONV, DN_D), lambda h: (0, h)), pl.BlockSpec((DN_CONV, DN_D), lambda h: (0, h)),
                   pl.BlockSpec((DN_CONV, DN_D), lambda h: (0, h)), small, small, small],
        out_shape=[jax.ShapeDtypeStruct((T, D), BF16)] * 4 + [jax.ShapeDtypeStruct((T, LANES), BF16)]
                  + [jax.ShapeDtypeStruct((DN_CONV, D), F32)] * 3 + [jax.ShapeDtypeStruct((1, LANES), F32)] * 3,
        scratch_shapes=[pltpu.VMEM((T, DN_D), F32)] * 3 + [pltpu.VMEM((T, 1), F32)] * 2
                       + [pltpu.VMEM((T, DN_D), F32), pltpu.VMEM((DN_D, DN_D), F32)],
        name=name, compiler_params=_params(("arbitrary",)),
    )(proj, proj, proj, proj, proj, conv_w, conv_w, conv_w, alog, dtb, og, o, states, dout)
    dpq, dpk, dpv, dpz, dpab, dcq, dck, dcv, dalog, ddtb, dog = outs
    dproj = jnp.concatenate([dpq, dpk, dpv, dpz, dpab], axis=1)
    dconv = jnp.concatenate([dcq, dck, dcv], axis=1)
    return dproj, dconv, dalog, ddtb, dog


def _sb_head_norm(x, g, h0):
    xx = x * x
    s0 = _rowsum(jnp.where(h0, xx, 0.0))
    s1 = _rowsum(jnp.where(h0, 0.0, xx))
    r = lax.rsqrt(jnp.where(h0, s0, s1) * (1.0 / SB_D) + EPS)
    return x * r, r


def _sb_fill(q_ref, k_ref, v_ref, qg_ref, kg_ref, qm_s, kn_s, vm_s):
    T = q_ref.shape[0]
    h0 = _iota((T, LANES), 1) < SB_D
    qh, rq = _sb_head_norm(q_ref[...], None, h0)
    kh, rk = _sb_head_norm(k_ref[...], None, h0)
    qs = qh * qg_ref[...] * (SB_D ** -0.5)
    qm_s[0] = jnp.where(h0, qs, 0.0).astype(BF16)
    qm_s[1] = jnp.where(h0, 0.0, qs).astype(BF16)
    kn_s[...] = (kh * kg_ref[...]).astype(BF16)
    v = v_ref[...]
    vm_s[0] = jnp.where(h0, v, 0.0).astype(BF16)
    vm_s[1] = jnp.where(h0, 0.0, v).astype(BF16)
    return h0, qh, rq, kh, rk


def _sb_core_fwd(qkv, qg2, kg2, name):
    T = qkv.shape[0]
    B = min(SB_TILE, T)
    NB = T // B
    NP = SB_H // 2

    def body(q_ref, k_ref, v_ref, qg_ref, kg_ref, o_ref, tot_ref, qm_s, kn_s, vm_s):
        _sb_fill(q_ref, k_ref, v_ref, qg_ref, kg_ref, qm_s, kn_s, vm_s)
        r, c = _iota((B, B), 0), _iota((B, B), 1)
        causal = c < r
        m_after = (r > c).astype(BF16)
        h0b = _iota((B, LANES), 1) < SB_D

        def tile(qb, kj, vj, R, acc, diag):
            z = _dot(qb, kj, NT)
            sp = _softplus(z)
            ls = z - sp
            lm = jnp.where(causal, -sp, 0.0) if diag else -sp
            cs = _dot2r(lm, m_after) + R
            a = jnp.exp(ls + cs)
            if diag:
                a = jnp.where(causal, a, 0.0)
            return R + _rowsum(lm), acc + _dot(a, vj)

        def qblock(i, carry):
            rows_i = pl.ds(pl.multiple_of(i * B, B), B)

            def both_heads(rows_j, st, diag):
                kj = kn_s[rows_j, :]
                out = []
                for hh in range(2):
                    out += list(tile(qm_s[hh, rows_i, :], kj, vm_s[hh, rows_j, :], st[2 * hh], st[2 * hh + 1], diag))
                return tuple(out)

            zero = (jnp.zeros((B, 1), F32), jnp.zeros((B, LANES), F32))
            st = both_heads(rows_i, zero + zero, True)
            st = lax.fori_loop(
                0, i, lambda s, st: both_heads(pl.ds(pl.multiple_of((i - 1 - s) * B, B), B), st, False), st)
            o_ref[rows_i, :] = (st[1] + st[3]).astype(o_ref.dtype)
            tot_ref[rows_i, :] = jnp.where(h0b, st[0], st[2])
            return carry

        lax.fori_loop(0, NB, qblock, 0)

    blk = lambda k: pl.BlockSpec((T, LANES), lambda p: (0, k * NP + p))
    small = pl.BlockSpec((1, LANES), lambda p: (0, 0))
    return pl.pallas_call(
        body, grid=(NP,), in_specs=[blk(0), blk(1), blk(2), small, small],
        out_specs=[pl.BlockSpec((T, LANES), lambda p: (0, p))] * 2,
        out_shape=[jax.ShapeDtypeStruct((T, D), BF16), jax.ShapeDtypeStruct((T, D), F32)],
        scratch_shapes=[pltpu.VMEM((2, T, LANES), BF16), pltpu.VMEM((T, LANES), BF16), pltpu.VMEM((2, T, LANES), BF16)],
        name=name, compiler_params=_params(("arbitrary",)),
    )(qkv, qkv, qkv, qg2, kg2)


def _sb_core_bwd(qkv, qg2, kg2, tot, dout, name):
    T = qkv.shape[0]
    B = min(SB_TILE, T)
    NB = T // B
    NP = SB_H // 2

    def body(q_ref, k_ref, v_ref, qg_ref, kg_ref, tot_ref, do_ref, dq_ref, dk_ref, dv_ref, dqg_ref, dkg_ref,
             qm_s, kn_s, vm_s, dom_s, dqn_s, dkn_s, dvv_s):
        p = pl.program_id(0)
        h0, qh, rq, kh, rk = _sb_fill(q_ref, k_ref, v_ref, qg_ref, kg_ref, qm_s, kn_s, vm_s)
        dov = do_ref[...]
        dom_s[0] = jnp.where(h0, dov, 0.0).astype(BF16)
        dom_s[1] = jnp.where(h0, 0.0, dov).astype(BF16)
        dkn_s[...] = jnp.zeros(dkn_s.shape, F32)
        dvv_s[...] = jnp.zeros(dvv_s.shape, F32)
        r, c = _iota((B, B), 0), _iota((B, B), 1)
        causal = c < r
        m_upto = (r <= c).astype(BF16)
        m_before = (r < c).astype(BF16)
        laneb = _iota((B, LANES), 1)

        def tile(qb, dob, tot_h, kj, vj, PL, P, dq, diag):
            z = _dot(qb, kj, NT)
            sp = _softplus(z)
            ls = z - sp
            lm = jnp.where(causal, -sp, 0.0) if diag else -sp
            cs = tot_h - PL - _dot2r(lm, m_upto)
            a = jnp.exp(ls + cs)
            if diag:
                a = jnp.where(causal, a, 0.0)
            e = _dot(dob, vj, NT) * a
            E = _dot2r(e, m_before) + P
            sig = jnp.exp(ls)
            dz = e * (1.0 - sig) - E * sig
            if diag:
                dz = jnp.where(causal, dz, 0.0)
            return (PL + _rowsum(lm), P + _rowsum(e), dq + _dot(dz, kj)), _dot(dz, qb, TN), _dot(a, dob, TN)

        def qblock(i, carry):
            rows_i = pl.ds(pl.multiple_of(i * B, B), B)
            totb = tot_ref[rows_i, :]
            tot_h = [_rowsum(jnp.where(laneb == hh * SB_D, totb, 0.0)) for hh in range(2)]

            def both_heads(rows_j, st, diag):
                kj = kn_s[rows_j, :]
                new, dk, dv = [], None, None
                for hh in range(2):
                    s3, dk_h, dv_h = tile(qm_s[hh, rows_i, :], dom_s[hh, rows_i, :], tot_h[hh], kj, vm_s[hh, rows_j, :],
                                          st[3 * hh], st[3 * hh + 1], st[3 * hh + 2], diag)
                    new += list(s3)
                    dk = dk_h if dk is None else dk + dk_h
                    dv = dv_h if dv is None else dv + dv_h
                dkn_s[rows_j, :] += dk
                dvv_s[rows_j, :] += dv
                return tuple(new)

            zero = (jnp.zeros((B, 1), F32), jnp.zeros((B, 1), F32), jnp.zeros((B, LANES), F32))
            st = lax.fori_loop(0, i, lambda j, st: both_heads(pl.ds(pl.multiple_of(j * B, B), B), st, False), zero + zero)
            st = both_heads(rows_i, st, True)
            dqn_s[rows_i, :] = jnp.where(laneb < SB_D, st[2], st[5]) * (SB_D ** -0.5)
            return carry

        lax.fori_loop(0, NB, qblock, 0)

        def norm_bwd(dn, xh, rr, g):
            dg = _colsum(dn * xh)
            t = dn * g
            tx = t * xh
            m0 = _rowsum(jnp.where(h0, tx, 0.0))
            m1 = _rowsum(jnp.where(h0, 0.0, tx))
            return rr * (t - xh * (jnp.where(h0, m0, m1) * (1.0 / SB_D))), dg

        dq, dqg = norm_bwd(dqn_s[...], qh, rq, qg_ref[...])
        dk, dkg = norm_bwd(dkn_s[...], kh, rk, kg_ref[...])
        dq_ref[...] = dq.astype(dq_ref.dtype)
        dk_ref[...] = dk.astype(dk_ref.dtype)
        dv_ref[...] = dvv_s[...].astype(dv_ref.dtype)

        @pl.when(p == 0)
        def _():
            dqg_ref[...] = dqg
            dkg_ref[...] = dkg

        @pl.when(p > 0)
        def _():
            dqg_ref[...] += dqg
            dkg_ref[...] += dkg

    blk = lambda k: pl.BlockSpec((T, LANES), lambda p: (0, k * NP + p))
    small = pl.BlockSpec((1, LANES), lambda p: (0, 0))
    own = pl.BlockSpec((T, LANES), lambda p: (0, p))
    dq, dk, dv, dqg, dkg = pl.pallas_call(
        body, grid=(NP,), in_specs=[blk(0), blk(1), blk(2), small, small, own, own],
        out_specs=[own, own, own, small, small],
        out_shape=[jax.ShapeDtypeStruct((T, D), BF16)] * 3 + [jax.ShapeDtypeStruct((1, LANES), F32)] * 2,
        scratch_shapes=[pltpu.VMEM((2, T, LANES), BF16), pltpu.VMEM((T, LANES), BF16), pltpu.VMEM((2, T, LANES), BF16),
                        pltpu.VMEM((2, T, LANES), BF16)] + [pltpu.VMEM((T, LANES), F32)] * 3,
        name=name, compiler_params=_params(("arbitrary",)),
    )(qkv, qkv, qkv, qg2, kg2, tot, dout)
    return jnp.concatenate([dq, dk, dv], axis=1), dqg, dkg


def _ada_fwd(c16, ada_w, ada_b_cols, name):
    L, _, cols = ada_w.shape

    def body(c_ref, w_ref, b_ref, o_ref):
        o_ref[0] = _dot(_silu(c_ref[...]), w_ref[0]) + b_ref[0]

    return pl.pallas_call(
        body, grid=(L,),
        in_specs=[pl.BlockSpec((16, D), lambda i: (0, 0)), pl.BlockSpec((1, D, cols), lambda i: (i, 0, 0)),
                  pl.BlockSpec((1, 1, cols), lambda i: (i, 0, 0))],
        out_specs=pl.BlockSpec((1, 16, cols), lambda i: (i, 0, 0)),
        out_shape=jax.ShapeDtypeStruct((L, 16, cols), F32), name=name, compiler_params=_params(("arbitrary",)),
    )(c16, ada_w, ada_b_cols)


def _ada_bwd(c16, dmod16, name):
    L, _, cols = dmod16.shape

    def body(c_ref, d_ref, o_ref):
        o_ref[0] = _dot(_silu(c_ref[...]), d_ref[0], TN)

    return pl.pallas_call(
        body, grid=(L,),
        in_specs=[pl.BlockSpec((16, D), lambda i: (0, 0)), pl.BlockSpec((1, 16, cols), lambda i: (i, 0, 0))],
        out_specs=pl.BlockSpec((1, D, cols), lambda i: (i, 0, 0)),
        out_shape=jax.ShapeDtypeStruct((L, D, cols), F32), name=name, compiler_params=_params(("arbitrary",)),
    )(c16, dmod16)


def _sum_sources(x, name, tile=512):
    n, R, C = x.shape
    tile = min(tile, R)

    def body(x_ref, o_ref):
        acc = x_ref[0].astype(F32)
        for k in range(1, n):
            acc = acc + x_ref[k].astype(F32)
        o_ref[...] = acc

    return pl.pallas_call(
        body, grid=(R // tile,), in_specs=[pl.BlockSpec((n, tile, C), lambda i: (0, i, 0))],
        out_specs=pl.BlockSpec((tile, C), lambda i: (i, 0)), out_shape=jax.ShapeDtypeStruct((R, C), F32),
        name=name, compiler_params=_params(("parallel",)),
    )(x)


def _adamw(w, g, m, v, name):
    shape = w.shape
    C = shape[-1]
    R = w.size // C
    w2, g2, m2, v2 = (a.reshape(R, C) for a in (w, g, m, v))
    tile = _rtile(R)
    c1 = 1.0 / (1.0 - ADAM_B1 ** ADAM_STEP)
    c2 = 1.0 / (1.0 - ADAM_B2 ** ADAM_STEP)

    def body(w_ref, g_ref, m_ref, v_ref, d_ref, nm_ref, nv_ref):
        gv = g_ref[...]
        nm = ADAM_B1 * m_ref[...] + (1.0 - ADAM_B1) * gv
        nv = ADAM_B2 * v_ref[...] + (1.0 - ADAM_B2) * (gv * gv)
        d_ref[...] = -ADAM_LR * ((nm * c1) / (jnp.sqrt(nv * c2) + ADAM_EPS) + ADAM_WD * w_ref[...])
        nm_ref[...] = nm
        nv_ref[...] = nv

    spec = pl.BlockSpec((tile, C), lambda i: (i, 0))
    outs = pl.pallas_call(
        body, grid=(R // tile,), in_specs=[spec] * 4, out_specs=[spec] * 3,
        out_shape=[jax.ShapeDtypeStruct((R, C), F32)] * 3, name=name, compiler_params=_params(("parallel",)),
    )(w2, g2, m2, v2)
    return tuple(o.reshape(shape) for o in outs)


def _all_gather(x, name, in_vmem):
    R, C = x.shape

    def body(x_ref, out_ref, send_sems, recv_sems, local_sem):
        mx, my, mc = lax.axis_index("x"), lax.axis_index("y"), lax.axis_index("c")
        me, sibling = (mx, my, mc), (mx, my, 1 - mc)
        chips = [(1 - mx, my), (mx, 1 - my), (1 - mx, 1 - my)]

        def slot(px, py, pc):
            return out_ref.at[4 * px + 2 * py + pc]

        def copy(k, block, to, src=None):
            return pltpu.make_async_remote_copy(
                src_ref=slot(*block) if src is None else src, dst_ref=slot(*block),
                send_sem=send_sems.at[k], recv_sem=recv_sems.at[k], device_id=to, device_id_type=MESH)

        mine = pltpu.make_async_copy(x_ref, slot(*me), local_sem)
        mine.start()
        first = [copy(0, me, sibling, src=x_ref)]
        first += [copy(1 + j, me, (*chip, mc), src=x_ref) for j, chip in enumerate(chips)]
        for cp in first:
            cp.start()
        passed = [copy(4 + j, (*chip, mc), sibling) for j, chip in enumerate(chips)]
        for j, chip in enumerate(chips):
            copy(1 + j, (*chip, mc), me).wait_recv()
            passed[j].start()
        copy(0, sibling, me).wait_recv()
        for j, chip in enumerate(chips):
            copy(4 + j, (*chip, 1 - mc), me).wait_recv()
        for cp in first + passed:
            cp.wait_send()
        mine.wait()

    space = pltpu.VMEM if in_vmem else pl.ANY
    return pl.pallas_call(
        body, out_shape=jax.ShapeDtypeStruct((NDEV, R, C), x.dtype),
        in_specs=[pl.BlockSpec(memory_space=space)], out_specs=pl.BlockSpec(memory_space=space),
        scratch_shapes=[pltpu.SemaphoreType.DMA((7,)), pltpu.SemaphoreType.DMA((7,)), pltpu.SemaphoreType.DMA],
        name=name, compiler_params=pltpu.CompilerParams(vmem_limit_bytes=VMEM_LIMIT),
    )(x)


def _all_to_all(x, name):
    n, R, C = x.shape

    def body(x_ref, out_ref, send_sems, recv_sems, local_sem):
        mx, my, mc = lax.axis_index("x"), lax.axis_index("y"), lax.axis_index("c")
        me = 4 * mx + 2 * my + mc
        mine = pltpu.make_async_copy(x_ref.at[me], out_ref.at[me], local_sem)
        mine.start()
        copies = []
        for k in range(1, NDEV):
            px, py, pc = mx ^ (k >> 2), my ^ ((k >> 1) & 1), mc ^ (k & 1)
            peer = 4 * px + 2 * py + pc
            copies.append(pltpu.make_async_remote_copy(
                src_ref=x_ref.at[peer], dst_ref=out_ref.at[me], send_sem=send_sems.at[k - 1],
                recv_sem=recv_sems.at[k - 1], device_id=(px, py, pc), device_id_type=MESH))
        for cp in copies:
            cp.start()
        for cp in copies:
            cp.wait_recv()
        for cp in copies:
            cp.wait_send()
        mine.wait()

    return pl.pallas_call(
        body, out_shape=jax.ShapeDtypeStruct((n, R, C), x.dtype),
        in_specs=[pl.BlockSpec(memory_space=pl.ANY)], out_specs=pl.BlockSpec(memory_space=pl.ANY),
        scratch_shapes=[pltpu.SemaphoreType.DMA((7,)), pltpu.SemaphoreType.DMA((7,)), pltpu.SemaphoreType.DMA],
        name=name, compiler_params=pltpu.CompilerParams(vmem_limit_bytes=VMEM_LIMIT),
    )(x)


def _local_step(x, target, mod, norm1_g, norm2_g, dn_w_in, dn_conv_w, dn_a_log, dn_dt_bias, dn_onorm_g, dn_w_out,
                sb_w_qkv, sb_q_norm_g, sb_k_norm_g, sb_w_out, ffn_w_in, ffn_w_out):
    row = lambda v: v.reshape(1, -1)
    pad128 = lambda v: jnp.pad(v.reshape(1, -1), ((0, 0), (0, LANES - v.size)))
    saved = []
    for i in range(DEPTH):
        j = i // 2
        m = [row(mod[i, k * D:(k + 1) * D]) for k in range(N_MOD)]
        sh1, sc1, gt1, sh2, sc2, gt2 = m
        g1, g2 = row(norm1_g[i]), row(norm2_g[i])
        h1 = _norm_mod(x, g1, sc1, sh1, f"norm1_{i}")
        if i % 2 == 0:
            proj = _mm(h1, dn_w_in[j], "nn", F32, f"dn_proj_{i}")
            alog, dtb, og = pad128(dn_a_log[j]), pad128(dn_dt_bias[j]), row(dn_onorm_g[j])
            om, o_pre, states = _dn_core_fwd(proj, dn_conv_w[j], alog, dtb, og, f"dn_core_{i}")
            y1 = _mm(om, dn_w_out[j], "nn", F32, f"dn_out_{i}")
            mix = (proj, alog, dtb, og, o_pre, states, om)
        else:
            qkv = _mm(h1, sb_w_qkv[j], "nn", F32, f"sb_qkv_{i}")
            qg2 = jnp.tile(row(sb_q_norm_g[j]), (1, 2))
            kg2 = jnp.tile(row(sb_k_norm_g[j]), (1, 2))
            om, tot = _sb_core_fwd(qkv, qg2, kg2, f"sb_core_{i}")
            y1 = _mm(om, sb_w_out[j], "nn", F32, f"sb_out_{i}")
            mix = (qkv, qg2, kg2, tot, om)
        x_mid = _gate_res(x, y1, gt1, f"res1_{i}")
        h2 = _norm_mod(x_mid, g2, sc2, sh2, f"norm2_{i}")
        u = _mm(h2, ffn_w_in[i], "nn", F32, f"ffn_in_{i}")
        a = _swiglu_act(u, f"ffn_act_{i}")
        y2 = _mm(a, ffn_w_out[i], "nn", F32, f"ffn_out_{i}")
        x_out = _gate_res(x_mid, y2, gt2, f"res2_{i}")
        saved.append((x, h1, mix, y1, x_mid, h2, u, a, y2, m, g1, g2))
        x = x_out

    dx, sq = _loss_head(x, target, "loss_head")

    gr = dict(mod=[None] * DEPTH, norm1_g=[None] * DEPTH, norm2_g=[None] * DEPTH, ffn_w_in=[None] * DEPTH,
              ffn_w_out=[None] * DEPTH, dn_w_in=[None] * 2, dn_w_out=[None] * 2, dn_conv_w=[None] * 2,
              dn_a_log=[None] * 2, dn_dt_bias=[None] * 2, dn_onorm_g=[None] * 2, sb_w_qkv=[None] * 2,
              sb_w_out=[None] * 2, sb_q_norm_g=[None] * 2, sb_k_norm_g=[None] * 2)
    for i in reversed(range(DEPTH)):
        j = i // 2
        x_in, h1, mix, y1, x_mid, h2, u, a, y2, m, g1, g2 = saved[i]
        sh1, sc1, gt1, sh2, sc2, gt2 = m
        dy2, dgt2 = _gate_res_bwd(dx, y2, gt2, f"res2_bwd_{i}")
        da = _mm(dy2, ffn_w_out[i], "nt", F32, f"ffn_out_bwd_{i}")
        gr["ffn_w_out"][i] = _mm(a, dy2, "tn", F32, f"ffn_out_wg_{i}")
        du = _swiglu_act_bwd(da, u, f"ffn_act_bwd_{i}")
        dh2 = _mm(du, ffn_w_in[i], "nt", F32, f"ffn_in_bwd_{i}")
        gr["ffn_w_in"][i] = _mm(h2, du, "tn", F32, f"ffn_in_wg_{i}")
        dx_mid, s2, dsh2 = _norm_mod_bwd(dh2, x_mid, dx, g2, sc2, f"norm2_bwd_{i}")
        dy1, dgt1 = _gate_res_bwd(dx_mid, y1, gt1, f"res1_bwd_{i}")
        if i % 2 == 0:
            proj, alog, dtb, og, o_pre, states, om = mix
            dom = _mm(dy1, dn_w_out[j], "nt", F32, f"dn_out_bwd_{i}")
            gr["dn_w_out"][j] = _mm(om, dy1, "tn", F32, f"dn_out_wg_{i}")
            dproj, dconv, dalog, ddtb, dog = _dn_core_bwd(proj, dn_conv_w[j], alog, dtb, og, o_pre, states, dom,
                                                          f"dn_core_bwd_{i}")
            dh1 = _mm(dproj, dn_w_in[j], "nt", F32, f"dn_proj_bwd_{i}")
            gr["dn_w_in"][j] = _mm(h1, dproj, "tn", F32, f"dn_proj_wg_{i}")[:, :DN_COLS]
            gr["dn_conv_w"][j] = dconv
            gr["dn_a_log"][j] = dalog[0, :DN_H]
            gr["dn_dt_bias"][j] = ddtb[0, :DN_H]
            gr["dn_onorm_g"][j] = dog[0]
        else:
            qkv, qg2, kg2, tot, om = mix
            dom = _mm(dy1, sb_w_out[j], "nt", F32, f"sb_out_bwd_{i}")
            gr["sb_w_out"][j] = _mm(om, dy1, "tn", F32, f"sb_out_wg_{i}")
            dqkv, dqg, dkg = _sb_core_bwd(qkv, qg2, kg2, tot, dom, f"sb_core_bwd_{i}")
            dh1 = _mm(dqkv, sb_w_qkv[j], "nt", F32, f"sb_qkv_bwd_{i}")
            gr["sb_w_qkv"][j] = _mm(h1, dqkv, "tn", F32, f"sb_qkv_wg_{i}")
            gr["sb_q_norm_g"][j] = dqg[0, :SB_D] + dqg[0, SB_D:]
            gr["sb_k_norm_g"][j] = dkg[0, :SB_D] + dkg[0, SB_D:]
        dx, s1, dsh1 = _norm_mod_bwd(dh1, x_in, dx_mid, g1, sc1, f"norm1_bwd_{i}")
        gr["mod"][i] = jnp.concatenate([dsh1, s1 * g1, dgt1, dsh2, s2 * g2, dgt2], axis=1)[0]
        gr["norm1_g"][i] = (s1 * (1.0 + sc1))[0]
        gr["norm2_g"][i] = (s2 * (1.0 + sc2))[0]
    return sq, dx, {k: jnp.stack(v) for k, v in gr.items()}


_BIG = (("dn_w_in", (2, 1024, 514)), ("dn_w_out", (2, 128, 1024)), ("sb_w_qkv", (2, 1024, 384)),
        ("sb_w_out", (2, 128, 1024)), ("ffn_w_in", (4, 1024, 704)), ("ffn_w_out", (4, 352, 1024)))
_BIG_COL_SHARDED = {"dn_w_in", "sb_w_qkv", "ffn_w_in"}
_BIG_ROWS_PAD = 6656


def _big_layout():
    out, r = [], 0
    for i, (n, s) in enumerate(_BIG):
        k = s[0] * s[1] * s[2] // 1024
        kp = -(-k // 16) * 16 if i < len(_BIG) - 1 else _BIG_ROWS_PAD - r
        out.append((n, s, k, kp, r))
        r += kp
    assert r == _BIG_ROWS_PAD
    return out


def _pack_big(parts, dtype):
    return jnp.concatenate([jnp.pad(parts[n].astype(dtype).reshape(k, 1024), ((0, kp - k), (0, 0)))
                            for n, _, k, kp, _ in _big_layout()], axis=0)


def _unpack_big(buf):
    return {n: buf[r:r + k].reshape(s) for n, s, k, _, r in _big_layout()}


def _full_from_gathered(g):
    out = {}
    for n, s, k, _, r in _big_layout():
        a = g[:, r:r + k].reshape((NDEV,) + s)
        if n in _BIG_COL_SHARDED:
            out[n] = jnp.transpose(a, (1, 2, 0, 3)).reshape(s[0], s[1], NDEV * s[2])
        else:
            out[n] = jnp.transpose(a, (1, 0, 2, 3)).reshape(s[0], NDEV * s[1], s[2])
    return out


def _shards_from_full(full):
    parts = []
    for n, s, k, kp, _ in _big_layout():
        a = full[n]
        if n in _BIG_COL_SHARDED:
            a = jnp.transpose(a.reshape(s[0], s[1], NDEV, s[2]), (2, 0, 1, 3))
        else:
            a = jnp.transpose(a.reshape(s[0], NDEV, s[1], s[2]), (1, 0, 2, 3))
        parts.append(jnp.pad(a.astype(BF16).reshape(NDEV, k, 1024), ((0, 0), (0, kp - k), (0, 0))))
    return jnp.concatenate(parts, axis=1)


def _device_index():
    return 4 * lax.axis_index("x") + 2 * lax.axis_index("y") + lax.axis_index("c")


def _gather_phase(w, c):
    me = _device_index()
    ada_cols = w["ada_w"].shape[-1]
    conv_cols = w["dn_conv_w"].shape[-1]
    blk = jnp.concatenate([c.reshape(1, D), w["dn_conv_w"].reshape(-1, D)], axis=0)
    g1 = _all_gather(blk, "gather_cond", True)
    c16 = jnp.pad(g1[:, 0, :], ((0, 8), (0, 0)))
    conv_full = jnp.transpose(g1[:, 1:, :].reshape(NDEV, 2, DN_CONV, conv_cols), (1, 2, 0, 3)).reshape(2, DN_CONV, -1)
    b_cols = lax.dynamic_slice_in_dim(w["ada_b"], me * ada_cols, ada_cols, axis=1).reshape(DEPTH, 1, ada_cols)
    mod_part = _ada_fwd(c16, w["ada_w"], b_cols, "ada_fwd")[:, :NDEV, :]
    g2 = _all_gather(mod_part.reshape(DEPTH * NDEV, ada_cols), "gather_mod", True)
    g2 = g2.reshape(NDEV, DEPTH, NDEV, ada_cols)
    mod = lax.dynamic_index_in_dim(g2, me, axis=2, keepdims=False)
    mod = jnp.transpose(mod, (1, 0, 2)).reshape(DEPTH, N_MOD * D)
    big = _all_gather(_pack_big(w, BF16), "gather_weights", False)
    full = _full_from_gathered(big)
    full["dn_w_in"] = jnp.pad(full["dn_w_in"], ((0, 0), (0, 0), (0, DN_COLS_PAD - DN_COLS)))
    return c16, conv_full, mod, full


def _reduce_phase(gr, c16):
    me = _device_index()
    ada_cols = N_MOD * D // NDEV
    conv_cols = 3 * DN_H * DN_D // NDEV
    recv = _all_to_all(_shards_from_full(gr), "exchange_grads")
    grads = _unpack_big(_sum_sources(recv, "sum_grads"))
    small = jnp.concatenate([gr["dn_a_log"].reshape(-1), gr["dn_dt_bias"].reshape(-1), gr["dn_onorm_g"].reshape(-1),
                             gr["sb_q_norm_g"].reshape(-1), gr["sb_k_norm_g"].reshape(-1)])
    small = jnp.pad(small, (0, D - small.size)).reshape(1, D)
    blk3 = jnp.concatenate([gr["mod"].reshape(-1, D), gr["norm1_g"], gr["norm2_g"], gr["dn_conv_w"].reshape(-1, D),
                            small], axis=0)
    blk3 = jnp.pad(blk3, ((0, 64 - blk3.shape[0]), (0, 0)))
    g3 = _all_gather(blk3, "gather_small_grads", True)
    tot = _sum_sources(g3, "sum_small_grads", tile=64)
    grads["ada_b"] = tot[:24].reshape(DEPTH, N_MOD * D)
    grads["norm1_g"] = tot[24:28]
    grads["norm2_g"] = tot[28:32]
    conv_g = tot[32:56].reshape(2, DN_CONV, NDEV * conv_cols)
    grads["dn_conv_w"] = lax.dynamic_slice_in_dim(conv_g, me * conv_cols, conv_cols, axis=2)
    sm = tot[56]
    grads["dn_a_log"] = sm[0:16].reshape(2, DN_H)
    grads["dn_dt_bias"] = sm[16:32].reshape(2, DN_H)
    grads["dn_onorm_g"] = sm[32:288].reshape(2, DN_D)
    grads["sb_q_norm_g"] = sm[288:416].reshape(2, SB_D)
    grads["sb_k_norm_g"] = sm[416:544].reshape(2, SB_D)
    dmod_all = g3[:, :24, :].reshape(NDEV, DEPTH, N_MOD * D)
    dmod_cols = lax.dynamic_slice_in_dim(dmod_all, me * ada_cols, ada_cols, axis=2)
    dmod16 = jnp.pad(jnp.transpose(dmod_cols, (1, 0, 2)), ((0, 0), (0, 8), (0, 0)))
    grads["ada_w"] = _ada_bwd(c16, dmod16, "ada_bwd")
    return grads


def kernel(x, c, ada_w, ada_b, norm1_g, norm2_g, dn_w_in, dn_conv_w, dn_a_log, dn_dt_bias, dn_onorm_g, dn_w_out, sb_w_qkv, sb_q_norm_g, sb_k_norm_g, sb_w_out, ffn_w_in, ffn_w_out, loss_target, m_ada_w, m_ada_b, m_norm1_g, m_norm2_g, m_dn_w_in, m_dn_conv_w, m_dn_a_log, m_dn_dt_bias, m_dn_onorm_g, m_dn_w_out, m_sb_w_qkv, m_sb_q_norm_g, m_sb_k_norm_g, m_sb_w_out, m_ffn_w_in, m_ffn_w_out, v_ada_w, v_ada_b, v_norm1_g, v_norm2_g, v_dn_w_in, v_dn_conv_w, v_dn_a_log, v_dn_dt_bias, v_dn_onorm_g, v_dn_w_out, v_sb_w_qkv, v_sb_q_norm_g, v_sb_k_norm_g, v_sb_w_out, v_ffn_w_in, v_ffn_w_out):
    w = dict(ada_w=ada_w, ada_b=ada_b, norm1_g=norm1_g, norm2_g=norm2_g, dn_w_in=dn_w_in, dn_conv_w=dn_conv_w,
             dn_a_log=dn_a_log, dn_dt_bias=dn_dt_bias, dn_onorm_g=dn_onorm_g, dn_w_out=dn_w_out, sb_w_qkv=sb_w_qkv,
             sb_q_norm_g=sb_q_norm_g, sb_k_norm_g=sb_k_norm_g, sb_w_out=sb_w_out, ffn_w_in=ffn_w_in, ffn_w_out=ffn_w_out)
    mom = dict(ada_w=m_ada_w, ada_b=m_ada_b, norm1_g=m_norm1_g, norm2_g=m_norm2_g, dn_w_in=m_dn_w_in,
               dn_conv_w=m_dn_conv_w, dn_a_log=m_dn_a_log, dn_dt_bias=m_dn_dt_bias, dn_onorm_g=m_dn_onorm_g,
               dn_w_out=m_dn_w_out, sb_w_qkv=m_sb_w_qkv, sb_q_norm_g=m_sb_q_norm_g, sb_k_norm_g=m_sb_k_norm_g,
               sb_w_out=m_sb_w_out, ffn_w_in=m_ffn_w_in, ffn_w_out=m_ffn_w_out)
    var = dict(ada_w=v_ada_w, ada_b=v_ada_b, norm1_g=v_norm1_g, norm2_g=v_norm2_g, dn_w_in=v_dn_w_in,
               dn_conv_w=v_dn_conv_w, dn_a_log=v_dn_a_log, dn_dt_bias=v_dn_dt_bias, dn_onorm_g=v_dn_onorm_g,
               dn_w_out=v_dn_w_out, sb_w_qkv=v_sb_w_qkv, sb_q_norm_g=v_sb_q_norm_g, sb_k_norm_g=v_sb_k_norm_g,
               sb_w_out=v_sb_w_out, ffn_w_in=v_ffn_w_in, ffn_w_out=v_ffn_w_out)
    names = list(w)
    c16, conv_full, mod, full = _gather_phase(w, c)
    sq, grad_x, gr = _local_step(
        x[0], loss_target[0], mod, norm1_g, norm2_g, full["dn_w_in"], conv_full, dn_a_log, dn_dt_bias, dn_onorm_g,
        full["dn_w_out"], full["sb_w_qkv"], sb_q_norm_g, sb_k_norm_g, full["sb_w_out"], full["ffn_w_in"],
        full["ffn_w_out"])
    loss = lax.psum(sq[0, 0] * (0.5 / D), ("x", "y", "c"))
    grads = _reduce_phase(gr, c16)
    delta, new_m, new_v = {}, {}, {}
    for n in names:
        delta[n], new_m[n], new_v[n] = _adamw(w[n], grads[n], mom[n], var[n], f"adamw_{n}")
    return (loss, grad_x[None], *[grads[n] for n in names], *[delta[n] for n in names],
            *[new_m[n] for n in names], *[new_v[n] for n in names])
```

```python
import functools

import jax
import jax.numpy as jnp
from jax import lax
from jax.experimental import pallas as pl
from jax.experimental.pallas import tpu as pltpu

F32, BF16 = jnp.float32, jnp.bfloat16

D = 1024
DEPTH = 4
N_MOD = 6
DN_H, DN_D, DN_C, DN_CONV = 8, 128, 64, 4
DN_UNROLL_FWD, DN_UNROLL_BWD = 16, 4
DN_COLS = 4 * DN_H * DN_D + 2 * DN_H
DN_COLS_PAD = 33 * 128
SB_H, SB_D = 16, 64
SB_TILE = 256
DFF = 2816
EPS = 1e-6
NDEV = 8
LANES = 128
VMEM_LIMIT = 48 * 1024 * 1024

ADAM_LR, ADAM_B1, ADAM_B2, ADAM_EPS, ADAM_WD, ADAM_STEP = 0.001, 0.9, 0.999, 1e-08, 0.01, 10

NN = ((1,), (0,))
NT = ((1,), (1,))
TN = ((0,), (0,))
MESH = pl.DeviceIdType.MESH


def _dot(a, b, dims=NN):
    a, b = a.astype(BF16), b.astype(BF16)
    if a.ndim == 2 and b.ndim == 2:
        return lax.dot_general(a, b, (dims, ((), ())), preferred_element_type=F32)
    n = a.shape[0] if a.ndim == 3 else b.shape[0]
    if a.ndim == 2:
        a = jnp.broadcast_to(a, (n,) + a.shape)
    if b.ndim == 2:
        b = jnp.broadcast_to(b, (n,) + b.shape)
    (ca,), (cb,) = dims
    return lax.dot_general(a, b, (((ca + 1,), (cb + 1,)), ((0,), (0,))), preferred_element_type=F32)


def _split(a):
    hi = a.astype(BF16)
    lo = (a - hi.astype(F32)).astype(BF16)
    return hi, lo


def _dot3(a, b, dims=NN):
    ah, al = _split(a)
    bh, bl = _split(b)
    return _dot(ah, bh, dims) + (_dot(al, bh, dims) + _dot(ah, bl, dims))


def _dot2r(a, m, dims=NN):
    ah, al = _split(a)
    return _dot(ah, m, dims) + _dot(al, m, dims)


def _dot2l(m, b, dims=NN):
    bh, bl = _split(b)
    return _dot(m, bh, dims) + _dot(m, bl, dims)


def _sigmoid(x):
    return 1.0 / (1.0 + jnp.exp(-x))


def _silu(x):
    return x * _sigmoid(x)


def _dsilu(x):
    s = _sigmoid(x)
    return s * (1.0 + x * (1.0 - s))


def _softplus(x):
    return jnp.maximum(x, 0.0) + jnp.log(1.0 + jnp.exp(-jnp.abs(x)))


def _iota(shape, dim):
    return lax.broadcasted_iota(jnp.int32, shape, dim)


def _rowsum(x):
    return jnp.sum(x, axis=-1, keepdims=True)


def _colsum(x):
    return jnp.sum(x, axis=-2, keepdims=True)


def _tile(n, pref):
    if n <= pref:
        return n
    best = None
    for t in range(LANES, pref + 1, LANES):
        if n % t == 0:
            best = t
    assert best is not None, (n, pref)
    return best


def _rtile(r, pref=512):
    best = None
    for t in range(8, min(r, pref) + 1, 8):
        if r % t == 0:
            best = t
    return best if best is not None else r


def _params(sem):
    return pltpu.CompilerParams(dimension_semantics=sem, vmem_limit_bytes=VMEM_LIMIT)


MM_TILE = 1408


def _mm(a, b, mode, out_dtype, name):
    if mode == "nn":
        (M, K), (K2, N) = a.shape, b.shape
        dims = NN
    elif mode == "nt":
        (M, K), (N, K2) = a.shape, b.shape
        dims = NT
    else:
        (K, M), (K2, N) = a.shape, b.shape
        dims = TN
    assert K == K2, (a.shape, b.shape, mode)
    tm, tn, tk = _tile(M, MM_TILE), _tile(N, MM_TILE), _tile(K, MM_TILE)
    nk = K // tk

    def body_single(a_ref, b_ref, o_ref):
        o_ref[...] = _dot(a_ref[...], b_ref[...], dims).astype(o_ref.dtype)

    def body_acc(a_ref, b_ref, o_ref, acc_ref):
        k = pl.program_id(2)
        p = _dot(a_ref[...], b_ref[...], dims)

        @pl.when(k == 0)
        def _():
            acc_ref[...] = p

        @pl.when(k > 0)
        def _():
            acc_ref[...] += p

        @pl.when(k == nk - 1)
        def _():
            o_ref[...] = acc_ref[...].astype(o_ref.dtype)

    if mode == "nn":
        a_spec = pl.BlockSpec((tm, tk), lambda i, j, k: (i, k))
        b_spec = pl.BlockSpec((tk, tn), lambda i, j, k: (k, j))
    elif mode == "nt":
        a_spec = pl.BlockSpec((tm, tk), lambda i, j, k: (i, k))
        b_spec = pl.BlockSpec((tn, tk), lambda i, j, k: (j, k))
    else:
        a_spec = pl.BlockSpec((tk, tm), lambda i, j, k: (k, i))
        b_spec = pl.BlockSpec((tk, tn), lambda i, j, k: (k, j))
    return pl.pallas_call(
        body_single if nk == 1 else body_acc, grid=(M // tm, N // tn, nk), in_specs=[a_spec, b_spec],
        out_specs=pl.BlockSpec((tm, tn), lambda i, j, k: (i, j)),
        out_shape=jax.ShapeDtypeStruct((M, N), out_dtype),
        scratch_shapes=[] if nk == 1 else [pltpu.VMEM((tm, tn), F32)], name=name,
        compiler_params=_params(("parallel", "parallel", "arbitrary")),
    )(a, b)


def _rowwise(fn, name, rows, bcasts, out_rows, out_reds=(), tile=256):
    T = rows[0].shape[0]
    tile = min(tile, T)
    nr, nb, no = len(rows), len(bcasts), len(out_rows)

    def body(*refs):
        rv = [r[...] for r in refs[:nr]]
        bv = [r[...] for r in refs[nr:nr + nb]]
        outs, reds = fn(rv, bv)
        for r, o in zip(refs[nr + nb:nr + nb + no], outs):
            r[...] = o.astype(r.dtype)
        red_refs = refs[nr + nb + no:]
        if red_refs:
            @pl.when(pl.program_id(0) == 0)
            def _():
                for r in red_refs:
                    r[...] = jnp.zeros(r.shape, F32)

            for r, v in zip(red_refs, reds):
                r[...] += v

    in_specs = [pl.BlockSpec((tile, a.shape[1]), lambda i: (i, 0)) for a in rows]
    in_specs += [pl.BlockSpec(b.shape, lambda i: (0, 0)) for b in bcasts]
    out_specs = [pl.BlockSpec((tile, c), lambda i: (i, 0)) for c, _ in out_rows]
    out_specs += [pl.BlockSpec(s, lambda i: (0, 0)) for s in out_reds]
    out_shape = [jax.ShapeDtypeStruct((T, c), dt) for c, dt in out_rows]
    out_shape += [jax.ShapeDtypeStruct(s, F32) for s in out_reds]
    return pl.pallas_call(
        body, grid=(T // tile,), in_specs=in_specs, out_specs=out_specs, out_shape=out_shape, name=name,
        compiler_params=_params(("arbitrary",)),
    )(*rows, *bcasts)


def _norm_mod(x, g, sc, sh, name):
    def fn(rv, bv):
        (xv,), (gv, scv, shv) = rv, bv
        r = lax.rsqrt(jnp.mean(xv * xv, axis=1, keepdims=True) + EPS)
        return [(xv * r * gv) * (1.0 + scv) + shv], []
    return _rowwise(fn, name, [x], [g, sc, sh], [(D, BF16)])[0]


def _norm_mod_bwd(dh, x, dres, g, sc, name):
    def fn(rv, bv):
        (dhv, xv, drv), (gv, scv) = rv, bv
        r = lax.rsqrt(jnp.mean(xv * xv, axis=1, keepdims=True) + EPS)
        xhat = xv * r
        dxhat = dhv * (gv * (1.0 + scv))
        dx = r * (dxhat - xhat * jnp.mean(dxhat * xhat, axis=1, keepdims=True)) + drv
        return [dx], [_colsum(dhv * xhat), _colsum(dhv)]
    return _rowwise(fn, name, [dh, x, dres], [g, sc], [(D, F32)], [(1, D), (1, D)])


def _gate_res(x, y, gt, name):
    def fn(rv, bv):
        return [rv[0] + bv[0] * rv[1]], []
    return _rowwise(fn, name, [x, y], [gt], [(D, F32)])[0]


def _gate_res_bwd(dxn, y, gt, name):
    def fn(rv, bv):
        return [rv[0] * bv[0]], [_colsum(rv[0] * rv[1])]
    return _rowwise(fn, name, [dxn, y], [gt], [(D, BF16)], [(1, D)])


def _swiglu_act(u, name):
    def fn(rv, bv):
        uv = rv[0]
        return [_silu(uv[:, :DFF]) * uv[:, DFF:]], []
    return _rowwise(fn, name, [u], [], [(DFF, BF16)])[0]


def _swiglu_act_bwd(da, u, name):
    def fn(rv, bv):
        dav, uv = rv
        gate, up = uv[:, :DFF], uv[:, DFF:]
        return [jnp.concatenate([dav * up * _dsilu(gate), dav * _silu(gate)], axis=1)], []
    return _rowwise(fn, name, [da, u], [], [(2 * DFF, BF16)])[0]


def _loss_head(y, target, name):
    def fn(rv, bv):
        err = rv[0] - rv[1]
        return [err * (1.0 / D)], [_colsum(_rowsum(err * err))]
    return _rowwise(fn, name, [y, target], [], [(D, F32)], [(1, 1)])


def _shift_rows(x, s):
    if s == 0:
        return x
    T = x.shape[0]
    r = pltpu.roll(x, s % T, axis=0)
    t = _iota(x.shape, 0)
    keep = (t >= s) if s > 0 else (t < T + s)
    return jnp.where(keep, r, 0.0)


def _dn_prep(pq, pk, pv, pab, cq_ref, ck_ref, cv_ref, alog, dtb, h):
    lane = _iota(pab.shape, 1)
    a_col = _rowsum(jnp.where(lane == h, pab, 0.0))
    b_col = _rowsum(jnp.where(lane == DN_H + h, pab, 0.0))
    lane1 = _iota(alog.shape, 1)
    alog_h = _rowsum(jnp.where(lane1 == h, alog, 0.0))
    dtb_h = _rowsum(jnp.where(lane1 == h, dtb, 0.0))
    pre = a_col + dtb_h
    neg_ea = -jnp.exp(alog_h)
    g = neg_ea * _softplus(pre)
    beta = _sigmoid(b_col)

    def conv(x, w_ref):
        acc = x * w_ref[DN_CONV - 1:DN_CONV, :]
        for i in range(DN_CONV - 1):
            acc = acc + _shift_rows(x, DN_CONV - 1 - i) * w_ref[i:i + 1, :]
        return acc

    xq, xk, xv = conv(pq, cq_ref), conv(pk, ck_ref), conv(pv, cv_ref)
    sq, sk, v = _silu(xq), _silu(xk), _silu(xv)
    rq = lax.rsqrt(_rowsum(sq * sq) + EPS)
    rk = lax.rsqrt(_rowsum(sk * sk) + EPS)
    return dict(g=g, beta=beta, pre=pre, neg_ea=neg_ea, xq=xq, xk=xk, xv=xv, rq=rq, rk=rk,
                qn=sq * rq, kn=sk * rk, v=v)


def _dn_masks():
    C = DN_C
    r, c = _iota((C, C), 0), _iota((C, C), 1)
    incl = r >= c
    strict = r > c
    blk16 = jnp.right_shift(r, 4) == jnp.right_shift(c, 4)
    blk32 = jnp.right_shift(r, 5) == jnp.right_shift(c, 5)
    return dict(incl=incl, strict=strict, upper=r <= c, blk16=blk16, blk32=blk32,
                tri=incl.astype(BF16), triT=(r <= c).astype(BF16), ones=jnp.ones((C, C), BF16),
                eye=(r == c).astype(F32), last=_iota((C, 1), 0) == C - 1)


def _tri_inverse(A, mk):
    P = -jnp.where(mk["blk16"], A, 0.0)
    X = mk["eye"] + P
    for _ in range(3):
        P = _dot3(P, P)
        X = X + _dot3(X, P)
    off1 = jnp.where(mk["blk32"] & (~mk["blk16"]), A, 0.0)
    X = X - _dot3(_dot3(X, off1), X)
    off2 = jnp.where(mk["blk32"], 0.0, A)
    X = X - _dot3(_dot3(X, off2), X)
    return X


def _dn_local(qc, kc, vc, gc, bc, mk):
    C = DN_C
    gm = jnp.broadcast_to(gc, gc.shape[:-1] + (C,))
    Gc = _dot2l(mk["tri"], gm)
    Gr = _dot2l(mk["ones"], jnp.where(mk["upper"], gm, 0.0))
    Dm = jnp.where(mk["incl"], jnp.exp(jnp.where(mk["incl"], Gc - Gr, 0.0)), 0.0)
    Gcol = jnp.max(Gc, axis=-1, keepdims=True)
    Gl = _colsum(jnp.where(mk["last"], Gcol, 0.0))
    eG = jnp.exp(Gcol)
    eT = jnp.exp(Gl - Gcol)
    gl = jnp.exp(Gl)
    kb = kc * bc
    vb = vc * bc
    KK = _dot(kb, kc, NT)
    Tinv = _tri_inverse(jnp.where(mk["strict"], KK * Dm, 0.0), mk)
    KBE = kb * eG
    QK = _dot(qc, kc, NT)
    return dict(Dm=Dm, eG=eG, eT=eT, gl=gl, kb=kb, vb=vb, KK=KK, Tinv=Tinv, KBE=KBE, U=_dot(Tinv, vb),
                W=_dot(Tinv, KBE), QK=QK, attn=QK * Dm, QD=qc * eG, KT=kc * eT)


def _dn_recur(f, S):
    vnew = f["U"] - _dot(f["W"], S)
    o = _dot(f["QD"], S) + _dot(f["attn"], vnew)
    return o, S * f["gl"] + _dot(f["KT"], vnew, TN), vnew


def _dn_bwd_chain(f, do, dS):
    dvnew = _dot(f["KT"], dS) + _dot(f["attn"], do, TN)
    return dvnew, dS * f["gl"] + _dot(f["QD"], do, TN) - _dot(f["W"], dvnew, TN)


def _dn_bwd_rest(qc, kc, vc, bc, f, S, vnew, do, dS, dvnew, mk):
    C = DN_C
    Dm, eG, eT, gl, kb, vb, KK, Tinv, KBE, QK = (
        f[n] for n in ("Dm", "eG", "eT", "gl", "kb", "vb", "KK", "Tinv", "KBE", "QK"))
    dKT = _dot(vnew, dS, NT)
    dgl = _colsum(_rowsum(dS * S))
    dQD = _dot(do, S, NT)
    dattn = _dot(do, vnew, NT)
    dU = dvnew
    dW = -_dot(dvnew, S, NT)
    dQK = dattn * Dm
    dD = dattn * QK
    dq = _dot(dQK, kc)
    dk = _dot(dQK, qc, TN)
    dTinv = _dot(dU, vb, NT) + _dot(dW, KBE, NT)
    dvb = _dot(Tinv, dU, TN)
    dKBE = _dot(Tinv, dW, TN)
    dA = -_dot3(_dot3(Tinv, dTinv, TN), Tinv, NT)
    dA = jnp.where(mk["strict"], dA, 0.0)
    dKK = dA * Dm
    dD = dD + dA * KK
    dkb = _dot(dKK, kc) + dKBE * eG
    dk = dk + _dot(dKK, kb, TN)
    deG = _rowsum(dKBE * kb)
    dk = dk + dkb * bc
    dbeta = _rowsum(dkb * kc) + _rowsum(dvb * vc)
    dv = dvb * bc
    dq = dq + dQD * eG
    deG = deG + _rowsum(dQD * qc)
    dk = dk + dKT * eT
    deT = _rowsum(dKT * kc)
    dGcol = deG * eG - deT * eT
    dGl = _colsum(deT * eT) + dgl * gl
    Y = dD * Dm
    ycol = jnp.max(_dot2r(Y, mk["ones"], TN), axis=-1, keepdims=True)
    dGcol = dGcol + _rowsum(Y) - ycol
    dGcol = dGcol + jnp.where(mk["last"], dGl, 0.0)
    dg = jnp.max(_dot2l(mk["triT"], jnp.broadcast_to(dGcol, dGcol.shape[:-1] + (C,))), axis=-1, keepdims=True)
    return dq, dk, dv, dg, dbeta


def _dn_core_fwd(proj, conv_w, alog, dtb, og, name, comm=None):
    T = proj.shape[0]
    N = T // DN_C

    def body(pq_ref, pk_ref, pv_ref, pz_ref, pab_ref, cq_ref, ck_ref, cv_ref, alog_ref, dtb_ref, og_ref,
             out_ref, o_ref, st_ref, q_s, k_s, v_s, g_s, b_s, S_s):
        h = pl.program_id(0)
        p = _dn_prep(pq_ref[...], pk_ref[...], pv_ref[...], pab_ref[...], cq_ref, ck_ref, cv_ref,
                     alog_ref[...], dtb_ref[...], h)
        q_s[...] = p["qn"] * (DN_D ** -0.5)
        k_s[...] = p["kn"]
        v_s[...] = p["v"]
        g_s[...] = p["g"]
        b_s[...] = p["beta"]
        S_s[...] = jnp.zeros(S_s.shape, F32)
        mk = _dn_masks()

        nu = min(DN_UNROLL_FWD, N)

        def step(it, carry):
            rows = pl.ds(pl.multiple_of(it * (nu * DN_C), nu * DN_C), nu * DN_C)
            loc = _dn_local(*(r[rows, :].reshape(nu, DN_C, r.shape[1]) for r in (q_s, k_s, v_s, g_s, b_s)), mk)
            S = S_s[...]
            outs = []
            for u in range(nu):
                st_ref[0, it * nu + u] = S
                o, S, _ = _dn_recur({n: v[u] for n, v in loc.items()}, S)
                outs.append(o)
            o_ref[rows, :] = jnp.concatenate(outs, axis=0)
            S_s[...] = S
            return carry

        lax.fori_loop(0, N // nu, step, 0)
        o = o_ref[...]
        ro = lax.rsqrt(jnp.mean(o * o, axis=1, keepdims=True) + EPS)
        out_ref[...] = ((o * ro * og_ref[...]) * _silu(pz_ref[...])).astype(out_ref.dtype)

    col = lambda k: pl.BlockSpec((T, DN_D), lambda h: (0, k * DN_H + h))
    cw = lambda k: pl.BlockSpec((DN_CONV, DN_D), lambda h: (0, k * DN_H + h))
    small = pl.BlockSpec((1, LANES), lambda h: (0, 0))
    return _grid_call(
        body, comm, grid=(DN_H,),
        in_specs=[col(0), col(1), col(2), col(3), pl.BlockSpec((T, LANES), lambda h: (0, 4 * DN_H)),
                  cw(0), cw(1), cw(2), small, small, small],
        out_specs=[pl.BlockSpec((T, DN_D), lambda h: (0, h)), pl.BlockSpec((T, DN_D), lambda h: (0, h)),
                   pl.BlockSpec((1, N, DN_D, DN_D), lambda h: (h, 0, 0, 0))],
        out_shape=[jax.ShapeDtypeStruct((T, D), BF16), jax.ShapeDtypeStruct((T, D), F32),
                   jax.ShapeDtypeStruct((DN_H, N, DN_D, DN_D), F32)],
        scratch_shapes=[pltpu.VMEM((T, DN_D), F32)] * 3 + [pltpu.VMEM((T, 1), F32)] * 2 + [pltpu.VMEM((DN_D, DN_D), F32)],
        name=name, args=(proj, proj, proj, proj, proj, conv_w, conv_w, conv_w, alog, dtb, og))


def _dn_core_bwd(proj, conv_w, alog, dtb, og, o, states, dout, name, comm=None):
    T = proj.shape[0]
    N = T // DN_C

    def body(pq_ref, pk_ref, pv_ref, pz_ref, pab_ref, cq_ref, ck_ref, cv_ref, alog_ref, dtb_ref, og_ref,
             o_ref, st_ref, dout_ref,
             dpq_ref, dpk_ref, dpv_ref, dpz_ref, dpab_ref, dcq_ref, dck_ref, dcv_ref, dalog_ref, ddtb_ref, dog_ref,
             q_s, k_s, v_s, g_s, b_s, do_s, dS_s):
        h = pl.program_id(0)
        scale = DN_D ** -0.5

        def prep():
            return _dn_prep(pq_ref[...], pk_ref[...], pv_ref[...], pab_ref[...], cq_ref, ck_ref, cv_ref,
                            alog_ref[...], dtb_ref[...], h)

        p = prep()
        q_s[...] = p["qn"] * scale
        k_s[...] = p["kn"]
        v_s[...] = p["v"]
        g_s[...] = p["g"]
        b_s[...] = p["beta"]
        del p

        o = o_ref[...]
        z = pz_ref[...]
        dout = dout_ref[...]
        ogv = og_ref[...]
        ro = lax.rsqrt(jnp.mean(o * o, axis=1, keepdims=True) + EPS)
        on = o * ro
        dy = dout * _silu(z)
        dpz_ref[...] = (dout * (on * ogv) * _dsilu(z)).astype(dpz_ref.dtype)
        dyg = dy * ogv
        do_s[...] = ro * (dyg - on * jnp.mean(dyg * on, axis=1, keepdims=True))
        dog_h = _colsum(dy * on)

        dS_s[...] = jnp.zeros(dS_s.shape, F32)
        mk = _dn_masks()

        nu = min(DN_UNROLL_BWD, N)

        def step(it, carry):
            c0 = (N // nu - 1 - it) * nu
            rows = pl.ds(pl.multiple_of(c0 * DN_C, nu * DN_C), nu * DN_C)
            q, k, v, g, b, do = (r[rows, :].reshape(nu, DN_C, r.shape[1]) for r in (q_s, k_s, v_s, g_s, b_s, do_s))
            loc = _dn_local(q, k, v, g, b, mk)
            Ss = st_ref[0, pl.ds(c0, nu)]
            vnew = loc["U"] - _dot(loc["W"], Ss)
            dS = dS_s[...]
            dS_in, dvnew = [None] * nu, [None] * nu
            for u in reversed(range(nu)):
                dS_in[u] = dS
                dvnew[u], dS = _dn_bwd_chain({n: x[u] for n, x in loc.items()}, do[u], dS)
            dS_s[...] = dS
            grads = _dn_bwd_rest(q, k, v, b, loc, Ss, vnew, do, jnp.stack(dS_in), jnp.stack(dvnew), mk)
            for r, d in zip((q_s, k_s, v_s, g_s, b_s), grads):
                r[rows, :] = d.reshape(nu * DN_C, r.shape[1])
            return carry

        lax.fori_loop(0, N // nu, step, 0)

        p = prep()
        pq, pk, pv = pq_ref[...], pk_ref[...], pv_ref[...]
        dqn = q_s[...] * scale
        dkn = k_s[...]
        qn, kn = p["qn"], p["kn"]
        dsq = p["rq"] * (dqn - qn * _rowsum(dqn * qn))
        dsk = p["rk"] * (dkn - kn * _rowsum(dkn * kn))
        dxq = dsq * _dsilu(p["xq"])
        dxk = dsk * _dsilu(p["xk"])
        dxv = v_s[...] * _dsilu(p["xv"])

        def conv_bwd(dx, x, w_ref, dp_ref, dc_ref):
            acc = dx * w_ref[DN_CONV - 1:DN_CONV, :]
            dc_ref[DN_CONV - 1:DN_CONV, :] = _colsum(dx * x)
            for i in range(DN_CONV - 1):
                s = DN_CONV - 1 - i
                acc = acc + _shift_rows(dx, -s) * w_ref[i:i + 1, :]
                dc_ref[i:i + 1, :] = _colsum(dx * _shift_rows(x, s))
            dp_ref[...] = acc.astype(dp_ref.dtype)

        conv_bwd(dxq, pq, cq_ref, dpq_ref, dcq_ref)
        conv_bwd(dxk, pk, ck_ref, dpk_ref, dck_ref)
        conv_bwd(dxv, pv, cv_ref, dpv_ref, dcv_ref)

        dg = g_s[...]
        beta = p["beta"]
        da_raw = dg * p["neg_ea"] * _sigmoid(p["pre"])
        db_raw = b_s[...] * beta * (1.0 - beta)
        lane = _iota((T, LANES), 1)
        contrib = jnp.where(lane == h, da_raw, 0.0) + jnp.where(lane == DN_H + h, db_raw, 0.0)
        lane1 = _iota((1, LANES), 1)
        dalog_h = jnp.where(lane1 == h, _colsum(dg * p["g"]), 0.0)
        ddtb_h = jnp.where(lane1 == h, _colsum(da_raw), 0.0)

        @pl.when(h == 0)
        def _():
            dpab_ref[...] = contrib.astype(dpab_ref.dtype)
            dalog_ref[...] = dalog_h
            ddtb_ref[...] = ddtb_h
            dog_ref[...] = dog_h

        @pl.when(h > 0)
        def _():
            dpab_ref[...] += contrib.astype(dpab_ref.dtype)
            dalog_ref[...] += dalog_h
            ddtb_ref[...] += ddtb_h
            dog_ref[...] += dog_h

    col = lambda k: pl.BlockSpec((T, DN_D), lambda h: (0, k * DN_H + h))
    cw = lambda k: pl.BlockSpec((DN_CONV, DN_D), lambda h: (0, k * DN_H + h))
    small = pl.BlockSpec((1, LANES), lambda h: (0, 0))
    ab = pl.BlockSpec((T, LANES), lambda h: (0, 4 * DN_H))
    hcol = pl.BlockSpec((T, DN_D), lambda h: (0, h))
    outs = _grid_call(
        body, comm, grid=(DN_H,),
        in_specs=[col(0), col(1), col(2), col(3), ab, cw(0), cw(1), cw(2), small, small, small,
                  hcol, pl.BlockSpec((1, N, DN_D, DN_D), lambda h: (h, 0, 0, 0)), hcol],
        out_specs=[hcol, hcol, hcol, hcol, pl.BlockSpec((T, LANES), lambda h: (0, 0)),
                   pl.BlockSpec((DN_CONV, DN_D), lambda h: (0, h)), pl.BlockSpec((DN_CONV, DN_D), lambda h: (0, h)),
                   pl.BlockSpec((DN_CONV, DN_D), lambda h: (0, h)), small, small, small],
        out_shape=[jax.ShapeDtypeStruct((T, D), BF16)] * 4 + [jax.ShapeDtypeStruct((T, LANES), BF16)]
                  + [jax.ShapeDtypeStruct((DN_CONV, D), F32)] * 3 + [jax.ShapeDtypeStruct((1, LANES), F32)] * 3,
        scratch_shapes=[pltpu.VMEM((T, DN_D), F32)] * 3 + [pltpu.VMEM((T, 1), F32)] * 2
                       + [pltpu.VMEM((T, DN_D), F32), pltpu.VMEM((DN_D, DN_D), F32)],
        name=name, args=(proj, proj, proj, proj, proj, conv_w, conv_w, conv_w, alog, dtb, og, o, states, dout))
    dpq, dpk, dpv, dpz, dpab, dcq, dck, dcv, dalog, ddtb, dog = outs[:11]
    dproj = jnp.concatenate([dpq, dpk, dpv, dpz, dpab], axis=1)
    dconv = jnp.concatenate([dcq, dck, dcv], axis=1)
    return dproj, dconv, dalog, ddtb, dog, (outs[11] if comm else None)


def _sb_head_norm(x, g, h0):
    xx = x * x
    s0 = _rowsum(jnp.where(h0, xx, 0.0))
    s1 = _rowsum(jnp.where(h0, 0.0, xx))
    r = lax.rsqrt(jnp.where(h0, s0, s1) * (1.0 / SB_D) + EPS)
    return x * r, r


def _sb_fill(q_ref, k_ref, v_ref, qg_ref, kg_ref, qm_s, kn_s, vm_s):
    T = q_ref.shape[0]
    h0 = _iota((T, LANES), 1) < SB_D
    qh, rq = _sb_head_norm(q_ref[...], None, h0)
    kh, rk = _sb_head_norm(k_ref[...], None, h0)
    qs = qh * qg_ref[...] * (SB_D ** -0.5)
    qm_s[0] = jnp.where(h0, qs, 0.0).astype(BF16)
    qm_s[1] = jnp.where(h0, 0.0, qs).astype(BF16)
    kn_s[...] = (kh * kg_ref[...]).astype(BF16)
    v = v_ref[...]
    vm_s[0] = jnp.where(h0, v, 0.0).astype(BF16)
    vm_s[1] = jnp.where(h0, 0.0, v).astype(BF16)
    return h0, qh, rq, kh, rk


def _sb_core_fwd(qkv, qg2, kg2, name, comm=None):
    T = qkv.shape[0]
    B = min(SB_TILE, T)
    NB = T // B
    NP = SB_H // 2

    def body(q_ref, k_ref, v_ref, qg_ref, kg_ref, o_ref, tot_ref, qm_s, kn_s, vm_s):
        _sb_fill(q_ref, k_ref, v_ref, qg_ref, kg_ref, qm_s, kn_s, vm_s)
        r, c = _iota((B, B), 0), _iota((B, B), 1)
        causal = c < r
        m_after = (r > c).astype(BF16)
        h0b = _iota((B, LANES), 1) < SB_D

        def tile(qb, kj, vj, R, acc, diag):
            z = _dot(qb, kj, NT)
            sp = _softplus(z)
            ls = z - sp
            lm = jnp.where(causal, -sp, 0.0) if diag else -sp
            cs = _dot2r(lm, m_after) + R
            a = jnp.exp(ls + cs)
            if diag:
                a = jnp.where(causal, a, 0.0)
            return R + _rowsum(lm), acc + _dot(a, vj)

        def qblock(i, carry):
            rows_i = pl.ds(pl.multiple_of(i * B, B), B)

            def both_heads(rows_j, st, diag):
                kj = kn_s[rows_j, :]
                out = []
                for hh in range(2):
                    out += list(tile(qm_s[hh, rows_i, :], kj, vm_s[hh, rows_j, :], st[2 * hh], st[2 * hh + 1], diag))
                return tuple(out)

            zero = (jnp.zeros((B, 1), F32), jnp.zeros((B, LANES), F32))
            st = both_heads(rows_i, zero + zero, True)
            st = lax.fori_loop(
                0, i, lambda s, st: both_heads(pl.ds(pl.multiple_of((i - 1 - s) * B, B), B), st, False), st)
            o_ref[rows_i, :] = (st[1] + st[3]).astype(o_ref.dtype)
            tot_ref[rows_i, :] = jnp.where(h0b, st[0], st[2])
            return carry

        lax.fori_loop(0, NB, qblock, 0)

    blk = lambda k: pl.BlockSpec((T, LANES), lambda p: (0, k * NP + p))
    small = pl.BlockSpec((1, LANES), lambda p: (0, 0))
    return _grid_call(
        body, comm, grid=(NP,), in_specs=[blk(0), blk(1), blk(2), small, small],
        out_specs=[pl.BlockSpec((T, LANES), lambda p: (0, p))] * 2,
        out_shape=[jax.ShapeDtypeStruct((T, D), BF16), jax.ShapeDtypeStruct((T, D), F32)],
        scratch_shapes=[pltpu.VMEM((2, T, LANES), BF16), pltpu.VMEM((T, LANES), BF16), pltpu.VMEM((2, T, LANES), BF16)],
        name=name, args=(qkv, qkv, qkv, qg2, kg2))


def _sb_core_bwd(qkv, qg2, kg2, tot, dout, name, comm=None):
    T = qkv.shape[0]
    B = min(SB_TILE, T)
    NB = T // B
    NP = SB_H // 2

    def body(q_ref, k_ref, v_ref, qg_ref, kg_ref, tot_ref, do_ref, dq_ref, dk_ref, dv_ref, dqg_ref, dkg_ref,
             qm_s, kn_s, vm_s, dom_s, dqn_s, dkn_s, dvv_s):
        p = pl.program_id(0)
        h0, qh, rq, kh, rk = _sb_fill(q_ref, k_ref, v_ref, qg_ref, kg_ref, qm_s, kn_s, vm_s)
        dov = do_ref[...]
        dom_s[0] = jnp.where(h0, dov, 0.0).astype(BF16)
        dom_s[1] = jnp.where(h0, 0.0, dov).astype(BF16)
        dkn_s[...] = jnp.zeros(dkn_s.shape, F32)
        dvv_s[...] = jnp.zeros(dvv_s.shape, F32)
        r, c = _iota((B, B), 0), _iota((B, B), 1)
        causal = c < r
        m_upto = (r <= c).astype(BF16)
        m_before = (r < c).astype(BF16)
        laneb = _iota((B, LANES), 1)

        def tile(qb, dob, tot_h, kj, vj, PL, P, dq, diag):
            z = _dot(qb, kj, NT)
            sp = _softplus(z)
            ls = z - sp
            lm = jnp.where(causal, -sp, 0.0) if diag else -sp
            cs = tot_h - PL - _dot2r(lm, m_upto)
            a = jnp.exp(ls + cs)
            if diag:
                a = jnp.where(causal, a, 0.0)
            e = _dot(dob, vj, NT) * a
            E = _dot2r(e, m_before) + P
            sig = jnp.exp(ls)
            dz = e * (1.0 - sig) - E * sig
            if diag:
                dz = jnp.where(causal, dz, 0.0)
            return (PL + _rowsum(lm), P + _rowsum(e), dq + _dot(dz, kj)), _dot(dz, qb, TN), _dot(a, dob, TN)

        def qblock(i, carry):
            rows_i = pl.ds(pl.multiple_of(i * B, B), B)
            totb = tot_ref[rows_i, :]
            tot_h = [_rowsum(jnp.where(laneb == hh * SB_D, totb, 0.0)) for hh in range(2)]

            def both_heads(rows_j, st, diag):
                kj = kn_s[rows_j, :]
                new, dk, dv = [], None, None
                for hh in range(2):
                    s3, dk_h, dv_h = tile(qm_s[hh, rows_i, :], dom_s[hh, rows_i, :], tot_h[hh], kj, vm_s[hh, rows_j, :],
                                          st[3 * hh], st[3 * hh + 1], st[3 * hh + 2], diag)
                    new += list(s3)
                    dk = dk_h if dk is None else dk + dk_h
                    dv = dv_h if dv is None else dv + dv_h
                dkn_s[rows_j, :] += dk
                dvv_s[rows_j, :] += dv
                return tuple(new)

            zero = (jnp.zeros((B, 1), F32), jnp.zeros((B, 1), F32), jnp.zeros((B, LANES), F32))
            st = lax.fori_loop(0, i, lambda j, st: both_heads(pl.ds(pl.multiple_of(j * B, B), B), st, False), zero + zero)
            st = both_heads(rows_i, st, True)
            dqn_s[rows_i, :] = jnp.where(laneb < SB_D, st[2], st[5]) * (SB_D ** -0.5)
            return carry

        lax.fori_loop(0, NB, qblock, 0)

        def norm_bwd(dn, xh, rr, g):
            dg = _colsum(dn * xh)
            t = dn * g
            tx = t * xh
            m0 = _rowsum(jnp.where(h0, tx, 0.0))
            m1 = _rowsum(jnp.where(h0, 0.0, tx))
            return rr * (t - xh * (jnp.where(h0, m0, m1) * (1.0 / SB_D))), dg

        dq, dqg = norm_bwd(dqn_s[...], qh, rq, qg_ref[...])
        dk, dkg = norm_bwd(dkn_s[...], kh, rk, kg_ref[...])
        dq_ref[...] = dq.astype(dq_ref.dtype)
        dk_ref[...] = dk.astype(dk_ref.dtype)
        dv_ref[...] = dvv_s[...].astype(dv_ref.dtype)

        @pl.when(p == 0)
        def _():
            dqg_ref[...] = dqg
            dkg_ref[...] = dkg

        @pl.when(p > 0)
        def _():
            dqg_ref[...] += dqg
            dkg_ref[...] += dkg

    blk = lambda k: pl.BlockSpec((T, LANES), lambda p: (0, k * NP + p))
    small = pl.BlockSpec((1, LANES), lambda p: (0, 0))
    own = pl.BlockSpec((T, LANES), lambda p: (0, p))
    outs = _grid_call(
        body, comm, grid=(NP,), in_specs=[blk(0), blk(1), blk(2), small, small, own, own],
        out_specs=[own, own, own, small, small],
        out_shape=[jax.ShapeDtypeStruct((T, D), BF16)] * 3 + [jax.ShapeDtypeStruct((1, LANES), F32)] * 2,
        scratch_shapes=[pltpu.VMEM((2, T, LANES), BF16), pltpu.VMEM((T, LANES), BF16), pltpu.VMEM((2, T, LANES), BF16),
                        pltpu.VMEM((2, T, LANES), BF16)] + [pltpu.VMEM((T, LANES), F32)] * 3,
        name=name, args=(qkv, qkv, qkv, qg2, kg2, tot, dout))
    dq, dk, dv, dqg, dkg = outs[:5]
    return jnp.concatenate([dq, dk, dv], axis=1), dqg, dkg, (outs[5] if comm else None)


def _ada_fwd(c16, ada_w, ada_b_cols, name):
    L, _, cols = ada_w.shape

    def body(c_ref, w_ref, b_ref, o_ref):
        o_ref[0] = _dot(_silu(c_ref[...]), w_ref[0]) + b_ref[0]

    return pl.pallas_call(
        body, grid=(L,),
        in_specs=[pl.BlockSpec((16, D), lambda i: (0, 0)), pl.BlockSpec((1, D, cols), lambda i: (i, 0, 0)),
                  pl.BlockSpec((1, 1, cols), lambda i: (i, 0, 0))],
        out_specs=pl.BlockSpec((1, 16, cols), lambda i: (i, 0, 0)),
        out_shape=jax.ShapeDtypeStruct((L, 16, cols), F32), name=name, compiler_params=_params(("arbitrary",)),
    )(c16, ada_w, ada_b_cols)


def _ada_bwd(c16, dmod16, name):
    L, _, cols = dmod16.shape

    def body(c_ref, d_ref, o_ref):
        o_ref[0] = _dot(_silu(c_ref[...]), d_ref[0], TN)

    return pl.pallas_call(
        body, grid=(L,),
        in_specs=[pl.BlockSpec((16, D), lambda i: (0, 0)), pl.BlockSpec((1, 16, cols), lambda i: (i, 0, 0))],
        out_specs=pl.BlockSpec((1, D, cols), lambda i: (i, 0, 0)),
        out_shape=jax.ShapeDtypeStruct((L, D, cols), F32), name=name, compiler_params=_params(("arbitrary",)),
    )(c16, dmod16)


def _sum_sources(x, name):
    n, R, C = x.shape
    tile = _sum_tile(R) or R

    def body(x_ref, o_ref):
        acc = x_ref[0].astype(F32)
        for k in range(1, n):
            acc = acc + x_ref[k].astype(F32)
        o_ref[...] = acc

    return pl.pallas_call(
        body, grid=(R // tile,), in_specs=[pl.BlockSpec((n, tile, C), lambda i: (0, i, 0))],
        out_specs=pl.BlockSpec((tile, C), lambda i: (i, 0)), out_shape=jax.ShapeDtypeStruct((R, C), F32),
        name=name, compiler_params=_params(("parallel",)),
    )(x)


def _adamw(w, g, m, v, name):
    shape = w.shape
    C = shape[-1]
    R = w.size // C
    w2, g2, m2, v2 = (a.reshape(R, C) for a in (w, g, m, v))
    tile = _rtile(R)
    c1 = 1.0 / (1.0 - ADAM_B1 ** ADAM_STEP)
    c2 = 1.0 / (1.0 - ADAM_B2 ** ADAM_STEP)

    def body(w_ref, g_ref, m_ref, v_ref, d_ref, nm_ref, nv_ref):
        gv = g_ref[...]
        nm = ADAM_B1 * m_ref[...] + (1.0 - ADAM_B1) * gv
        nv = ADAM_B2 * v_ref[...] + (1.0 - ADAM_B2) * (gv * gv)
        d_ref[...] = -ADAM_LR * ((nm * c1) / (jnp.sqrt(nv * c2) + ADAM_EPS) + ADAM_WD * w_ref[...])
        nm_ref[...] = nm
        nv_ref[...] = nv

    spec = pl.BlockSpec((tile, C), lambda i: (i, 0))
    outs = pl.pallas_call(
        body, grid=(R // tile,), in_specs=[spec] * 4, out_specs=[spec] * 3,
        out_shape=[jax.ShapeDtypeStruct((R, C), F32)] * 3, name=name, compiler_params=_params(("parallel",)),
    )(w2, g2, m2, v2)
    return tuple(o.reshape(shape) for o in outs)


def _gather_steps(x_ref, out_ref, send_sems, recv_sems, local_sem):
    mx, my, mc = lax.axis_index("x"), lax.axis_index("y"), lax.axis_index("c")
    me, sibling = (mx, my, mc), (mx, my, 1 - mc)
    chips = [(1 - mx, my), (mx, 1 - my), (1 - mx, 1 - my)]

    def slot(px, py, pc):
        return out_ref.at[4 * px + 2 * py + pc]

    def copy(k, block, to, src=None):
        return pltpu.make_async_remote_copy(
            src_ref=slot(*block) if src is None else src, dst_ref=slot(*block),
            send_sem=send_sems.at[k], recv_sem=recv_sems.at[k], device_id=to, device_id_type=MESH)

    mine = pltpu.make_async_copy(x_ref, slot(*me), local_sem)
    first = [copy(0, me, sibling, src=x_ref)] + [copy(1 + j, me, (*chip, mc), src=x_ref) for j, chip in enumerate(chips)]
    passed = [copy(4 + j, (*chip, mc), sibling) for j, chip in enumerate(chips)]

    def start():
        mine.start()
        for cp in first:
            cp.start()

    def forward():
        for j, chip in enumerate(chips):
            copy(1 + j, (*chip, mc), me).wait_recv()
            passed[j].start()

    def finish():
        copy(0, sibling, me).wait_recv()
        for j, chip in enumerate(chips):
            copy(4 + j, (*chip, 1 - mc), me).wait_recv()
        for cp in first + passed:
            cp.wait_send()
        mine.wait()

    return start, forward, finish


def _exchange_steps(x_ref, out_ref, send_sems, recv_sems, local_sem):
    mx, my, mc = lax.axis_index("x"), lax.axis_index("y"), lax.axis_index("c")
    me = 4 * mx + 2 * my + mc
    mine = pltpu.make_async_copy(x_ref.at[me], out_ref.at[me], local_sem)
    copies = []
    for k in range(1, NDEV):
        px, py, pc = mx ^ (k >> 2), my ^ ((k >> 1) & 1), mc ^ (k & 1)
        copies.append(pltpu.make_async_remote_copy(
            src_ref=x_ref.at[4 * px + 2 * py + pc], dst_ref=out_ref.at[me], send_sem=send_sems.at[k - 1],
            recv_sem=recv_sems.at[k - 1], device_id=(px, py, pc), device_id_type=MESH))

    def start():
        mine.start()
        for cp in copies:
            cp.start()

    def finish():
        for cp in copies:
            cp.wait_recv()
        for cp in copies:
            cp.wait_send()
        mine.wait()

    return start, finish


def _comm_sems():
    return [pltpu.SemaphoreType.DMA((7,)), pltpu.SemaphoreType.DMA((7,)), pltpu.SemaphoreType.DMA]


def _comm_out_shape(comm):
    kind, x = comm
    return jax.ShapeDtypeStruct(((NDEV,) + x.shape) if kind == "gather" else x.shape, x.dtype)


def _ride_along(comm, step, n_steps, refs, at_end):
    if comm[0] == "gather":
        start, forward, finish = _gather_steps(*refs)
        todo = [(n_steps - 1, finish)] if at_end else [(0, start), (n_steps // 2, forward)]
    else:
        start, finish = _exchange_steps(*refs)
        todo = [(n_steps - 1, finish)] if at_end else [(0, start)]
    for at, fn in todo:
        pl.when(step == at)(fn)


def _grid_call(body, comm, *, grid, in_specs, out_specs, out_shape, scratch_shapes, name, args):
    if comm is None:
        return pl.pallas_call(body, grid=grid, in_specs=in_specs, out_specs=out_specs, out_shape=out_shape,
                              scratch_shapes=scratch_shapes, name=name, compiler_params=_params(("arbitrary",)))(*args)
    n_in, n_out, n_steps = len(in_specs), len(out_specs), grid[0]

    def with_comm(*refs):
        ins, outs, scr = refs[:n_in], refs[n_in + 1:n_in + 1 + n_out], refs[n_in + 2 + n_out:-3]
        comm_refs = (refs[n_in], refs[n_in + 1 + n_out]) + refs[-3:]
        step = pl.program_id(0)
        _ride_along(comm, step, n_steps, comm_refs, False)
        body(*ins, *outs, *scr)
        _ride_along(comm, step, n_steps, comm_refs, True)

    hbm = pl.BlockSpec(memory_space=pl.ANY)
    return pl.pallas_call(
        with_comm, grid=grid, in_specs=list(in_specs) + [hbm], out_specs=list(out_specs) + [hbm],
        out_shape=list(out_shape) + [_comm_out_shape(comm)], scratch_shapes=list(scratch_shapes) + _comm_sems(),
        name=name, compiler_params=_params(("arbitrary",)))(*args, comm[1])


def _all_gather(x, name, in_vmem):
    def body(x_ref, out_ref, send_sems, recv_sems, local_sem):
        start, forward, finish = _gather_steps(x_ref, out_ref, send_sems, recv_sems, local_sem)
        start()
        forward()
        finish()

    space = pltpu.VMEM if in_vmem else pl.ANY
    return pl.pallas_call(
        body, out_shape=jax.ShapeDtypeStruct((NDEV,) + x.shape, x.dtype),
        in_specs=[pl.BlockSpec(memory_space=space)], out_specs=pl.BlockSpec(memory_space=space),
        scratch_shapes=_comm_sems(), name=name, compiler_params=pltpu.CompilerParams(vmem_limit_bytes=VMEM_LIMIT),
    )(x)


def _all_to_all(x, name):
    def body(x_ref, out_ref, send_sems, recv_sems, local_sem):
        start, finish = _exchange_steps(x_ref, out_ref, send_sems, recv_sems, local_sem)
        start()
        finish()

    return pl.pallas_call(
        body, out_shape=jax.ShapeDtypeStruct(x.shape, x.dtype),
        in_specs=[pl.BlockSpec(memory_space=pl.ANY)], out_specs=pl.BlockSpec(memory_space=pl.ANY),
        scratch_shapes=_comm_sems(), name=name, compiler_params=pltpu.CompilerParams(vmem_limit_bytes=VMEM_LIMIT),
    )(x)


def _layer_parts(i):
    j = i // 2
    if i % 2 == 0:
        mixer = [("dn_w_in", j, (D, DN_COLS // NDEV), True), ("dn_w_out", j, (D // NDEV, D), False)]
    else:
        mixer = [("sb_w_qkv", j, (D, 3 * D // NDEV), True), ("sb_w_out", j, (D // NDEV, D), False)]
    return mixer + [("ffn_w_in", i, (D, 2 * DFF // NDEV), True), ("ffn_w_out", i, (DFF // NDEV, D), False)]


def _sum_tile(rows):
    for t in range(512, 191, -16):
        if rows % t == 0:
            return t
    return None


def _layer_layout(i):
    out, r = [], 0
    for n, j, s, by_cols in _layer_parts(i):
        k = s[0] * s[1] // D
        kp = -(-k // 16) * 16
        out.append((n, j, s, by_cols, k, kp, r))
        r += kp
    while _sum_tile(r) is None:
        r += 16
    return out, r


def _pack_layer(w, i):
    parts, rows = _layer_layout(i)
    bufs = [jnp.pad(w[n][j].astype(BF16).reshape(k, D), ((0, kp - k), (0, 0))) for n, j, _, _, k, kp, _ in parts]
    used = parts[-1][6] + parts[-1][5]
    if rows > used:
        bufs.append(jnp.zeros((rows - used, D), BF16))
    return jnp.concatenate(bufs, axis=0)


def _full_layer(g, i):
    out = {}
    for n, _, s, by_cols, k, _, r in _layer_layout(i)[0]:
        a = g[:, r:r + k].reshape((NDEV,) + s)
        out[n] = jnp.transpose(a, (1, 0, 2)).reshape(s[0], NDEV * s[1]) if by_cols else a.reshape(NDEV * s[0], s[1])
    if "dn_w_in" in out:
        out["dn_w_in"] = jnp.pad(out["dn_w_in"], ((0, 0), (0, DN_COLS_PAD - DN_COLS)))
    return out


def _shards_layer(full, i):
    parts, rows = _layer_layout(i)
    bufs = []
    for n, _, s, by_cols, k, kp, _ in parts:
        a = full[n].astype(BF16)
        a = jnp.transpose(a.reshape(s[0], NDEV, s[1]), (1, 0, 2)) if by_cols else a.reshape(NDEV, s[0], s[1])
        bufs.append(jnp.pad(a.reshape(NDEV, k, D), ((0, 0), (0, kp - k), (0, 0))))
    used = parts[-1][6] + parts[-1][5]
    if rows > used:
        bufs.append(jnp.zeros((NDEV, rows - used, D), BF16))
    return jnp.concatenate(bufs, axis=1)


def _unpack_layer(buf, i):
    return {n: buf[r:r + k].reshape(s) for n, _, s, _, k, _, r in _layer_layout(i)[0]}


def _local_step(x, target, mod, small, w0, rest):
    dist = isinstance(rest, tuple)
    packs = rest[1] if dist else None
    W = [w0, None, None, None] if dist else [w0] + list(rest)
    row = lambda v: v.reshape(1, -1)
    pad128 = lambda v: jnp.pad(v.reshape(1, -1), ((0, 0), (0, LANES - v.size)))
    saved = []
    for i in range(DEPTH):
        j = i // 2
        w = W[i]
        m = [row(mod[i, k * D:(k + 1) * D]) for k in range(N_MOD)]
        sh1, sc1, gt1, sh2, sc2, gt2 = m
        g1, g2 = row(small["norm1_g"][i]), row(small["norm2_g"][i])
        h1 = _norm_mod(x, g1, sc1, sh1, f"norm1_{i}")
        if i % 2 == 0:
            proj = _mm(h1, w["dn_w_in"], "nn", F32, f"dn_proj_{i}")
            alog, dtb, og = pad128(small["dn_a_log"][j]), pad128(small["dn_dt_bias"][j]), row(small["dn_onorm_g"][j])
            comm = ("gather", packs[0]) if dist and i == 0 else None
            res = _dn_core_fwd(proj, small["dn_conv_w"][j], alog, dtb, og, f"dn_core_{i}", comm)
            om, o_pre, states = res[:3]
            if comm:
                W[1] = _full_layer(res[3], 1)
            y1 = _mm(om, w["dn_w_out"], "nn", F32, f"dn_out_{i}")
            mix = (proj, alog, dtb, og, o_pre, states, om)
        else:
            qkv = _mm(h1, w["sb_w_qkv"], "nn", F32, f"sb_qkv_{i}")
            qg2 = jnp.tile(row(small["sb_q_norm_g"][j]), (1, 2))
            kg2 = jnp.tile(row(small["sb_k_norm_g"][j]), (1, 2))
            comm = ("gather", jnp.concatenate(packs[1:], axis=0)) if dist and i == 1 else None
            res = _sb_core_fwd(qkv, qg2, kg2, f"sb_core_{i}", comm)
            om, tot = res[:2]
            if comm:
                r2 = packs[1].shape[0]
                W[2], W[3] = _full_layer(res[2][:, :r2], 2), _full_layer(res[2][:, r2:], 3)
            y1 = _mm(om, w["sb_w_out"], "nn", F32, f"sb_out_{i}")
            mix = (qkv, qg2, kg2, tot, om)
        x_mid = _gate_res(x, y1, gt1, f"res1_{i}")
        h2 = _norm_mod(x_mid, g2, sc2, sh2, f"norm2_{i}")
        u = _mm(h2, w["ffn_w_in"], "nn", F32, f"ffn_in_{i}")
        a = _swiglu_act(u, f"ffn_act_{i}")
        y2 = _mm(a, w["ffn_w_out"], "nn", F32, f"ffn_out_{i}")
        x_out = _gate_res(x_mid, y2, gt2, f"res2_{i}")
        saved.append((x, h1, mix, y1, x_mid, h2, u, a, y2, m, g1, g2))
        x = x_out

    dx, sq = _loss_head(x, target, "loss_head")

    sg = dict(mod=[None] * DEPTH, norm1_g=[None] * DEPTH, norm2_g=[None] * DEPTH, dn_conv_w=[None] * 2,
              dn_a_log=[None] * 2, dn_dt_bias=[None] * 2, dn_onorm_g=[None] * 2, sb_q_norm_g=[None] * 2,
              sb_k_norm_g=[None] * 2)
    big = [None] * DEPTH
    waiting = None
    for i in reversed(range(DEPTH)):
        j = i // 2
        w = W[i]
        x_in, h1, mix, y1, x_mid, h2, u, a, y2, m, g1, g2 = saved[i]
        sh1, sc1, gt1, sh2, sc2, gt2 = m
        gw = {}
        dy2, dgt2 = _gate_res_bwd(dx, y2, gt2, f"res2_bwd_{i}")
        da = _mm(dy2, w["ffn_w_out"], "nt", F32, f"ffn_out_bwd_{i}")
        gw["ffn_w_out"] = _mm(a, dy2, "tn", F32, f"ffn_out_wg_{i}")
        du = _swiglu_act_bwd(da, u, f"ffn_act_bwd_{i}")
        dh2 = _mm(du, w["ffn_w_in"], "nt", F32, f"ffn_in_bwd_{i}")
        gw["ffn_w_in"] = _mm(h2, du, "tn", F32, f"ffn_in_wg_{i}")
        dx_mid, s2, dsh2 = _norm_mod_bwd(dh2, x_mid, dx, g2, sc2, f"norm2_bwd_{i}")
        dy1, dgt1 = _gate_res_bwd(dx_mid, y1, gt1, f"res1_bwd_{i}")
        comm = ("exchange", waiting[1]) if waiting is not None else None
        if i % 2 == 0:
            proj, alog, dtb, og, o_pre, states, om = mix
            dom = _mm(dy1, w["dn_w_out"], "nt", F32, f"dn_out_bwd_{i}")
            gw["dn_w_out"] = _mm(om, dy1, "tn", F32, f"dn_out_wg_{i}")
            dproj, dconv, dalog, ddtb, dog, got = _dn_core_bwd(proj, small["dn_conv_w"][j], alog, dtb, og, o_pre, states,
                                                               dom, f"dn_core_bwd_{i}", comm)
            dh1 = _mm(dproj, w["dn_w_in"], "nt", F32, f"dn_proj_bwd_{i}")
            gw["dn_w_in"] = _mm(h1, dproj, "tn", F32, f"dn_proj_wg_{i}")[:, :DN_COLS]
            sg["dn_conv_w"][j] = dconv
            sg["dn_a_log"][j] = dalog[0, :DN_H]
            sg["dn_dt_bias"][j] = ddtb[0, :DN_H]
            sg["dn_onorm_g"][j] = dog[0]
        else:
            qkv, qg2, kg2, tot, om = mix
            dom = _mm(dy1, w["sb_w_out"], "nt", F32, f"sb_out_bwd_{i}")
            gw["sb_w_out"] = _mm(om, dy1, "tn", F32, f"sb_out_wg_{i}")
            dqkv, dqg, dkg, got = _sb_core_bwd(qkv, qg2, kg2, tot, dom, f"sb_core_bwd_{i}", comm)
            dh1 = _mm(dqkv, w["sb_w_qkv"], "nt", F32, f"sb_qkv_bwd_{i}")
            gw["sb_w_qkv"] = _mm(h1, dqkv, "tn", F32, f"sb_qkv_wg_{i}")
            sg["sb_q_norm_g"][j] = dqg[0, :SB_D] + dqg[0, SB_D:]
            sg["sb_k_norm_g"][j] = dkg[0, :SB_D] + dkg[0, SB_D:]
        if comm:
            big[waiting[0]] = got
        dx, s1, dsh1 = _norm_mod_bwd(dh1, x_in, dx_mid, g1, sc1, f"norm1_bwd_{i}")
        sg["mod"][i] = jnp.concatenate([dsh1, s1 * g1, dgt1, dsh2, s2 * g2, dgt2], axis=1)[0]
        sg["norm1_g"][i] = (s1 * (1.0 + sc1))[0]
        sg["norm2_g"][i] = (s2 * (1.0 + sc2))[0]
        if dist:
            waiting = (i, _shards_layer(gw, i))
        else:
            big[i] = gw
    if dist:
        big[0] = _all_to_all(waiting[1], "exchange_grads_0")
    return sq, dx, {k: jnp.stack(v) for k, v in sg.items()}, big


def _device_index():
    return 4 * lax.axis_index("x") + 2 * lax.axis_index("y") + lax.axis_index("c")


def _gather_small(w, c):
    me = _device_index()
    ada_cols = w["ada_w"].shape[-1]
    conv_cols = w["dn_conv_w"].shape[-1]
    blk = jnp.concatenate([c.reshape(1, D), w["dn_conv_w"].reshape(-1, D)], axis=0)
    g1 = _all_gather(blk, "gather_cond", True)
    c16 = jnp.pad(g1[:, 0, :], ((0, 8), (0, 0)))
    conv_full = jnp.transpose(g1[:, 1:, :].reshape(NDEV, 2, DN_CONV, conv_cols), (1, 2, 0, 3)).reshape(2, DN_CONV, -1)
    b_cols = lax.dynamic_slice_in_dim(w["ada_b"], me * ada_cols, ada_cols, axis=1).reshape(DEPTH, 1, ada_cols)
    mod_part = _ada_fwd(c16, w["ada_w"], b_cols, "ada_fwd")[:, :NDEV, :]
    g2 = _all_gather(mod_part.reshape(DEPTH * NDEV, ada_cols), "gather_mod", True)
    g2 = g2.reshape(NDEV, DEPTH, NDEV, ada_cols)
    mod = lax.dynamic_index_in_dim(g2, me, axis=2, keepdims=False)
    mod = jnp.transpose(mod, (1, 0, 2)).reshape(DEPTH, N_MOD * D)
    return c16, conv_full, mod


def _reduce_small(gr, c16):
    me = _device_index()
    ada_cols = N_MOD * D // NDEV
    conv_cols = 3 * DN_H * DN_D // NDEV
    grads = {}
    small = jnp.concatenate([gr["dn_a_log"].reshape(-1), gr["dn_dt_bias"].reshape(-1), gr["dn_onorm_g"].reshape(-1),
                             gr["sb_q_norm_g"].reshape(-1), gr["sb_k_norm_g"].reshape(-1)])
    small = jnp.pad(small, (0, D - small.size)).reshape(1, D)
    blk3 = jnp.concatenate([gr["mod"].reshape(-1, D), gr["norm1_g"], gr["norm2_g"], gr["dn_conv_w"].reshape(-1, D),
                            small], axis=0)
    blk3 = jnp.pad(blk3, ((0, 64 - blk3.shape[0]), (0, 0)))
    g3 = _all_gather(blk3, "gather_small_grads", True)
    tot = _sum_sources(g3, "sum_small_grads")
    grads["ada_b"] = tot[:24].reshape(DEPTH, N_MOD * D)
    grads["norm1_g"] = tot[24:28]
    grads["norm2_g"] = tot[28:32]
    conv_g = tot[32:56].reshape(2, DN_CONV, NDEV * conv_cols)
    grads["dn_conv_w"] = lax.dynamic_slice_in_dim(conv_g, me * conv_cols, conv_cols, axis=2)
    sm = tot[56]
    grads["dn_a_log"] = sm[0:16].reshape(2, DN_H)
    grads["dn_dt_bias"] = sm[16:32].reshape(2, DN_H)
    grads["dn_onorm_g"] = sm[32:288].reshape(2, DN_D)
    grads["sb_q_norm_g"] = sm[288:416].reshape(2, SB_D)
    grads["sb_k_norm_g"] = sm[416:544].reshape(2, SB_D)
    dmod_all = g3[:, :24, :].reshape(NDEV, DEPTH, N_MOD * D)
    dmod_cols = lax.dynamic_slice_in_dim(dmod_all, me * ada_cols, ada_cols, axis=2)
    dmod16 = jnp.pad(jnp.transpose(dmod_cols, (1, 0, 2)), ((0, 0), (0, 8), (0, 0)))
    grads["ada_w"] = _ada_bwd(c16, dmod16, "ada_bwd")
    return grads


def _reduce_big(recv):
    per_layer = [_unpack_layer(_sum_sources(recv[i], f"sum_grads_{i}"), i) for i in range(DEPTH)]
    out = {}
    for i in range(DEPTH):
        for n, a in per_layer[i].items():
            out.setdefault(n, []).append(a)
    return {n: jnp.stack(v) for n, v in out.items()}


def kernel(x, c, ada_w, ada_b, norm1_g, norm2_g, dn_w_in, dn_conv_w, dn_a_log, dn_dt_bias, dn_onorm_g, dn_w_out, sb_w_qkv, sb_q_norm_g, sb_k_norm_g, sb_w_out, ffn_w_in, ffn_w_out, loss_target, m_ada_w, m_ada_b, m_norm1_g, m_norm2_g, m_dn_w_in, m_dn_conv_w, m_dn_a_log, m_dn_dt_bias, m_dn_onorm_g, m_dn_w_out, m_sb_w_qkv, m_sb_q_norm_g, m_sb_k_norm_g, m_sb_w_out, m_ffn_w_in, m_ffn_w_out, v_ada_w, v_ada_b, v_norm1_g, v_norm2_g, v_dn_w_in, v_dn_conv_w, v_dn_a_log, v_dn_dt_bias, v_dn_onorm_g, v_dn_w_out, v_sb_w_qkv, v_sb_q_norm_g, v_sb_k_norm_g, v_sb_w_out, v_ffn_w_in, v_ffn_w_out):
    w = dict(ada_w=ada_w, ada_b=ada_b, norm1_g=norm1_g, norm2_g=norm2_g, dn_w_in=dn_w_in, dn_conv_w=dn_conv_w,
             dn_a_log=dn_a_log, dn_dt_bias=dn_dt_bias, dn_onorm_g=dn_onorm_g, dn_w_out=dn_w_out, sb_w_qkv=sb_w_qkv,
             sb_q_norm_g=sb_q_norm_g, sb_k_norm_g=sb_k_norm_g, sb_w_out=sb_w_out, ffn_w_in=ffn_w_in, ffn_w_out=ffn_w_out)
    mom = dict(ada_w=m_ada_w, ada_b=m_ada_b, norm1_g=m_norm1_g, norm2_g=m_norm2_g, dn_w_in=m_dn_w_in,
               dn_conv_w=m_dn_conv_w, dn_a_log=m_dn_a_log, dn_dt_bias=m_dn_dt_bias, dn_onorm_g=m_dn_onorm_g,
               dn_w_out=m_dn_w_out, sb_w_qkv=m_sb_w_qkv, sb_q_norm_g=m_sb_q_norm_g, sb_k_norm_g=m_sb_k_norm_g,
               sb_w_out=m_sb_w_out, ffn_w_in=m_ffn_w_in, ffn_w_out=m_ffn_w_out)
    var = dict(ada_w=v_ada_w, ada_b=v_ada_b, norm1_g=v_norm1_g, norm2_g=v_norm2_g, dn_w_in=v_dn_w_in,
               dn_conv_w=v_dn_conv_w, dn_a_log=v_dn_a_log, dn_dt_bias=v_dn_dt_bias, dn_onorm_g=v_dn_onorm_g,
               dn_w_out=v_dn_w_out, sb_w_qkv=v_sb_w_qkv, sb_q_norm_g=v_sb_q_norm_g, sb_k_norm_g=v_sb_k_norm_g,
               sb_w_out=v_sb_w_out, ffn_w_in=v_ffn_w_in, ffn_w_out=v_ffn_w_out)
    names = list(w)
    c16, conv_full, mod = _gather_small(w, c)
    packs = [_pack_layer(w, i) for i in range(DEPTH)]
    w0 = _full_layer(_all_gather(packs[0], "gather_weights_0", False), 0)
    small = dict(norm1_g=norm1_g, norm2_g=norm2_g, dn_conv_w=conv_full, dn_a_log=dn_a_log, dn_dt_bias=dn_dt_bias,
                 dn_onorm_g=dn_onorm_g, sb_q_norm_g=sb_q_norm_g, sb_k_norm_g=sb_k_norm_g)
    sq, grad_x, sgr, recv = _local_step(x[0], loss_target[0], mod, small, w0, ("packed", packs[1:]))
    loss = lax.psum(sq[0, 0] * (0.5 / D), ("x", "y", "c"))
    grads = {**_reduce_big(recv), **_reduce_small(sgr, c16)}
    delta, new_m, new_v = {}, {}, {}
    for n in names:
        delta[n], new_m[n], new_v[n] = _adamw(w[n], grads[n], mom[n], var[n], f"adamw_{n}")
    return (loss, grad_x[None], *[grads[n] for n in names], *[delta[n] for n in names],
            *[new_m[n] for n in names], *[new_v[n] for n in names])
```

```python
import functools

import jax
import jax.numpy as jnp
from jax import lax
from jax.experimental import pallas as pl
from jax.experimental.pallas import tpu as pltpu

F32, BF16 = jnp.float32, jnp.bfloat16

D = 1024
DEPTH = 4
N_MOD = 6
DN_H, DN_D, DN_C, DN_CONV = 8, 128, 64, 4
DN_UNROLL_FWD, DN_UNROLL_BWD = 16, 4
DN_COLS = 4 * DN_H * DN_D + 2 * DN_H
DN_COLS_PAD = 33 * 128
SB_H, SB_D = 16, 64
SB_TILE = 256
SB_HPB = 4
SB_W = SB_HPB * SB_D
DFF = 2816
EPS = 1e-6
NDEV = 8
LANES = 128
VMEM_LIMIT = 56 * 1024 * 1024

ADAM_LR, ADAM_B1, ADAM_B2, ADAM_EPS, ADAM_WD, ADAM_STEP = 0.001, 0.9, 0.999, 1e-08, 0.01, 10

NN = ((1,), (0,))
NT = ((1,), (1,))
TN = ((0,), (0,))
MESH = pl.DeviceIdType.MESH


def _dot(a, b, dims=NN):
    a, b = a.astype(BF16), b.astype(BF16)
    if a.ndim == 2 and b.ndim == 2:
        return lax.dot_general(a, b, (dims, ((), ())), preferred_element_type=F32)
    n = a.shape[0] if a.ndim == 3 else b.shape[0]
    if a.ndim == 2:
        a = jnp.broadcast_to(a, (n,) + a.shape)
    if b.ndim == 2:
        b = jnp.broadcast_to(b, (n,) + b.shape)
    (ca,), (cb,) = dims
    return lax.dot_general(a, b, (((ca + 1,), (cb + 1,)), ((0,), (0,))), preferred_element_type=F32)


def _split(a):
    hi = a.astype(BF16)
    lo = (a - hi.astype(F32)).astype(BF16)
    return hi, lo


def _dot3(a, b, dims=NN):
    ah, al = _split(a)
    bh, bl = _split(b)
    return _dot(ah, bh, dims) + (_dot(al, bh, dims) + _dot(ah, bl, dims))


def _dot2r(a, m, dims=NN):
    ah, al = _split(a)
    return _dot(ah, m, dims) + _dot(al, m, dims)


def _dot2l(m, b, dims=NN):
    bh, bl = _split(b)
    return _dot(m, bh, dims) + _dot(m, bl, dims)


def _sigmoid(x):
    return 1.0 / (1.0 + jnp.exp(-x))


def _silu(x):
    return x * _sigmoid(x)


def _dsilu(x):
    s = _sigmoid(x)
    return s * (1.0 + x * (1.0 - s))


def _softplus(x):
    return jnp.maximum(x, 0.0) + jnp.log(1.0 + jnp.exp(-jnp.abs(x)))


def _iota(shape, dim):
    return lax.broadcasted_iota(jnp.int32, shape, dim)


def _rowsum(x):
    return jnp.sum(x, axis=-1, keepdims=True)


def _colsum(x):
    return jnp.sum(x, axis=-2, keepdims=True)


def _tile(n, pref):
    if n <= pref:
        return n
    best = None
    for t in range(LANES, pref + 1, LANES):
        if n % t == 0:
            best = t
    assert best is not None, (n, pref)
    return best


def _rtile(r, pref=512):
    best = None
    for t in range(8, min(r, pref) + 1, 8):
        if r % t == 0:
            best = t
    return best if best is not None else r


def _params(sem):
    return pltpu.CompilerParams(dimension_semantics=sem, vmem_limit_bytes=VMEM_LIMIT)


MM_TILE = 1408


def _mm(a, b, mode, out_dtype, name):
    if mode == "nn":
        (M, K), (K2, N) = a.shape, b.shape
        dims = NN
    elif mode == "nt":
        (M, K), (N, K2) = a.shape, b.shape
        dims = NT
    else:
        (K, M), (K2, N) = a.shape, b.shape
        dims = TN
    assert K == K2, (a.shape, b.shape, mode)
    tm, tn, tk = _tile(M, MM_TILE), _tile(N, MM_TILE), _tile(K, MM_TILE)
    nk = K // tk

    def body_single(a_ref, b_ref, o_ref):
        o_ref[...] = _dot(a_ref[...], b_ref[...], dims).astype(o_ref.dtype)

    def body_acc(a_ref, b_ref, o_ref, acc_ref):
        k = pl.program_id(2)
        p = _dot(a_ref[...], b_ref[...], dims)

        @pl.when(k == 0)
        def _():
            acc_ref[...] = p

        @pl.when(k > 0)
        def _():
            acc_ref[...] += p

        @pl.when(k == nk - 1)
        def _():
            o_ref[...] = acc_ref[...].astype(o_ref.dtype)

    if mode == "nn":
        a_spec = pl.BlockSpec((tm, tk), lambda i, j, k: (i, k))
        b_spec = pl.BlockSpec((tk, tn), lambda i, j, k: (k, j))
    elif mode == "nt":
        a_spec = pl.BlockSpec((tm, tk), lambda i, j, k: (i, k))
        b_spec = pl.BlockSpec((tn, tk), lambda i, j, k: (j, k))
    else:
        a_spec = pl.BlockSpec((tk, tm), lambda i, j, k: (k, i))
        b_spec = pl.BlockSpec((tk, tn), lambda i, j, k: (k, j))
    return pl.pallas_call(
        body_single if nk == 1 else body_acc, grid=(M // tm, N // tn, nk), in_specs=[a_spec, b_spec],
        out_specs=pl.BlockSpec((tm, tn), lambda i, j, k: (i, j)),
        out_shape=jax.ShapeDtypeStruct((M, N), out_dtype),
        scratch_shapes=[] if nk == 1 else [pltpu.VMEM((tm, tn), F32)], name=name,
        compiler_params=_params(("parallel", "parallel", "arbitrary")),
    )(a, b)


def _rowwise(fn, name, rows, bcasts, out_rows, out_reds=(), tile=256):
    T = rows[0].shape[0]
    tile = min(tile, T)
    nr, nb, no = len(rows), len(bcasts), len(out_rows)

    def body(*refs):
        rv = [r[...] for r in refs[:nr]]
        bv = [r[...] for r in refs[nr:nr + nb]]
        outs, reds = fn(rv, bv)
        for r, o in zip(refs[nr + nb:nr + nb + no], outs):
            r[...] = o.astype(r.dtype)
        red_refs = refs[nr + nb + no:]
        if red_refs:
            @pl.when(pl.program_id(0) == 0)
            def _():
                for r in red_refs:
                    r[...] = jnp.zeros(r.shape, F32)

            for r, v in zip(red_refs, reds):
                r[...] += v

    in_specs = [pl.BlockSpec((tile, a.shape[1]), lambda i: (i, 0)) for a in rows]
    in_specs += [pl.BlockSpec(b.shape, lambda i: (0, 0)) for b in bcasts]
    out_specs = [pl.BlockSpec((tile, c), lambda i: (i, 0)) for c, _ in out_rows]
    out_specs += [pl.BlockSpec(s, lambda i: (0, 0)) for s in out_reds]
    out_shape = [jax.ShapeDtypeStruct((T, c), dt) for c, dt in out_rows]
    out_shape += [jax.ShapeDtypeStruct(s, F32) for s in out_reds]
    return pl.pallas_call(
        body, grid=(T // tile,), in_specs=in_specs, out_specs=out_specs, out_shape=out_shape, name=name,
        compiler_params=_params(("arbitrary",)),
    )(*rows, *bcasts)


def _norm_mod(x, g, sc, sh, name):
    def fn(rv, bv):
        (xv,), (gv, scv, shv) = rv, bv
        r = lax.rsqrt(jnp.mean(xv * xv, axis=1, keepdims=True) + EPS)
        return [(xv * r * gv) * (1.0 + scv) + shv], []
    return _rowwise(fn, name, [x], [g, sc, sh], [(D, BF16)])[0]


def _norm_mod_bwd(dh, x, dres, g, sc, name):
    def fn(rv, bv):
        (dhv, xv, drv), (gv, scv) = rv, bv
        r = lax.rsqrt(jnp.mean(xv * xv, axis=1, keepdims=True) + EPS)
        xhat = xv * r
        dxhat = dhv * (gv * (1.0 + scv))
        dx = r * (dxhat - xhat * jnp.mean(dxhat * xhat, axis=1, keepdims=True)) + drv
        return [dx], [_colsum(dhv * xhat), _colsum(dhv)]
    return _rowwise(fn, name, [dh, x, dres], [g, sc], [(D, F32)], [(1, D), (1, D)])


def _gate_res(x, y, gt, name):
    def fn(rv, bv):
        return [rv[0] + bv[0] * rv[1]], []
    return _rowwise(fn, name, [x, y], [gt], [(D, F32)])[0]


def _gate_res_bwd(dxn, y, gt, name):
    def fn(rv, bv):
        return [rv[0] * bv[0]], [_colsum(rv[0] * rv[1])]
    return _rowwise(fn, name, [dxn, y], [gt], [(D, BF16)], [(1, D)])


def _swiglu_act(u, name):
    def fn(rv, bv):
        uv = rv[0]
        return [_silu(uv[:, :DFF]) * uv[:, DFF:]], []
    return _rowwise(fn, name, [u], [], [(DFF, BF16)])[0]


def _swiglu_act_bwd(da, u, name):
    def fn(rv, bv):
        dav, uv = rv
        gate, up = uv[:, :DFF], uv[:, DFF:]
        return [jnp.concatenate([dav * up * _dsilu(gate), dav * _silu(gate)], axis=1)], []
    return _rowwise(fn, name, [da, u], [], [(2 * DFF, BF16)])[0]


def _loss_head(y, target, name):
    def fn(rv, bv):
        err = rv[0] - rv[1]
        return [err * (1.0 / D)], [_colsum(_rowsum(err * err))]
    return _rowwise(fn, name, [y, target], [], [(D, F32)], [(1, 1)])


def _shift_rows(x, s):
    if s == 0:
        return x
    T = x.shape[0]
    r = pltpu.roll(x, s % T, axis=0)
    t = _iota(x.shape, 0)
    keep = (t >= s) if s > 0 else (t < T + s)
    return jnp.where(keep, r, 0.0)


def _dn_prep(pq, pk, pv, pab, cq_ref, ck_ref, cv_ref, alog, dtb, h):
    lane = _iota(pab.shape, 1)
    a_col = _rowsum(jnp.where(lane == h, pab, 0.0))
    b_col = _rowsum(jnp.where(lane == DN_H + h, pab, 0.0))
    lane1 = _iota(alog.shape, 1)
    alog_h = _rowsum(jnp.where(lane1 == h, alog, 0.0))
    dtb_h = _rowsum(jnp.where(lane1 == h, dtb, 0.0))
    pre = a_col + dtb_h
    neg_ea = -jnp.exp(alog_h)
    g = neg_ea * _softplus(pre)
    beta = _sigmoid(b_col)

    def conv(x, w_ref):
        acc = x * w_ref[DN_CONV - 1:DN_CONV, :]
        for i in range(DN_CONV - 1):
            acc = acc + _shift_rows(x, DN_CONV - 1 - i) * w_ref[i:i + 1, :]
        return acc

    xq, xk, xv = conv(pq, cq_ref), conv(pk, ck_ref), conv(pv, cv_ref)
    sq, sk, v = _silu(xq), _silu(xk), _silu(xv)
    rq = lax.rsqrt(_rowsum(sq * sq) + EPS)
    rk = lax.rsqrt(_rowsum(sk * sk) + EPS)
    return dict(g=g, beta=beta, pre=pre, neg_ea=neg_ea, xq=xq, xk=xk, xv=xv, rq=rq, rk=rk,
                qn=sq * rq, kn=sk * rk, v=v)


def _dn_masks():
    C = DN_C
    r, c = _iota((C, C), 0), _iota((C, C), 1)
    incl = r >= c
    strict = r > c
    blk16 = jnp.right_shift(r, 4) == jnp.right_shift(c, 4)
    blk32 = jnp.right_shift(r, 5) == jnp.right_shift(c, 5)
    return dict(incl=incl, strict=strict, upper=r <= c, blk16=blk16, blk32=blk32,
                tri=incl.astype(BF16), triT=(r <= c).astype(BF16), ones=jnp.ones((C, C), BF16),
                eye=(r == c).astype(F32), last=_iota((C, 1), 0) == C - 1)


def _tri_inverse(A, mk):
    P = -jnp.where(mk["blk16"], A, 0.0)
    X = mk["eye"] + P
    for _ in range(3):
        P = _dot3(P, P)
        X = X + _dot3(X, P)
    off1 = jnp.where(mk["blk32"] & (~mk["blk16"]), A, 0.0)
    X = X - _dot3(_dot3(X, off1), X)
    off2 = jnp.where(mk["blk32"], 0.0, A)
    X = X - _dot3(_dot3(X, off2), X)
    return X


def _dn_local(qc, kc, vc, gc, bc, mk):
    C = DN_C
    gm = jnp.broadcast_to(gc, gc.shape[:-1] + (C,))
    Gc = _dot2l(mk["tri"], gm)
    Gr = _dot2l(mk["ones"], jnp.where(mk["upper"], gm, 0.0))
    Dm = jnp.where(mk["incl"], jnp.exp(jnp.where(mk["incl"], Gc - Gr, 0.0)), 0.0)
    Gcol = jnp.max(Gc, axis=-1, keepdims=True)
    Gl = _colsum(jnp.where(mk["last"], Gcol, 0.0))
    eG = jnp.exp(Gcol)
    eT = jnp.exp(Gl - Gcol)
    gl = jnp.exp(Gl)
    kb = kc * bc
    vb = vc * bc
    KK = _dot(kb, kc, NT)
    Tinv = _tri_inverse(jnp.where(mk["strict"], KK * Dm, 0.0), mk)
    KBE = kb * eG
    QK = _dot(qc, kc, NT)
    return dict(Dm=Dm, eG=eG, eT=eT, gl=gl, kb=kb, vb=vb, KK=KK, Tinv=Tinv, KBE=KBE, U=_dot(Tinv, vb),
                W=_dot(Tinv, KBE), QK=QK, attn=QK * Dm, QD=qc * eG, KT=kc * eT)


def _dn_recur(f, S):
    vnew = f["U"] - _dot(f["W"], S)
    o = _dot(f["QD"], S) + _dot(f["attn"], vnew)
    return o, S * f["gl"] + _dot(f["KT"], vnew, TN), vnew


def _dn_bwd_chain(f, do, dS):
    dvnew = _dot(f["KT"], dS) + _dot(f["attn"], do, TN)
    return dvnew, dS * f["gl"] + _dot(f["QD"], do, TN) - _dot(f["W"], dvnew, TN)


def _dn_bwd_rest(qc, kc, vc, bc, f, S, vnew, do, dS, dvnew, mk):
    C = DN_C
    Dm, eG, eT, gl, kb, vb, KK, Tinv, KBE, QK = (
        f[n] for n in ("Dm", "eG", "eT", "gl", "kb", "vb", "KK", "Tinv", "KBE", "QK"))
    dKT = _dot(vnew, dS, NT)
    dgl = _colsum(_rowsum(dS * S))
    dQD = _dot(do, S, NT)
    dattn = _dot(do, vnew, NT)
    dU = dvnew
    dW = -_dot(dvnew, S, NT)
    dQK = dattn * Dm
    dD = dattn * QK
    dq = _dot(dQK, kc)
    dk = _dot(dQK, qc, TN)
    dTinv = _dot(dU, vb, NT) + _dot(dW, KBE, NT)
    dvb = _dot(Tinv, dU, TN)
    dKBE = _dot(Tinv, dW, TN)
    dA = -_dot3(_dot3(Tinv, dTinv, TN), Tinv, NT)
    dA = jnp.where(mk["strict"], dA, 0.0)
    dKK = dA * Dm
    dD = dD + dA * KK
    dkb = _dot(dKK, kc) + dKBE * eG
    dk = dk + _dot(dKK, kb, TN)
    deG = _rowsum(dKBE * kb)
    dk = dk + dkb * bc
    dbeta = _rowsum(dkb * kc) + _rowsum(dvb * vc)
    dv = dvb * bc
    dq = dq + dQD * eG
    deG = deG + _rowsum(dQD * qc)
    dk = dk + dKT * eT
    deT = _rowsum(dKT * kc)
    dGcol = deG * eG - deT * eT
    dGl = _colsum(deT * eT) + dgl * gl
    Y = dD * Dm
    ycol = jnp.max(_dot2r(Y, mk["ones"], TN), axis=-1, keepdims=True)
    dGcol = dGcol + _rowsum(Y) - ycol
    dGcol = dGcol + jnp.where(mk["last"], dGl, 0.0)
    dg = jnp.max(_dot2l(mk["triT"], jnp.broadcast_to(dGcol, dGcol.shape[:-1] + (C,))), axis=-1, keepdims=True)
    return dq, dk, dv, dg, dbeta


def _dn_core_fwd(proj, conv_w, alog, dtb, og, name, comm=None):
    T = proj.shape[0]
    N = T // DN_C

    def body(pq_ref, pk_ref, pv_ref, pz_ref, pab_ref, cq_ref, ck_ref, cv_ref, alog_ref, dtb_ref, og_ref,
             out_ref, o_ref, st_ref, q_s, k_s, v_s, g_s, b_s, S_s):
        h = pl.program_id(0)
        p = _dn_prep(pq_ref[...], pk_ref[...], pv_ref[...], pab_ref[...], cq_ref, ck_ref, cv_ref,
                     alog_ref[...], dtb_ref[...], h)
        q_s[...] = p["qn"] * (DN_D ** -0.5)
        k_s[...] = p["kn"]
        v_s[...] = p["v"]
        g_s[...] = p["g"]
        b_s[...] = p["beta"]
        S_s[...] = jnp.zeros(S_s.shape, F32)
        mk = _dn_masks()

        nu = min(DN_UNROLL_FWD, N)

        def step(it, carry):
            rows = pl.ds(pl.multiple_of(it * (nu * DN_C), nu * DN_C), nu * DN_C)
            loc = _dn_local(*(r[rows, :].reshape(nu, DN_C, r.shape[1]) for r in (q_s, k_s, v_s, g_s, b_s)), mk)
            S = S_s[...]
            outs = []
            for u in range(nu):
                st_ref[0, it * nu + u] = S
                o, S, _ = _dn_recur({n: v[u] for n, v in loc.items()}, S)
                outs.append(o)
            o_ref[rows, :] = jnp.concatenate(outs, axis=0)
            S_s[...] = S
            return carry

        lax.fori_loop(0, N // nu, step, 0)
        o = o_ref[...]
        ro = lax.rsqrt(jnp.mean(o * o, axis=1, keepdims=True) + EPS)
        out_ref[...] = ((o * ro * og_ref[...]) * _silu(pz_ref[...])).astype(out_ref.dtype)

    col = lambda k: pl.BlockSpec((T, DN_D), lambda h: (0, k * DN_H + h))
    cw = lambda k: pl.BlockSpec((DN_CONV, DN_D), lambda h: (0, k * DN_H + h))
    small = pl.BlockSpec((1, LANES), lambda h: (0, 0))
    return _grid_call(
        body, comm, grid=(DN_H,),
        in_specs=[col(0), col(1), col(2), col(3), pl.BlockSpec((T, LANES), lambda h: (0, 4 * DN_H)),
                  cw(0), cw(1), cw(2), small, small, small],
        out_specs=[pl.BlockSpec((T, DN_D), lambda h: (0, h)), pl.BlockSpec((T, DN_D), lambda h: (0, h)),
                   pl.BlockSpec((1, N, DN_D, DN_D), lambda h: (h, 0, 0, 0))],
        out_shape=[jax.ShapeDtypeStruct((T, D), BF16), jax.ShapeDtypeStruct((T, D), F32),
                   jax.ShapeDtypeStruct((DN_H, N, DN_D, DN_D), F32)],
        scratch_shapes=[pltpu.VMEM((T, DN_D), F32)] * 3 + [pltpu.VMEM((T, 1), F32)] * 2 + [pltpu.VMEM((DN_D, DN_D), F32)],
        name=name, args=(proj, proj, proj, proj, proj, conv_w, conv_w, conv_w, alog, dtb, og))


def _dn_core_bwd(proj, conv_w, alog, dtb, og, o, states, dout, name, comm=None):
    T = proj.shape[0]
    N = T // DN_C

    def body(pq_ref, pk_ref, pv_ref, pz_ref, pab_ref, cq_ref, ck_ref, cv_ref, alog_ref, dtb_ref, og_ref,
             o_ref, st_ref, dout_ref,
             dpq_ref, dpk_ref, dpv_ref, dpz_ref, dpab_ref, dcq_ref, dck_ref, dcv_ref, dalog_ref, ddtb_ref, dog_ref,
             q_s, k_s, v_s, g_s, b_s, do_s, dS_s):
        h = pl.program_id(0)
        scale = DN_D ** -0.5

        def prep():
            return _dn_prep(pq_ref[...], pk_ref[...], pv_ref[...], pab_ref[...], cq_ref, ck_ref, cv_ref,
                            alog_ref[...], dtb_ref[...], h)

        p = prep()
        q_s[...] = p["qn"] * scale
        k_s[...] = p["kn"]
        v_s[...] = p["v"]
        g_s[...] = p["g"]
        b_s[...] = p["beta"]
        del p

        o = o_ref[...]
        z = pz_ref[...]
        dout = dout_ref[...]
        ogv = og_ref[...]
        ro = lax.rsqrt(jnp.mean(o * o, axis=1, keepdims=True) + EPS)
        on = o * ro
        dy = dout * _silu(z)
        dpz_ref[...] = (dout * (on * ogv) * _dsilu(z)).astype(dpz_ref.dtype)
        dyg = dy * ogv
        do_s[...] = ro * (dyg - on * jnp.mean(dyg * on, axis=1, keepdims=True))
        dog_h = _colsum(dy * on)

        dS_s[...] = jnp.zeros(dS_s.shape, F32)
        mk = _dn_masks()

        nu = min(DN_UNROLL_BWD, N)

        def step(it, carry):
            c0 = (N // nu - 1 - it) * nu
            rows = pl.ds(pl.multiple_of(c0 * DN_C, nu * DN_C), nu * DN_C)
            q, k, v, g, b, do = (r[rows, :].reshape(nu, DN_C, r.shape[1]) for r in (q_s, k_s, v_s, g_s, b_s, do_s))
            loc = _dn_local(q, k, v, g, b, mk)
            Ss = st_ref[0, pl.ds(c0, nu)]
            vnew = loc["U"] - _dot(loc["W"], Ss)
            dS = dS_s[...]
            dS_in, dvnew = [None] * nu, [None] * nu
            for u in reversed(range(nu)):
                dS_in[u] = dS
                dvnew[u], dS = _dn_bwd_chain({n: x[u] for n, x in loc.items()}, do[u], dS)
            dS_s[...] = dS
            grads = _dn_bwd_rest(q, k, v, b, loc, Ss, vnew, do, jnp.stack(dS_in), jnp.stack(dvnew), mk)
            for r, d in zip((q_s, k_s, v_s, g_s, b_s), grads):
                r[rows, :] = d.reshape(nu * DN_C, r.shape[1])
            return carry

        lax.fori_loop(0, N // nu, step, 0)

        p = prep()
        pq, pk, pv = pq_ref[...], pk_ref[...], pv_ref[...]
        dqn = q_s[...] * scale
        dkn = k_s[...]
        qn, kn = p["qn"], p["kn"]
        dsq = p["rq"] * (dqn - qn * _rowsum(dqn * qn))
        dsk = p["rk"] * (dkn - kn * _rowsum(dkn * kn))
        dxq = dsq * _dsilu(p["xq"])
        dxk = dsk * _dsilu(p["xk"])
        dxv = v_s[...] * _dsilu(p["xv"])

        def conv_bwd(dx, x, w_ref, dp_ref, dc_ref):
            acc = dx * w_ref[DN_CONV - 1:DN_CONV, :]
            dc_ref[DN_CONV - 1:DN_CONV, :] = _colsum(dx * x)
            for i in range(DN_CONV - 1):
                s = DN_CONV - 1 - i
                acc = acc + _shift_rows(dx, -s) * w_ref[i:i + 1, :]
                dc_ref[i:i + 1, :] = _colsum(dx * _shift_rows(x, s))
            dp_ref[...] = acc.astype(dp_ref.dtype)

        conv_bwd(dxq, pq, cq_ref, dpq_ref, dcq_ref)
        conv_bwd(dxk, pk, ck_ref, dpk_ref, dck_ref)
        conv_bwd(dxv, pv, cv_ref, dpv_ref, dcv_ref)

        dg = g_s[...]
        beta = p["beta"]
        da_raw = dg * p["neg_ea"] * _sigmoid(p["pre"])
        db_raw = b_s[...] * beta * (1.0 - beta)
        lane = _iota((T, LANES), 1)
        contrib = jnp.where(lane == h, da_raw, 0.0) + jnp.where(lane == DN_H + h, db_raw, 0.0)
        lane1 = _iota((1, LANES), 1)
        dalog_h = jnp.where(lane1 == h, _colsum(dg * p["g"]), 0.0)
        ddtb_h = jnp.where(lane1 == h, _colsum(da_raw), 0.0)

        @pl.when(h == 0)
        def _():
            dpab_ref[...] = contrib.astype(dpab_ref.dtype)
            dalog_ref[...] = dalog_h
            ddtb_ref[...] = ddtb_h
            dog_ref[...] = dog_h

        @pl.when(h > 0)
        def _():
            dpab_ref[...] += contrib.astype(dpab_ref.dtype)
            dalog_ref[...] += dalog_h
            ddtb_ref[...] += ddtb_h
            dog_ref[...] += dog_h

    col = lambda k: pl.BlockSpec((T, DN_D), lambda h: (0, k * DN_H + h))
    cw = lambda k: pl.BlockSpec((DN_CONV, DN_D), lambda h: (0, k * DN_H + h))
    small = pl.BlockSpec((1, LANES), lambda h: (0, 0))
    ab = pl.BlockSpec((T, LANES), lambda h: (0, 4 * DN_H))
    hcol = pl.BlockSpec((T, DN_D), lambda h: (0, h))
    outs = _grid_call(
        body, comm, grid=(DN_H,),
        in_specs=[col(0), col(1), col(2), col(3), ab, cw(0), cw(1), cw(2), small, small, small,
                  hcol, pl.BlockSpec((1, N, DN_D, DN_D), lambda h: (h, 0, 0, 0)), hcol],
        out_specs=[hcol, hcol, hcol, hcol, pl.BlockSpec((T, LANES), lambda h: (0, 0)),
                   pl.BlockSpec((DN_CONV, DN_D), lambda h: (0, h)), pl.BlockSpec((DN_CONV, DN_D), lambda h: (0, h)),
                   pl.BlockSpec((DN_CONV, DN_D), lambda h: (0, h)), small, small, small],
        out_shape=[jax.ShapeDtypeStruct((T, D), BF16)] * 4 + [jax.ShapeDtypeStruct((T, LANES), BF16)]
                  + [jax.ShapeDtypeStruct((DN_CONV, D), F32)] * 3 + [jax.ShapeDtypeStruct((1, LANES), F32)] * 3,
        scratch_shapes=[pltpu.VMEM((T, DN_D), F32)] * 3 + [pltpu.VMEM((T, 1), F32)] * 2
                       + [pltpu.VMEM((T, DN_D), F32), pltpu.VMEM((DN_D, DN_D), F32)],
        name=name, args=(proj, proj, proj, proj, proj, conv_w, conv_w, conv_w, alog, dtb, og, o, states, dout))
    dpq, dpk, dpv, dpz, dpab, dcq, dck, dcv, dalog, ddtb, dog = outs[:11]
    dproj = jnp.concatenate([dpq, dpk, dpv, dpz, dpab], axis=1)
    dconv = jnp.concatenate([dcq, dck, dcv], axis=1)
    return dproj, dconv, dalog, ddtb, dog, (outs[11] if comm else None)


def _sb_head_of(shape):
    return jnp.right_shift(_iota(shape, 1), 6)


def _sb_head_sums(x, head):
    out = jnp.zeros_like(x)
    for hh in range(SB_HPB):
        out = jnp.where(head == hh, _rowsum(jnp.where(head == hh, x, 0.0)), out)
    return out


def _sb_head_norm(x, head):
    r = lax.rsqrt(_sb_head_sums(x * x, head) * (1.0 / SB_D) + EPS)
    return x * r, r


def _sb_fill(q_ref, k_ref, v_ref, qg_ref, kg_ref, qs_s, kn_s, v_s):
    head = _sb_head_of(q_ref.shape)
    qs_s[...] = (_sb_head_norm(q_ref[...], head)[0] * qg_ref[...] * (SB_D ** -0.5)).astype(BF16)
    kn_s[...] = (_sb_head_norm(k_ref[...], head)[0] * kg_ref[...]).astype(BF16)
    v_s[...] = v_ref[...].astype(BF16)


def _sb_head_masks(dtype):
    return jnp.stack([(_sb_head_of((1, SB_W)) == hh).astype(dtype) for hh in range(SB_HPB)])


def _sb_core_fwd(qkv, qg, kg, name, comm=None):
    T = qkv.shape[0]
    B = min(SB_TILE, T)
    NB = T // B
    NP = SB_H // SB_HPB

    def body(q_ref, k_ref, v_ref, qg_ref, kg_ref, o_ref, tot_ref, qs_s, kn_s, v_s):
        _sb_fill(q_ref, k_ref, v_ref, qg_ref, kg_ref, qs_s, kn_s, v_s)
        r, c = _iota((B, B), 0), _iota((B, B), 1)
        causal = c < r
        m_after = (r > c).astype(BF16)
        head_b = _sb_head_of((B, SB_W))
        hm = _sb_head_masks(BF16)

        def tile(qb, kj, vj, R, acc, diag):
            z = _dot(qb, kj, NT)
            sp = _softplus(z)
            ls = z - sp
            lm = jnp.where(causal, -sp, 0.0) if diag else -sp
            cs = _dot2r(lm, m_after) + R
            a = jnp.exp(ls + cs)
            if diag:
                a = jnp.where(causal, a, 0.0)
            return R + _rowsum(lm), acc + jnp.sum(_dot(a, vj), axis=0)

        def qblock(i, carry):
            rows_i = pl.ds(pl.multiple_of(i * B, B), B)
            qb = qs_s[rows_i, :][None] * hm

            def step(rows_j, st, diag):
                return tile(qb, kn_s[rows_j, :], v_s[rows_j, :][None] * hm, st[0], st[1], diag)

            st = step(rows_i, (jnp.zeros((SB_HPB, B, 1), F32), jnp.zeros((B, SB_W), F32)), True)
            st = lax.fori_loop(0, i, lambda s, st: step(pl.ds(pl.multiple_of((i - 1 - s) * B, B), B), st, False), st)
            o_ref[rows_i, :] = st[1].astype(o_ref.dtype)
            tot = jnp.zeros((B, SB_W), F32)
            for hh in range(SB_HPB):
                tot = jnp.where(head_b == hh, st[0][hh], tot)
            tot_ref[rows_i, :] = tot
            return carry

        lax.fori_loop(0, NB, qblock, 0)

    blk = lambda k: pl.BlockSpec((T, SB_W), lambda p: (0, k * NP + p))
    small = pl.BlockSpec((1, SB_W), lambda p: (0, 0))
    return _grid_call(
        body, comm, grid=(NP,), in_specs=[blk(0), blk(1), blk(2), small, small],
        out_specs=[pl.BlockSpec((T, SB_W), lambda p: (0, p))] * 2,
        out_shape=[jax.ShapeDtypeStruct((T, D), BF16), jax.ShapeDtypeStruct((T, D), F32)],
        scratch_shapes=[pltpu.VMEM((T, SB_W), BF16)] * 3, name=name, args=(qkv, qkv, qkv, qg, kg))


def _sb_core_bwd(qkv, qg, kg, tot, dout, name, comm=None):
    T = qkv.shape[0]
    B = min(SB_TILE, T)
    NB = T // B
    NP = SB_H // SB_HPB

    def body(q_ref, k_ref, v_ref, qg_ref, kg_ref, tot_ref, do_ref, dq_ref, dk_ref, dv_ref, dqg_ref, dkg_ref,
             qs_s, kn_s, v_s, dqn_s, dkn_s, dvv_s):
        p = pl.program_id(0)
        _sb_fill(q_ref, k_ref, v_ref, qg_ref, kg_ref, qs_s, kn_s, v_s)
        dkn_s[...] = jnp.zeros(dkn_s.shape, F32)
        dvv_s[...] = jnp.zeros(dvv_s.shape, F32)
        r, c = _iota((B, B), 0), _iota((B, B), 1)
        causal = c < r
        m_upto = (r <= c).astype(BF16)
        m_before = (r < c).astype(BF16)
        lane_b = _iota((B, SB_W), 1)
        hm, hmf = _sb_head_masks(BF16), _sb_head_masks(F32)

        def tile(qb, dob, tot_h, kj, vj, PL, P, dq, diag):
            z = _dot(qb, kj, NT)
            sp = _softplus(z)
            ls = z - sp
            lm = jnp.where(causal, -sp, 0.0) if diag else -sp
            cs = tot_h - PL - _dot2r(lm, m_upto)
            a = jnp.exp(ls + cs)
            if diag:
                a = jnp.where(causal, a, 0.0)
            e = _dot(dob, vj, NT) * a
            E = _dot2r(e, m_before) + P
            sig = jnp.exp(ls)
            dz = e * (1.0 - sig) - E * sig
            if diag:
                dz = jnp.where(causal, dz, 0.0)
            dq = dq + jnp.sum(_dot(dz, kj) * hmf, axis=0)
            return (PL + _rowsum(lm), P + _rowsum(e), dq), jnp.sum(_dot(dz, qb, TN), axis=0), jnp.sum(_dot(a, dob, TN), axis=0)

        def qblock(i, carry):
            rows_i = pl.ds(pl.multiple_of(i * B, B), B)
            totb = tot_ref[rows_i, :]
            tot_h = jnp.stack([_rowsum(jnp.where(lane_b == hh * SB_D, totb, 0.0)) for hh in range(SB_HPB)])
            qb = qs_s[rows_i, :][None] * hm
            dob = do_ref[rows_i, :][None] * hm

            def step(rows_j, st, diag):
                st, dk, dv = tile(qb, dob, tot_h, kn_s[rows_j, :], v_s[rows_j, :][None] * hm, st[0], st[1], st[2], diag)
                dkn_s[rows_j, :] += dk
                dvv_s[rows_j, :] += dv
                return st

            zero = (jnp.zeros((SB_HPB, B, 1), F32), jnp.zeros((SB_HPB, B, 1), F32), jnp.zeros((B, SB_W), F32))
            st = lax.fori_loop(0, i, lambda j, st: step(pl.ds(pl.multiple_of(j * B, B), B), st, False), zero)
            st = step(rows_i, st, True)
            dqn_s[rows_i, :] = st[2] * (SB_D ** -0.5)
            return carry

        lax.fori_loop(0, NB, qblock, 0)

        head = _sb_head_of((T, SB_W))

        def norm_bwd(dn, x_ref, g):
            xh, rr = _sb_head_norm(x_ref[...], head)
            t = dn * g
            return rr * (t - xh * (_sb_head_sums(t * xh, head) * (1.0 / SB_D))), _colsum(dn * xh)

        dq, dqg = norm_bwd(dqn_s[...], q_ref, qg_ref[...])
        dq_ref[...] = dq.astype(dq_ref.dtype)
        dk, dkg = norm_bwd(dkn_s[...], k_ref, kg_ref[...])
        dk_ref[...] = dk.astype(dk_ref.dtype)
        dv_ref[...] = dvv_s[...].astype(dv_ref.dtype)

        @pl.when(p == 0)
        def _():
            dqg_ref[...] = dqg
            dkg_ref[...] = dkg

        @pl.when(p > 0)
        def _():
            dqg_ref[...] += dqg
            dkg_ref[...] += dkg

    blk = lambda k: pl.BlockSpec((T, SB_W), lambda p: (0, k * NP + p))
    small = pl.BlockSpec((1, SB_W), lambda p: (0, 0))
    own = pl.BlockSpec((T, SB_W), lambda p: (0, p))
    outs = _grid_call(
        body, comm, grid=(NP,), in_specs=[blk(0), blk(1), blk(2), small, small, own, own],
        out_specs=[own, own, own, small, small],
        out_shape=[jax.ShapeDtypeStruct((T, D), BF16)] * 3 + [jax.ShapeDtypeStruct((1, SB_W), F32)] * 2,
        scratch_shapes=[pltpu.VMEM((T, SB_W), BF16)] * 3 + [pltpu.VMEM((T, SB_W), F32)] * 3,
        name=name, args=(qkv, qkv, qkv, qg, kg, tot, dout))
    dq, dk, dv, dqg, dkg = outs[:5]
    return jnp.concatenate([dq, dk, dv], axis=1), dqg, dkg, (outs[5] if comm else None)


def _ada_fwd(c16, ada_w, ada_b_cols, name):
    L, _, cols = ada_w.shape

    def body(c_ref, w_ref, b_ref, o_ref):
        o_ref[0] = _dot(_silu(c_ref[...]), w_ref[0]) + b_ref[0]

    return pl.pallas_call(
        body, grid=(L,),
        in_specs=[pl.BlockSpec((16, D), lambda i: (0, 0)), pl.BlockSpec((1, D, cols), lambda i: (i, 0, 0)),
                  pl.BlockSpec((1, 1, cols), lambda i: (i, 0, 0))],
        out_specs=pl.BlockSpec((1, 16, cols), lambda i: (i, 0, 0)),
        out_shape=jax.ShapeDtypeStruct((L, 16, cols), F32), name=name, compiler_params=_params(("arbitrary",)),
    )(c16, ada_w, ada_b_cols)


def _ada_bwd(c16, dmod16, name):
    L, _, cols = dmod16.shape

    def body(c_ref, d_ref, o_ref):
        o_ref[0] = _dot(_silu(c_ref[...]), d_ref[0], TN)

    return pl.pallas_call(
        body, grid=(L,),
        in_specs=[pl.BlockSpec((16, D), lambda i: (0, 0)), pl.BlockSpec((1, 16, cols), lambda i: (i, 0, 0))],
        out_specs=pl.BlockSpec((1, D, cols), lambda i: (i, 0, 0)),
        out_shape=jax.ShapeDtypeStruct((L, D, cols), F32), name=name, compiler_params=_params(("arbitrary",)),
    )(c16, dmod16)


def _sum_sources(x, name):
    n, R, C = x.shape
    tile = _sum_tile(R) or R

    def body(x_ref, o_ref):
        acc = x_ref[0].astype(F32)
        for k in range(1, n):
            acc = acc + x_ref[k].astype(F32)
        o_ref[...] = acc

    return pl.pallas_call(
        body, grid=(R // tile,), in_specs=[pl.BlockSpec((n, tile, C), lambda i: (0, i, 0))],
        out_specs=pl.BlockSpec((tile, C), lambda i: (i, 0)), out_shape=jax.ShapeDtypeStruct((R, C), F32),
        name=name, compiler_params=_params(("parallel",)),
    )(x)


def _adamw(w, g, m, v, name):
    shape = w.shape
    C = shape[-1]
    R = w.size // C
    w2, g2, m2, v2 = (a.reshape(R, C) for a in (w, g, m, v))
    tile = _rtile(R)
    c1 = 1.0 / (1.0 - ADAM_B1 ** ADAM_STEP)
    c2 = 1.0 / (1.0 - ADAM_B2 ** ADAM_STEP)

    def body(w_ref, g_ref, m_ref, v_ref, d_ref, nm_ref, nv_ref):
        gv = g_ref[...]
        nm = ADAM_B1 * m_ref[...] + (1.0 - ADAM_B1) * gv
        nv = ADAM_B2 * v_ref[...] + (1.0 - ADAM_B2) * (gv * gv)
        d_ref[...] = -ADAM_LR * ((nm * c1) / (jnp.sqrt(nv * c2) + ADAM_EPS) + ADAM_WD * w_ref[...])
        nm_ref[...] = nm
        nv_ref[...] = nv

    spec = pl.BlockSpec((tile, C), lambda i: (i, 0))
    outs = pl.pallas_call(
        body, grid=(R // tile,), in_specs=[spec] * 4, out_specs=[spec] * 3,
        out_shape=[jax.ShapeDtypeStruct((R, C), F32)] * 3, name=name, compiler_params=_params(("parallel",)),
    )(w2, g2, m2, v2)
    return tuple(o.reshape(shape) for o in outs)


def _gather_steps(x_ref, out_ref, send_sems, recv_sems, local_sem):
    mx, my, mc = lax.axis_index("x"), lax.axis_index("y"), lax.axis_index("c")
    me, sibling = (mx, my, mc), (mx, my, 1 - mc)
    chips = [(1 - mx, my), (mx, 1 - my), (1 - mx, 1 - my)]

    def slot(px, py, pc):
        return out_ref.at[4 * px + 2 * py + pc]

    def copy(k, block, to, src=None):
        return pltpu.make_async_remote_copy(
            src_ref=slot(*block) if src is None else src, dst_ref=slot(*block),
            send_sem=send_sems.at[k], recv_sem=recv_sems.at[k], device_id=to, device_id_type=MESH)

    mine = pltpu.make_async_copy(x_ref, slot(*me), local_sem)
    first = [copy(0, me, sibling, src=x_ref)] + [copy(1 + j, me, (*chip, mc), src=x_ref) for j, chip in enumerate(chips)]
    passed = [copy(4 + j, (*chip, mc), sibling) for j, chip in enumerate(chips)]

    def start():
        mine.start()
        for cp in first:
            cp.start()

    def forward():
        for j, chip in enumerate(chips):
            copy(1 + j, (*chip, mc), me).wait_recv()
            passed[j].start()

    def finish():
        copy(0, sibling, me).wait_recv()
        for j, chip in enumerate(chips):
            copy(4 + j, (*chip, 1 - mc), me).wait_recv()
        for cp in first + passed:
            cp.wait_send()
        mine.wait()

    return start, forward, finish


def _exchange_steps(x_ref, out_ref, send_sems, recv_sems, local_sem):
    mx, my, mc = lax.axis_index("x"), lax.axis_index("y"), lax.axis_index("c")
    me = 4 * mx + 2 * my + mc
    mine = pltpu.make_async_copy(x_ref.at[me], out_ref.at[me], local_sem)
    copies = []
    for k in range(1, NDEV):
        px, py, pc = mx ^ (k >> 2), my ^ ((k >> 1) & 1), mc ^ (k & 1)
        copies.append(pltpu.make_async_remote_copy(
            src_ref=x_ref.at[4 * px + 2 * py + pc], dst_ref=out_ref.at[me], send_sem=send_sems.at[k - 1],
            recv_sem=recv_sems.at[k - 1], device_id=(px, py, pc), device_id_type=MESH))

    def start():
        mine.start()
        for cp in copies:
            cp.start()

    def finish():
        for cp in copies:
            cp.wait_recv()
        for cp in copies:
            cp.wait_send()
        mine.wait()

    return start, finish


def _comm_sems():
    return [pltpu.SemaphoreType.DMA((7,)), pltpu.SemaphoreType.DMA((7,)), pltpu.SemaphoreType.DMA]


def _comm_out_shape(comm):
    kind, x = comm
    return jax.ShapeDtypeStruct(((NDEV,) + x.shape) if kind == "gather" else x.shape, x.dtype)


def _ride_along(comm, step, n_steps, refs, at_end):
    if comm[0] == "gather":
        start, forward, finish = _gather_steps(*refs)
        todo = [(n_steps - 1, finish)] if at_end else [(0, start), (n_steps // 2, forward)]
    else:
        start, finish = _exchange_steps(*refs)
        todo = [(n_steps - 1, finish)] if at_end else [(0, start)]
    for at, fn in todo:
        pl.when(step == at)(fn)


def _grid_call(body, comm, *, grid, in_specs, out_specs, out_shape, scratch_shapes, name, args):
    if comm is None:
        return pl.pallas_call(body, grid=grid, in_specs=in_specs, out_specs=out_specs, out_shape=out_shape,
                              scratch_shapes=scratch_shapes, name=name, compiler_params=_params(("arbitrary",)))(*args)
    n_in, n_out, n_steps = len(in_specs), len(out_specs), grid[0]

    def with_comm(*refs):
        ins, outs, scr = refs[:n_in], refs[n_in + 1:n_in + 1 + n_out], refs[n_in + 2 + n_out:-3]
        comm_refs = (refs[n_in], refs[n_in + 1 + n_out]) + refs[-3:]
        step = pl.program_id(0)
        _ride_along(comm, step, n_steps, comm_refs, False)
        body(*ins, *outs, *scr)
        _ride_along(comm, step, n_steps, comm_refs, True)

    hbm = pl.BlockSpec(memory_space=pl.ANY)
    return pl.pallas_call(
        with_comm, grid=grid, in_specs=list(in_specs) + [hbm], out_specs=list(out_specs) + [hbm],
        out_shape=list(out_shape) + [_comm_out_shape(comm)], scratch_shapes=list(scratch_shapes) + _comm_sems(),
        name=name, compiler_params=_params(("arbitrary",)))(*args, comm[1])


def _all_gather(x, name, in_vmem):
    def body(x_ref, out_ref, send_sems, recv_sems, local_sem):
        start, forward, finish = _gather_steps(x_ref, out_ref, send_sems, recv_sems, local_sem)
        start()
        forward()
        finish()

    space = pltpu.VMEM if in_vmem else pl.ANY
    return pl.pallas_call(
        body, out_shape=jax.ShapeDtypeStruct((NDEV,) + x.shape, x.dtype),
        in_specs=[pl.BlockSpec(memory_space=space)], out_specs=pl.BlockSpec(memory_space=space),
        scratch_shapes=_comm_sems(), name=name, compiler_params=pltpu.CompilerParams(vmem_limit_bytes=VMEM_LIMIT),
    )(x)


def _all_to_all(x, name):
    def body(x_ref, out_ref, send_sems, recv_sems, local_sem):
        start, finish = _exchange_steps(x_ref, out_ref, send_sems, recv_sems, local_sem)
        start()
        finish()

    return pl.pallas_call(
        body, out_shape=jax.ShapeDtypeStruct(x.shape, x.dtype),
        in_specs=[pl.BlockSpec(memory_space=pl.ANY)], out_specs=pl.BlockSpec(memory_space=pl.ANY),
        scratch_shapes=_comm_sems(), name=name, compiler_params=pltpu.CompilerParams(vmem_limit_bytes=VMEM_LIMIT),
    )(x)


def _stage_parts(st):
    parts = []
    if st >= 1:
        parts += [("ffn_w_in", st - 1, (D, 2 * DFF // NDEV), True), ("ffn_w_out", st - 1, (DFF // NDEV, D), False)]
    if st < DEPTH:
        j = st // 2
        if st % 2 == 0:
            parts += [("dn_w_in", j, (D, DN_COLS // NDEV), True), ("dn_w_out", j, (D // NDEV, D), False)]
        else:
            parts += [("sb_w_qkv", j, (D, 3 * D // NDEV), True), ("sb_w_out", j, (D // NDEV, D), False)]
    return parts


def _sum_tile(rows):
    for t in range(512, 191, -16):
        if rows % t == 0:
            return t
    return None


def _stage_layout(st):
    out, r = [], 0
    for n, j, s, by_cols in _stage_parts(st):
        k = s[0] * s[1] // D
        kp = -(-k // 16) * 16
        out.append((n, j, s, by_cols, k, kp, r))
        r += kp
    while _sum_tile(r) is None:
        r += 16
    return out, r


def _pack_stage(w, st):
    parts, rows = _stage_layout(st)
    bufs = [jnp.pad(w[n][j].astype(BF16).reshape(k, D), ((0, kp - k), (0, 0))) for n, j, _, _, k, kp, _ in parts]
    used = parts[-1][6] + parts[-1][5]
    if rows > used:
        bufs.append(jnp.zeros((rows - used, D), BF16))
    return jnp.concatenate(bufs, axis=0)


def _full_stage(g, st):
    out = {}
    for n, _, s, by_cols, k, _, r in _stage_layout(st)[0]:
        a = g[:, r:r + k].reshape((NDEV,) + s)
        out[n] = jnp.transpose(a, (1, 0, 2)).reshape(s[0], NDEV * s[1]) if by_cols else a.reshape(NDEV * s[0], s[1])
    if "dn_w_in" in out:
        out["dn_w_in"] = jnp.pad(out["dn_w_in"], ((0, 0), (0, DN_COLS_PAD - DN_COLS)))
    return out


def _shards_stage(full, st):
    parts, rows = _stage_layout(st)
    bufs = []
    for n, _, s, by_cols, k, kp, _ in parts:
        a = full[n].astype(BF16)
        a = jnp.transpose(a.reshape(s[0], NDEV, s[1]), (1, 0, 2)) if by_cols else a.reshape(NDEV, s[0], s[1])
        bufs.append(jnp.pad(a.reshape(NDEV, k, D), ((0, 0), (0, kp - k), (0, 0))))
    used = parts[-1][6] + parts[-1][5]
    if rows > used:
        bufs.append(jnp.zeros((NDEV, rows - used, D), BF16))
    return jnp.concatenate(bufs, axis=1)


def _unpack_stage(buf, st):
    return {(n, j): buf[r:r + k].reshape(s) for n, j, s, _, k, _, r in _stage_layout(st)[0]}


def _local_step(x, target, mod, small, w0, rest):
    dist = isinstance(rest, tuple)
    packs = rest[1] if dist else None
    W = [dict(w0), {}, {}, {}] if dist else [w0] + list(rest)

    def arrived(k, g):
        for n, a in _full_stage(g, k + 1).items():
            W[k if n.startswith("ffn") else k + 1][n] = a

    row = lambda v: v.reshape(1, -1)
    pad128 = lambda v: jnp.pad(v.reshape(1, -1), ((0, 0), (0, LANES - v.size)))
    saved = []
    for i in range(DEPTH):
        j = i // 2
        w = W[i]
        m = [row(mod[i, k * D:(k + 1) * D]) for k in range(N_MOD)]
        sh1, sc1, gt1, sh2, sc2, gt2 = m
        g1, g2 = row(small["norm1_g"][i]), row(small["norm2_g"][i])
        h1 = _norm_mod(x, g1, sc1, sh1, f"norm1_{i}")
        if i % 2 == 0:
            proj = _mm(h1, w["dn_w_in"], "nn", F32, f"dn_proj_{i}")
            alog, dtb, og = pad128(small["dn_a_log"][j]), pad128(small["dn_dt_bias"][j]), row(small["dn_onorm_g"][j])
            comm = ("gather", packs[i]) if dist else None
            res = _dn_core_fwd(proj, small["dn_conv_w"][j], alog, dtb, og, f"dn_core_{i}", comm)
            om, o_pre, states = res[:3]
            if comm:
                arrived(i, res[3])
            y1 = _mm(om, w["dn_w_out"], "nn", F32, f"dn_out_{i}")
            mix = (proj, alog, dtb, og, o_pre, states, om)
        else:
            qkv = _mm(h1, w["sb_w_qkv"], "nn", F32, f"sb_qkv_{i}")
            qg2 = jnp.tile(row(small["sb_q_norm_g"][j]), (1, SB_HPB))
            kg2 = jnp.tile(row(small["sb_k_norm_g"][j]), (1, SB_HPB))
            comm = ("gather", packs[i]) if dist else None
            res = _sb_core_fwd(qkv, qg2, kg2, f"sb_core_{i}", comm)
            om, tot = res[:2]
            if comm:
                arrived(i, res[2])
            y1 = _mm(om, w["sb_w_out"], "nn", F32, f"sb_out_{i}")
            mix = (qkv, qg2, kg2, tot, om)
        x_mid = _gate_res(x, y1, gt1, f"res1_{i}")
        h2 = _norm_mod(x_mid, g2, sc2, sh2, f"norm2_{i}")
        u = _mm(h2, w["ffn_w_in"], "nn", F32, f"ffn_in_{i}")
        a = _swiglu_act(u, f"ffn_act_{i}")
        y2 = _mm(a, w["ffn_w_out"], "nn", F32, f"ffn_out_{i}")
        x_out = _gate_res(x_mid, y2, gt2, f"res2_{i}")
        saved.append((x, h1, mix, y1, x_mid, h2, u, a, y2, m, g1, g2))
        x = x_out

    dx, sq = _loss_head(x, target, "loss_head")

    sg = dict(mod=[None] * DEPTH, norm1_g=[None] * DEPTH, norm2_g=[None] * DEPTH, dn_conv_w=[None] * 2,
              dn_a_log=[None] * 2, dn_dt_bias=[None] * 2, dn_onorm_g=[None] * 2, sb_q_norm_g=[None] * 2,
              sb_k_norm_g=[None] * 2)
    big = [None] * (DEPTH + 1 if dist else DEPTH)
    mixer_gw = {}
    for i in reversed(range(DEPTH)):
        j = i // 2
        w = W[i]
        x_in, h1, mix, y1, x_mid, h2, u, a, y2, m, g1, g2 = saved[i]
        sh1, sc1, gt1, sh2, sc2, gt2 = m
        gw = {}
        dy2, dgt2 = _gate_res_bwd(dx, y2, gt2, f"res2_bwd_{i}")
        da = _mm(dy2, w["ffn_w_out"], "nt", F32, f"ffn_out_bwd_{i}")
        gw["ffn_w_out"] = _mm(a, dy2, "tn", F32, f"ffn_out_wg_{i}")
        du = _swiglu_act_bwd(da, u, f"ffn_act_bwd_{i}")
        dh2 = _mm(du, w["ffn_w_in"], "nt", F32, f"ffn_in_bwd_{i}")
        gw["ffn_w_in"] = _mm(h2, du, "tn", F32, f"ffn_in_wg_{i}")
        dx_mid, s2, dsh2 = _norm_mod_bwd(dh2, x_mid, dx, g2, sc2, f"norm2_bwd_{i}")
        dy1, dgt1 = _gate_res_bwd(dx_mid, y1, gt1, f"res1_bwd_{i}")
        comm = ("exchange", _shards_stage({**gw, **mixer_gw}, i + 1)) if dist else None
        if i % 2 == 0:
            proj, alog, dtb, og, o_pre, states, om = mix
            dom = _mm(dy1, w["dn_w_out"], "nt", F32, f"dn_out_bwd_{i}")
            gw["dn_w_out"] = _mm(om, dy1, "tn", F32, f"dn_out_wg_{i}")
            dproj, dconv, dalog, ddtb, dog, got = _dn_core_bwd(proj, small["dn_conv_w"][j], alog, dtb, og, o_pre, states,
                                                               dom, f"dn_core_bwd_{i}", comm)
            dh1 = _mm(dproj, w["dn_w_in"], "nt", F32, f"dn_proj_bwd_{i}")
            gw["dn_w_in"] = _mm(h1, dproj, "tn", F32, f"dn_proj_wg_{i}")[:, :DN_COLS]
            sg["dn_conv_w"][j] = dconv
            sg["dn_a_log"][j] = dalog[0, :DN_H]
            sg["dn_dt_bias"][j] = ddtb[0, :DN_H]
            sg["dn_onorm_g"][j] = dog[0]
        else:
            qkv, qg2, kg2, tot, om = mix
            dom = _mm(dy1, w["sb_w_out"], "nt", BF16, f"sb_out_bwd_{i}")
            gw["sb_w_out"] = _mm(om, dy1, "tn", F32, f"sb_out_wg_{i}")
            dqkv, dqg, dkg, got = _sb_core_bwd(qkv, qg2, kg2, tot, dom, f"sb_core_bwd_{i}", comm)
            dh1 = _mm(dqkv, w["sb_w_qkv"], "nt", F32, f"sb_qkv_bwd_{i}")
            gw["sb_w_qkv"] = _mm(h1, dqkv, "tn", F32, f"sb_qkv_wg_{i}")
            sg["sb_q_norm_g"][j] = jnp.sum(dqg.reshape(SB_HPB, SB_D), axis=0)
            sg["sb_k_norm_g"][j] = jnp.sum(dkg.reshape(SB_HPB, SB_D), axis=0)
        if comm:
            big[i + 1] = got
        dx, s1, dsh1 = _norm_mod_bwd(dh1, x_in, dx_mid, g1, sc1, f"norm1_bwd_{i}")
        sg["mod"][i] = jnp.concatenate([dsh1, s1 * g1, dgt1, dsh2, s2 * g2, dgt2], axis=1)[0]
        sg["norm1_g"][i] = (s1 * (1.0 + sc1))[0]
        sg["norm2_g"][i] = (s2 * (1.0 + sc2))[0]
        if dist:
            mixer_gw = {n: a for n, a in gw.items() if not n.startswith("ffn")}
        else:
            big[i] = gw
    if dist:
        big[0] = _all_to_all(_shards_stage(mixer_gw, 0), "exchange_grads_0")
    return sq, dx, {k: jnp.stack(v) for k, v in sg.items()}, big


def _device_index():
    return 4 * lax.axis_index("x") + 2 * lax.axis_index("y") + lax.axis_index("c")


def _gather_small(w, c):
    me = _device_index()
    ada_cols = w["ada_w"].shape[-1]
    conv_cols = w["dn_conv_w"].shape[-1]
    blk = jnp.concatenate([c.reshape(1, D), w["dn_conv_w"].reshape(-1, D)], axis=0)
    g1 = _all_gather(blk, "gather_cond", True)
    c16 = jnp.pad(g1[:, 0, :], ((0, 8), (0, 0)))
    conv_full = jnp.transpose(g1[:, 1:, :].reshape(NDEV, 2, DN_CONV, conv_cols), (1, 2, 0, 3)).reshape(2, DN_CONV, -1)
    b_cols = lax.dynamic_slice_in_dim(w["ada_b"], me * ada_cols, ada_cols, axis=1).reshape(DEPTH, 1, ada_cols)
    mod_part = _ada_fwd(c16, w["ada_w"], b_cols, "ada_fwd")[:, :NDEV, :]
    g2 = _all_gather(mod_part.reshape(DEPTH * NDEV, ada_cols), "gather_mod", True)
    g2 = g2.reshape(NDEV, DEPTH, NDEV, ada_cols)
    mod = lax.dynamic_index_in_dim(g2, me, axis=2, keepdims=False)
    mod = jnp.transpose(mod, (1, 0, 2)).reshape(DEPTH, N_MOD * D)
    return c16, conv_full, mod


def _reduce_small(gr, c16):
    me = _device_index()
    ada_cols = N_MOD * D // NDEV
    conv_cols = 3 * DN_H * DN_D // NDEV
    grads = {}
    small = jnp.concatenate([gr["dn_a_log"].reshape(-1), gr["dn_dt_bias"].reshape(-1), gr["dn_onorm_g"].reshape(-1),
                             gr["sb_q_norm_g"].reshape(-1), gr["sb_k_norm_g"].reshape(-1)])
    small = jnp.pad(small, (0, D - small.size)).reshape(1, D)
    blk3 = jnp.concatenate([gr["mod"].reshape(-1, D), gr["norm1_g"], gr["norm2_g"], gr["dn_conv_w"].reshape(-1, D),
                            small], axis=0)
    blk3 = jnp.pad(blk3, ((0, 64 - blk3.shape[0]), (0, 0)))
    g3 = _all_gather(blk3, "gather_small_grads", True)
    tot = _sum_sources(g3, "sum_small_grads")
    grads["ada_b"] = tot[:24].reshape(DEPTH, N_MOD * D)
    grads["norm1_g"] = tot[24:28]
    grads["norm2_g"] = tot[28:32]
    conv_g = tot[32:56].reshape(2, DN_CONV, NDEV * conv_cols)
    grads["dn_conv_w"] = lax.dynamic_slice_in_dim(conv_g, me * conv_cols, conv_cols, axis=2)
    sm = tot[56]
    grads["dn_a_log"] = sm[0:16].reshape(2, DN_H)
    grads["dn_dt_bias"] = sm[16:32].reshape(2, DN_H)
    grads["dn_onorm_g"] = sm[32:288].reshape(2, DN_D)
    grads["sb_q_norm_g"] = sm[288:416].reshape(2, SB_D)
    grads["sb_k_norm_g"] = sm[416:544].reshape(2, SB_D)
    dmod_all = g3[:, :24, :].reshape(NDEV, DEPTH, N_MOD * D)
    dmod_cols = lax.dynamic_slice_in_dim(dmod_all, me * ada_cols, ada_cols, axis=2)
    dmod16 = jnp.pad(jnp.transpose(dmod_cols, (1, 0, 2)), ((0, 0), (0, 8), (0, 0)))
    grads["ada_w"] = _ada_bwd(c16, dmod16, "ada_bwd")
    return grads


def _reduce_big(recv):
    parts = {}
    for st in range(DEPTH + 1):
        parts.update(_unpack_stage(_sum_sources(recv[st], f"sum_grads_{st}"), st))
    out = {}
    for (n, j) in sorted(parts):
        out.setdefault(n, []).append(parts[(n, j)])
    return {n: jnp.stack(v) for n, v in out.items()}


def kernel(x, c, ada_w, ada_b, norm1_g, norm2_g, dn_w_in, dn_conv_w, dn_a_log, dn_dt_bias, dn_onorm_g, dn_w_out, sb_w_qkv, sb_q_norm_g, sb_k_norm_g, sb_w_out, ffn_w_in, ffn_w_out, loss_target, m_ada_w, m_ada_b, m_norm1_g, m_norm2_g, m_dn_w_in, m_dn_conv_w, m_dn_a_log, m_dn_dt_bias, m_dn_onorm_g, m_dn_w_out, m_sb_w_qkv, m_sb_q_norm_g, m_sb_k_norm_g, m_sb_w_out, m_ffn_w_in, m_ffn_w_out, v_ada_w, v_ada_b, v_norm1_g, v_norm2_g, v_dn_w_in, v_dn_conv_w, v_dn_a_log, v_dn_dt_bias, v_dn_onorm_g, v_dn_w_out, v_sb_w_qkv, v_sb_q_norm_g, v_sb_k_norm_g, v_sb_w_out, v_ffn_w_in, v_ffn_w_out):
    w = dict(ada_w=ada_w, ada_b=ada_b, norm1_g=norm1_g, norm2_g=norm2_g, dn_w_in=dn_w_in, dn_conv_w=dn_conv_w,
             dn_a_log=dn_a_log, dn_dt_bias=dn_dt_bias, dn_onorm_g=dn_onorm_g, dn_w_out=dn_w_out, sb_w_qkv=sb_w_qkv,
             sb_q_norm_g=sb_q_norm_g, sb_k_norm_g=sb_k_norm_g, sb_w_out=sb_w_out, ffn_w_in=ffn_w_in, ffn_w_out=ffn_w_out)
    mom = dict(ada_w=m_ada_w, ada_b=m_ada_b, norm1_g=m_norm1_g, norm2_g=m_norm2_g, dn_w_in=m_dn_w_in,
               dn_conv_w=m_dn_conv_w, dn_a_log=m_dn_a_log, dn_dt_bias=m_dn_dt_bias, dn_onorm_g=m_dn_onorm_g,
               dn_w_out=m_dn_w_out, sb_w_qkv=m_sb_w_qkv, sb_q_norm_g=m_sb_q_norm_g, sb_k_norm_g=m_sb_k_norm_g,
               sb_w_out=m_sb_w_out, ffn_w_in=m_ffn_w_in, ffn_w_out=m_ffn_w_out)
    var = dict(ada_w=v_ada_w, ada_b=v_ada_b, norm1_g=v_norm1_g, norm2_g=v_norm2_g, dn_w_in=v_dn_w_in,
               dn_conv_w=v_dn_conv_w, dn_a_log=v_dn_a_log, dn_dt_bias=v_dn_dt_bias, dn_onorm_g=v_dn_onorm_g,
               dn_w_out=v_dn_w_out, sb_w_qkv=v_sb_w_qkv, sb_q_norm_g=v_sb_q_norm_g, sb_k_norm_g=v_sb_k_norm_g,
               sb_w_out=v_sb_w_out, ffn_w_in=v_ffn_w_in, ffn_w_out=v_ffn_w_out)
    names = list(w)
    c16, conv_full, mod = _gather_small(w, c)
    packs = [_pack_stage(w, st) for st in range(DEPTH + 1)]
    w0 = _full_stage(_all_gather(packs[0], "gather_weights_0", False), 0)
    small = dict(norm1_g=norm1_g, norm2_g=norm2_g, dn_conv_w=conv_full, dn_a_log=dn_a_log, dn_dt_bias=dn_dt_bias,
                 dn_onorm_g=dn_onorm_g, sb_q_norm_g=sb_q_norm_g, sb_k_norm_g=sb_k_norm_g)
    sq, grad_x, sgr, recv = _local_step(x[0], loss_target[0], mod, small, w0, ("packed", packs[1:]))
    loss = lax.psum(sq[0, 0] * (0.5 / D), ("x", "y", "c"))
    grads = {**_reduce_big(recv), **_reduce_small(sgr, c16)}
    delta, new_m, new_v = {}, {}, {}
    for n in names:
        delta[n], new_m[n], new_v[n] = _adamw(w[n], grads[n], mom[n], var[n], f"adamw_{n}")
    return (loss, grad_x[None], *[grads[n] for n in names], *[delta[n] for n in names],
            *[new_m[n] for n in names], *[new_v[n] for n in names])
```

```python
import jax
import jax.numpy as jnp
from jax import lax
from jax.experimental import pallas as pl
from jax.experimental.pallas import tpu as pltpu

F32, BF16 = jnp.float32, jnp.bfloat16

D = 1024
DEPTH = 4
N_MOD = 6
DN_H, DN_D, DN_C, DN_CONV = 8, 128, 64, 4
DN_UNROLL_FWD, DN_UNROLL_BWD = 16, 8
DN_COLS = 4 * DN_H * DN_D + 2 * DN_H
DN_COLS_PAD = 33 * 128
SB_H, SB_D = 16, 64
SB_TILE = 256
SB_HPB = 4
SB_W = SB_HPB * SB_D
DFF = 2816
EPS = 1e-6
NDEV = 8
LANES = 128
VMEM_LIMIT = 56 * 1024 * 1024

ADAM_LR, ADAM_B1, ADAM_B2, ADAM_EPS, ADAM_WD, ADAM_STEP = 0.001, 0.9, 0.999, 1e-08, 0.01, 10

NN = ((1,), (0,))
NT = ((1,), (1,))
TN = ((0,), (0,))
MESH = pl.DeviceIdType.MESH


def _dot(a, b, dims=NN):
    a, b = a.astype(BF16), b.astype(BF16)
    if a.ndim == 2 and b.ndim == 2:
        return lax.dot_general(a, b, (dims, ((), ())), preferred_element_type=F32)
    n = a.shape[0] if a.ndim == 3 else b.shape[0]
    if a.ndim == 2:
        a = jnp.broadcast_to(a, (n,) + a.shape)
    if b.ndim == 2:
        b = jnp.broadcast_to(b, (n,) + b.shape)
    (ca,), (cb,) = dims
    return lax.dot_general(a, b, (((ca + 1,), (cb + 1,)), ((0,), (0,))), preferred_element_type=F32)


def _split(a):
    hi = a.astype(BF16)
    lo = (a - hi.astype(F32)).astype(BF16)
    return hi, lo


def _dot2r(a, m, dims=NN):
    ah, al = _split(a)
    return _dot(ah, m, dims) + _dot(al, m, dims)


def _dot2l(m, b, dims=NN):
    bh, bl = _split(b)
    return _dot(m, bh, dims) + _dot(m, bl, dims)


def _sigmoid(x):
    return 1.0 / (1.0 + jnp.exp(-x))


def _silu(x):
    return x * _sigmoid(x)


def _dsilu(x):
    s = _sigmoid(x)
    return s * (1.0 + x * (1.0 - s))


def _softplus(x):
    return jnp.maximum(x, 0.0) + jnp.log(1.0 + jnp.exp(-jnp.abs(x)))


def _iota(shape, dim):
    return lax.broadcasted_iota(jnp.int32, shape, dim)


def _rowsum(x):
    return jnp.sum(x, axis=-1, keepdims=True)


def _colsum(x):
    return jnp.sum(x, axis=-2, keepdims=True)


def _tile(n, pref):
    if n <= pref:
        return n
    best = None
    for t in range(LANES, pref + 1, LANES):
        if n % t == 0:
            best = t
    assert best is not None, (n, pref)
    return best


def _rtile(r, pref=512):
    best = None
    for t in range(8, min(r, pref) + 1, 8):
        if r % t == 0:
            best = t
    return best if best is not None else r


def _params(sem):
    return pltpu.CompilerParams(dimension_semantics=sem, vmem_limit_bytes=VMEM_LIMIT)


MM_TILE = 1408


def _mm(a, b, mode, out_dtype, name):
    if mode == "nn":
        (M, K), (K2, N) = a.shape, b.shape
        dims = NN
    elif mode == "nt":
        (M, K), (N, K2) = a.shape, b.shape
        dims = NT
    else:
        (K, M), (K2, N) = a.shape, b.shape
        dims = TN
    assert K == K2, (a.shape, b.shape, mode)
    tm, tn, tk = _tile(M, MM_TILE), _tile(N, MM_TILE), _tile(K, MM_TILE)
    nk = K // tk

    def body_single(a_ref, b_ref, o_ref):
        o_ref[...] = _dot(a_ref[...], b_ref[...], dims).astype(o_ref.dtype)

    def body_acc(a_ref, b_ref, o_ref, acc_ref):
        k = pl.program_id(2)
        p = _dot(a_ref[...], b_ref[...], dims)

        @pl.when(k == 0)
        def _():
            acc_ref[...] = p

        @pl.when(k > 0)
        def _():
            acc_ref[...] += p

        @pl.when(k == nk - 1)
        def _():
            o_ref[...] = acc_ref[...].astype(o_ref.dtype)

    if mode == "nn":
        a_spec = pl.BlockSpec((tm, tk), lambda i, j, k: (i, k))
        b_spec = pl.BlockSpec((tk, tn), lambda i, j, k: (k, j))
    elif mode == "nt":
        a_spec = pl.BlockSpec((tm, tk), lambda i, j, k: (i, k))
        b_spec = pl.BlockSpec((tn, tk), lambda i, j, k: (j, k))
    else:
        a_spec = pl.BlockSpec((tk, tm), lambda i, j, k: (k, i))
        b_spec = pl.BlockSpec((tk, tn), lambda i, j, k: (k, j))
    return pl.pallas_call(
        body_single if nk == 1 else body_acc, grid=(M // tm, N // tn, nk), in_specs=[a_spec, b_spec],
        out_specs=pl.BlockSpec((tm, tn), lambda i, j, k: (i, j)),
        out_shape=jax.ShapeDtypeStruct((M, N), out_dtype),
        scratch_shapes=[] if nk == 1 else [pltpu.VMEM((tm, tn), F32)], name=name,
        compiler_params=_params(("parallel", "parallel", "arbitrary")),
    )(a, b)


def _rowwise(fn, name, rows, bcasts, out_rows, out_reds=(), tile=256):
    T = rows[0].shape[0]
    tile = min(tile, T)
    nr, nb, no = len(rows), len(bcasts), len(out_rows)

    def body(*refs):
        rv = [r[...] for r in refs[:nr]]
        bv = [r[...] for r in refs[nr:nr + nb]]
        outs, reds = fn(rv, bv)
        for r, o in zip(refs[nr + nb:nr + nb + no], outs):
            r[...] = o.astype(r.dtype)
        red_refs = refs[nr + nb + no:]
        if red_refs:
            @pl.when(pl.program_id(0) == 0)
            def _():
                for r in red_refs:
                    r[...] = jnp.zeros(r.shape, F32)

            for r, v in zip(red_refs, reds):
                r[...] += v

    in_specs = [pl.BlockSpec((tile, a.shape[1]), lambda i: (i, 0)) for a in rows]
    in_specs += [pl.BlockSpec(b.shape, lambda i: (0, 0)) for b in bcasts]
    out_specs = [pl.BlockSpec((tile, c), lambda i: (i, 0)) for c, _ in out_rows]
    out_specs += [pl.BlockSpec(s, lambda i: (0, 0)) for s in out_reds]
    out_shape = [jax.ShapeDtypeStruct((T, c), dt) for c, dt in out_rows]
    out_shape += [jax.ShapeDtypeStruct(s, F32) for s in out_reds]
    return pl.pallas_call(
        body, grid=(T // tile,), in_specs=in_specs, out_specs=out_specs, out_shape=out_shape, name=name,
        compiler_params=_params(("arbitrary",)),
    )(*rows, *bcasts)


def _norm_mod(x, g, sc, sh, name):
    def fn(rv, bv):
        (xv,), (gv, scv, shv) = rv, bv
        r = lax.rsqrt(jnp.mean(xv * xv, axis=1, keepdims=True) + EPS)
        return [(xv * r * gv) * (1.0 + scv) + shv], []
    return _rowwise(fn, name, [x], [g, sc, sh], [(D, BF16)])[0]


def _norm_mod_bwd(dh, x, dres, g, sc, name):
    def fn(rv, bv):
        (dhv, xv, drv), (gv, scv) = rv, bv
        r = lax.rsqrt(jnp.mean(xv * xv, axis=1, keepdims=True) + EPS)
        xhat = xv * r
        dxhat = dhv * (gv * (1.0 + scv))
        dx = r * (dxhat - xhat * jnp.mean(dxhat * xhat, axis=1, keepdims=True)) + drv
        return [dx], [_colsum(dhv * xhat), _colsum(dhv)]
    return _rowwise(fn, name, [dh, x, dres], [g, sc], [(D, F32)], [(1, D), (1, D)])


def _gate_res(x, y, gt, name):
    def fn(rv, bv):
        return [rv[0] + bv[0] * rv[1]], []
    return _rowwise(fn, name, [x, y], [gt], [(D, F32)])[0]


def _gate_res_bwd(dxn, y, gt, name):
    def fn(rv, bv):
        return [rv[0] * bv[0]], [_colsum(rv[0] * rv[1])]
    return _rowwise(fn, name, [dxn, y], [gt], [(D, BF16)], [(1, D)])


def _swiglu_act(u, name):
    def fn(rv, bv):
        uv = rv[0].astype(F32)
        return [_silu(uv[:, :DFF]) * uv[:, DFF:]], []
    return _rowwise(fn, name, [u], [], [(DFF, BF16)])[0]


def _swiglu_act_bwd(da, u, name):
    def fn(rv, bv):
        dav, uv = rv[0].astype(F32), rv[1].astype(F32)
        gate, up = uv[:, :DFF], uv[:, DFF:]
        return [jnp.concatenate([dav * up * _dsilu(gate), dav * _silu(gate)], axis=1)], []
    return _rowwise(fn, name, [da, u], [], [(2 * DFF, BF16)])[0]


def _loss_head(y, target, name):
    def fn(rv, bv):
        err = rv[0] - rv[1]
        return [err * (1.0 / D)], [_colsum(_rowsum(err * err))]
    return _rowwise(fn, name, [y, target], [], [(D, F32)], [(1, 1)])


def _shift_rows(x, s):
    if s == 0:
        return x
    T = x.shape[0]
    r = pltpu.roll(x, s % T, axis=0)
    t = _iota(x.shape, 0)
    keep = (t >= s) if s > 0 else (t < T + s)
    return jnp.where(keep, r, 0.0)


def _dn_prep(pq, pk, pv, pab, cq_ref, ck_ref, cv_ref, alog, dtb, h):
    lane = _iota(pab.shape, 1)
    a_col = _rowsum(jnp.where(lane == h, pab, 0.0))
    b_col = _rowsum(jnp.where(lane == DN_H + h, pab, 0.0))
    lane1 = _iota(alog.shape, 1)
    alog_h = _rowsum(jnp.where(lane1 == h, alog, 0.0))
    dtb_h = _rowsum(jnp.where(lane1 == h, dtb, 0.0))
    pre = a_col + dtb_h
    neg_ea = -jnp.exp(alog_h)
    g = neg_ea * _softplus(pre)
    beta = _sigmoid(b_col)

    def conv(x, w_ref):
        acc = x * w_ref[DN_CONV - 1:DN_CONV, :]
        for i in range(DN_CONV - 1):
            acc = acc + _shift_rows(x, DN_CONV - 1 - i) * w_ref[i:i + 1, :]
        return acc

    xq, xk, xv = conv(pq, cq_ref), conv(pk, ck_ref), conv(pv, cv_ref)
    sq, sk, v = _silu(xq), _silu(xk), _silu(xv)
    rq = lax.rsqrt(_rowsum(sq * sq) + EPS)
    rk = lax.rsqrt(_rowsum(sk * sk) + EPS)
    return dict(g=g, beta=beta, pre=pre, neg_ea=neg_ea, xq=xq, xk=xk, xv=xv, rq=rq, rk=rk,
                qn=sq * rq, kn=sk * rk, v=v)


def _dn_masks():
    C = DN_C
    r, c = _iota((C, C), 0), _iota((C, C), 1)
    incl = r >= c
    strict = r > c
    blk16 = jnp.right_shift(r, 4) == jnp.right_shift(c, 4)
    blk32 = jnp.right_shift(r, 5) == jnp.right_shift(c, 5)
    return dict(incl=incl, strict=strict, upper=r <= c, blk16=blk16, blk32=blk32,
                tri=incl.astype(BF16), triT=(r <= c).astype(BF16), ones=jnp.ones((C, C), BF16),
                eye=(r == c).astype(F32), last=_iota((C, 1), 0) == C - 1)


def _tri_inverse(A, mk):
    P = -jnp.where(mk["blk16"], A, 0.0)
    X = mk["eye"] + P
    for _ in range(3):
        P = _dot(P, P)
        X = X + _dot(X, P)
    off1 = jnp.where(mk["blk32"] & (~mk["blk16"]), A, 0.0)
    X = X - _dot(_dot(X, off1), X)
    off2 = jnp.where(mk["blk32"], 0.0, A)
    X = X - _dot(_dot(X, off2), X)
    return X


def _dn_local(qc, kc, vc, gc, bc, mk):
    C = DN_C
    gm = jnp.broadcast_to(gc, gc.shape[:-1] + (C,))
    Gc = _dot2l(mk["tri"], gm)
    Gr = _dot2l(mk["ones"], jnp.where(mk["upper"], gm, 0.0))
    Dm = jnp.where(mk["incl"], jnp.exp(jnp.where(mk["incl"], Gc - Gr, 0.0)), 0.0)
    Gcol = jnp.max(Gc, axis=-1, keepdims=True)
    Gl = _colsum(jnp.where(mk["last"], Gcol, 0.0))
    eG = jnp.exp(Gcol)
    eT = jnp.exp(Gl - Gcol)
    gl = jnp.exp(Gl)
    kb = kc * bc
    vb = vc * bc
    KK = _dot(kb, kc, NT)
    Tinv = _tri_inverse(jnp.where(mk["strict"], KK * Dm, 0.0), mk)
    KBE = kb * eG
    QK = _dot(qc, kc, NT)
    return dict(Dm=Dm, eG=eG, eT=eT, gl=gl, kb=kb, vb=vb, KK=KK, Tinv=Tinv, KBE=KBE, U=_dot(Tinv, vb),
                W=_dot(Tinv, KBE), QK=QK, attn=QK * Dm, QD=qc * eG, KT=kc * eT)


def _dn_recur(f, S):
    vnew = f["U"] - _dot(f["W"], S)
    o = _dot(f["QD"], S) + _dot(f["attn"], vnew)
    return o, S * f["gl"] + _dot(f["KT"], vnew, TN), vnew


def _dn_bwd_chain(f, do, dS):
    dvnew = _dot(f["KT"], dS) + _dot(f["attn"], do, TN)
    return dvnew, dS * f["gl"] + _dot(f["QD"], do, TN) - _dot(f["W"], dvnew, TN)


def _dn_bwd_rest(qc, kc, vc, bc, f, S, vnew, do, dS, dvnew, mk):
    C = DN_C
    Dm, eG, eT, gl, kb, vb, KK, Tinv, KBE, QK = (
        f[n] for n in ("Dm", "eG", "eT", "gl", "kb", "vb", "KK", "Tinv", "KBE", "QK"))
    dKT = _dot(vnew, dS, NT)
    dgl = _colsum(_rowsum(dS * S))
    dQD = _dot(do, S, NT)
    dattn = _dot(do, vnew, NT)
    dU = dvnew
    dW = -_dot(dvnew, S, NT)
    dQK = dattn * Dm
    dD = dattn * QK
    dq = _dot(dQK, kc)
    dk = _dot(dQK, qc, TN)
    dTinv = _dot(dU, vb, NT) + _dot(dW, KBE, NT)
    dvb = _dot(Tinv, dU, TN)
    dKBE = _dot(Tinv, dW, TN)
    dA = -_dot(_dot(Tinv, dTinv, TN), Tinv, NT)
    dA = jnp.where(mk["strict"], dA, 0.0)
    dKK = dA * Dm
    dD = dD + dA * KK
    dkb = _dot(dKK, kc) + dKBE * eG
    dk = dk + _dot(dKK, kb, TN)
    deG = _rowsum(dKBE * kb)
    dk = dk + dkb * bc
    dbeta = _rowsum(dkb * kc) + _rowsum(dvb * vc)
    dv = dvb * bc
    dq = dq + dQD * eG
    deG = deG + _rowsum(dQD * qc)
    dk = dk + dKT * eT
    deT = _rowsum(dKT * kc)
    dGcol = deG * eG - deT * eT
    dGl = _colsum(deT * eT) + dgl * gl
    Y = dD * Dm
    ycol = jnp.max(_dot2r(Y, mk["ones"], TN), axis=-1, keepdims=True)
    dGcol = dGcol + _rowsum(Y) - ycol
    dGcol = dGcol + jnp.where(mk["last"], dGl, 0.0)
    dg = jnp.max(_dot2l(mk["triT"], jnp.broadcast_to(dGcol, dGcol.shape[:-1] + (C,))), axis=-1, keepdims=True)
    return dq, dk, dv, dg, dbeta


def _dn_core_fwd(proj, conv_w, alog, dtb, og, name, comm=None):
    T = proj.shape[0]
    N = T // DN_C

    def body(pq_ref, pk_ref, pv_ref, pz_ref, pab_ref, cq_ref, ck_ref, cv_ref, alog_ref, dtb_ref, og_ref,
             out_ref, o_ref, st_ref, q_s, k_s, v_s, g_s, b_s, S_s):
        h = pl.program_id(0)
        p = _dn_prep(pq_ref[...], pk_ref[...], pv_ref[...], pab_ref[...], cq_ref, ck_ref, cv_ref,
                     alog_ref[...], dtb_ref[...], h)
        q_s[...] = p["qn"] * (DN_D ** -0.5)
        k_s[...] = p["kn"]
        v_s[...] = p["v"]
        g_s[...] = p["g"]
        b_s[...] = p["beta"]
        S_s[...] = jnp.zeros(S_s.shape, F32)
        mk = _dn_masks()

        nu = min(DN_UNROLL_FWD, N)

        def step(it, carry):
            rows = pl.ds(pl.multiple_of(it * (nu * DN_C), nu * DN_C), nu * DN_C)
            loc = _dn_local(*(r[rows, :].reshape(nu, DN_C, r.shape[1]) for r in (q_s, k_s, v_s, g_s, b_s)), mk)
            S = S_s[...]
            outs = []
            for u in range(nu):
                st_ref[0, it * nu + u] = S
                o, S, _ = _dn_recur({n: v[u] for n, v in loc.items()}, S)
                outs.append(o)
            o_ref[rows, :] = jnp.concatenate(outs, axis=0)
            S_s[...] = S
            return carry

        lax.fori_loop(0, N // nu, step, 0)
        o = o_ref[...]
        ro = lax.rsqrt(jnp.mean(o * o, axis=1, keepdims=True) + EPS)
        out_ref[...] = ((o * ro * og_ref[...]) * _silu(pz_ref[...])).astype(out_ref.dtype)

    col = lambda k: pl.BlockSpec((T, DN_D), lambda h: (0, k * DN_H + h))
    cw = lambda k: pl.BlockSpec((DN_CONV, DN_D), lambda h: (0, k * DN_H + h))
    small = pl.BlockSpec((1, LANES), lambda h: (0, 0))
    return _grid_call(
        body, comm, grid=(DN_H,),
        in_specs=[col(0), col(1), col(2), col(3), pl.BlockSpec((T, LANES), lambda h: (0, 4 * DN_H)),
                  cw(0), cw(1), cw(2), small, small, small],
        out_specs=[pl.BlockSpec((T, DN_D), lambda h: (0, h)), pl.BlockSpec((T, DN_D), lambda h: (0, h)),
                   pl.BlockSpec((1, N, DN_D, DN_D), lambda h: (h, 0, 0, 0))],
        out_shape=[jax.ShapeDtypeStruct((T, D), BF16), jax.ShapeDtypeStruct((T, D), F32),
                   jax.ShapeDtypeStruct((DN_H, N, DN_D, DN_D), F32)],
        scratch_shapes=[pltpu.VMEM((T, DN_D), F32)] * 3 + [pltpu.VMEM((T, 1), F32)] * 2 + [pltpu.VMEM((DN_D, DN_D), F32)],
        name=name, args=(proj, proj, proj, proj, proj, conv_w, conv_w, conv_w, alog, dtb, og))


def _dn_core_bwd(proj, conv_w, alog, dtb, og, o, states, dout, name, comm=None):
    T = proj.shape[0]
    N = T // DN_C

    def body(pq_ref, pk_ref, pv_ref, pz_ref, pab_ref, cq_ref, ck_ref, cv_ref, alog_ref, dtb_ref, og_ref,
             o_ref, st_ref, dout_ref,
             dpq_ref, dpk_ref, dpv_ref, dpz_ref, dpab_ref, dcq_ref, dck_ref, dcv_ref, dalog_ref, ddtb_ref, dog_ref,
             q_s, k_s, v_s, g_s, b_s, do_s, dS_s):
        h = pl.program_id(0)
        scale = DN_D ** -0.5

        def prep():
            return _dn_prep(pq_ref[...], pk_ref[...], pv_ref[...], pab_ref[...], cq_ref, ck_ref, cv_ref,
                            alog_ref[...], dtb_ref[...], h)

        p = prep()
        q_s[...] = p["qn"] * scale
        k_s[...] = p["kn"]
        v_s[...] = p["v"]
        g_s[...] = p["g"]
        b_s[...] = p["beta"]
        del p

        o = o_ref[...]
        z = pz_ref[...]
        dout = dout_ref[...]
        ogv = og_ref[...]
        ro = lax.rsqrt(jnp.mean(o * o, axis=1, keepdims=True) + EPS)
        on = o * ro
        dy = dout * _silu(z)
        dpz_ref[...] = (dout * (on * ogv) * _dsilu(z)).astype(dpz_ref.dtype)
        dyg = dy * ogv
        do_s[...] = ro * (dyg - on * jnp.mean(dyg * on, axis=1, keepdims=True))
        dog_h = _colsum(dy * on)

        dS_s[...] = jnp.zeros(dS_s.shape, F32)
        mk = _dn_masks()

        nu = min(DN_UNROLL_BWD, N)

        def step(it, carry):
            c0 = (N // nu - 1 - it) * nu
            rows = pl.ds(pl.multiple_of(c0 * DN_C, nu * DN_C), nu * DN_C)
            q, k, v, g, b, do = (r[rows, :].reshape(nu, DN_C, r.shape[1]) for r in (q_s, k_s, v_s, g_s, b_s, do_s))
            loc = _dn_local(q, k, v, g, b, mk)
            Ss = st_ref[0, pl.ds(c0, nu)]
            vnew = loc["U"] - _dot(loc["W"], Ss)
            dS = dS_s[...]
            dS_in, dvnew = [None] * nu, [None] * nu
            for u in reversed(range(nu)):
                dS_in[u] = dS
                dvnew[u], dS = _dn_bwd_chain({n: x[u] for n, x in loc.items()}, do[u], dS)
            dS_s[...] = dS
            grads = _dn_bwd_rest(q, k, v, b, loc, Ss, vnew, do, jnp.stack(dS_in), jnp.stack(dvnew), mk)
            for r, d in zip((q_s, k_s, v_s, g_s, b_s), grads):
                r[rows, :] = d.reshape(nu * DN_C, r.shape[1])
            return carry

        lax.fori_loop(0, N // nu, step, 0)

        p = prep()
        pq, pk, pv = pq_ref[...], pk_ref[...], pv_ref[...]
        dqn = q_s[...] * scale
        dkn = k_s[...]
        qn, kn = p["qn"], p["kn"]
        dsq = p["rq"] * (dqn - qn * _rowsum(dqn * qn))
        dsk = p["rk"] * (dkn - kn * _rowsum(dkn * kn))
        dxq = dsq * _dsilu(p["xq"])
        dxk = dsk * _dsilu(p["xk"])
        dxv = v_s[...] * _dsilu(p["xv"])

        def conv_bwd(dx, x, w_ref, dp_ref, dc_ref):
            acc = dx * w_ref[DN_CONV - 1:DN_CONV, :]
            dc_ref[DN_CONV - 1:DN_CONV, :] = _colsum(dx * x)
            for i in range(DN_CONV - 1):
                s = DN_CONV - 1 - i
                acc = acc + _shift_rows(dx, -s) * w_ref[i:i + 1, :]
                dc_ref[i:i + 1, :] = _colsum(dx * _shift_rows(x, s))
            dp_ref[...] = acc.astype(dp_ref.dtype)

        conv_bwd(dxq, pq, cq_ref, dpq_ref, dcq_ref)
        conv_bwd(dxk, pk, ck_ref, dpk_ref, dck_ref)
        conv_bwd(dxv, pv, cv_ref, dpv_ref, dcv_ref)

        dg = g_s[...]
        beta = p["beta"]
        da_raw = dg * p["neg_ea"] * _sigmoid(p["pre"])
        db_raw = b_s[...] * beta * (1.0 - beta)
        lane = _iota((T, LANES), 1)
        contrib = jnp.where(lane == h, da_raw, 0.0) + jnp.where(lane == DN_H + h, db_raw, 0.0)
        lane1 = _iota((1, LANES), 1)
        dalog_h = jnp.where(lane1 == h, _colsum(dg * p["g"]), 0.0)
        ddtb_h = jnp.where(lane1 == h, _colsum(da_raw), 0.0)

        @pl.when(h == 0)
        def _():
            dpab_ref[...] = contrib.astype(dpab_ref.dtype)
            dalog_ref[...] = dalog_h
            ddtb_ref[...] = ddtb_h
            dog_ref[...] = dog_h

        @pl.when(h > 0)
        def _():
            dpab_ref[...] += contrib.astype(dpab_ref.dtype)
            dalog_ref[...] += dalog_h
            ddtb_ref[...] += ddtb_h
            dog_ref[...] += dog_h

    col = lambda k: pl.BlockSpec((T, DN_D), lambda h: (0, k * DN_H + h))
    cw = lambda k: pl.BlockSpec((DN_CONV, DN_D), lambda h: (0, k * DN_H + h))
    small = pl.BlockSpec((1, LANES), lambda h: (0, 0))
    ab = pl.BlockSpec((T, LANES), lambda h: (0, 4 * DN_H))
    hcol = pl.BlockSpec((T, DN_D), lambda h: (0, h))
    outs = _grid_call(
        body, comm, grid=(DN_H,),
        in_specs=[col(0), col(1), col(2), col(3), ab, cw(0), cw(1), cw(2), small, small, small,
                  hcol, pl.BlockSpec((1, N, DN_D, DN_D), lambda h: (h, 0, 0, 0)), hcol],
        out_specs=[hcol, hcol, hcol, hcol, pl.BlockSpec((T, LANES), lambda h: (0, 0)),
                   pl.BlockSpec((DN_CONV, DN_D), lambda h: (0, h)), pl.BlockSpec((DN_CONV, DN_D), lambda h: (0, h)),
                   pl.BlockSpec((DN_CONV, DN_D), lambda h: (0, h)), small, small, small],
        out_shape=[jax.ShapeDtypeStruct((T, D), BF16)] * 4 + [jax.ShapeDtypeStruct((T, LANES), BF16)]
                  + [jax.ShapeDtypeStruct((DN_CONV, D), F32)] * 3 + [jax.ShapeDtypeStruct((1, LANES), F32)] * 3,
        scratch_shapes=[pltpu.VMEM((T, DN_D), F32)] * 3 + [pltpu.VMEM((T, 1), F32)] * 2
                       + [pltpu.VMEM((T, DN_D), F32), pltpu.VMEM((DN_D, DN_D), F32)],
        name=name, args=(proj, proj, proj, proj, proj, conv_w, conv_w, conv_w, alog, dtb, og, o, states, dout))
    dpq, dpk, dpv, dpz, dpab, dcq, dck, dcv, dalog, ddtb, dog = outs[:11]
    dproj = jnp.concatenate([dpq, dpk, dpv, dpz, dpab], axis=1)
    dconv = jnp.concatenate([dcq, dck, dcv], axis=1)
    return dproj, dconv, dalog, ddtb, dog, (outs[11] if comm else None)


def _sb_head_of(shape):
    return jnp.right_shift(_iota(shape, 1), 6)


def _sb_head_sums(x, head):
    out = jnp.zeros_like(x)
    for hh in range(SB_HPB):
        out = jnp.where(head == hh, _rowsum(jnp.where(head == hh, x, 0.0)), out)
    return out


def _sb_head_norm(x, head):
    r = lax.rsqrt(_sb_head_sums(x * x, head) * (1.0 / SB_D) + EPS)
    return x * r, r


def _sb_fill(q_ref, k_ref, v_ref, qg_ref, kg_ref, qs_s, kn_s, v_s):
    head = _sb_head_of(q_ref.shape)
    qs_s[...] = (_sb_head_norm(q_ref[...], head)[0] * qg_ref[...] * (SB_D ** -0.5)).astype(BF16)
    kn_s[...] = (_sb_head_norm(k_ref[...], head)[0] * kg_ref[...]).astype(BF16)
    v_s[...] = v_ref[...].astype(BF16)


def _sb_head_masks(dtype):
    return jnp.stack([(_sb_head_of((1, SB_W)) == hh).astype(dtype) for hh in range(SB_HPB)])


def _sb_core_fwd(qkv, qg, kg, name, comm=None):
    T = qkv.shape[0]
    B = min(SB_TILE, T)
    NB = T // B
    NP = SB_H // SB_HPB

    def body(q_ref, k_ref, v_ref, qg_ref, kg_ref, o_ref, tot_ref, qs_s, kn_s, v_s):
        _sb_fill(q_ref, k_ref, v_ref, qg_ref, kg_ref, qs_s, kn_s, v_s)
        r, c = _iota((B, B), 0), _iota((B, B), 1)
        causal = c < r
        m_after = (r > c).astype(BF16)
        head_b = _sb_head_of((B, SB_W))
        hm = _sb_head_masks(BF16)

        def tile(qb, kj, vj, R, acc, diag):
            z = _dot(qb, kj, NT)
            sp = _softplus(z)
            ls = z - sp
            lm = jnp.where(causal, -sp, 0.0) if diag else -sp
            cs = _dot2r(lm, m_after) + R
            a = jnp.exp(ls + cs)
            if diag:
                a = jnp.where(causal, a, 0.0)
            return R + _rowsum(lm), acc + jnp.sum(_dot(a, vj), axis=0)

        def qblock(i, carry):
            rows_i = pl.ds(pl.multiple_of(i * B, B), B)
            qb = qs_s[rows_i, :][None] * hm

            def step(rows_j, st, diag):
                return tile(qb, kn_s[rows_j, :], v_s[rows_j, :][None] * hm, st[0], st[1], diag)

            st = step(rows_i, (jnp.zeros((SB_HPB, B, 1), F32), jnp.zeros((B, SB_W), F32)), True)
            st = lax.fori_loop(0, i, lambda s, st: step(pl.ds(pl.multiple_of((i - 1 - s) * B, B), B), st, False), st)
            o_ref[rows_i, :] = st[1].astype(o_ref.dtype)
            tot = jnp.zeros((B, SB_W), F32)
            for hh in range(SB_HPB):
                tot = jnp.where(head_b == hh, st[0][hh], tot)
            tot_ref[rows_i, :] = tot
            return carry

        lax.fori_loop(0, NB, qblock, 0)

    blk = lambda k: pl.BlockSpec((T, SB_W), lambda p: (0, k * NP + p))
    small = pl.BlockSpec((1, SB_W), lambda p: (0, 0))
    return _grid_call(
        body, comm, grid=(NP,), in_specs=[blk(0), blk(1), blk(2), small, small],
        out_specs=[pl.BlockSpec((T, SB_W), lambda p: (0, p))] * 2,
        out_shape=[jax.ShapeDtypeStruct((T, D), BF16), jax.ShapeDtypeStruct((T, D), F32)],
        scratch_shapes=[pltpu.VMEM((T, SB_W), BF16)] * 3, name=name, args=(qkv, qkv, qkv, qg, kg))


def _sb_core_bwd(qkv, qg, kg, tot, dout, name, comm=None):
    T = qkv.shape[0]
    B = min(SB_TILE, T)
    NB = T // B
    NP = SB_H // SB_HPB

    def body(q_ref, k_ref, v_ref, qg_ref, kg_ref, tot_ref, do_ref, dq_ref, dk_ref, dv_ref, dqg_ref, dkg_ref,
             qs_s, kn_s, v_s, dqn_s, dkn_s, dvv_s):
        p = pl.program_id(0)
        _sb_fill(q_ref, k_ref, v_ref, qg_ref, kg_ref, qs_s, kn_s, v_s)
        dkn_s[...] = jnp.zeros(dkn_s.shape, F32)
        dvv_s[...] = jnp.zeros(dvv_s.shape, F32)
        r, c = _iota((B, B), 0), _iota((B, B), 1)
        causal = c < r
        m_upto = (r <= c).astype(BF16)
        m_before = (r < c).astype(BF16)
        lane_b = _iota((B, SB_W), 1)
        hm, hmf = _sb_head_masks(BF16), _sb_head_masks(F32)

        def tile(qb, dob, tot_h, kj, vj, PL, P, dq, diag):
            z = _dot(qb, kj, NT)
            sp = _softplus(z)
            ls = z - sp
            lm = jnp.where(causal, -sp, 0.0) if diag else -sp
            cs = tot_h - PL - _dot2r(lm, m_upto)
            a = jnp.exp(ls + cs)
            if diag:
                a = jnp.where(causal, a, 0.0)
            e = _dot(dob, vj, NT) * a
            E = _dot2r(e, m_before) + P
            sig = jnp.exp(ls)
            dz = e * (1.0 - sig) - E * sig
            if diag:
                dz = jnp.where(causal, dz, 0.0)
            dq = dq + jnp.sum(_dot(dz, kj) * hmf, axis=0)
            return (PL + _rowsum(lm), P + _rowsum(e), dq), jnp.sum(_dot(dz, qb, TN), axis=0), jnp.sum(_dot(a, dob, TN), axis=0)

        def qblock(i, carry):
            rows_i = pl.ds(pl.multiple_of(i * B, B), B)
            totb = tot_ref[rows_i, :]
            tot_h = jnp.stack([_rowsum(jnp.where(lane_b == hh * SB_D, totb, 0.0)) for hh in range(SB_HPB)])
            qb = qs_s[rows_i, :][None] * hm
            dob = do_ref[rows_i, :][None] * hm

            def step(rows_j, st, diag):
                st, dk, dv = tile(qb, dob, tot_h, kn_s[rows_j, :], v_s[rows_j, :][None] * hm, st[0], st[1], st[2], diag)
                dkn_s[rows_j, :] += dk
                dvv_s[rows_j, :] += dv
                return st

            zero = (jnp.zeros((SB_HPB, B, 1), F32), jnp.zeros((SB_HPB, B, 1), F32), jnp.zeros((B, SB_W), F32))
            st = lax.fori_loop(0, i, lambda j, st: step(pl.ds(pl.multiple_of(j * B, B), B), st, False), zero)
            st = step(rows_i, st, True)
            dqn_s[rows_i, :] = st[2] * (SB_D ** -0.5)
            return carry

        lax.fori_loop(0, NB, qblock, 0)

        head = _sb_head_of((T, SB_W))

        def norm_bwd(dn, x_ref, g):
            xh, rr = _sb_head_norm(x_ref[...], head)
            t = dn * g
            return rr * (t - xh * (_sb_head_sums(t * xh, head) * (1.0 / SB_D))), _colsum(dn * xh)

        dq, dqg = norm_bwd(dqn_s[...], q_ref, qg_ref[...])
        dq_ref[...] = dq.astype(dq_ref.dtype)
        dk, dkg = norm_bwd(dkn_s[...], k_ref, kg_ref[...])
        dk_ref[...] = dk.astype(dk_ref.dtype)
        dv_ref[...] = dvv_s[...].astype(dv_ref.dtype)

        @pl.when(p == 0)
        def _():
            dqg_ref[...] = dqg
            dkg_ref[...] = dkg

        @pl.when(p > 0)
        def _():
            dqg_ref[...] += dqg
            dkg_ref[...] += dkg

    blk = lambda k: pl.BlockSpec((T, SB_W), lambda p: (0, k * NP + p))
    small = pl.BlockSpec((1, SB_W), lambda p: (0, 0))
    own = pl.BlockSpec((T, SB_W), lambda p: (0, p))
    outs = _grid_call(
        body, comm, grid=(NP,), in_specs=[blk(0), blk(1), blk(2), small, small, own, own],
        out_specs=[own, own, own, small, small],
        out_shape=[jax.ShapeDtypeStruct((T, D), BF16)] * 3 + [jax.ShapeDtypeStruct((1, SB_W), F32)] * 2,
        scratch_shapes=[pltpu.VMEM((T, SB_W), BF16)] * 3 + [pltpu.VMEM((T, SB_W), F32)] * 3,
        name=name, args=(qkv, qkv, qkv, qg, kg, tot, dout))
    dq, dk, dv, dqg, dkg = outs[:5]
    return jnp.concatenate([dq, dk, dv], axis=1), dqg, dkg, (outs[5] if comm else None)


def _ada_fwd(c16, ada_w, ada_b_cols, name):
    L, _, cols = ada_w.shape

    def body(c_ref, w_ref, b_ref, o_ref):
        o_ref[0] = _dot(_silu(c_ref[...]), w_ref[0]) + b_ref[0]

    return pl.pallas_call(
        body, grid=(L,),
        in_specs=[pl.BlockSpec((16, D), lambda i: (0, 0)), pl.BlockSpec((1, D, cols), lambda i: (i, 0, 0)),
                  pl.BlockSpec((1, 1, cols), lambda i: (i, 0, 0))],
        out_specs=pl.BlockSpec((1, 16, cols), lambda i: (i, 0, 0)),
        out_shape=jax.ShapeDtypeStruct((L, 16, cols), F32), name=name, compiler_params=_params(("arbitrary",)),
    )(c16, ada_w, ada_b_cols)


def _ada_bwd(c16, dmod16, name):
    L, _, cols = dmod16.shape

    def body(c_ref, d_ref, o_ref):
        o_ref[0] = _dot(_silu(c_ref[...]), d_ref[0], TN)

    return pl.pallas_call(
        body, grid=(L,),
        in_specs=[pl.BlockSpec((16, D), lambda i: (0, 0)), pl.BlockSpec((1, 16, cols), lambda i: (i, 0, 0))],
        out_specs=pl.BlockSpec((1, D, cols), lambda i: (i, 0, 0)),
        out_shape=jax.ShapeDtypeStruct((L, D, cols), F32), name=name, compiler_params=_params(("arbitrary",)),
    )(c16, dmod16)


def _sum_sources(x, name):
    n, R, C = x.shape
    tile = _sum_tile(R) or R

    def body(x_ref, o_ref):
        acc = x_ref[0].astype(F32)
        for k in range(1, n):
            acc = acc + x_ref[k].astype(F32)
        o_ref[...] = acc

    return pl.pallas_call(
        body, grid=(R // tile,), in_specs=[pl.BlockSpec((n, tile, C), lambda i: (0, i, 0))],
        out_specs=pl.BlockSpec((tile, C), lambda i: (i, 0)), out_shape=jax.ShapeDtypeStruct((R, C), F32),
        name=name, compiler_params=_params(("parallel",)),
    )(x)


def _adamw(w, g, m, v, name):
    shape = w.shape
    C = shape[-1]
    R = w.size // C
    w2, g2, m2, v2 = (a.reshape(R, C) for a in (w, g, m, v))
    tile = _rtile(R)
    c1 = 1.0 / (1.0 - ADAM_B1 ** ADAM_STEP)
    c2 = 1.0 / (1.0 - ADAM_B2 ** ADAM_STEP)

    def body(w_ref, g_ref, m_ref, v_ref, d_ref, nm_ref, nv_ref):
        gv = g_ref[...]
        nm = ADAM_B1 * m_ref[...] + (1.0 - ADAM_B1) * gv
        nv = ADAM_B2 * v_ref[...] + (1.0 - ADAM_B2) * (gv * gv)
        d_ref[...] = -ADAM_LR * ((nm * c1) / (jnp.sqrt(nv * c2) + ADAM_EPS) + ADAM_WD * w_ref[...])
        nm_ref[...] = nm
        nv_ref[...] = nv

    spec = pl.BlockSpec((tile, C), lambda i: (i, 0))
    outs = pl.pallas_call(
        body, grid=(R // tile,), in_specs=[spec] * 4, out_specs=[spec] * 3,
        out_shape=[jax.ShapeDtypeStruct((R, C), F32)] * 3, name=name, compiler_params=_params(("parallel",)),
    )(w2, g2, m2, v2)
    return tuple(o.reshape(shape) for o in outs)


def _gather_steps(x_ref, out_ref, send_sems, recv_sems, local_sem):
    mx, my, mc = lax.axis_index("x"), lax.axis_index("y"), lax.axis_index("c")
    me, sibling = (mx, my, mc), (mx, my, 1 - mc)
    chips = [(1 - mx, my), (mx, 1 - my), (1 - mx, 1 - my)]

    def slot(px, py, pc):
        return out_ref.at[4 * px + 2 * py + pc]

    def copy(k, block, to, src=None):
        return pltpu.make_async_remote_copy(
            src_ref=slot(*block) if src is None else src, dst_ref=slot(*block),
            send_sem=send_sems.at[k], recv_sem=recv_sems.at[k], device_id=to, device_id_type=MESH)

    mine = pltpu.make_async_copy(x_ref, slot(*me), local_sem)
    first = [copy(0, me, sibling, src=x_ref)] + [copy(1 + j, me, (*chip, mc), src=x_ref) for j, chip in enumerate(chips)]
    passed = [copy(4 + j, (*chip, mc), sibling) for j, chip in enumerate(chips)]

    def start():
        mine.start()
        for cp in first:
            cp.start()

    def forward():
        for j, chip in enumerate(chips):
            copy(1 + j, (*chip, mc), me).wait_recv()
            passed[j].start()

    def finish():
        copy(0, sibling, me).wait_recv()
        for j, chip in enumerate(chips):
            copy(4 + j, (*chip, 1 - mc), me).wait_recv()
        for cp in first + passed:
            cp.wait_send()
        mine.wait()

    return start, forward, finish


def _exchange_steps(x_ref, out_ref, send_sems, recv_sems, local_sem):
    mx, my, mc = lax.axis_index("x"), lax.axis_index("y"), lax.axis_index("c")
    me = 4 * mx + 2 * my + mc
    mine = pltpu.make_async_copy(x_ref.at[me], out_ref.at[me], local_sem)
    copies = []
    for k in range(1, NDEV):
        px, py, pc = mx ^ (k >> 2), my ^ ((k >> 1) & 1), mc ^ (k & 1)
        copies.append(pltpu.make_async_remote_copy(
            src_ref=x_ref.at[4 * px + 2 * py + pc], dst_ref=out_ref.at[me], send_sem=send_sems.at[k - 1],
            recv_sem=recv_sems.at[k - 1], device_id=(px, py, pc), device_id_type=MESH))

    def start():
        mine.start()
        for cp in copies:
            cp.start()

    def finish():
        for cp in copies:
            cp.wait_recv()
        for cp in copies:
            cp.wait_send()
        mine.wait()

    return start, finish


def _comm_sems():
    return [pltpu.SemaphoreType.DMA((7,)), pltpu.SemaphoreType.DMA((7,)), pltpu.SemaphoreType.DMA]


def _comm_out_shape(comm):
    kind, x = comm
    return jax.ShapeDtypeStruct(((NDEV,) + x.shape) if kind == "gather" else x.shape, x.dtype)


def _ride_along(comm, step, n_steps, refs, at_end):
    if comm[0] == "gather":
        start, forward, finish = _gather_steps(*refs)
        todo = [(n_steps - 1, finish)] if at_end else [(0, start), (n_steps // 2, forward)]
    else:
        start, finish = _exchange_steps(*refs)
        todo = [(n_steps - 1, finish)] if at_end else [(0, start)]
    for at, fn in todo:
        pl.when(step == at)(fn)


def _grid_call(body, comm, *, grid, in_specs, out_specs, out_shape, scratch_shapes, name, args):
    if comm is None:
        return pl.pallas_call(body, grid=grid, in_specs=in_specs, out_specs=out_specs, out_shape=out_shape,
                              scratch_shapes=scratch_shapes, name=name, compiler_params=_params(("arbitrary",)))(*args)
    n_in, n_out, n_steps = len(in_specs), len(out_specs), grid[0]

    def with_comm(*refs):
        ins, outs, scr = refs[:n_in], refs[n_in + 1:n_in + 1 + n_out], refs[n_in + 2 + n_out:-3]
        comm_refs = (refs[n_in], refs[n_in + 1 + n_out]) + refs[-3:]
        step = pl.program_id(0)
        _ride_along(comm, step, n_steps, comm_refs, False)
        body(*ins, *outs, *scr)
        _ride_along(comm, step, n_steps, comm_refs, True)

    hbm = pl.BlockSpec(memory_space=pl.ANY)
    return pl.pallas_call(
        with_comm, grid=grid, in_specs=list(in_specs) + [hbm], out_specs=list(out_specs) + [hbm],
        out_shape=list(out_shape) + [_comm_out_shape(comm)], scratch_shapes=list(scratch_shapes) + _comm_sems(),
        name=name, compiler_params=_params(("arbitrary",)))(*args, comm[1])


def _all_gather(x, name, in_vmem):
    def body(x_ref, out_ref, send_sems, recv_sems, local_sem):
        start, forward, finish = _gather_steps(x_ref, out_ref, send_sems, recv_sems, local_sem)
        start()
        forward()
        finish()

    space = pltpu.VMEM if in_vmem else pl.ANY
    return pl.pallas_call(
        body, out_shape=jax.ShapeDtypeStruct((NDEV,) + x.shape, x.dtype),
        in_specs=[pl.BlockSpec(memory_space=space)], out_specs=pl.BlockSpec(memory_space=space),
        scratch_shapes=_comm_sems(), name=name, compiler_params=pltpu.CompilerParams(vmem_limit_bytes=VMEM_LIMIT),
    )(x)


def _all_to_all(x, name):
    def body(x_ref, out_ref, send_sems, recv_sems, local_sem):
        start, finish = _exchange_steps(x_ref, out_ref, send_sems, recv_sems, local_sem)
        start()
        finish()

    return pl.pallas_call(
        body, out_shape=jax.ShapeDtypeStruct(x.shape, x.dtype),
        in_specs=[pl.BlockSpec(memory_space=pl.ANY)], out_specs=pl.BlockSpec(memory_space=pl.ANY),
        scratch_shapes=_comm_sems(), name=name, compiler_params=pltpu.CompilerParams(vmem_limit_bytes=VMEM_LIMIT),
    )(x)


def _stage_parts(st):
    parts = []
    if st >= 1:
        parts += [("ffn_w_in", st - 1, (D, 2 * DFF // NDEV), True), ("ffn_w_out", st - 1, (DFF // NDEV, D), False)]
    if st < DEPTH:
        j = st // 2
        if st % 2 == 0:
            parts += [("dn_w_in", j, (D, DN_COLS // NDEV), True), ("dn_w_out", j, (D // NDEV, D), False)]
        else:
            parts += [("sb_w_qkv", j, (D, 3 * D // NDEV), True), ("sb_w_out", j, (D // NDEV, D), False)]
    return parts


def _sum_tile(rows):
    for t in range(512, 191, -16):
        if rows % t == 0:
            return t
    return None


def _stage_layout(st):
    out, r = [], 0
    for n, j, s, by_cols in _stage_parts(st):
        k = s[0] * s[1] // D
        kp = -(-k // 16) * 16
        out.append((n, j, s, by_cols, k, kp, r))
        r += kp
    while _sum_tile(r) is None:
        r += 16
    return out, r


def _pack_stage(w, st):
    parts, rows = _stage_layout(st)
    bufs = [jnp.pad(w[n][j].astype(BF16).reshape(k, D), ((0, kp - k), (0, 0))) for n, j, _, _, k, kp, _ in parts]
    used = parts[-1][6] + parts[-1][5]
    if rows > used:
        bufs.append(jnp.zeros((rows - used, D), BF16))
    return jnp.concatenate(bufs, axis=0)


def _full_stage(g, st):
    out = {}
    for n, _, s, by_cols, k, _, r in _stage_layout(st)[0]:
        a = g[:, r:r + k].reshape((NDEV,) + s)
        out[n] = jnp.transpose(a, (1, 0, 2)).reshape(s[0], NDEV * s[1]) if by_cols else a.reshape(NDEV * s[0], s[1])
    if "dn_w_in" in out:
        out["dn_w_in"] = jnp.pad(out["dn_w_in"], ((0, 0), (0, DN_COLS_PAD - DN_COLS)))
    return out


def _shards_stage(full, st):
    parts, rows = _stage_layout(st)
    bufs = []
    for n, _, s, by_cols, k, kp, _ in parts:
        a = full[n].astype(BF16)
        a = jnp.transpose(a.reshape(s[0], NDEV, s[1]), (1, 0, 2)) if by_cols else a.reshape(NDEV, s[0], s[1])
        bufs.append(jnp.pad(a.reshape(NDEV, k, D), ((0, 0), (0, kp - k), (0, 0))))
    used = parts[-1][6] + parts[-1][5]
    if rows > used:
        bufs.append(jnp.zeros((NDEV, rows - used, D), BF16))
    return jnp.concatenate(bufs, axis=1)


def _unpack_stage(buf, st):
    return {(n, j): buf[r:r + k].reshape(s) for n, j, s, _, k, _, r in _stage_layout(st)[0]}


def _local_step(x, target, mod, small, w0, rest):
    dist = isinstance(rest, tuple)
    packs = rest[1] if dist else None
    W = [dict(w0), {}, {}, {}] if dist else [w0] + list(rest)

    def arrived(k, g):
        for n, a in _full_stage(g, k + 1).items():
            W[k if n.startswith("ffn") else k + 1][n] = a

    row = lambda v: v.reshape(1, -1)
    pad128 = lambda v: jnp.pad(v.reshape(1, -1), ((0, 0), (0, LANES - v.size)))
    saved = []
    for i in range(DEPTH):
        j = i // 2
        w = W[i]
        m = [row(mod[i, k * D:(k + 1) * D]) for k in range(N_MOD)]
        sh1, sc1, gt1, sh2, sc2, gt2 = m
        g1, g2 = row(small["norm1_g"][i]), row(small["norm2_g"][i])
        h1 = _norm_mod(x, g1, sc1, sh1, f"norm1_{i}")
        if i % 2 == 0:
            proj = _mm(h1, w["dn_w_in"], "nn", F32, f"dn_proj_{i}")
            alog, dtb, og = pad128(small["dn_a_log"][j]), pad128(small["dn_dt_bias"][j]), row(small["dn_onorm_g"][j])
            comm = ("gather", packs[i]) if dist else None
            res = _dn_core_fwd(proj, small["dn_conv_w"][j], alog, dtb, og, f"dn_core_{i}", comm)
            om, o_pre, states = res[:3]
            if comm:
                arrived(i, res[3])
            y1 = _mm(om, w["dn_w_out"], "nn", F32, f"dn_out_{i}")
            mix = (proj, alog, dtb, og, o_pre, states, om)
        else:
            qkv = _mm(h1, w["sb_w_qkv"], "nn", F32, f"sb_qkv_{i}")
            qg2 = jnp.tile(row(small["sb_q_norm_g"][j]), (1, SB_HPB))
            kg2 = jnp.tile(row(small["sb_k_norm_g"][j]), (1, SB_HPB))
            comm = ("gather", packs[i]) if dist else None
            res = _sb_core_fwd(qkv, qg2, kg2, f"sb_core_{i}", comm)
            om, tot = res[:2]
            if comm:
                arrived(i, res[2])
            y1 = _mm(om, w["sb_w_out"], "nn", F32, f"sb_out_{i}")
            mix = (qkv, qg2, kg2, tot, om)
        x_mid = _gate_res(x, y1, gt1, f"res1_{i}")
        h2 = _norm_mod(x_mid, g2, sc2, sh2, f"norm2_{i}")
        u = _mm(h2, w["ffn_w_in"], "nn", BF16, f"ffn_in_{i}")
        a = _swiglu_act(u, f"ffn_act_{i}")
        y2 = _mm(a, w["ffn_w_out"], "nn", F32, f"ffn_out_{i}")
        x_out = _gate_res(x_mid, y2, gt2, f"res2_{i}")
        saved.append((x, h1, mix, y1, x_mid, h2, u, a, y2, m, g1, g2))
        x = x_out

    dx, sq = _loss_head(x, target, "loss_head")

    sg = dict(mod=[None] * DEPTH, norm1_g=[None] * DEPTH, norm2_g=[None] * DEPTH, dn_conv_w=[None] * 2,
              dn_a_log=[None] * 2, dn_dt_bias=[None] * 2, dn_onorm_g=[None] * 2, sb_q_norm_g=[None] * 2,
              sb_k_norm_g=[None] * 2)
    big = [None] * (DEPTH + 1 if dist else DEPTH)
    mixer_gw = {}
    for i in reversed(range(DEPTH)):
        j = i // 2
        w = W[i]
        x_in, h1, mix, y1, x_mid, h2, u, a, y2, m, g1, g2 = saved[i]
        sh1, sc1, gt1, sh2, sc2, gt2 = m
        gw = {}
        dy2, dgt2 = _gate_res_bwd(dx, y2, gt2, f"res2_bwd_{i}")
        da = _mm(dy2, w["ffn_w_out"], "nt", BF16, f"ffn_out_bwd_{i}")
        gw["ffn_w_out"] = _mm(a, dy2, "tn", BF16, f"ffn_out_wg_{i}")
        du = _swiglu_act_bwd(da, u, f"ffn_act_bwd_{i}")
        dh2 = _mm(du, w["ffn_w_in"], "nt", F32, f"ffn_in_bwd_{i}")
        gw["ffn_w_in"] = _mm(h2, du, "tn", BF16, f"ffn_in_wg_{i}")
        dx_mid, s2, dsh2 = _norm_mod_bwd(dh2, x_mid, dx, g2, sc2, f"norm2_bwd_{i}")
        dy1, dgt1 = _gate_res_bwd(dx_mid, y1, gt1, f"res1_bwd_{i}")
        comm = ("exchange", _shards_stage({**gw, **mixer_gw}, i + 1)) if dist else None
        if i % 2 == 0:
            proj, alog, dtb, og, o_pre, states, om = mix
            dom = _mm(dy1, w["dn_w_out"], "nt", F32, f"dn_out_bwd_{i}")
            gw["dn_w_out"] = _mm(om, dy1, "tn", BF16, f"dn_out_wg_{i}")
            dproj, dconv, dalog, ddtb, dog, got = _dn_core_bwd(proj, small["dn_conv_w"][j], alog, dtb, og, o_pre, states,
                                                               dom, f"dn_core_bwd_{i}", comm)
            dh1 = _mm(dproj, w["dn_w_in"], "nt", F32, f"dn_proj_bwd_{i}")
            gw["dn_w_in"] = _mm(h1, dproj, "tn", BF16, f"dn_proj_wg_{i}")[:, :DN_COLS]
            sg["dn_conv_w"][j] = dconv
            sg["dn_a_log"][j] = dalog[0, :DN_H]
            sg["dn_dt_bias"][j] = ddtb[0, :DN_H]
            sg["dn_onorm_g"][j] = dog[0]
        else:
            qkv, qg2, kg2, tot, om = mix
            dom = _mm(dy1, w["sb_w_out"], "nt", BF16, f"sb_out_bwd_{i}")
            gw["sb_w_out"] = _mm(om, dy1, "tn", BF16, f"sb_out_wg_{i}")
            dqkv, dqg, dkg, got = _sb_core_bwd(qkv, qg2, kg2, tot, dom, f"sb_core_bwd_{i}", comm)
            dh1 = _mm(dqkv, w["sb_w_qkv"], "nt", F32, f"sb_qkv_bwd_{i}")
            gw["sb_w_qkv"] = _mm(h1, dqkv, "tn", BF16, f"sb_qkv_wg_{i}")
            sg["sb_q_norm_g"][j] = jnp.sum(dqg.reshape(SB_HPB, SB_D), axis=0)
            sg["sb_k_norm_g"][j] = jnp.sum(dkg.reshape(SB_HPB, SB_D), axis=0)
        if comm:
            big[i + 1] = got
        dx, s1, dsh1 = _norm_mod_bwd(dh1, x_in, dx_mid, g1, sc1, f"norm1_bwd_{i}")
        sg["mod"][i] = jnp.concatenate([dsh1, s1 * g1, dgt1, dsh2, s2 * g2, dgt2], axis=1)[0]
        sg["norm1_g"][i] = (s1 * (1.0 + sc1))[0]
        sg["norm2_g"][i] = (s2 * (1.0 + sc2))[0]
        if dist:
            mixer_gw = {n: a for n, a in gw.items() if not n.startswith("ffn")}
        else:
            big[i] = gw
    if dist:
        big[0] = _all_to_all(_shards_stage(mixer_gw, 0), "exchange_grads_0")
    return sq, dx, {k: jnp.stack(v) for k, v in sg.items()}, big


def _device_index():
    return 4 * lax.axis_index("x") + 2 * lax.axis_index("y") + lax.axis_index("c")


def _gather_small(w, c):
    me = _device_index()
    ada_cols = w["ada_w"].shape[-1]
    conv_cols = w["dn_conv_w"].shape[-1]
    blk = jnp.concatenate([c.reshape(1, D), w["dn_conv_w"].reshape(-1, D)], axis=0)
    g1 = _all_gather(blk, "gather_cond", True)
    c16 = jnp.pad(g1[:, 0, :], ((0, 8), (0, 0)))
    conv_full = jnp.transpose(g1[:, 1:, :].reshape(NDEV, 2, DN_CONV, conv_cols), (1, 2, 0, 3)).reshape(2, DN_CONV, -1)
    b_cols = lax.dynamic_slice_in_dim(w["ada_b"], me * ada_cols, ada_cols, axis=1).reshape(DEPTH, 1, ada_cols)
    mod_part = _ada_fwd(c16, w["ada_w"], b_cols, "ada_fwd")[:, :NDEV, :]
    g2 = _all_gather(mod_part.reshape(DEPTH * NDEV, ada_cols), "gather_mod", True)
    g2 = g2.reshape(NDEV, DEPTH, NDEV, ada_cols)
    mod = lax.dynamic_index_in_dim(g2, me, axis=2, keepdims=False)
    mod = jnp.transpose(mod, (1, 0, 2)).reshape(DEPTH, N_MOD * D)
    return c16, conv_full, mod


def _reduce_small(gr, c16):
    me = _device_index()
    ada_cols = N_MOD * D // NDEV
    conv_cols = 3 * DN_H * DN_D // NDEV
    grads = {}
    small = jnp.concatenate([gr["dn_a_log"].reshape(-1), gr["dn_dt_bias"].reshape(-1), gr["dn_onorm_g"].reshape(-1),
                             gr["sb_q_norm_g"].reshape(-1), gr["sb_k_norm_g"].reshape(-1)])
    small = jnp.pad(small, (0, D - small.size)).reshape(1, D)
    blk3 = jnp.concatenate([gr["mod"].reshape(-1, D), gr["norm1_g"], gr["norm2_g"], gr["dn_conv_w"].reshape(-1, D),
                            small], axis=0)
    blk3 = jnp.pad(blk3, ((0, 64 - blk3.shape[0]), (0, 0)))
    g3 = _all_gather(blk3, "gather_small_grads", True)
    tot = _sum_sources(g3, "sum_small_grads")
    grads["ada_b"] = tot[:24].reshape(DEPTH, N_MOD * D)
    grads["norm1_g"] = tot[24:28]
    grads["norm2_g"] = tot[28:32]
    conv_g = tot[32:56].reshape(2, DN_CONV, NDEV * conv_cols)
    grads["dn_conv_w"] = lax.dynamic_slice_in_dim(conv_g, me * conv_cols, conv_cols, axis=2)
    sm = tot[56]
    grads["dn_a_log"] = sm[0:16].reshape(2, DN_H)
    grads["dn_dt_bias"] = sm[16:32].reshape(2, DN_H)
    grads["dn_onorm_g"] = sm[32:288].reshape(2, DN_D)
    grads["sb_q_norm_g"] = sm[288:416].reshape(2, SB_D)
    grads["sb_k_norm_g"] = sm[416:544].reshape(2, SB_D)
    dmod_all = g3[:, :24, :].reshape(NDEV, DEPTH, N_MOD * D)
    dmod_cols = lax.dynamic_slice_in_dim(dmod_all, me * ada_cols, ada_cols, axis=2)
    dmod16 = jnp.pad(jnp.transpose(dmod_cols, (1, 0, 2)), ((0, 0), (0, 8), (0, 0)))
    grads["ada_w"] = _ada_bwd(c16, dmod16, "ada_bwd")
    return grads


def _reduce_big(recv):
    parts = {}
    for st in range(DEPTH + 1):
        parts.update(_unpack_stage(_sum_sources(recv[st], f"sum_grads_{st}"), st))
    out = {}
    for (n, j) in sorted(parts):
        out.setdefault(n, []).append(parts[(n, j)])
    return {n: jnp.stack(v) for n, v in out.items()}


def kernel(x, c, ada_w, ada_b, norm1_g, norm2_g, dn_w_in, dn_conv_w, dn_a_log, dn_dt_bias, dn_onorm_g, dn_w_out, sb_w_qkv, sb_q_norm_g, sb_k_norm_g, sb_w_out, ffn_w_in, ffn_w_out, loss_target, m_ada_w, m_ada_b, m_norm1_g, m_norm2_g, m_dn_w_in, m_dn_conv_w, m_dn_a_log, m_dn_dt_bias, m_dn_onorm_g, m_dn_w_out, m_sb_w_qkv, m_sb_q_norm_g, m_sb_k_norm_g, m_sb_w_out, m_ffn_w_in, m_ffn_w_out, v_ada_w, v_ada_b, v_norm1_g, v_norm2_g, v_dn_w_in, v_dn_conv_w, v_dn_a_log, v_dn_dt_bias, v_dn_onorm_g, v_dn_w_out, v_sb_w_qkv, v_sb_q_norm_g, v_sb_k_norm_g, v_sb_w_out, v_ffn_w_in, v_ffn_w_out):
    w = dict(ada_w=ada_w, ada_b=ada_b, norm1_g=norm1_g, norm2_g=norm2_g, dn_w_in=dn_w_in, dn_conv_w=dn_conv_w,
             dn_a_log=dn_a_log, dn_dt_bias=dn_dt_bias, dn_onorm_g=dn_onorm_g, dn_w_out=dn_w_out, sb_w_qkv=sb_w_qkv,
             sb_q_norm_g=sb_q_norm_g, sb_k_norm_g=sb_k_norm_g, sb_w_out=sb_w_out, ffn_w_in=ffn_w_in, ffn_w_out=ffn_w_out)
    mom = dict(ada_w=m_ada_w, ada_b=m_ada_b, norm1_g=m_norm1_g, norm2_g=m_norm2_g, dn_w_in=m_dn_w_in,
               dn_conv_w=m_dn_conv_w, dn_a_log=m_dn_a_log, dn_dt_bias=m_dn_dt_bias, dn_onorm_g=m_dn_onorm_g,
               dn_w_out=m_dn_w_out, sb_w_qkv=m_sb_w_qkv, sb_q_norm_g=m_sb_q_norm_g, sb_k_norm_g=m_sb_k_norm_g,
               sb_w_out=m_sb_w_out, ffn_w_in=m_ffn_w_in, ffn_w_out=m_ffn_w_out)
    var = dict(ada_w=v_ada_w, ada_b=v_ada_b, norm1_g=v_norm1_g, norm2_g=v_norm2_g, dn_w_in=v_dn_w_in,
               dn_conv_w=v_dn_conv_w, dn_a_log=v_dn_a_log, dn_dt_bias=v_dn_dt_bias, dn_onorm_g=v_dn_onorm_g,
               dn_w_out=v_dn_w_out, sb_w_qkv=v_sb_w_qkv, sb_q_norm_g=v_sb_q_norm_g, sb_k_norm_g=v_sb_k_norm_g,
               sb_w_out=v_sb_w_out, ffn_w_in=v_ffn_w_in, ffn_w_out=v_ffn_w_out)
    names = list(w)
    c16, conv_full, mod = _gather_small(w, c)
    packs = [_pack_stage(w, st) for st in range(DEPTH + 1)]
    w0 = _full_stage(_all_gather(packs[0], "gather_weights_0", False), 0)
    small = dict(norm1_g=norm1_g, norm2_g=norm2_g, dn_conv_w=conv_full, dn_a_log=dn_a_log, dn_dt_bias=dn_dt_bias,
                 dn_onorm_g=dn_onorm_g, sb_q_norm_g=sb_q_norm_g, sb_k_norm_g=sb_k_norm_g)
    sq, grad_x, sgr, recv = _local_step(x[0], loss_target[0], mod, small, w0, ("packed", packs[1:]))
    loss = lax.psum(sq[0, 0] * (0.5 / D), ("x", "y", "c"))
    grads = {**_reduce_big(recv), **_reduce_small(sgr, c16)}
    delta, new_m, new_v = {}, {}, {}
    for n in names:
        delta[n], new_m[n], new_v[n] = _adamw(w[n], grads[n], mom[n], var[n], f"adamw_{n}")
    return (loss, grad_x[None], *[grads[n] for n in names], *[delta[n] for n in names],
            *[new_m[n] for n in names], *[new_v[n] for n in names])
```

```python
import jax
import jax.numpy as jnp
from jax import lax
from jax.experimental import pallas as pl
from jax.experimental.pallas import tpu as pltpu

F32, BF16 = jnp.float32, jnp.bfloat16

D = 1024
DEPTH = 4
N_MOD = 6
DN_H, DN_D, DN_C, DN_CONV = 8, 128, 64, 4
DN_UNROLL_FWD, DN_UNROLL_BWD = 16, 8
DN_COLS = 4 * DN_H * DN_D + 2 * DN_H
DN_COLS_PAD = 33 * 128
SB_H, SB_D = 16, 64
SB_TILE = 256
SB_HPB = 4
SB_W = SB_HPB * SB_D
DFF = 2816
EPS = 1e-6
NDEV = 8
LANES = 128
VMEM_LIMIT = 56 * 1024 * 1024

ADAM_LR, ADAM_B1, ADAM_B2, ADAM_EPS, ADAM_WD, ADAM_STEP = 0.001, 0.9, 0.999, 1e-08, 0.01, 10

NN = ((1,), (0,))
NT = ((1,), (1,))
TN = ((0,), (0,))
MESH = pl.DeviceIdType.MESH


def _dot(a, b, dims=NN):
    a, b = a.astype(BF16), b.astype(BF16)
    if a.ndim == 2 and b.ndim == 2:
        return lax.dot_general(a, b, (dims, ((), ())), preferred_element_type=F32)
    n = a.shape[0] if a.ndim == 3 else b.shape[0]
    if a.ndim == 2:
        a = jnp.broadcast_to(a, (n,) + a.shape)
    if b.ndim == 2:
        b = jnp.broadcast_to(b, (n,) + b.shape)
    (ca,), (cb,) = dims
    return lax.dot_general(a, b, (((ca + 1,), (cb + 1,)), ((0,), (0,))), preferred_element_type=F32)


def _split(a):
    hi = a.astype(BF16)
    lo = (a - hi.astype(F32)).astype(BF16)
    return hi, lo


def _dot2r(a, m, dims=NN):
    ah, al = _split(a)
    return _dot(ah, m, dims) + _dot(al, m, dims)


def _dot2l(m, b, dims=NN):
    bh, bl = _split(b)
    return _dot(m, bh, dims) + _dot(m, bl, dims)


def _sigmoid(x):
    return 1.0 / (1.0 + jnp.exp(-x))


def _silu(x):
    return x * _sigmoid(x)


def _dsilu(x):
    s = _sigmoid(x)
    return s * (1.0 + x * (1.0 - s))


def _softplus(x):
    return jnp.maximum(x, 0.0) + jnp.log(1.0 + jnp.exp(-jnp.abs(x)))


def _iota(shape, dim):
    return lax.broadcasted_iota(jnp.int32, shape, dim)


def _rowsum(x):
    return jnp.sum(x, axis=-1, keepdims=True)


def _colsum(x):
    return jnp.sum(x, axis=-2, keepdims=True)


def _tile(n, pref):
    if n <= pref:
        return n
    best = None
    for t in range(LANES, pref + 1, LANES):
        if n % t == 0:
            best = t
    assert best is not None, (n, pref)
    return best


def _rtile(r, pref=512):
    best = None
    for t in range(8, min(r, pref) + 1, 8):
        if r % t == 0:
            best = t
    return best if best is not None else r


def _params(sem):
    return pltpu.CompilerParams(dimension_semantics=sem, vmem_limit_bytes=VMEM_LIMIT)


MM_TILE = 1408


def _mm(a, b, mode, out_dtype, name):
    if mode == "nn":
        (M, K), (K2, N) = a.shape, b.shape
        dims = NN
    elif mode == "nt":
        (M, K), (N, K2) = a.shape, b.shape
        dims = NT
    else:
        (K, M), (K2, N) = a.shape, b.shape
        dims = TN
    assert K == K2, (a.shape, b.shape, mode)
    tm, tn, tk = _tile(M, MM_TILE), _tile(N, MM_TILE), _tile(K, MM_TILE)
    nk = K // tk

    def body_single(a_ref, b_ref, o_ref):
        o_ref[...] = _dot(a_ref[...], b_ref[...], dims).astype(o_ref.dtype)

    def body_acc(a_ref, b_ref, o_ref, acc_ref):
        k = pl.program_id(2)
        p = _dot(a_ref[...], b_ref[...], dims)

        @pl.when(k == 0)
        def _():
            acc_ref[...] = p

        @pl.when(k > 0)
        def _():
            acc_ref[...] += p

        @pl.when(k == nk - 1)
        def _():
            o_ref[...] = acc_ref[...].astype(o_ref.dtype)

    if mode == "nn":
        a_spec = pl.BlockSpec((tm, tk), lambda i, j, k: (i, k))
        b_spec = pl.BlockSpec((tk, tn), lambda i, j, k: (k, j))
    elif mode == "nt":
        a_spec = pl.BlockSpec((tm, tk), lambda i, j, k: (i, k))
        b_spec = pl.BlockSpec((tn, tk), lambda i, j, k: (j, k))
    else:
        a_spec = pl.BlockSpec((tk, tm), lambda i, j, k: (k, i))
        b_spec = pl.BlockSpec((tk, tn), lambda i, j, k: (k, j))
    return pl.pallas_call(
        body_single if nk == 1 else body_acc, grid=(M // tm, N // tn, nk), in_specs=[a_spec, b_spec],
        out_specs=pl.BlockSpec((tm, tn), lambda i, j, k: (i, j)),
        out_shape=jax.ShapeDtypeStruct((M, N), out_dtype),
        scratch_shapes=[] if nk == 1 else [pltpu.VMEM((tm, tn), F32)], name=name,
        compiler_params=_params(("parallel", "parallel", "arbitrary")),
    )(a, b)


def _rowwise(fn, name, rows, bcasts, out_rows, out_reds=(), tile=256):
    T = rows[0].shape[0]
    tile = min(tile, T)
    nr, nb, no = len(rows), len(bcasts), len(out_rows)

    def body(*refs):
        rv = [r[...] for r in refs[:nr]]
        bv = [r[...] for r in refs[nr:nr + nb]]
        outs, reds = fn(rv, bv)
        for r, o in zip(refs[nr + nb:nr + nb + no], outs):
            r[...] = o.astype(r.dtype)
        red_refs = refs[nr + nb + no:]
        if red_refs:
            @pl.when(pl.program_id(0) == 0)
            def _():
                for r in red_refs:
                    r[...] = jnp.zeros(r.shape, F32)

            for r, v in zip(red_refs, reds):
                r[...] += v

    in_specs = [pl.BlockSpec((tile, a.shape[1]), lambda i: (i, 0)) for a in rows]
    in_specs += [pl.BlockSpec(b.shape, lambda i: (0, 0)) for b in bcasts]
    out_specs = [pl.BlockSpec((tile, c), lambda i: (i, 0)) for c, _ in out_rows]
    out_specs += [pl.BlockSpec(s, lambda i: (0, 0)) for s in out_reds]
    out_shape = [jax.ShapeDtypeStruct((T, c), dt) for c, dt in out_rows]
    out_shape += [jax.ShapeDtypeStruct(s, F32) for s in out_reds]
    return pl.pallas_call(
        body, grid=(T // tile,), in_specs=in_specs, out_specs=out_specs, out_shape=out_shape, name=name,
        compiler_params=_params(("arbitrary",)),
    )(*rows, *bcasts)


def _norm_mod(x, g, sc, sh, name):
    def fn(rv, bv):
        (xv,), (gv, scv, shv) = rv, bv
        r = lax.rsqrt(jnp.mean(xv * xv, axis=1, keepdims=True) + EPS)
        return [(xv * r * gv) * (1.0 + scv) + shv], []
    return _rowwise(fn, name, [x], [g, sc, sh], [(D, BF16)])[0]


def _norm_mod_bwd(dh, x, dres, g, sc, name):
    def fn(rv, bv):
        (dhv, xv, drv), (gv, scv) = rv, bv
        r = lax.rsqrt(jnp.mean(xv * xv, axis=1, keepdims=True) + EPS)
        xhat = xv * r
        dxhat = dhv * (gv * (1.0 + scv))
        dx = r * (dxhat - xhat * jnp.mean(dxhat * xhat, axis=1, keepdims=True)) + drv
        return [dx], [_colsum(dhv * xhat), _colsum(dhv)]
    return _rowwise(fn, name, [dh, x, dres], [g, sc], [(D, F32)], [(1, D), (1, D)])


def _gate_res(x, y, gt, name):
    def fn(rv, bv):
        return [rv[0] + bv[0] * rv[1]], []
    return _rowwise(fn, name, [x, y], [gt], [(D, F32)])[0]


def _gate_res_bwd(dxn, y, gt, name):
    def fn(rv, bv):
        return [rv[0] * bv[0]], [_colsum(rv[0] * rv[1])]
    return _rowwise(fn, name, [dxn, y], [gt], [(D, BF16)], [(1, D)])


def _swiglu_act(u, name):
    def fn(rv, bv):
        uv = rv[0].astype(F32)
        return [_silu(uv[:, :DFF]) * uv[:, DFF:]], []
    return _rowwise(fn, name, [u], [], [(DFF, BF16)])[0]


def _swiglu_act_bwd(da, u, name):
    def fn(rv, bv):
        dav, uv = rv[0].astype(F32), rv[1].astype(F32)
        gate, up = uv[:, :DFF], uv[:, DFF:]
        return [jnp.concatenate([dav * up * _dsilu(gate), dav * _silu(gate)], axis=1)], []
    return _rowwise(fn, name, [da, u], [], [(2 * DFF, BF16)])[0]


def _loss_head(y, target, name):
    def fn(rv, bv):
        err = rv[0] - rv[1]
        return [err * (1.0 / D)], [_colsum(_rowsum(err * err))]
    return _rowwise(fn, name, [y, target], [], [(D, F32)], [(1, 1)])


def _shift_rows(x, s):
    if s == 0:
        return x
    T = x.shape[0]
    r = pltpu.roll(x, s % T, axis=0)
    t = _iota(x.shape, 0)
    keep = (t >= s) if s > 0 else (t < T + s)
    return jnp.where(keep, r, 0.0)


def _dn_prep(pq, pk, pv, pab, cq_ref, ck_ref, cv_ref, alog, dtb, h):
    lane = _iota(pab.shape, 1)
    a_col = _rowsum(jnp.where(lane == h, pab, 0.0))
    b_col = _rowsum(jnp.where(lane == DN_H + h, pab, 0.0))
    lane1 = _iota(alog.shape, 1)
    alog_h = _rowsum(jnp.where(lane1 == h, alog, 0.0))
    dtb_h = _rowsum(jnp.where(lane1 == h, dtb, 0.0))
    pre = a_col + dtb_h
    neg_ea = -jnp.exp(alog_h)
    g = neg_ea * _softplus(pre)
    beta = _sigmoid(b_col)

    def conv(x, w_ref):
        acc = x * w_ref[DN_CONV - 1:DN_CONV, :]
        for i in range(DN_CONV - 1):
            acc = acc + _shift_rows(x, DN_CONV - 1 - i) * w_ref[i:i + 1, :]
        return acc

    xq, xk, xv = conv(pq, cq_ref), conv(pk, ck_ref), conv(pv, cv_ref)
    sq, sk, v = _silu(xq), _silu(xk), _silu(xv)
    rq = lax.rsqrt(_rowsum(sq * sq) + EPS)
    rk = lax.rsqrt(_rowsum(sk * sk) + EPS)
    return dict(g=g, beta=beta, pre=pre, neg_ea=neg_ea, xq=xq, xk=xk, xv=xv, rq=rq, rk=rk,
                qn=sq * rq, kn=sk * rk, v=v)


def _dn_masks():
    C = DN_C
    r, c = _iota((C, C), 0), _iota((C, C), 1)
    incl = r >= c
    strict = r > c
    blk16 = jnp.right_shift(r, 4) == jnp.right_shift(c, 4)
    blk32 = jnp.right_shift(r, 5) == jnp.right_shift(c, 5)
    return dict(incl=incl, strict=strict, upper=r <= c, blk16=blk16, blk32=blk32,
                tri=incl.astype(BF16), triT=(r <= c).astype(BF16), ones=jnp.ones((C, C), BF16),
                eye=(r == c).astype(F32), last=_iota((C, 1), 0) == C - 1)


def _tri_inverse(A, mk):
    P = -jnp.where(mk["blk16"], A, 0.0)
    X = mk["eye"] + P
    for _ in range(3):
        P = _dot(P, P)
        X = X + _dot(X, P)
    off1 = jnp.where(mk["blk32"] & (~mk["blk16"]), A, 0.0)
    X = X - _dot(_dot(X, off1), X)
    off2 = jnp.where(mk["blk32"], 0.0, A)
    X = X - _dot(_dot(X, off2), X)
    return X


def _dn_local(qc, kc, vc, gc, bc, mk):
    C = DN_C
    gm = jnp.broadcast_to(gc, gc.shape[:-1] + (C,))
    Gc = _dot2l(mk["tri"], gm)
    Gr = _dot2l(mk["ones"], jnp.where(mk["upper"], gm, 0.0))
    Dm = jnp.where(mk["incl"], jnp.exp(jnp.where(mk["incl"], Gc - Gr, 0.0)), 0.0)
    Gcol = jnp.max(Gc, axis=-1, keepdims=True)
    Gl = _colsum(jnp.where(mk["last"], Gcol, 0.0))
    eG = jnp.exp(Gcol)
    eT = jnp.exp(Gl - Gcol)
    gl = jnp.exp(Gl)
    kb = kc * bc
    vb = vc * bc
    KK = _dot(kb, kc, NT)
    Tinv = _tri_inverse(jnp.where(mk["strict"], KK * Dm, 0.0), mk)
    KBE = kb * eG
    QK = _dot(qc, kc, NT)
    return dict(Dm=Dm, eG=eG, eT=eT, gl=gl, kb=kb, vb=vb, KK=KK, Tinv=Tinv, KBE=KBE, U=_dot(Tinv, vb),
                W=_dot(Tinv, KBE), QK=QK, attn=QK * Dm, QD=qc * eG, KT=kc * eT)


def _dn_recur(f, S):
    vnew = f["U"] - _dot(f["W"], S)
    o = _dot(f["QD"], S) + _dot(f["attn"], vnew)
    return o, S * f["gl"] + _dot(f["KT"], vnew, TN), vnew


def _dn_bwd_chain(f, do, dS):
    dvnew = _dot(f["KT"], dS) + _dot(f["attn"], do, TN)
    return dvnew, dS * f["gl"] + _dot(f["QD"], do, TN) - _dot(f["W"], dvnew, TN)


def _dn_bwd_rest(qc, kc, vc, bc, f, S, vnew, do, dS, dvnew, mk):
    C = DN_C
    Dm, eG, eT, gl, kb, vb, KK, Tinv, KBE, QK = (
        f[n] for n in ("Dm", "eG", "eT", "gl", "kb", "vb", "KK", "Tinv", "KBE", "QK"))
    dKT = _dot(vnew, dS, NT)
    dgl = _colsum(_rowsum(dS * S))
    dQD = _dot(do, S, NT)
    dattn = _dot(do, vnew, NT)
    dU = dvnew
    dW = -_dot(dvnew, S, NT)
    dQK = dattn * Dm
    dD = dattn * QK
    dq = _dot(dQK, kc)
    dk = _dot(dQK, qc, TN)
    dTinv = _dot(dU, vb, NT) + _dot(dW, KBE, NT)
    dvb = _dot(Tinv, dU, TN)
    dKBE = _dot(Tinv, dW, TN)
    dA = -_dot(_dot(Tinv, dTinv, TN), Tinv, NT)
    dA = jnp.where(mk["strict"], dA, 0.0)
    dKK = dA * Dm
    dD = dD + dA * KK
    dkb = _dot(dKK, kc) + dKBE * eG
    dk = dk + _dot(dKK, kb, TN)
    deG = _rowsum(dKBE * kb)
    dk = dk + dkb * bc
    dbeta = _rowsum(dkb * kc) + _rowsum(dvb * vc)
    dv = dvb * bc
    dq = dq + dQD * eG
    deG = deG + _rowsum(dQD * qc)
    dk = dk + dKT * eT
    deT = _rowsum(dKT * kc)
    dGcol = deG * eG - deT * eT
    dGl = _colsum(deT * eT) + dgl * gl
    Y = dD * Dm
    ycol = jnp.max(_dot2r(Y, mk["ones"], TN), axis=-1, keepdims=True)
    dGcol = dGcol + _rowsum(Y) - ycol
    dGcol = dGcol + jnp.where(mk["last"], dGl, 0.0)
    dg = jnp.max(_dot2l(mk["triT"], jnp.broadcast_to(dGcol, dGcol.shape[:-1] + (C,))), axis=-1, keepdims=True)
    return dq, dk, dv, dg, dbeta


def _dn_core_fwd(proj, conv_w, alog, dtb, og, name, comm=None):
    T = proj.shape[0]
    N = T // DN_C

    def body(pq_ref, pk_ref, pv_ref, pz_ref, pab_ref, cq_ref, ck_ref, cv_ref, alog_ref, dtb_ref, og_ref,
             out_ref, o_ref, st_ref, q_s, k_s, v_s, g_s, b_s, S_s):
        h = pl.program_id(0)
        p = _dn_prep(pq_ref[...], pk_ref[...], pv_ref[...], pab_ref[...], cq_ref, ck_ref, cv_ref,
                     alog_ref[...], dtb_ref[...], h)
        q_s[...] = p["qn"] * (DN_D ** -0.5)
        k_s[...] = p["kn"]
        v_s[...] = p["v"]
        g_s[...] = p["g"]
        b_s[...] = p["beta"]
        S_s[...] = jnp.zeros(S_s.shape, F32)
        mk = _dn_masks()

        nu = min(DN_UNROLL_FWD, N)

        def step(it, carry):
            rows = pl.ds(pl.multiple_of(it * (nu * DN_C), nu * DN_C), nu * DN_C)
            loc = _dn_local(*(r[rows, :].reshape(nu, DN_C, r.shape[1]) for r in (q_s, k_s, v_s, g_s, b_s)), mk)
            S = S_s[...]
            outs = []
            for u in range(nu):
                st_ref[0, it * nu + u] = S
                o, S, _ = _dn_recur({n: v[u] for n, v in loc.items()}, S)
                outs.append(o)
            o_ref[rows, :] = jnp.concatenate(outs, axis=0)
            S_s[...] = S
            return carry

        lax.fori_loop(0, N // nu, step, 0)
        o = o_ref[...]
        ro = lax.rsqrt(jnp.mean(o * o, axis=1, keepdims=True) + EPS)
        out_ref[...] = ((o * ro * og_ref[...]) * _silu(pz_ref[...])).astype(out_ref.dtype)

    col = lambda k: pl.BlockSpec((T, DN_D), lambda h: (0, k * DN_H + h))
    cw = lambda k: pl.BlockSpec((DN_CONV, DN_D), lambda h: (0, k * DN_H + h))
    small = pl.BlockSpec((1, LANES), lambda h: (0, 0))
    return _grid_call(
        body, comm, grid=(DN_H,),
        in_specs=[col(0), col(1), col(2), col(3), pl.BlockSpec((T, LANES), lambda h: (0, 4 * DN_H)),
                  cw(0), cw(1), cw(2), small, small, small],
        out_specs=[pl.BlockSpec((T, DN_D), lambda h: (0, h)), pl.BlockSpec((T, DN_D), lambda h: (0, h)),
                   pl.BlockSpec((1, N, DN_D, DN_D), lambda h: (h, 0, 0, 0))],
        out_shape=[jax.ShapeDtypeStruct((T, D), BF16), jax.ShapeDtypeStruct((T, D), F32),
                   jax.ShapeDtypeStruct((DN_H, N, DN_D, DN_D), F32)],
        scratch_shapes=[pltpu.VMEM((T, DN_D), F32)] * 3 + [pltpu.VMEM((T, 1), F32)] * 2 + [pltpu.VMEM((DN_D, DN_D), F32)],
        name=name, args=(proj, proj, proj, proj, proj, conv_w, conv_w, conv_w, alog, dtb, og))


def _dn_core_bwd(proj, conv_w, alog, dtb, og, o, states, dout, name, comm=None):
    T = proj.shape[0]
    N = T // DN_C

    def body(pq_ref, pk_ref, pv_ref, pz_ref, pab_ref, cq_ref, ck_ref, cv_ref, alog_ref, dtb_ref, og_ref,
             o_ref, st_ref, dout_ref,
             dpq_ref, dpk_ref, dpv_ref, dpz_ref, dpab_ref, dcq_ref, dck_ref, dcv_ref, dalog_ref, ddtb_ref, dog_ref,
             q_s, k_s, v_s, g_s, b_s, do_s, dS_s):
        h = pl.program_id(0)
        scale = DN_D ** -0.5

        def prep():
            return _dn_prep(pq_ref[...], pk_ref[...], pv_ref[...], pab_ref[...], cq_ref, ck_ref, cv_ref,
                            alog_ref[...], dtb_ref[...], h)

        p = prep()
        q_s[...] = p["qn"] * scale
        k_s[...] = p["kn"]
        v_s[...] = p["v"]
        g_s[...] = p["g"]
        b_s[...] = p["beta"]
        del p

        o = o_ref[...]
        z = pz_ref[...]
        dout = dout_ref[...]
        ogv = og_ref[...]
        ro = lax.rsqrt(jnp.mean(o * o, axis=1, keepdims=True) + EPS)
        on = o * ro
        dy = dout * _silu(z)
        dpz_ref[...] = (dout * (on * ogv) * _dsilu(z)).astype(dpz_ref.dtype)
        dyg = dy * ogv
        do_s[...] = ro * (dyg - on * jnp.mean(dyg * on, axis=1, keepdims=True))
        dog_h = _colsum(dy * on)

        dS_s[...] = jnp.zeros(dS_s.shape, F32)
        mk = _dn_masks()

        nu = min(DN_UNROLL_BWD, N)

        def step(it, carry):
            c0 = (N // nu - 1 - it) * nu
            rows = pl.ds(pl.multiple_of(c0 * DN_C, nu * DN_C), nu * DN_C)
            q, k, v, g, b, do = (r[rows, :].reshape(nu, DN_C, r.shape[1]) for r in (q_s, k_s, v_s, g_s, b_s, do_s))
            loc = _dn_local(q, k, v, g, b, mk)
            Ss = st_ref[0, pl.ds(c0, nu)]
            vnew = loc["U"] - _dot(loc["W"], Ss)
            dS = dS_s[...]
            dS_in, dvnew = [None] * nu, [None] * nu
            for u in reversed(range(nu)):
                dS_in[u] = dS
                dvnew[u], dS = _dn_bwd_chain({n: x[u] for n, x in loc.items()}, do[u], dS)
            dS_s[...] = dS
            grads = _dn_bwd_rest(q, k, v, b, loc, Ss, vnew, do, jnp.stack(dS_in), jnp.stack(dvnew), mk)
            for r, d in zip((q_s, k_s, v_s, g_s, b_s), grads):
                r[rows, :] = d.reshape(nu * DN_C, r.shape[1])
            return carry

        lax.fori_loop(0, N // nu, step, 0)

        p = prep()
        pq, pk, pv = pq_ref[...], pk_ref[...], pv_ref[...]
        dqn = q_s[...] * scale
        dkn = k_s[...]
        qn, kn = p["qn"], p["kn"]
        dsq = p["rq"] * (dqn - qn * _rowsum(dqn * qn))
        dsk = p["rk"] * (dkn - kn * _rowsum(dkn * kn))
        dxq = dsq * _dsilu(p["xq"])
        dxk = dsk * _dsilu(p["xk"])
        dxv = v_s[...] * _dsilu(p["xv"])

        def conv_bwd(dx, x, w_ref, dp_ref, dc_ref):
            acc = dx * w_ref[DN_CONV - 1:DN_CONV, :]
            dc_ref[DN_CONV - 1:DN_CONV, :] = _colsum(dx * x)
            for i in range(DN_CONV - 1):
                s = DN_CONV - 1 - i
                acc = acc + _shift_rows(dx, -s) * w_ref[i:i + 1, :]
                dc_ref[i:i + 1, :] = _colsum(dx * _shift_rows(x, s))
            dp_ref[...] = acc.astype(dp_ref.dtype)

        conv_bwd(dxq, pq, cq_ref, dpq_ref, dcq_ref)
        conv_bwd(dxk, pk, ck_ref, dpk_ref, dck_ref)
        conv_bwd(dxv, pv, cv_ref, dpv_ref, dcv_ref)

        dg = g_s[...]
        beta = p["beta"]
        da_raw = dg * p["neg_ea"] * _sigmoid(p["pre"])
        db_raw = b_s[...] * beta * (1.0 - beta)
        lane = _iota((T, LANES), 1)
        contrib = jnp.where(lane == h, da_raw, 0.0) + jnp.where(lane == DN_H + h, db_raw, 0.0)
        lane1 = _iota((1, LANES), 1)
        dalog_h = jnp.where(lane1 == h, _colsum(dg * p["g"]), 0.0)
        ddtb_h = jnp.where(lane1 == h, _colsum(da_raw), 0.0)

        @pl.when(h == 0)
        def _():
            dpab_ref[...] = contrib.astype(dpab_ref.dtype)
            dalog_ref[...] = dalog_h
            ddtb_ref[...] = ddtb_h
            dog_ref[...] = dog_h

        @pl.when(h > 0)
        def _():
            dpab_ref[...] += contrib.astype(dpab_ref.dtype)
            dalog_ref[...] += dalog_h
            ddtb_ref[...] += ddtb_h
            dog_ref[...] += dog_h

    col = lambda k: pl.BlockSpec((T, DN_D), lambda h: (0, k * DN_H + h))
    cw = lambda k: pl.BlockSpec((DN_CONV, DN_D), lambda h: (0, k * DN_H + h))
    small = pl.BlockSpec((1, LANES), lambda h: (0, 0))
    ab = pl.BlockSpec((T, LANES), lambda h: (0, 4 * DN_H))
    hcol = pl.BlockSpec((T, DN_D), lambda h: (0, h))
    outs = _grid_call(
        body, comm, grid=(DN_H,),
        in_specs=[col(0), col(1), col(2), col(3), ab, cw(0), cw(1), cw(2), small, small, small,
                  hcol, pl.BlockSpec((1, N, DN_D, DN_D), lambda h: (h, 0, 0, 0)), hcol],
        out_specs=[hcol, hcol, hcol, hcol, pl.BlockSpec((T, LANES), lambda h: (0, 0)),
                   pl.BlockSpec((DN_CONV, DN_D), lambda h: (0, h)), pl.BlockSpec((DN_CONV, DN_D), lambda h: (0, h)),
                   pl.BlockSpec((DN_CONV, DN_D), lambda h: (0, h)), small, small, small],
        out_shape=[jax.ShapeDtypeStruct((T, D), BF16)] * 4 + [jax.ShapeDtypeStruct((T, LANES), BF16)]
                  + [jax.ShapeDtypeStruct((DN_CONV, D), F32)] * 3 + [jax.ShapeDtypeStruct((1, LANES), F32)] * 3,
        scratch_shapes=[pltpu.VMEM((T, DN_D), F32)] * 3 + [pltpu.VMEM((T, 1), F32)] * 2
                       + [pltpu.VMEM((T, DN_D), F32), pltpu.VMEM((DN_D, DN_D), F32)],
        name=name, args=(proj, proj, proj, proj, proj, conv_w, conv_w, conv_w, alog, dtb, og, o, states, dout))
    dpq, dpk, dpv, dpz, dpab, dcq, dck, dcv, dalog, ddtb, dog = outs[:11]
    dproj = jnp.concatenate([dpq, dpk, dpv, dpz, dpab], axis=1)
    dconv = jnp.concatenate([dcq, dck, dcv], axis=1)
    return dproj, dconv, dalog, ddtb, dog, (outs[11] if comm else None)


def _sb_head_of(shape):
    return jnp.right_shift(_iota(shape, 1), 6)


def _sb_head_sums(x, head):
    out = jnp.zeros_like(x)
    for hh in range(SB_HPB):
        out = jnp.where(head == hh, _rowsum(jnp.where(head == hh, x, 0.0)), out)
    return out


def _sb_head_norm(x, head):
    r = lax.rsqrt(_sb_head_sums(x * x, head) * (1.0 / SB_D) + EPS)
    return x * r, r


def _sb_fill(q_ref, k_ref, v_ref, qg_ref, kg_ref, qs_s, kn_s, v_s):
    head = _sb_head_of(q_ref.shape)
    qs_s[...] = (_sb_head_norm(q_ref[...], head)[0] * qg_ref[...] * (SB_D ** -0.5)).astype(BF16)
    kn_s[...] = (_sb_head_norm(k_ref[...], head)[0] * kg_ref[...]).astype(BF16)
    v_s[...] = v_ref[...].astype(BF16)


def _sb_head_masks(dtype):
    return jnp.stack([(_sb_head_of((1, SB_W)) == hh).astype(dtype) for hh in range(SB_HPB)])


def _sb_core_fwd(qkv, qg, kg, name, comm=None):
    T = qkv.shape[0]
    B = min(SB_TILE, T)
    NB = T // B
    NP = SB_H // SB_HPB

    def body(q_ref, k_ref, v_ref, qg_ref, kg_ref, o_ref, tot_ref, qs_s, kn_s, v_s):
        _sb_fill(q_ref, k_ref, v_ref, qg_ref, kg_ref, qs_s, kn_s, v_s)
        r, c = _iota((B, B), 0), _iota((B, B), 1)
        causal = c < r
        m_after = (r > c).astype(BF16)
        head_b = _sb_head_of((B, SB_W))
        hm = _sb_head_masks(BF16)

        def tile(qb, kj, vj, R, acc, diag):
            z = _dot(qb, kj, NT)
            sp = _softplus(z)
            ls = z - sp
            lm = jnp.where(causal, -sp, 0.0) if diag else -sp
            cs = _dot2r(lm, m_after) + R
            a = jnp.exp(ls + cs)
            if diag:
                a = jnp.where(causal, a, 0.0)
            return R + _rowsum(lm), acc + jnp.sum(_dot(a, vj), axis=0)

        def qblock(i, carry):
            rows_i = pl.ds(pl.multiple_of(i * B, B), B)
            qb = qs_s[rows_i, :][None] * hm

            def step(rows_j, st, diag):
                return tile(qb, kn_s[rows_j, :], v_s[rows_j, :][None] * hm, st[0], st[1], diag)

            st = step(rows_i, (jnp.zeros((SB_HPB, B, 1), F32), jnp.zeros((B, SB_W), F32)), True)
            st = lax.fori_loop(0, i, lambda s, st: step(pl.ds(pl.multiple_of((i - 1 - s) * B, B), B), st, False), st)
            o_ref[rows_i, :] = st[1].astype(o_ref.dtype)
            tot = jnp.zeros((B, SB_W), F32)
            for hh in range(SB_HPB):
                tot = jnp.where(head_b == hh, st[0][hh], tot)
            tot_ref[rows_i, :] = tot
            return carry

        lax.fori_loop(0, NB, qblock, 0)

    blk = lambda k: pl.BlockSpec((T, SB_W), lambda p: (0, k * NP + p))
    small = pl.BlockSpec((1, SB_W), lambda p: (0, 0))
    return _grid_call(
        body, comm, grid=(NP,), in_specs=[blk(0), blk(1), blk(2), small, small],
        out_specs=[pl.BlockSpec((T, SB_W), lambda p: (0, p))] * 2,
        out_shape=[jax.ShapeDtypeStruct((T, D), BF16), jax.ShapeDtypeStruct((T, D), F32)],
        scratch_shapes=[pltpu.VMEM((T, SB_W), BF16)] * 3, name=name, args=(qkv, qkv, qkv, qg, kg))


def _sb_core_bwd(qkv, qg, kg, tot, dout, name, comm=None):
    T = qkv.shape[0]
    B = min(SB_TILE, T)
    NB = T // B
    NP = SB_H // SB_HPB

    def body(q_ref, k_ref, v_ref, qg_ref, kg_ref, tot_ref, do_ref, dq_ref, dk_ref, dv_ref, dqg_ref, dkg_ref,
             qs_s, kn_s, v_s, dqn_s, dkn_s, dvv_s):
        p = pl.program_id(0)
        _sb_fill(q_ref, k_ref, v_ref, qg_ref, kg_ref, qs_s, kn_s, v_s)
        dkn_s[...] = jnp.zeros(dkn_s.shape, F32)
        dvv_s[...] = jnp.zeros(dvv_s.shape, F32)
        r, c = _iota((B, B), 0), _iota((B, B), 1)
        causal = c < r
        m_upto = (r <= c).astype(BF16)
        m_before = (r < c).astype(BF16)
        lane_b = _iota((B, SB_W), 1)
        hm, hmf = _sb_head_masks(BF16), _sb_head_masks(F32)

        def tile(qb, dob, tot_h, kj, vj, PL, P, dq, diag):
            z = _dot(qb, kj, NT)
            sp = _softplus(z)
            ls = z - sp
            lm = jnp.where(causal, -sp, 0.0) if diag else -sp
            cs = tot_h - PL - _dot2r(lm, m_upto)
            a = jnp.exp(ls + cs)
            if diag:
                a = jnp.where(causal, a, 0.0)
            e = _dot(dob, vj, NT) * a
            E = _dot(e, m_before) + P
            sig = jnp.exp(ls)
            dz = e * (1.0 - sig) - E * sig
            if diag:
                dz = jnp.where(causal, dz, 0.0)
            dq = dq + jnp.sum(_dot(dz, kj) * hmf, axis=0)
            return (PL + _rowsum(lm), P + _rowsum(e), dq), jnp.sum(_dot(dz, qb, TN), axis=0), jnp.sum(_dot(a, dob, TN), axis=0)

        def qblock(i, carry):
            rows_i = pl.ds(pl.multiple_of(i * B, B), B)
            totb = tot_ref[rows_i, :]
            tot_h = jnp.stack([_rowsum(jnp.where(lane_b == hh * SB_D, totb, 0.0)) for hh in range(SB_HPB)])
            qb = qs_s[rows_i, :][None] * hm
            dob = do_ref[rows_i, :][None] * hm

            def step(rows_j, st, diag):
                st, dk, dv = tile(qb, dob, tot_h, kn_s[rows_j, :], v_s[rows_j, :][None] * hm, st[0], st[1], st[2], diag)
                dkn_s[rows_j, :] += dk
                dvv_s[rows_j, :] += dv
                return st

            zero = (jnp.zeros((SB_HPB, B, 1), F32), jnp.zeros((SB_HPB, B, 1), F32), jnp.zeros((B, SB_W), F32))
            st = lax.fori_loop(0, i, lambda j, st: step(pl.ds(pl.multiple_of(j * B, B), B), st, False), zero)
            st = step(rows_i, st, True)
            dqn_s[rows_i, :] = st[2] * (SB_D ** -0.5)
            return carry

        lax.fori_loop(0, NB, qblock, 0)

        head = _sb_head_of((T, SB_W))

        def norm_bwd(dn, x_ref, g):
            xh, rr = _sb_head_norm(x_ref[...], head)
            t = dn * g
            return rr * (t - xh * (_sb_head_sums(t * xh, head) * (1.0 / SB_D))), _colsum(dn * xh)

        dq, dqg = norm_bwd(dqn_s[...], q_ref, qg_ref[...])
        dq_ref[...] = dq.astype(dq_ref.dtype)
        dk, dkg = norm_bwd(dkn_s[...], k_ref, kg_ref[...])
        dk_ref[...] = dk.astype(dk_ref.dtype)
        dv_ref[...] = dvv_s[...].astype(dv_ref.dtype)

        @pl.when(p == 0)
        def _():
            dqg_ref[...] = dqg
            dkg_ref[...] = dkg

        @pl.when(p > 0)
        def _():
            dqg_ref[...] += dqg
            dkg_ref[...] += dkg

    blk = lambda k: pl.BlockSpec((T, SB_W), lambda p: (0, k * NP + p))
    small = pl.BlockSpec((1, SB_W), lambda p: (0, 0))
    own = pl.BlockSpec((T, SB_W), lambda p: (0, p))
    outs = _grid_call(
        body, comm, grid=(NP,), in_specs=[blk(0), blk(1), blk(2), small, small, own, own],
        out_specs=[own, own, own, small, small],
        out_shape=[jax.ShapeDtypeStruct((T, D), BF16)] * 3 + [jax.ShapeDtypeStruct((1, SB_W), F32)] * 2,
        scratch_shapes=[pltpu.VMEM((T, SB_W), BF16)] * 3 + [pltpu.VMEM((T, SB_W), F32)] * 3,
        name=name, args=(qkv, qkv, qkv, qg, kg, tot, dout))
    dq, dk, dv, dqg, dkg = outs[:5]
    return jnp.concatenate([dq, dk, dv], axis=1), dqg, dkg, (outs[5] if comm else None)


def _ada_fwd(c16, ada_w, ada_b_cols, name):
    L, _, cols = ada_w.shape

    def body(c_ref, w_ref, b_ref, o_ref):
        o_ref[0] = _dot(_silu(c_ref[...]), w_ref[0]) + b_ref[0]

    return pl.pallas_call(
        body, grid=(L,),
        in_specs=[pl.BlockSpec((16, D), lambda i: (0, 0)), pl.BlockSpec((1, D, cols), lambda i: (i, 0, 0)),
                  pl.BlockSpec((1, 1, cols), lambda i: (i, 0, 0))],
        out_specs=pl.BlockSpec((1, 16, cols), lambda i: (i, 0, 0)),
        out_shape=jax.ShapeDtypeStruct((L, 16, cols), F32), name=name, compiler_params=_params(("arbitrary",)),
    )(c16, ada_w, ada_b_cols)


def _ada_bwd(c16, dmod16, name):
    L, _, cols = dmod16.shape

    def body(c_ref, d_ref, o_ref):
        o_ref[0] = _dot(_silu(c_ref[...]), d_ref[0], TN)

    return pl.pallas_call(
        body, grid=(L,),
        in_specs=[pl.BlockSpec((16, D), lambda i: (0, 0)), pl.BlockSpec((1, 16, cols), lambda i: (i, 0, 0))],
        out_specs=pl.BlockSpec((1, D, cols), lambda i: (i, 0, 0)),
        out_shape=jax.ShapeDtypeStruct((L, D, cols), F32), name=name, compiler_params=_params(("arbitrary",)),
    )(c16, dmod16)


def _sum_sources(x, name):
    n, R, C = x.shape
    tile = _sum_tile(R) or R

    def body(x_ref, o_ref):
        acc = x_ref[0].astype(F32)
        for k in range(1, n):
            acc = acc + x_ref[k].astype(F32)
        o_ref[...] = acc

    return pl.pallas_call(
        body, grid=(R // tile,), in_specs=[pl.BlockSpec((n, tile, C), lambda i: (0, i, 0))],
        out_specs=pl.BlockSpec((tile, C), lambda i: (i, 0)), out_shape=jax.ShapeDtypeStruct((R, C), F32),
        name=name, compiler_params=_params(("parallel",)),
    )(x)


def _adamw(w, g, m, v, name):
    shape = w.shape
    C = shape[-1]
    R = w.size // C
    w2, g2, m2, v2 = (a.reshape(R, C) for a in (w, g, m, v))
    tile = _rtile(R)
    c1 = 1.0 / (1.0 - ADAM_B1 ** ADAM_STEP)
    c2 = 1.0 / (1.0 - ADAM_B2 ** ADAM_STEP)

    def body(w_ref, g_ref, m_ref, v_ref, d_ref, nm_ref, nv_ref):
        gv = g_ref[...]
        nm = ADAM_B1 * m_ref[...] + (1.0 - ADAM_B1) * gv
        nv = ADAM_B2 * v_ref[...] + (1.0 - ADAM_B2) * (gv * gv)
        d_ref[...] = -ADAM_LR * ((nm * c1) / (jnp.sqrt(nv * c2) + ADAM_EPS) + ADAM_WD * w_ref[...])
        nm_ref[...] = nm
        nv_ref[...] = nv

    spec = pl.BlockSpec((tile, C), lambda i: (i, 0))
    outs = pl.pallas_call(
        body, grid=(R // tile,), in_specs=[spec] * 4, out_specs=[spec] * 3,
        out_shape=[jax.ShapeDtypeStruct((R, C), F32)] * 3, name=name, compiler_params=_params(("parallel",)),
    )(w2, g2, m2, v2)
    return tuple(o.reshape(shape) for o in outs)


def _gather_steps(x_ref, out_ref, send_sems, recv_sems, local_sem):
    mx, my, mc = lax.axis_index("x"), lax.axis_index("y"), lax.axis_index("c")
    me, sibling = (mx, my, mc), (mx, my, 1 - mc)
    chips = [(1 - mx, my), (mx, 1 - my), (1 - mx, 1 - my)]

    def slot(px, py, pc):
        return out_ref.at[4 * px + 2 * py + pc]

    def copy(k, block, to, src=None):
        return pltpu.make_async_remote_copy(
            src_ref=slot(*block) if src is None else src, dst_ref=slot(*block),
            send_sem=send_sems.at[k], recv_sem=recv_sems.at[k], device_id=to, device_id_type=MESH)

    mine = pltpu.make_async_copy(x_ref, slot(*me), local_sem)
    first = [copy(0, me, sibling, src=x_ref)] + [copy(1 + j, me, (*chip, mc), src=x_ref) for j, chip in enumerate(chips)]
    passed = [copy(4 + j, (*chip, mc), sibling) for j, chip in enumerate(chips)]

    def start():
        mine.start()
        for cp in first:
            cp.start()

    def forward():
        for j, chip in enumerate(chips):
            copy(1 + j, (*chip, mc), me).wait_recv()
            passed[j].start()

    def finish():
        copy(0, sibling, me).wait_recv()
        for j, chip in enumerate(chips):
            copy(4 + j, (*chip, 1 - mc), me).wait_recv()
        for cp in first + passed:
            cp.wait_send()
        mine.wait()

    return start, forward, finish


def _exchange_steps(x_ref, out_ref, send_sems, recv_sems, local_sem):
    mx, my, mc = lax.axis_index("x"), lax.axis_index("y"), lax.axis_index("c")
    me = 4 * mx + 2 * my + mc
    mine = pltpu.make_async_copy(x_ref.at[me], out_ref.at[me], local_sem)
    copies = []
    for k in range(1, NDEV):
        px, py, pc = mx ^ (k >> 2), my ^ ((k >> 1) & 1), mc ^ (k & 1)
        copies.append(pltpu.make_async_remote_copy(
            src_ref=x_ref.at[4 * px + 2 * py + pc], dst_ref=out_ref.at[me], send_sem=send_sems.at[k - 1],
            recv_sem=recv_sems.at[k - 1], device_id=(px, py, pc), device_id_type=MESH))

    def start():
        mine.start()
        for cp in copies:
            cp.start()

    def finish():
        for cp in copies:
            cp.wait_recv()
        for cp in copies:
            cp.wait_send()
        mine.wait()

    return start, finish


def _comm_sems():
    return [pltpu.SemaphoreType.DMA((7,)), pltpu.SemaphoreType.DMA((7,)), pltpu.SemaphoreType.DMA]


def _comm_out_shape(comm):
    kind, x = comm
    return jax.ShapeDtypeStruct(((NDEV,) + x.shape) if kind == "gather" else x.shape, x.dtype)


def _ride_along(comm, step, n_steps, refs, at_end):
    if comm[0] == "gather":
        start, forward, finish = _gather_steps(*refs)
        todo = [(n_steps - 1, finish)] if at_end else [(0, start), (n_steps // 2, forward)]
    else:
        start, finish = _exchange_steps(*refs)
        todo = [(n_steps - 1, finish)] if at_end else [(0, start)]
    for at, fn in todo:
        pl.when(step == at)(fn)


def _grid_call(body, comm, *, grid, in_specs, out_specs, out_shape, scratch_shapes, name, args):
    if comm is None:
        return pl.pallas_call(body, grid=grid, in_specs=in_specs, out_specs=out_specs, out_shape=out_shape,
                              scratch_shapes=scratch_shapes, name=name, compiler_params=_params(("arbitrary",)))(*args)
    n_in, n_out, n_steps = len(in_specs), len(out_specs), grid[0]

    def with_comm(*refs):
        ins, outs, scr = refs[:n_in], refs[n_in + 1:n_in + 1 + n_out], refs[n_in + 2 + n_out:-3]
        comm_refs = (refs[n_in], refs[n_in + 1 + n_out]) + refs[-3:]
        step = pl.program_id(0)
        _ride_along(comm, step, n_steps, comm_refs, False)
        body(*ins, *outs, *scr)
        _ride_along(comm, step, n_steps, comm_refs, True)

    hbm = pl.BlockSpec(memory_space=pl.ANY)
    return pl.pallas_call(
        with_comm, grid=grid, in_specs=list(in_specs) + [hbm], out_specs=list(out_specs) + [hbm],
        out_shape=list(out_shape) + [_comm_out_shape(comm)], scratch_shapes=list(scratch_shapes) + _comm_sems(),
        name=name, compiler_params=_params(("arbitrary",)))(*args, comm[1])


def _all_gather(x, name, in_vmem):
    def body(x_ref, out_ref, send_sems, recv_sems, local_sem):
        start, forward, finish = _gather_steps(x_ref, out_ref, send_sems, recv_sems, local_sem)
        start()
        forward()
        finish()

    space = pltpu.VMEM if in_vmem else pl.ANY
    return pl.pallas_call(
        body, out_shape=jax.ShapeDtypeStruct((NDEV,) + x.shape, x.dtype),
        in_specs=[pl.BlockSpec(memory_space=space)], out_specs=pl.BlockSpec(memory_space=space),
        scratch_shapes=_comm_sems(), name=name, compiler_params=pltpu.CompilerParams(vmem_limit_bytes=VMEM_LIMIT),
    )(x)


def _all_to_all(x, name):
    def body(x_ref, out_ref, send_sems, recv_sems, local_sem):
        start, finish = _exchange_steps(x_ref, out_ref, send_sems, recv_sems, local_sem)
        start()
        finish()

    return pl.pallas_call(
        body, out_shape=jax.ShapeDtypeStruct(x.shape, x.dtype),
        in_specs=[pl.BlockSpec(memory_space=pl.ANY)], out_specs=pl.BlockSpec(memory_space=pl.ANY),
        scratch_shapes=_comm_sems(), name=name, compiler_params=pltpu.CompilerParams(vmem_limit_bytes=VMEM_LIMIT),
    )(x)


def _stage_parts(st):
    parts = []
    if st >= 1:
        parts += [("ffn_w_in", st - 1, (D, 2 * DFF // NDEV), "colsT"), ("ffn_w_out", st - 1, (DFF // NDEV, D), "rows")]
    if st < DEPTH:
        j = st // 2
        if st % 2 == 0:
            parts += [("dn_w_in", j, (D, DN_COLS // NDEV), "cols"), ("dn_w_out", j, (D // NDEV, D), "rows")]
        else:
            parts += [("sb_w_qkv", j, (D, 3 * D // NDEV), "colsT"), ("sb_w_out", j, (D // NDEV, D), "rows")]
    return parts


def _sum_tile(rows):
    for t in range(512, 191, -16):
        if rows % t == 0:
            return t
    return None


def _stage_layout(st):
    out, r = [], 0
    for n, j, s, kind in _stage_parts(st):
        k = s[0] * s[1] // D
        kp = -(-k // 16) * 16
        out.append((n, j, s, kind, k, kp, r))
        r += kp
    while _sum_tile(r) is None:
        r += 16
    return out, r


def _pack_stage(w, st):
    parts, rows = _stage_layout(st)
    bufs = [jnp.pad((w[n][j].T if kind == "colsT" else w[n][j]).astype(BF16).reshape(k, D), ((0, kp - k), (0, 0)))
            for n, j, _, kind, k, kp, _ in parts]
    used = parts[-1][6] + parts[-1][5]
    if rows > used:
        bufs.append(jnp.zeros((rows - used, D), BF16))
    return jnp.concatenate(bufs, axis=0)


def _full_stage(g, st):
    out = {}
    for n, _, s, kind, k, _, r in _stage_layout(st)[0]:
        a = g[:, r:r + k]
        if kind == "cols":
            out[n] = jnp.transpose(a.reshape((NDEV,) + s), (1, 0, 2)).reshape(s[0], NDEV * s[1])
        else:
            out[n] = a.reshape((NDEV * s[0], s[1]) if kind == "rows" else (NDEV * s[1], s[0]))
    if "dn_w_in" in out:
        out["dn_w_in"] = jnp.pad(out["dn_w_in"], ((0, 0), (0, DN_COLS_PAD - DN_COLS)))
    return out


def _shards_stage(full, st):
    parts, rows = _stage_layout(st)
    bufs = []
    for n, _, s, kind, k, kp, _ in parts:
        a = full[n].astype(BF16)
        if kind == "cols":
            a = jnp.transpose(a.reshape(s[0], NDEV, s[1]), (1, 0, 2))
        bufs.append(jnp.pad(a.reshape(NDEV, k, D), ((0, 0), (0, kp - k), (0, 0))))
    used = parts[-1][6] + parts[-1][5]
    if rows > used:
        bufs.append(jnp.zeros((NDEV, rows - used, D), BF16))
    return jnp.concatenate(bufs, axis=1)


def _unpack_stage(buf, st):
    return {(n, j): buf[r:r + k].reshape(s[1], s[0]).T if kind == "colsT" else buf[r:r + k].reshape(s)
            for n, j, s, kind, k, _, r in _stage_layout(st)[0]}


def _local_step(x, target, mod, small, w0, rest):
    dist = isinstance(rest, tuple)
    packs = rest[1] if dist else None
    W = [dict(w0), {}, {}, {}] if dist else [w0] + list(rest)

    def arrived(k, g):
        for n, a in _full_stage(g, k + 1).items():
            W[k if n.startswith("ffn") else k + 1][n] = a

    row = lambda v: v.reshape(1, -1)
    pad128 = lambda v: jnp.pad(v.reshape(1, -1), ((0, 0), (0, LANES - v.size)))
    saved = []
    for i in range(DEPTH):
        j = i // 2
        w = W[i]
        m = [row(mod[i, k * D:(k + 1) * D]) for k in range(N_MOD)]
        sh1, sc1, gt1, sh2, sc2, gt2 = m
        g1, g2 = row(small["norm1_g"][i]), row(small["norm2_g"][i])
        h1 = _norm_mod(x, g1, sc1, sh1, f"norm1_{i}")
        if i % 2 == 0:
            proj = _mm(h1, w["dn_w_in"], "nn", F32, f"dn_proj_{i}")
            alog, dtb, og = pad128(small["dn_a_log"][j]), pad128(small["dn_dt_bias"][j]), row(small["dn_onorm_g"][j])
            comm = ("gather", packs[i]) if dist else None
            res = _dn_core_fwd(proj, small["dn_conv_w"][j], alog, dtb, og, f"dn_core_{i}", comm)
            om, o_pre, states = res[:3]
            if comm:
                arrived(i, res[3])
            y1 = _mm(om, w["dn_w_out"], "nn", F32, f"dn_out_{i}")
            mix = (proj, alog, dtb, og, o_pre, states, om)
        else:
            qkv = _mm(h1, w["sb_w_qkv"], "nt", F32, f"sb_qkv_{i}")
            qg2 = jnp.tile(row(small["sb_q_norm_g"][j]), (1, SB_HPB))
            kg2 = jnp.tile(row(small["sb_k_norm_g"][j]), (1, SB_HPB))
            comm = ("gather", packs[i]) if dist else None
            res = _sb_core_fwd(qkv, qg2, kg2, f"sb_core_{i}", comm)
            om, tot = res[:2]
            if comm:
                arrived(i, res[2])
            y1 = _mm(om, w["sb_w_out"], "nn", F32, f"sb_out_{i}")
            mix = (qkv, qg2, kg2, tot, om)
        x_mid = _gate_res(x, y1, gt1, f"res1_{i}")
        h2 = _norm_mod(x_mid, g2, sc2, sh2, f"norm2_{i}")
        u = _mm(h2, w["ffn_w_in"], "nt", BF16, f"ffn_in_{i}")
        a = _swiglu_act(u, f"ffn_act_{i}")
        y2 = _mm(a, w["ffn_w_out"], "nn", F32, f"ffn_out_{i}")
        x_out = _gate_res(x_mid, y2, gt2, f"res2_{i}")
        saved.append((x, h1, mix, y1, x_mid, h2, u, a, y2, m, g1, g2))
        x = x_out

    dx, sq = _loss_head(x, target, "loss_head")

    sg = dict(mod=[None] * DEPTH, norm1_g=[None] * DEPTH, norm2_g=[None] * DEPTH, dn_conv_w=[None] * 2,
              dn_a_log=[None] * 2, dn_dt_bias=[None] * 2, dn_onorm_g=[None] * 2, sb_q_norm_g=[None] * 2,
              sb_k_norm_g=[None] * 2)
    big = [None] * (DEPTH + 1 if dist else DEPTH)
    mixer_gw = {}
    for i in reversed(range(DEPTH)):
        j = i // 2
        w = W[i]
        x_in, h1, mix, y1, x_mid, h2, u, a, y2, m, g1, g2 = saved[i]
        sh1, sc1, gt1, sh2, sc2, gt2 = m
        gw = {}
        dy2, dgt2 = _gate_res_bwd(dx, y2, gt2, f"res2_bwd_{i}")
        da = _mm(dy2, w["ffn_w_out"], "nt", BF16, f"ffn_out_bwd_{i}")
        gw["ffn_w_out"] = _mm(a, dy2, "tn", BF16, f"ffn_out_wg_{i}")
        du = _swiglu_act_bwd(da, u, f"ffn_act_bwd_{i}")
        dh2 = _mm(du, w["ffn_w_in"], "nn", F32, f"ffn_in_bwd_{i}")
        gw["ffn_w_in"] = _mm(du, h2, "tn", BF16, f"ffn_in_wg_{i}")
        dx_mid, s2, dsh2 = _norm_mod_bwd(dh2, x_mid, dx, g2, sc2, f"norm2_bwd_{i}")
        dy1, dgt1 = _gate_res_bwd(dx_mid, y1, gt1, f"res1_bwd_{i}")
        comm = ("exchange", _shards_stage({**gw, **mixer_gw}, i + 1)) if dist else None
        if i % 2 == 0:
            proj, alog, dtb, og, o_pre, states, om = mix
            dom = _mm(dy1, w["dn_w_out"], "nt", F32, f"dn_out_bwd_{i}")
            gw["dn_w_out"] = _mm(om, dy1, "tn", BF16, f"dn_out_wg_{i}")
            dproj, dconv, dalog, ddtb, dog, got = _dn_core_bwd(proj, small["dn_conv_w"][j], alog, dtb, og, o_pre, states,
                                                               dom, f"dn_core_bwd_{i}", comm)
            dh1 = _mm(dproj, w["dn_w_in"], "nt", F32, f"dn_proj_bwd_{i}")
            gw["dn_w_in"] = _mm(h1, dproj, "tn", BF16, f"dn_proj_wg_{i}")[:, :DN_COLS]
            sg["dn_conv_w"][j] = dconv
            sg["dn_a_log"][j] = dalog[0, :DN_H]
            sg["dn_dt_bias"][j] = ddtb[0, :DN_H]
            sg["dn_onorm_g"][j] = dog[0]
        else:
            qkv, qg2, kg2, tot, om = mix
            dom = _mm(dy1, w["sb_w_out"], "nt", BF16, f"sb_out_bwd_{i}")
            gw["sb_w_out"] = _mm(om, dy1, "tn", BF16, f"sb_out_wg_{i}")
            dqkv, dqg, dkg, got = _sb_core_bwd(qkv, qg2, kg2, tot, dom, f"sb_core_bwd_{i}", comm)
            dh1 = _mm(dqkv, w["sb_w_qkv"], "nn", F32, f"sb_qkv_bwd_{i}")
            gw["sb_w_qkv"] = _mm(dqkv, h1, "tn", BF16, f"sb_qkv_wg_{i}")
            sg["sb_q_norm_g"][j] = jnp.sum(dqg.reshape(SB_HPB, SB_D), axis=0)
            sg["sb_k_norm_g"][j] = jnp.sum(dkg.reshape(SB_HPB, SB_D), axis=0)
        if comm:
            big[i + 1] = got
        dx, s1, dsh1 = _norm_mod_bwd(dh1, x_in, dx_mid, g1, sc1, f"norm1_bwd_{i}")
        sg["mod"][i] = jnp.concatenate([dsh1, s1 * g1, dgt1, dsh2, s2 * g2, dgt2], axis=1)[0]
        sg["norm1_g"][i] = (s1 * (1.0 + sc1))[0]
        sg["norm2_g"][i] = (s2 * (1.0 + sc2))[0]
        if dist:
            mixer_gw = {n: a for n, a in gw.items() if not n.startswith("ffn")}
        else:
            big[i] = gw
    if dist:
        big[0] = _all_to_all(_shards_stage(mixer_gw, 0), "exchange_grads_0")
    return sq, dx, {k: jnp.stack(v) for k, v in sg.items()}, big


def _device_index():
    return 4 * lax.axis_index("x") + 2 * lax.axis_index("y") + lax.axis_index("c")


def _gather_small(w, c):
    me = _device_index()
    ada_cols = w["ada_w"].shape[-1]
    conv_cols = w["dn_conv_w"].shape[-1]
    blk = jnp.concatenate([c.reshape(1, D), w["dn_conv_w"].reshape(-1, D)], axis=0)
    g1 = _all_gather(blk, "gather_cond", True)
    c16 = jnp.pad(g1[:, 0, :], ((0, 8), (0, 0)))
    conv_full = jnp.transpose(g1[:, 1:, :].reshape(NDEV, 2, DN_CONV, conv_cols), (1, 2, 0, 3)).reshape(2, DN_CONV, -1)
    b_cols = lax.dynamic_slice_in_dim(w["ada_b"], me * ada_cols, ada_cols, axis=1).reshape(DEPTH, 1, ada_cols)
    mod_part = _ada_fwd(c16, w["ada_w"], b_cols, "ada_fwd")[:, :NDEV, :]
    g2 = _all_gather(mod_part.reshape(DEPTH * NDEV, ada_cols), "gather_mod", True)
    g2 = g2.reshape(NDEV, DEPTH, NDEV, ada_cols)
    mod = lax.dynamic_index_in_dim(g2, me, axis=2, keepdims=False)
    mod = jnp.transpose(mod, (1, 0, 2)).reshape(DEPTH, N_MOD * D)
    return c16, conv_full, mod


def _reduce_small(gr, c16):
    me = _device_index()
    ada_cols = N_MOD * D // NDEV
    conv_cols = 3 * DN_H * DN_D // NDEV
    grads = {}
    small = jnp.concatenate([gr["dn_a_log"].reshape(-1), gr["dn_dt_bias"].reshape(-1), gr["dn_onorm_g"].reshape(-1),
                             gr["sb_q_norm_g"].reshape(-1), gr["sb_k_norm_g"].reshape(-1)])
    small = jnp.pad(small, (0, D - small.size)).reshape(1, D)
    blk3 = jnp.concatenate([gr["mod"].reshape(-1, D), gr["norm1_g"], gr["norm2_g"], gr["dn_conv_w"].reshape(-1, D),
                            small], axis=0)
    blk3 = jnp.pad(blk3, ((0, 64 - blk3.shape[0]), (0, 0)))
    g3 = _all_gather(blk3, "gather_small_grads", True)
    tot = _sum_sources(g3, "sum_small_grads")
    grads["ada_b"] = tot[:24].reshape(DEPTH, N_MOD * D)
    grads["norm1_g"] = tot[24:28]
    grads["norm2_g"] = tot[28:32]
    conv_g = tot[32:56].reshape(2, DN_CONV, NDEV * conv_cols)
    grads["dn_conv_w"] = lax.dynamic_slice_in_dim(conv_g, me * conv_cols, conv_cols, axis=2)
    sm = tot[56]
    grads["dn_a_log"] = sm[0:16].reshape(2, DN_H)
    grads["dn_dt_bias"] = sm[16:32].reshape(2, DN_H)
    grads["dn_onorm_g"] = sm[32:288].reshape(2, DN_D)
    grads["sb_q_norm_g"] = sm[288:416].reshape(2, SB_D)
    grads["sb_k_norm_g"] = sm[416:544].reshape(2, SB_D)
    dmod_all = g3[:, :24, :].reshape(NDEV, DEPTH, N_MOD * D)
    dmod_cols = lax.dynamic_slice_in_dim(dmod_all, me * ada_cols, ada_cols, axis=2)
    dmod16 = jnp.pad(jnp.transpose(dmod_cols, (1, 0, 2)), ((0, 0), (0, 8), (0, 0)))
    grads["ada_w"] = _ada_bwd(c16, dmod16, "ada_bwd")
    return grads


def _reduce_big(recv):
    parts = {}
    for st in range(DEPTH + 1):
        parts.update(_unpack_stage(_sum_sources(recv[st], f"sum_grads_{st}"), st))
    out = {}
    for (n, j) in sorted(parts):
        out.setdefault(n, []).append(parts[(n, j)])
    return {n: jnp.stack(v) for n, v in out.items()}


def kernel(x, c, ada_w, ada_b, norm1_g, norm2_g, dn_w_in, dn_conv_w, dn_a_log, dn_dt_bias, dn_onorm_g, dn_w_out, sb_w_qkv, sb_q_norm_g, sb_k_norm_g, sb_w_out, ffn_w_in, ffn_w_out, loss_target, m_ada_w, m_ada_b, m_norm1_g, m_norm2_g, m_dn_w_in, m_dn_conv_w, m_dn_a_log, m_dn_dt_bias, m_dn_onorm_g, m_dn_w_out, m_sb_w_qkv, m_sb_q_norm_g, m_sb_k_norm_g, m_sb_w_out, m_ffn_w_in, m_ffn_w_out, v_ada_w, v_ada_b, v_norm1_g, v_norm2_g, v_dn_w_in, v_dn_conv_w, v_dn_a_log, v_dn_dt_bias, v_dn_onorm_g, v_dn_w_out, v_sb_w_qkv, v_sb_q_norm_g, v_sb_k_norm_g, v_sb_w_out, v_ffn_w_in, v_ffn_w_out):
    w = dict(ada_w=ada_w, ada_b=ada_b, norm1_g=norm1_g, norm2_g=norm2_g, dn_w_in=dn_w_in, dn_conv_w=dn_conv_w,
             dn_a_log=dn_a_log, dn_dt_bias=dn_dt_bias, dn_onorm_g=dn_onorm_g, dn_w_out=dn_w_out, sb_w_qkv=sb_w_qkv,
             sb_q_norm_g=sb_q_norm_g, sb_k_norm_g=sb_k_norm_g, sb_w_out=sb_w_out, ffn_w_in=ffn_w_in, ffn_w_out=ffn_w_out)
    mom = dict(ada_w=m_ada_w, ada_b=m_ada_b, norm1_g=m_norm1_g, norm2_g=m_norm2_g, dn_w_in=m_dn_w_in,
               dn_conv_w=m_dn_conv_w, dn_a_log=m_dn_a_log, dn_dt_bias=m_dn_dt_bias, dn_onorm_g=m_dn_onorm_g,
               dn_w_out=m_dn_w_out, sb_w_qkv=m_sb_w_qkv, sb_q_norm_g=m_sb_q_norm_g, sb_k_norm_g=m_sb_k_norm_g,
               sb_w_out=m_sb_w_out, ffn_w_in=m_ffn_w_in, ffn_w_out=m_ffn_w_out)
    var = dict(ada_w=v_ada_w, ada_b=v_ada_b, norm1_g=v_norm1_g, norm2_g=v_norm2_g, dn_w_in=v_dn_w_in,
               dn_conv_w=v_dn_conv_w, dn_a_log=v_dn_a_log, dn_dt_bias=v_dn_dt_bias, dn_onorm_g=v_dn_onorm_g,
               dn_w_out=v_dn_w_out, sb_w_qkv=v_sb_w_qkv, sb_q_norm_g=v_sb_q_norm_g, sb_k_norm_g=v_sb_k_norm_g,
               sb_w_out=v_sb_w_out, ffn_w_in=v_ffn_w_in, ffn_w_out=v_ffn_w_out)
    names = list(w)
    c16, conv_full, mod = _gather_small(w, c)
    packs = [_pack_stage(w, st) for st in range(DEPTH + 1)]
    w0 = _full_stage(_all_gather(packs[0], "gather_weights_0", False), 0)
    small = dict(norm1_g=norm1_g, norm2_g=norm2_g, dn_conv_w=conv_full, dn_a_log=dn_a_log, dn_dt_bias=dn_dt_bias,
                 dn_onorm_g=dn_onorm_g, sb_q_norm_g=sb_q_norm_g, sb_k_norm_g=sb_k_norm_g)
    sq, grad_x, sgr, recv = _local_step(x[0], loss_target[0], mod, small, w0, ("packed", packs[1:]))
    loss = lax.psum(sq[0, 0] * (0.5 / D), ("x", "y", "c"))
    grads = {**_reduce_big(recv), **_reduce_small(sgr, c16)}
    delta, new_m, new_v = {}, {}, {}
    for n in names:
        delta[n], new_m[n], new_v[n] = _adamw(w[n], grads[n], mom[n], var[n], f"adamw_{n}")
    return (loss, grad_x[None], *[grads[n] for n in names], *[delta[n] for n in names],
            *[new_m[n] for n in names], *[new_v[n] for n in names])
```

```python
import jax
import jax.numpy as jnp
from jax import lax
from jax.experimental import pallas as pl
from jax.experimental.pallas import tpu as pltpu

F32, BF16 = jnp.float32, jnp.bfloat16

D = 1024
DEPTH = 4
N_MOD = 6
DN_H, DN_D, DN_C, DN_CONV = 8, 128, 64, 4
DN_UNROLL_FWD, DN_UNROLL_BWD = 16, 8
DN_COLS = 4 * DN_H * DN_D + 2 * DN_H
DN_COLS_PAD = 33 * 128
SB_H, SB_D = 16, 64
SB_TILE = 256
SB_HPB = 4
SB_W = SB_HPB * SB_D
DFF = 2816
EPS = 1e-6
NDEV = 8
LANES = 128
VMEM_LIMIT = 56 * 1024 * 1024

ADAM_LR, ADAM_B1, ADAM_B2, ADAM_EPS, ADAM_WD, ADAM_STEP = 0.001, 0.9, 0.999, 1e-08, 0.01, 10

NN = ((1,), (0,))
NT = ((1,), (1,))
TN = ((0,), (0,))
MESH = pl.DeviceIdType.MESH


def _dot(a, b, dims=NN):
    a, b = a.astype(BF16), b.astype(BF16)
    if a.ndim == 2 and b.ndim == 2:
        return lax.dot_general(a, b, (dims, ((), ())), preferred_element_type=F32)
    n = a.shape[0] if a.ndim == 3 else b.shape[0]
    if a.ndim == 2:
        a = jnp.broadcast_to(a, (n,) + a.shape)
    if b.ndim == 2:
        b = jnp.broadcast_to(b, (n,) + b.shape)
    (ca,), (cb,) = dims
    return lax.dot_general(a, b, (((ca + 1,), (cb + 1,)), ((0,), (0,))), preferred_element_type=F32)


def _split(a):
    hi = a.astype(BF16)
    lo = (a - hi.astype(F32)).astype(BF16)
    return hi, lo


def _dot2r(a, m, dims=NN):
    ah, al = _split(a)
    return _dot(ah, m, dims) + _dot(al, m, dims)


def _dot2l(m, b, dims=NN):
    bh, bl = _split(b)
    return _dot(m, bh, dims) + _dot(m, bl, dims)


def _sigmoid(x):
    return 1.0 / (1.0 + jnp.exp(-x))


def _silu(x):
    return x * _sigmoid(x)


def _dsilu(x):
    s = _sigmoid(x)
    return s * (1.0 + x * (1.0 - s))


def _softplus(x):
    return jnp.maximum(x, 0.0) + jnp.log(1.0 + jnp.exp(-jnp.abs(x)))


def _iota(shape, dim):
    return lax.broadcasted_iota(jnp.int32, shape, dim)


def _rowsum(x):
    return jnp.sum(x, axis=-1, keepdims=True)


def _colsum(x):
    return jnp.sum(x, axis=-2, keepdims=True)


def _tile(n, pref):
    if n <= pref:
        return n
    best = None
    for t in range(LANES, pref + 1, LANES):
        if n % t == 0:
            best = t
    assert best is not None, (n, pref)
    return best


def _rtile(r, pref=512):
    best = None
    for t in range(8, min(r, pref) + 1, 8):
        if r % t == 0:
            best = t
    return best if best is not None else r


def _params(sem):
    return pltpu.CompilerParams(dimension_semantics=sem, vmem_limit_bytes=VMEM_LIMIT)


MM_TILE = 1408


def _mm(a, b, mode, out_dtype, name):
    if mode == "nn":
        (M, K), (K2, N) = a.shape, b.shape
        dims = NN
    elif mode == "nt":
        (M, K), (N, K2) = a.shape, b.shape
        dims = NT
    else:
        (K, M), (K2, N) = a.shape, b.shape
        dims = TN
    assert K == K2, (a.shape, b.shape, mode)
    tm, tn, tk = _tile(M, MM_TILE), _tile(N, MM_TILE), _tile(K, MM_TILE)
    nk = K // tk

    def body_single(a_ref, b_ref, o_ref):
        o_ref[...] = _dot(a_ref[...], b_ref[...], dims).astype(o_ref.dtype)

    def body_acc(a_ref, b_ref, o_ref, acc_ref):
        k = pl.program_id(2)
        p = _dot(a_ref[...], b_ref[...], dims)

        @pl.when(k == 0)
        def _():
            acc_ref[...] = p

        @pl.when(k > 0)
        def _():
            acc_ref[...] += p

        @pl.when(k == nk - 1)
        def _():
            o_ref[...] = acc_ref[...].astype(o_ref.dtype)

    if mode == "nn":
        a_spec = pl.BlockSpec((tm, tk), lambda i, j, k: (i, k))
        b_spec = pl.BlockSpec((tk, tn), lambda i, j, k: (k, j))
    elif mode == "nt":
        a_spec = pl.BlockSpec((tm, tk), lambda i, j, k: (i, k))
        b_spec = pl.BlockSpec((tn, tk), lambda i, j, k: (j, k))
    else:
        a_spec = pl.BlockSpec((tk, tm), lambda i, j, k: (k, i))
        b_spec = pl.BlockSpec((tk, tn), lambda i, j, k: (k, j))
    return pl.pallas_call(
        body_single if nk == 1 else body_acc, grid=(M // tm, N // tn, nk), in_specs=[a_spec, b_spec],
        out_specs=pl.BlockSpec((tm, tn), lambda i, j, k: (i, j)),
        out_shape=jax.ShapeDtypeStruct((M, N), out_dtype),
        scratch_shapes=[] if nk == 1 else [pltpu.VMEM((tm, tn), F32)], name=name,
        compiler_params=_params(("parallel", "parallel", "arbitrary")),
    )(a, b)


def _rowwise(fn, name, rows, bcasts, out_rows, out_reds=(), tile=256):
    T = rows[0].shape[0]
    tile = min(tile, T)
    nr, nb, no = len(rows), len(bcasts), len(out_rows)

    def body(*refs):
        rv = [r[...] for r in refs[:nr]]
        bv = [r[...] for r in refs[nr:nr + nb]]
        outs, reds = fn(rv, bv)
        for r, o in zip(refs[nr + nb:nr + nb + no], outs):
            r[...] = o.astype(r.dtype)
        red_refs = refs[nr + nb + no:]
        if red_refs:
            @pl.when(pl.program_id(0) == 0)
            def _():
                for r in red_refs:
                    r[...] = jnp.zeros(r.shape, F32)

            for r, v in zip(red_refs, reds):
                r[...] += v

    in_specs = [pl.BlockSpec((tile, a.shape[1]), lambda i: (i, 0)) for a in rows]
    in_specs += [pl.BlockSpec(b.shape, lambda i: (0, 0)) for b in bcasts]
    out_specs = [pl.BlockSpec((tile, c), lambda i: (i, 0)) for c, _ in out_rows]
    out_specs += [pl.BlockSpec(s, lambda i: (0, 0)) for s in out_reds]
    out_shape = [jax.ShapeDtypeStruct((T, c), dt) for c, dt in out_rows]
    out_shape += [jax.ShapeDtypeStruct(s, F32) for s in out_reds]
    return pl.pallas_call(
        body, grid=(T // tile,), in_specs=in_specs, out_specs=out_specs, out_shape=out_shape, name=name,
        compiler_params=_params(("arbitrary",)),
    )(*rows, *bcasts)


def _norm_mod(x, g, sc, sh, name):
    def fn(rv, bv):
        (xv,), (gv, scv, shv) = rv, bv
        r = lax.rsqrt(jnp.mean(xv * xv, axis=1, keepdims=True) + EPS)
        return [(xv * r * gv) * (1.0 + scv) + shv], []
    return _rowwise(fn, name, [x], [g, sc, sh], [(D, BF16)])[0]


def _norm_mod_bwd(dh, x, dres, g, sc, name):
    def fn(rv, bv):
        (dhv, xv, drv), (gv, scv) = rv, bv
        r = lax.rsqrt(jnp.mean(xv * xv, axis=1, keepdims=True) + EPS)
        xhat = xv * r
        dxhat = dhv * (gv * (1.0 + scv))
        dx = r * (dxhat - xhat * jnp.mean(dxhat * xhat, axis=1, keepdims=True)) + drv
        return [dx], [_colsum(dhv * xhat), _colsum(dhv)]
    return _rowwise(fn, name, [dh, x, dres], [g, sc], [(D, F32)], [(1, D), (1, D)])


def _gate_res(x, y, gt, name):
    def fn(rv, bv):
        return [rv[0] + bv[0] * rv[1]], []
    return _rowwise(fn, name, [x, y], [gt], [(D, F32)])[0]


def _gate_res_bwd(dxn, y, gt, name):
    def fn(rv, bv):
        return [rv[0] * bv[0]], [_colsum(rv[0] * rv[1])]
    return _rowwise(fn, name, [dxn, y], [gt], [(D, BF16)], [(1, D)])


def _swiglu_act(u, name):
    def fn(rv, bv):
        uv = rv[0].astype(F32)
        return [_silu(uv[:, :DFF]) * uv[:, DFF:]], []
    return _rowwise(fn, name, [u], [], [(DFF, BF16)])[0]


def _swiglu_act_bwd(da, u, name):
    def fn(rv, bv):
        dav, uv = rv[0].astype(F32), rv[1].astype(F32)
        gate, up = uv[:, :DFF], uv[:, DFF:]
        return [jnp.concatenate([dav * up * _dsilu(gate), dav * _silu(gate)], axis=1)], []
    return _rowwise(fn, name, [da, u], [], [(2 * DFF, BF16)])[0]


def _loss_head(y, target, name):
    def fn(rv, bv):
        err = rv[0] - rv[1]
        return [err * (1.0 / D)], [_colsum(_rowsum(err * err))]
    return _rowwise(fn, name, [y, target], [], [(D, F32)], [(1, 1)])


def _shift_rows(x, s):
    if s == 0:
        return x
    T = x.shape[0]
    r = pltpu.roll(x, s % T, axis=0)
    t = _iota(x.shape, 0)
    keep = (t >= s) if s > 0 else (t < T + s)
    return jnp.where(keep, r, 0.0)


def _dn_prep(pq, pk, pv, pab, cq_ref, ck_ref, cv_ref, alog, dtb, h):
    lane = _iota(pab.shape, 1)
    a_col = _rowsum(jnp.where(lane == h, pab, 0.0))
    b_col = _rowsum(jnp.where(lane == DN_H + h, pab, 0.0))
    lane1 = _iota(alog.shape, 1)
    alog_h = _rowsum(jnp.where(lane1 == h, alog, 0.0))
    dtb_h = _rowsum(jnp.where(lane1 == h, dtb, 0.0))
    pre = a_col + dtb_h
    neg_ea = -jnp.exp(alog_h)
    g = neg_ea * _softplus(pre)
    beta = _sigmoid(b_col)

    def conv(x, w_ref):
        acc = x * w_ref[DN_CONV - 1:DN_CONV, :]
        for i in range(DN_CONV - 1):
            acc = acc + _shift_rows(x, DN_CONV - 1 - i) * w_ref[i:i + 1, :]
        return acc

    xq, xk, xv = conv(pq, cq_ref), conv(pk, ck_ref), conv(pv, cv_ref)
    sq, sk, v = _silu(xq), _silu(xk), _silu(xv)
    rq = lax.rsqrt(_rowsum(sq * sq) + EPS)
    rk = lax.rsqrt(_rowsum(sk * sk) + EPS)
    return dict(g=g, beta=beta, pre=pre, neg_ea=neg_ea, xq=xq, xk=xk, xv=xv, rq=rq, rk=rk,
                qn=sq * rq, kn=sk * rk, v=v)


def _dn_masks():
    C = DN_C
    r, c = _iota((C, C), 0), _iota((C, C), 1)
    incl = r >= c
    strict = r > c
    blk16 = jnp.right_shift(r, 4) == jnp.right_shift(c, 4)
    blk32 = jnp.right_shift(r, 5) == jnp.right_shift(c, 5)
    return dict(incl=incl, strict=strict, upper=r <= c, blk16=blk16, blk32=blk32,
                tri=incl.astype(BF16), triT=(r <= c).astype(BF16), ones=jnp.ones((C, C), BF16),
                eye=(r == c).astype(F32), last=_iota((C, 1), 0) == C - 1)


def _tri_inverse(A, mk):
    P = -jnp.where(mk["blk16"], A, 0.0)
    X = mk["eye"] + P
    for _ in range(3):
        P = _dot(P, P)
        X = X + _dot(X, P)
    off1 = jnp.where(mk["blk32"] & (~mk["blk16"]), A, 0.0)
    X = X - _dot(_dot(X, off1), X)
    off2 = jnp.where(mk["blk32"], 0.0, A)
    X = X - _dot(_dot(X, off2), X)
    return X


def _dn_local(qc, kc, vc, gc, bc, mk):
    C = DN_C
    gm = jnp.broadcast_to(gc, gc.shape[:-1] + (C,))
    Gc = _dot2l(mk["tri"], gm)
    Gr = _dot2l(mk["ones"], jnp.where(mk["upper"], gm, 0.0))
    Dm = jnp.where(mk["incl"], jnp.exp(jnp.where(mk["incl"], Gc - Gr, 0.0)), 0.0)
    Gcol = jnp.max(Gc, axis=-1, keepdims=True)
    Gl = _colsum(jnp.where(mk["last"], Gcol, 0.0))
    eG = jnp.exp(Gcol)
    eT = jnp.exp(Gl - Gcol)
    gl = jnp.exp(Gl)
    kb = kc * bc
    vb = vc * bc
    KK = _dot(kb, kc, NT)
    Tinv = _tri_inverse(jnp.where(mk["strict"], KK * Dm, 0.0), mk)
    KBE = kb * eG
    QK = _dot(qc, kc, NT)
    return dict(Dm=Dm, eG=eG, eT=eT, gl=gl, kb=kb, vb=vb, KK=KK, Tinv=Tinv, KBE=KBE, U=_dot(Tinv, vb),
                W=_dot(Tinv, KBE), QK=QK, attn=QK * Dm, QD=qc * eG, KT=kc * eT)


def _dn_recur(f, S):
    vnew = f["U"] - _dot(f["W"], S)
    o = _dot(f["QD"], S) + _dot(f["attn"], vnew)
    return o, S * f["gl"] + _dot(f["KT"], vnew, TN), vnew


def _dn_bwd_chain(f, do, dS):
    dvnew = _dot(f["KT"], dS) + _dot(f["attn"], do, TN)
    return dvnew, dS * f["gl"] + _dot(f["QD"], do, TN) - _dot(f["W"], dvnew, TN)


def _dn_bwd_rest(qc, kc, vc, bc, f, S, vnew, do, dS, dvnew, mk):
    C = DN_C
    Dm, eG, eT, gl, kb, vb, KK, Tinv, KBE, QK = (
        f[n] for n in ("Dm", "eG", "eT", "gl", "kb", "vb", "KK", "Tinv", "KBE", "QK"))
    dKT = _dot(vnew, dS, NT)
    dgl = _colsum(_rowsum(dS * S))
    dQD = _dot(do, S, NT)
    dattn = _dot(do, vnew, NT)
    dU = dvnew
    dW = -_dot(dvnew, S, NT)
    dQK = dattn * Dm
    dD = dattn * QK
    dq = _dot(dQK, kc)
    dk = _dot(dQK, qc, TN)
    dTinv = _dot(dU, vb, NT) + _dot(dW, KBE, NT)
    dvb = _dot(Tinv, dU, TN)
    dKBE = _dot(Tinv, dW, TN)
    dA = -_dot(_dot(Tinv, dTinv, TN), Tinv, NT)
    dA = jnp.where(mk["strict"], dA, 0.0)
    dKK = dA * Dm
    dD = dD + dA * KK
    dkb = _dot(dKK, kc) + dKBE * eG
    dk = dk + _dot(dKK, kb, TN)
    deG = _rowsum(dKBE * kb)
    dk = dk + dkb * bc
    dbeta = _rowsum(dkb * kc) + _rowsum(dvb * vc)
    dv = dvb * bc
    dq = dq + dQD * eG
    deG = deG + _rowsum(dQD * qc)
    dk = dk + dKT * eT
    deT = _rowsum(dKT * kc)
    dGcol = deG * eG - deT * eT
    dGl = _colsum(deT * eT) + dgl * gl
    Y = dD * Dm
    ycol = jnp.max(_dot2r(Y, mk["ones"], TN), axis=-1, keepdims=True)
    dGcol = dGcol + _rowsum(Y) - ycol
    dGcol = dGcol + jnp.where(mk["last"], dGl, 0.0)
    dg = jnp.max(_dot2l(mk["triT"], jnp.broadcast_to(dGcol, dGcol.shape[:-1] + (C,))), axis=-1, keepdims=True)
    return dq, dk, dv, dg, dbeta


def _dn_core_fwd(proj, conv_w, alog, dtb, og, name, comm=None):
    T = proj.shape[0]
    N = T // DN_C

    def body(pq_ref, pk_ref, pv_ref, pz_ref, pab_ref, cq_ref, ck_ref, cv_ref, alog_ref, dtb_ref, og_ref,
             out_ref, o_ref, st_ref, q_s, k_s, v_s, g_s, b_s, S_s):
        h = pl.program_id(0)
        p = _dn_prep(pq_ref[...], pk_ref[...], pv_ref[...], pab_ref[...], cq_ref, ck_ref, cv_ref,
                     alog_ref[...], dtb_ref[...], h)
        q_s[...] = p["qn"] * (DN_D ** -0.5)
        k_s[...] = p["kn"]
        v_s[...] = p["v"]
        g_s[...] = p["g"]
        b_s[...] = p["beta"]
        S_s[...] = jnp.zeros(S_s.shape, F32)
        mk = _dn_masks()

        nu = min(DN_UNROLL_FWD, N)

        def step(it, carry):
            rows = pl.ds(pl.multiple_of(it * (nu * DN_C), nu * DN_C), nu * DN_C)
            loc = _dn_local(*(r[rows, :].reshape(nu, DN_C, r.shape[1]) for r in (q_s, k_s, v_s, g_s, b_s)), mk)
            S = S_s[...]
            outs = []
            for u in range(nu):
                st_ref[0, it * nu + u] = S
                o, S, _ = _dn_recur({n: v[u] for n, v in loc.items()}, S)
                outs.append(o)
            o_ref[rows, :] = jnp.concatenate(outs, axis=0)
            S_s[...] = S
            return carry

        lax.fori_loop(0, N // nu, step, 0)
        o = o_ref[...]
        ro = lax.rsqrt(jnp.mean(o * o, axis=1, keepdims=True) + EPS)
        out_ref[...] = ((o * ro * og_ref[...]) * _silu(pz_ref[...])).astype(out_ref.dtype)

    col = lambda k: pl.BlockSpec((T, DN_D), lambda h: (0, k * DN_H + h))
    cw = lambda k: pl.BlockSpec((DN_CONV, DN_D), lambda h: (0, k * DN_H + h))
    small = pl.BlockSpec((1, LANES), lambda h: (0, 0))
    return _grid_call(
        body, comm, grid=(DN_H,),
        in_specs=[col(0), col(1), col(2), col(3), pl.BlockSpec((T, LANES), lambda h: (0, 4 * DN_H)),
                  cw(0), cw(1), cw(2), small, small, small],
        out_specs=[pl.BlockSpec((T, DN_D), lambda h: (0, h)), pl.BlockSpec((T, DN_D), lambda h: (0, h)),
                   pl.BlockSpec((1, N, DN_D, DN_D), lambda h: (h, 0, 0, 0))],
        out_shape=[jax.ShapeDtypeStruct((T, D), BF16), jax.ShapeDtypeStruct((T, D), F32),
                   jax.ShapeDtypeStruct((DN_H, N, DN_D, DN_D), F32)],
        scratch_shapes=[pltpu.VMEM((T, DN_D), F32)] * 3 + [pltpu.VMEM((T, 1), F32)] * 2 + [pltpu.VMEM((DN_D, DN_D), F32)],
        name=name, args=(proj, proj, proj, proj, proj, conv_w, conv_w, conv_w, alog, dtb, og))


def _dn_core_bwd(proj, conv_w, alog, dtb, og, o, states, dout, name, comm=None):
    T = proj.shape[0]
    N = T // DN_C

    def body(pq_ref, pk_ref, pv_ref, pz_ref, pab_ref, cq_ref, ck_ref, cv_ref, alog_ref, dtb_ref, og_ref,
             o_ref, st_ref, dout_ref,
             dpq_ref, dpk_ref, dpv_ref, dpz_ref, dpab_ref, dcq_ref, dck_ref, dcv_ref, dalog_ref, ddtb_ref, dog_ref,
             q_s, k_s, v_s, g_s, b_s, do_s, dS_s):
        h = pl.program_id(0)
        scale = DN_D ** -0.5

        def prep():
            return _dn_prep(pq_ref[...], pk_ref[...], pv_ref[...], pab_ref[...], cq_ref, ck_ref, cv_ref,
                            alog_ref[...], dtb_ref[...], h)

        p = prep()
        q_s[...] = p["qn"] * scale
        k_s[...] = p["kn"]
        v_s[...] = p["v"]
        g_s[...] = p["g"]
        b_s[...] = p["beta"]
        del p

        o = o_ref[...]
        z = pz_ref[...]
        dout = dout_ref[...]
        ogv = og_ref[...]
        ro = lax.rsqrt(jnp.mean(o * o, axis=1, keepdims=True) + EPS)
        on = o * ro
        dy = dout * _silu(z)
        dpz_ref[...] = (dout * (on * ogv) * _dsilu(z)).astype(dpz_ref.dtype)
        dyg = dy * ogv
        do_s[...] = ro * (dyg - on * jnp.mean(dyg * on, axis=1, keepdims=True))
        dog_h = _colsum(dy * on)

        dS_s[...] = jnp.zeros(dS_s.shape, F32)
        mk = _dn_masks()

        nu = min(DN_UNROLL_BWD, N)

        def step(it, carry):
            c0 = (N // nu - 1 - it) * nu
            rows = pl.ds(pl.multiple_of(c0 * DN_C, nu * DN_C), nu * DN_C)
            q, k, v, g, b, do = (r[rows, :].reshape(nu, DN_C, r.shape[1]) for r in (q_s, k_s, v_s, g_s, b_s, do_s))
            loc = _dn_local(q, k, v, g, b, mk)
            Ss = st_ref[0, pl.ds(c0, nu)]
            vnew = loc["U"] - _dot(loc["W"], Ss)
            dS = dS_s[...]
            dS_in, dvnew = [None] * nu, [None] * nu
            for u in reversed(range(nu)):
                dS_in[u] = dS
                dvnew[u], dS = _dn_bwd_chain({n: x[u] for n, x in loc.items()}, do[u], dS)
            dS_s[...] = dS
            grads = _dn_bwd_rest(q, k, v, b, loc, Ss, vnew, do, jnp.stack(dS_in), jnp.stack(dvnew), mk)
            for r, d in zip((q_s, k_s, v_s, g_s, b_s), grads):
                r[rows, :] = d.reshape(nu * DN_C, r.shape[1])
            return carry

        lax.fori_loop(0, N // nu, step, 0)

        p = prep()
        pq, pk, pv = pq_ref[...], pk_ref[...], pv_ref[...]
        dqn = q_s[...] * scale
        dkn = k_s[...]
        qn, kn = p["qn"], p["kn"]
        dsq = p["rq"] * (dqn - qn * _rowsum(dqn * qn))
        dsk = p["rk"] * (dkn - kn * _rowsum(dkn * kn))
        dxq = dsq * _dsilu(p["xq"])
        dxk = dsk * _dsilu(p["xk"])
        dxv = v_s[...] * _dsilu(p["xv"])

        def conv_bwd(dx, x, w_ref, dp_ref, dc_ref):
            acc = dx * w_ref[DN_CONV - 1:DN_CONV, :]
            dc_ref[DN_CONV - 1:DN_CONV, :] = _colsum(dx * x)
            for i in range(DN_CONV - 1):
                s = DN_CONV - 1 - i
                acc = acc + _shift_rows(dx, -s) * w_ref[i:i + 1, :]
                dc_ref[i:i + 1, :] = _colsum(dx * _shift_rows(x, s))
            dp_ref[...] = acc.astype(dp_ref.dtype)

        conv_bwd(dxq, pq, cq_ref, dpq_ref, dcq_ref)
        conv_bwd(dxk, pk, ck_ref, dpk_ref, dck_ref)
        conv_bwd(dxv, pv, cv_ref, dpv_ref, dcv_ref)

        dg = g_s[...]
        beta = p["beta"]
        da_raw = dg * p["neg_ea"] * _sigmoid(p["pre"])
        db_raw = b_s[...] * beta * (1.0 - beta)
        lane = _iota((T, LANES), 1)
        contrib = jnp.where(lane == h, da_raw, 0.0) + jnp.where(lane == DN_H + h, db_raw, 0.0)
        lane1 = _iota((1, LANES), 1)
        dalog_h = jnp.where(lane1 == h, _colsum(dg * p["g"]), 0.0)
        ddtb_h = jnp.where(lane1 == h, _colsum(da_raw), 0.0)

        @pl.when(h == 0)
        def _():
            dpab_ref[...] = contrib.astype(dpab_ref.dtype)
            dalog_ref[...] = dalog_h
            ddtb_ref[...] = ddtb_h
            dog_ref[...] = dog_h

        @pl.when(h > 0)
        def _():
            dpab_ref[...] += contrib.astype(dpab_ref.dtype)
            dalog_ref[...] += dalog_h
            ddtb_ref[...] += ddtb_h
            dog_ref[...] += dog_h

    col = lambda k: pl.BlockSpec((T, DN_D), lambda h: (0, k * DN_H + h))
    cw = lambda k: pl.BlockSpec((DN_CONV, DN_D), lambda h: (0, k * DN_H + h))
    small = pl.BlockSpec((1, LANES), lambda h: (0, 0))
    ab = pl.BlockSpec((T, LANES), lambda h: (0, 4 * DN_H))
    hcol = pl.BlockSpec((T, DN_D), lambda h: (0, h))
    outs = _grid_call(
        body, comm, grid=(DN_H,),
        in_specs=[col(0), col(1), col(2), col(3), ab, cw(0), cw(1), cw(2), small, small, small,
                  hcol, pl.BlockSpec((1, N, DN_D, DN_D), lambda h: (h, 0, 0, 0)), hcol],
        out_specs=[hcol, hcol, hcol, hcol, pl.BlockSpec((T, LANES), lambda h: (0, 0)),
                   pl.BlockSpec((DN_CONV, DN_D), lambda h: (0, h)), pl.BlockSpec((DN_CONV, DN_D), lambda h: (0, h)),
                   pl.BlockSpec((DN_CONV, DN_D), lambda h: (0, h)), small, small, small],
        out_shape=[jax.ShapeDtypeStruct((T, D), BF16)] * 4 + [jax.ShapeDtypeStruct((T, LANES), BF16)]
                  + [jax.ShapeDtypeStruct((DN_CONV, D), F32)] * 3 + [jax.ShapeDtypeStruct((1, LANES), F32)] * 3,
        scratch_shapes=[pltpu.VMEM((T, DN_D), F32)] * 3 + [pltpu.VMEM((T, 1), F32)] * 2
                       + [pltpu.VMEM((T, DN_D), F32), pltpu.VMEM((DN_D, DN_D), F32)],
        name=name, args=(proj, proj, proj, proj, proj, conv_w, conv_w, conv_w, alog, dtb, og, o, states, dout))
    dpq, dpk, dpv, dpz, dpab, dcq, dck, dcv, dalog, ddtb, dog = outs[:11]
    dproj = jnp.concatenate([dpq, dpk, dpv, dpz, dpab], axis=1)
    dconv = jnp.concatenate([dcq, dck, dcv], axis=1)
    return dproj, dconv, dalog, ddtb, dog, (outs[11] if comm else None)


def _sb_head_of(shape):
    return jnp.right_shift(_iota(shape, 1), 6)


def _sb_head_sums(x, head):
    out = jnp.zeros_like(x)
    for hh in range(SB_HPB):
        out = jnp.where(head == hh, _rowsum(jnp.where(head == hh, x, 0.0)), out)
    return out


def _sb_head_norm(x, head):
    r = lax.rsqrt(_sb_head_sums(x * x, head) * (1.0 / SB_D) + EPS)
    return x * r, r


def _sb_fill(q_ref, k_ref, v_ref, qg_ref, kg_ref, qs_s, kn_s, v_s):
    head = _sb_head_of(q_ref.shape)
    qs_s[...] = (_sb_head_norm(q_ref[...], head)[0] * qg_ref[...] * (SB_D ** -0.5)).astype(BF16)
    kn_s[...] = (_sb_head_norm(k_ref[...], head)[0] * kg_ref[...]).astype(BF16)
    v_s[...] = v_ref[...].astype(BF16)


def _sb_head_masks(dtype):
    return jnp.stack([(_sb_head_of((1, SB_W)) == hh).astype(dtype) for hh in range(SB_HPB)])


def _sb_core_fwd(qkv, qg, kg, name, comm=None):
    T = qkv.shape[0]
    B = min(SB_TILE, T)
    NB = T // B
    NP = SB_H // SB_HPB

    def body(q_ref, k_ref, v_ref, qg_ref, kg_ref, o_ref, tot_ref, qs_s, kn_s, v_s):
        _sb_fill(q_ref, k_ref, v_ref, qg_ref, kg_ref, qs_s, kn_s, v_s)
        r, c = _iota((B, B), 0), _iota((B, B), 1)
        causal = c < r
        m_after = (r > c).astype(BF16)
        head_b = _sb_head_of((B, SB_W))
        hm = _sb_head_masks(BF16)

        def tile(qb, kj, vj, R, acc, diag):
            z = _dot(qb, kj, NT)
            sp = _softplus(z)
            ls = z - sp
            lm = jnp.where(causal, -sp, 0.0) if diag else -sp
            cs = _dot2r(lm, m_after) + R
            a = jnp.exp(ls + cs)
            if diag:
                a = jnp.where(causal, a, 0.0)
            return R + _rowsum(lm), acc + jnp.sum(_dot(a, vj), axis=0)

        def qblock(i, carry):
            rows_i = pl.ds(pl.multiple_of(i * B, B), B)
            qb = qs_s[rows_i, :][None] * hm

            def step(rows_j, st, diag):
                return tile(qb, kn_s[rows_j, :], v_s[rows_j, :][None] * hm, st[0], st[1], diag)

            st = step(rows_i, (jnp.zeros((SB_HPB, B, 1), F32), jnp.zeros((B, SB_W), F32)), True)
            st = lax.fori_loop(0, i, lambda s, st: step(pl.ds(pl.multiple_of((i - 1 - s) * B, B), B), st, False), st)
            o_ref[rows_i, :] = st[1].astype(o_ref.dtype)
            tot = jnp.zeros((B, SB_W), F32)
            for hh in range(SB_HPB):
                tot = jnp.where(head_b == hh, st[0][hh], tot)
            tot_ref[rows_i, :] = tot
            return carry

        lax.fori_loop(0, NB, qblock, 0)

    blk = lambda k: pl.BlockSpec((T, SB_W), lambda p: (0, k * NP + p))
    small = pl.BlockSpec((1, SB_W), lambda p: (0, 0))
    return _grid_call(
        body, comm, grid=(NP,), in_specs=[blk(0), blk(1), blk(2), small, small],
        out_specs=[pl.BlockSpec((T, SB_W), lambda p: (0, p))] * 2,
        out_shape=[jax.ShapeDtypeStruct((T, D), BF16), jax.ShapeDtypeStruct((T, D), F32)],
        scratch_shapes=[pltpu.VMEM((T, SB_W), BF16)] * 3, name=name, args=(qkv, qkv, qkv, qg, kg))


def _sb_core_bwd(qkv, qg, kg, tot, dout, name, comm=None):
    T = qkv.shape[0]
    B = min(SB_TILE, T)
    NB = T // B
    NP = SB_H // SB_HPB

    def body(q_ref, k_ref, v_ref, qg_ref, kg_ref, tot_ref, do_ref, dq_ref, dk_ref, dv_ref, dqg_ref, dkg_ref,
             qs_s, kn_s, v_s, dqn_s, dkn_s, dvv_s):
        p = pl.program_id(0)
        _sb_fill(q_ref, k_ref, v_ref, qg_ref, kg_ref, qs_s, kn_s, v_s)
        dkn_s[...] = jnp.zeros(dkn_s.shape, F32)
        dvv_s[...] = jnp.zeros(dvv_s.shape, F32)
        r, c = _iota((B, B), 0), _iota((B, B), 1)
        causal = c < r
        m_upto = (r <= c).astype(BF16)
        m_before = (r < c).astype(BF16)
        lane_b = _iota((B, SB_W), 1)
        hm, hmf = _sb_head_masks(BF16), _sb_head_masks(F32)

        def tile(qb, dob, tot_h, kj, vj, PL, P, dq, diag):
            z = _dot(qb, kj, NT)
            sp = _softplus(z)
            ls = z - sp
            lm = jnp.where(causal, -sp, 0.0) if diag else -sp
            cs = tot_h - PL - _dot2r(lm, m_upto)
            a = jnp.exp(ls + cs)
            if diag:
                a = jnp.where(causal, a, 0.0)
            e = _dot(dob, vj, NT) * a
            E = _dot(e, m_before) + P
            sig = jnp.exp(ls)
            dz = e * (1.0 - sig) - E * sig
            if diag:
                dz = jnp.where(causal, dz, 0.0)
            dq = dq + jnp.sum(_dot(dz, kj) * hmf, axis=0)
            return (PL + _rowsum(lm), P + _rowsum(e), dq), jnp.sum(_dot(dz, qb, TN), axis=0), jnp.sum(_dot(a, dob, TN), axis=0)

        def qblock(i, carry):
            rows_i = pl.ds(pl.multiple_of(i * B, B), B)
            totb = tot_ref[rows_i, :]
            tot_h = jnp.stack([_rowsum(jnp.where(lane_b == hh * SB_D, totb, 0.0)) for hh in range(SB_HPB)])
            qb = qs_s[rows_i, :][None] * hm
            dob = do_ref[rows_i, :][None] * hm

            def step(rows_j, st, diag):
                st, dk, dv = tile(qb, dob, tot_h, kn_s[rows_j, :], v_s[rows_j, :][None] * hm, st[0], st[1], st[2], diag)
                dkn_s[rows_j, :] += dk
                dvv_s[rows_j, :] += dv
                return st

            zero = (jnp.zeros((SB_HPB, B, 1), F32), jnp.zeros((SB_HPB, B, 1), F32), jnp.zeros((B, SB_W), F32))
            st = lax.fori_loop(0, i, lambda j, st: step(pl.ds(pl.multiple_of(j * B, B), B), st, False), zero)
            st = step(rows_i, st, True)
            dqn_s[rows_i, :] = st[2] * (SB_D ** -0.5)
            return carry

        lax.fori_loop(0, NB, qblock, 0)

        head = _sb_head_of((T, SB_W))

        def norm_bwd(dn, x_ref, g):
            xh, rr = _sb_head_norm(x_ref[...], head)
            t = dn * g
            return rr * (t - xh * (_sb_head_sums(t * xh, head) * (1.0 / SB_D))), _colsum(dn * xh)

        dq, dqg = norm_bwd(dqn_s[...], q_ref, qg_ref[...])
        dq_ref[...] = dq.astype(dq_ref.dtype)
        dk, dkg = norm_bwd(dkn_s[...], k_ref, kg_ref[...])
        dk_ref[...] = dk.astype(dk_ref.dtype)
        dv_ref[...] = dvv_s[...].astype(dv_ref.dtype)

        @pl.when(p == 0)
        def _():
            dqg_ref[...] = dqg
            dkg_ref[...] = dkg

        @pl.when(p > 0)
        def _():
            dqg_ref[...] += dqg
            dkg_ref[...] += dkg

    blk = lambda k: pl.BlockSpec((T, SB_W), lambda p: (0, k * NP + p))
    small = pl.BlockSpec((1, SB_W), lambda p: (0, 0))
    own = pl.BlockSpec((T, SB_W), lambda p: (0, p))
    outs = _grid_call(
        body, comm, grid=(NP,), in_specs=[blk(0), blk(1), blk(2), small, small, own, own],
        out_specs=[own, own, own, small, small],
        out_shape=[jax.ShapeDtypeStruct((T, D), BF16)] * 3 + [jax.ShapeDtypeStruct((1, SB_W), F32)] * 2,
        scratch_shapes=[pltpu.VMEM((T, SB_W), BF16)] * 3 + [pltpu.VMEM((T, SB_W), F32)] * 3,
        name=name, args=(qkv, qkv, qkv, qg, kg, tot, dout))
    dq, dk, dv, dqg, dkg = outs[:5]
    return jnp.concatenate([dq, dk, dv], axis=1), dqg, dkg, (outs[5] if comm else None)


def _ada_fwd(c16, ada_w, ada_b_cols, name):
    L, _, cols = ada_w.shape

    def body(c_ref, w_ref, b_ref, o_ref):
        o_ref[0] = _dot(_silu(c_ref[...]), w_ref[0]) + b_ref[0]

    return pl.pallas_call(
        body, grid=(L,),
        in_specs=[pl.BlockSpec((16, D), lambda i: (0, 0)), pl.BlockSpec((1, D, cols), lambda i: (i, 0, 0)),
                  pl.BlockSpec((1, 1, cols), lambda i: (i, 0, 0))],
        out_specs=pl.BlockSpec((1, 16, cols), lambda i: (i, 0, 0)),
        out_shape=jax.ShapeDtypeStruct((L, 16, cols), F32), name=name, compiler_params=_params(("arbitrary",)),
    )(c16, ada_w, ada_b_cols)


def _ada_bwd(c16, dmod16, name):
    L, _, cols = dmod16.shape

    def body(c_ref, d_ref, o_ref):
        o_ref[0] = _dot(_silu(c_ref[...]), d_ref[0], TN)

    return pl.pallas_call(
        body, grid=(L,),
        in_specs=[pl.BlockSpec((16, D), lambda i: (0, 0)), pl.BlockSpec((1, 16, cols), lambda i: (i, 0, 0))],
        out_specs=pl.BlockSpec((1, D, cols), lambda i: (i, 0, 0)),
        out_shape=jax.ShapeDtypeStruct((L, D, cols), F32), name=name, compiler_params=_params(("arbitrary",)),
    )(c16, dmod16)


def _sum_sources(x, name):
    n, R, C = x.shape
    tile = _sum_tile(R) or R

    def body(x_ref, o_ref):
        acc = x_ref[0].astype(F32)
        for k in range(1, n):
            acc = acc + x_ref[k].astype(F32)
        o_ref[...] = acc

    return pl.pallas_call(
        body, grid=(R // tile,), in_specs=[pl.BlockSpec((n, tile, C), lambda i: (0, i, 0))],
        out_specs=pl.BlockSpec((tile, C), lambda i: (i, 0)), out_shape=jax.ShapeDtypeStruct((R, C), F32),
        name=name, compiler_params=_params(("parallel",)),
    )(x)


def _adamw(w, g, m, v, name):
    shape = w.shape
    C = shape[-1]
    R = w.size // C
    w2, g2, m2, v2 = (a.reshape(R, C) for a in (w, g, m, v))
    tile = _rtile(R)
    c1 = 1.0 / (1.0 - ADAM_B1 ** ADAM_STEP)
    c2 = 1.0 / (1.0 - ADAM_B2 ** ADAM_STEP)

    def body(w_ref, g_ref, m_ref, v_ref, d_ref, nm_ref, nv_ref):
        gv = g_ref[...]
        nm = ADAM_B1 * m_ref[...] + (1.0 - ADAM_B1) * gv
        nv = ADAM_B2 * v_ref[...] + (1.0 - ADAM_B2) * (gv * gv)
        d_ref[...] = -ADAM_LR * ((nm * c1) / (jnp.sqrt(nv * c2) + ADAM_EPS) + ADAM_WD * w_ref[...])
        nm_ref[...] = nm
        nv_ref[...] = nv

    spec = pl.BlockSpec((tile, C), lambda i: (i, 0))
    outs = pl.pallas_call(
        body, grid=(R // tile,), in_specs=[spec] * 4, out_specs=[spec] * 3,
        out_shape=[jax.ShapeDtypeStruct((R, C), F32)] * 3, name=name, compiler_params=_params(("parallel",)),
    )(w2, g2, m2, v2)
    return tuple(o.reshape(shape) for o in outs)


def _gather_steps(x_ref, out_ref, send_sems, recv_sems, local_sem):
    mx, my, mc = lax.axis_index("x"), lax.axis_index("y"), lax.axis_index("c")
    me, sibling = (mx, my, mc), (mx, my, 1 - mc)
    chips = [(1 - mx, my), (mx, 1 - my), (1 - mx, 1 - my)]

    def slot(px, py, pc):
        return out_ref.at[4 * px + 2 * py + pc]

    def copy(k, block, to, src=None):
        return pltpu.make_async_remote_copy(
            src_ref=slot(*block) if src is None else src, dst_ref=slot(*block),
            send_sem=send_sems.at[k], recv_sem=recv_sems.at[k], device_id=to, device_id_type=MESH)

    mine = pltpu.make_async_copy(x_ref, slot(*me), local_sem)
    first = [copy(0, me, sibling, src=x_ref)] + [copy(1 + j, me, (*chip, mc), src=x_ref) for j, chip in enumerate(chips)]
    passed = [copy(4 + j, (*chip, mc), sibling) for j, chip in enumerate(chips)]

    def start():
        mine.start()
        for cp in first:
            cp.start()

    def forward():
        for j, chip in enumerate(chips):
            copy(1 + j, (*chip, mc), me).wait_recv()
            passed[j].start()

    def finish():
        copy(0, sibling, me).wait_recv()
        for j, chip in enumerate(chips):
            copy(4 + j, (*chip, 1 - mc), me).wait_recv()
        for cp in first + passed:
            cp.wait_send()
        mine.wait()

    return start, forward, finish


def _exchange_steps(x_ref, out_ref, send_sems, recv_sems, local_sem):
    mx, my, mc = lax.axis_index("x"), lax.axis_index("y"), lax.axis_index("c")
    me = 4 * mx + 2 * my + mc
    mine = pltpu.make_async_copy(x_ref.at[me], out_ref.at[me], local_sem)
    copies = []
    for k in range(1, NDEV):
        px, py, pc = mx ^ (k >> 2), my ^ ((k >> 1) & 1), mc ^ (k & 1)
        copies.append(pltpu.make_async_remote_copy(
            src_ref=x_ref.at[4 * px + 2 * py + pc], dst_ref=out_ref.at[me], send_sem=send_sems.at[k - 1],
            recv_sem=recv_sems.at[k - 1], device_id=(px, py, pc), device_id_type=MESH))

    def start():
        mine.start()
        for cp in copies:
            cp.start()

    def finish():
        for cp in copies:
            cp.wait_recv()
        for cp in copies:
            cp.wait_send()
        mine.wait()

    return start, finish


def _comm_sems():
    return [pltpu.SemaphoreType.DMA((7,)), pltpu.SemaphoreType.DMA((7,)), pltpu.SemaphoreType.DMA]


def _comm_out_shape(comm):
    kind, x = comm
    return jax.ShapeDtypeStruct(((NDEV,) + x.shape) if kind == "gather" else x.shape, x.dtype)


def _ride_along(comm, step, n_steps, refs, at_end):
    if comm[0] == "gather":
        start, forward, finish = _gather_steps(*refs)
        todo = [(n_steps - 1, finish)] if at_end else [(0, start), (n_steps - 1, forward)]
    else:
        start, finish = _exchange_steps(*refs)
        todo = [(n_steps - 1, finish)] if at_end else [(0, start)]
    for at, fn in todo:
        pl.when(step == at)(fn)


def _grid_call(body, comm, *, grid, in_specs, out_specs, out_shape, scratch_shapes, name, args):
    if comm is None:
        return pl.pallas_call(body, grid=grid, in_specs=in_specs, out_specs=out_specs, out_shape=out_shape,
                              scratch_shapes=scratch_shapes, name=name, compiler_params=_params(("arbitrary",)))(*args)
    n_in, n_out, n_steps = len(in_specs), len(out_specs), grid[0]

    def with_comm(*refs):
        ins, outs, scr = refs[:n_in], refs[n_in + 1:n_in + 1 + n_out], refs[n_in + 2 + n_out:-3]
        comm_refs = (refs[n_in], refs[n_in + 1 + n_out]) + refs[-3:]
        step = pl.program_id(0)
        _ride_along(comm, step, n_steps, comm_refs, False)
        body(*ins, *outs, *scr)
        _ride_along(comm, step, n_steps, comm_refs, True)

    hbm = pl.BlockSpec(memory_space=pl.ANY)
    return pl.pallas_call(
        with_comm, grid=grid, in_specs=list(in_specs) + [hbm], out_specs=list(out_specs) + [hbm],
        out_shape=list(out_shape) + [_comm_out_shape(comm)], scratch_shapes=list(scratch_shapes) + _comm_sems(),
        name=name, compiler_params=_params(("arbitrary",)))(*args, comm[1])


def _all_gather(x, name, in_vmem):
    def body(x_ref, out_ref, send_sems, recv_sems, local_sem):
        start, forward, finish = _gather_steps(x_ref, out_ref, send_sems, recv_sems, local_sem)
        start()
        forward()
        finish()

    space = pltpu.VMEM if in_vmem else pl.ANY
    return pl.pallas_call(
        body, out_shape=jax.ShapeDtypeStruct((NDEV,) + x.shape, x.dtype),
        in_specs=[pl.BlockSpec(memory_space=space)], out_specs=pl.BlockSpec(memory_space=space),
        scratch_shapes=_comm_sems(), name=name, compiler_params=pltpu.CompilerParams(vmem_limit_bytes=VMEM_LIMIT),
    )(x)


def _all_to_all(x, name):
    def body(x_ref, out_ref, send_sems, recv_sems, local_sem):
        start, finish = _exchange_steps(x_ref, out_ref, send_sems, recv_sems, local_sem)
        start()
        finish()

    return pl.pallas_call(
        body, out_shape=jax.ShapeDtypeStruct(x.shape, x.dtype),
        in_specs=[pl.BlockSpec(memory_space=pl.ANY)], out_specs=pl.BlockSpec(memory_space=pl.ANY),
        scratch_shapes=_comm_sems(), name=name, compiler_params=pltpu.CompilerParams(vmem_limit_bytes=VMEM_LIMIT),
    )(x)


def _stage_parts(st):
    parts = []
    if st >= 1:
        parts += [("ffn_w_in", st - 1, (D, 2 * DFF // NDEV), "colsT"), ("ffn_w_out", st - 1, (DFF // NDEV, D), "rows")]
    if st < DEPTH:
        j = st // 2
        if st % 2 == 0:
            parts += [("dn_w_in", j, (D, DN_COLS // NDEV), "colsT"), ("dn_w_out", j, (D // NDEV, D), "rows")]
        else:
            parts += [("sb_w_qkv", j, (D, 3 * D // NDEV), "colsT"), ("sb_w_out", j, (D // NDEV, D), "rows")]
    return parts


def _sum_tile(rows):
    for t in range(512, 191, -16):
        if rows % t == 0:
            return t
    return None


def _stage_layout(st):
    out, r = [], 0
    for n, j, s, kind in _stage_parts(st):
        k = s[0] * s[1] // D
        kp = -(-k // 16) * 16
        out.append((n, j, s, kind, k, kp, r))
        r += kp
    while _sum_tile(r) is None:
        r += 16
    return out, r


def _pack_stage(w, st):
    parts, rows = _stage_layout(st)
    bufs = [jnp.pad((w[n][j].T if kind == "colsT" else w[n][j]).astype(BF16).reshape(k, D), ((0, kp - k), (0, 0)))
            for n, j, _, kind, k, kp, _ in parts]
    used = parts[-1][6] + parts[-1][5]
    if rows > used:
        bufs.append(jnp.zeros((rows - used, D), BF16))
    return jnp.concatenate(bufs, axis=0)


def _full_stage(g, st):
    out = {}
    for n, _, s, kind, k, _, r in _stage_layout(st)[0]:
        out[n] = g[:, r:r + k].reshape((NDEV * s[0], s[1]) if kind == "rows" else (NDEV * s[1], s[0]))
    if "dn_w_in" in out:
        out["dn_w_in"] = jnp.pad(out["dn_w_in"], ((0, DN_COLS_PAD - DN_COLS), (0, 0)))
    return out


def _shards_stage(full, st):
    parts, rows = _stage_layout(st)
    bufs = []
    for n, _, s, kind, k, kp, _ in parts:
        bufs.append(jnp.pad(full[n].astype(BF16).reshape(NDEV, k, D), ((0, 0), (0, kp - k), (0, 0))))
    used = parts[-1][6] + parts[-1][5]
    if rows > used:
        bufs.append(jnp.zeros((NDEV, rows - used, D), BF16))
    return jnp.concatenate(bufs, axis=1)


def _unpack_stage(buf, st):
    return {(n, j): buf[r:r + k].reshape(s[1], s[0]).T if kind == "colsT" else buf[r:r + k].reshape(s)
            for n, j, s, kind, k, _, r in _stage_layout(st)[0]}


def _local_step(x, target, mod, small, w0, rest):
    dist = isinstance(rest, tuple)
    packs = rest[1] if dist else None
    W = [dict(w0), {}, {}, {}] if dist else [w0] + list(rest)

    def arrived(k, g):
        for n, a in _full_stage(g, k + 1).items():
            W[k if n.startswith("ffn") else k + 1][n] = a

    row = lambda v: v.reshape(1, -1)
    pad128 = lambda v: jnp.pad(v.reshape(1, -1), ((0, 0), (0, LANES - v.size)))
    saved = []
    for i in range(DEPTH):
        j = i // 2
        w = W[i]
        m = [row(mod[i, k * D:(k + 1) * D]) for k in range(N_MOD)]
        sh1, sc1, gt1, sh2, sc2, gt2 = m
        g1, g2 = row(small["norm1_g"][i]), row(small["norm2_g"][i])
        h1 = _norm_mod(x, g1, sc1, sh1, f"norm1_{i}")
        if i % 2 == 0:
            proj = _mm(h1, w["dn_w_in"], "nt", F32, f"dn_proj_{i}")
            alog, dtb, og = pad128(small["dn_a_log"][j]), pad128(small["dn_dt_bias"][j]), row(small["dn_onorm_g"][j])
            comm = ("gather", packs[i]) if dist else None
            res = _dn_core_fwd(proj, small["dn_conv_w"][j], alog, dtb, og, f"dn_core_{i}", comm)
            om, o_pre, states = res[:3]
            if comm:
                arrived(i, res[3])
            y1 = _mm(om, w["dn_w_out"], "nn", F32, f"dn_out_{i}")
            mix = (proj, alog, dtb, og, o_pre, states, om)
        else:
            qkv = _mm(h1, w["sb_w_qkv"], "nt", F32, f"sb_qkv_{i}")
            qg2 = jnp.tile(row(small["sb_q_norm_g"][j]), (1, SB_HPB))
            kg2 = jnp.tile(row(small["sb_k_norm_g"][j]), (1, SB_HPB))
            comm = ("gather", packs[i]) if dist else None
            res = _sb_core_fwd(qkv, qg2, kg2, f"sb_core_{i}", comm)
            om, tot = res[:2]
            if comm:
                arrived(i, res[2])
            y1 = _mm(om, w["sb_w_out"], "nn", F32, f"sb_out_{i}")
            mix = (qkv, qg2, kg2, tot, om)
        x_mid = _gate_res(x, y1, gt1, f"res1_{i}")
        h2 = _norm_mod(x_mid, g2, sc2, sh2, f"norm2_{i}")
        u = _mm(h2, w["ffn_w_in"], "nt", BF16, f"ffn_in_{i}")
        a = _swiglu_act(u, f"ffn_act_{i}")
        y2 = _mm(a, w["ffn_w_out"], "nn", F32, f"ffn_out_{i}")
        x_out = _gate_res(x_mid, y2, gt2, f"res2_{i}")
        saved.append((x, h1, mix, y1, x_mid, h2, u, a, y2, m, g1, g2))
        x = x_out

    dx, sq = _loss_head(x, target, "loss_head")

    sg = dict(mod=[None] * DEPTH, norm1_g=[None] * DEPTH, norm2_g=[None] * DEPTH, dn_conv_w=[None] * 2,
              dn_a_log=[None] * 2, dn_dt_bias=[None] * 2, dn_onorm_g=[None] * 2, sb_q_norm_g=[None] * 2,
              sb_k_norm_g=[None] * 2)
    big = [None] * (DEPTH + 1 if dist else DEPTH)
    mixer_gw = {}
    for i in reversed(range(DEPTH)):
        j = i // 2
        w = W[i]
        x_in, h1, mix, y1, x_mid, h2, u, a, y2, m, g1, g2 = saved[i]
        sh1, sc1, gt1, sh2, sc2, gt2 = m
        gw = {}
        dy2, dgt2 = _gate_res_bwd(dx, y2, gt2, f"res2_bwd_{i}")
        da = _mm(dy2, w["ffn_w_out"], "nt", BF16, f"ffn_out_bwd_{i}")
        gw["ffn_w_out"] = _mm(a, dy2, "tn", BF16, f"ffn_out_wg_{i}")
        du = _swiglu_act_bwd(da, u, f"ffn_act_bwd_{i}")
        dh2 = _mm(du, w["ffn_w_in"], "nn", F32, f"ffn_in_bwd_{i}")
        gw["ffn_w_in"] = _mm(du, h2, "tn", BF16, f"ffn_in_wg_{i}")
        dx_mid, s2, dsh2 = _norm_mod_bwd(dh2, x_mid, dx, g2, sc2, f"norm2_bwd_{i}")
        dy1, dgt1 = _gate_res_bwd(dx_mid, y1, gt1, f"res1_bwd_{i}")
        comm = ("exchange", _shards_stage({**gw, **mixer_gw}, i + 1)) if dist else None
        if i % 2 == 0:
            proj, alog, dtb, og, o_pre, states, om = mix
            dom = _mm(dy1, w["dn_w_out"], "nt", F32, f"dn_out_bwd_{i}")
            gw["dn_w_out"] = _mm(om, dy1, "tn", BF16, f"dn_out_wg_{i}")
            dproj, dconv, dalog, ddtb, dog, got = _dn_core_bwd(proj, small["dn_conv_w"][j], alog, dtb, og, o_pre, states,
                                                               dom, f"dn_core_bwd_{i}", comm)
            dh1 = _mm(dproj, w["dn_w_in"], "nn", F32, f"dn_proj_bwd_{i}")
            gw["dn_w_in"] = _mm(dproj, h1, "tn", BF16, f"dn_proj_wg_{i}")[:DN_COLS]
            sg["dn_conv_w"][j] = dconv
            sg["dn_a_log"][j] = dalog[0, :DN_H]
            sg["dn_dt_bias"][j] = ddtb[0, :DN_H]
            sg["dn_onorm_g"][j] = dog[0]
        else:
            qkv, qg2, kg2, tot, om = mix
            dom = _mm(dy1, w["sb_w_out"], "nt", BF16, f"sb_out_bwd_{i}")
            gw["sb_w_out"] = _mm(om, dy1, "tn", BF16, f"sb_out_wg_{i}")
            dqkv, dqg, dkg, got = _sb_core_bwd(qkv, qg2, kg2, tot, dom, f"sb_core_bwd_{i}", comm)
            dh1 = _mm(dqkv, w["sb_w_qkv"], "nn", F32, f"sb_qkv_bwd_{i}")
            gw["sb_w_qkv"] = _mm(dqkv, h1, "tn", BF16, f"sb_qkv_wg_{i}")
            sg["sb_q_norm_g"][j] = jnp.sum(dqg.reshape(SB_HPB, SB_D), axis=0)
            sg["sb_k_norm_g"][j] = jnp.sum(dkg.reshape(SB_HPB, SB_D), axis=0)
        if comm:
            big[i + 1] = got
        dx, s1, dsh1 = _norm_mod_bwd(dh1, x_in, dx_mid, g1, sc1, f"norm1_bwd_{i}")
        sg["mod"][i] = jnp.concatenate([dsh1, s1 * g1, dgt1, dsh2, s2 * g2, dgt2], axis=1)[0]
        sg["norm1_g"][i] = (s1 * (1.0 + sc1))[0]
        sg["norm2_g"][i] = (s2 * (1.0 + sc2))[0]
        if dist:
            mixer_gw = {n: a for n, a in gw.items() if not n.startswith("ffn")}
        else:
            big[i] = gw
    if dist:
        big[0] = _all_to_all(_shards_stage(mixer_gw, 0), "exchange_grads_0")
    return sq, dx, {k: jnp.stack(v) for k, v in sg.items()}, big


def _device_index():
    return 4 * lax.axis_index("x") + 2 * lax.axis_index("y") + lax.axis_index("c")


def _gather_small(w, c):
    me = _device_index()
    ada_cols = w["ada_w"].shape[-1]
    conv_cols = w["dn_conv_w"].shape[-1]
    blk = jnp.concatenate([c.reshape(1, D), w["dn_conv_w"].reshape(-1, D)], axis=0)
    g1 = _all_gather(blk, "gather_cond", True)
    c16 = jnp.pad(g1[:, 0, :], ((0, 8), (0, 0)))
    conv_full = jnp.transpose(g1[:, 1:, :].reshape(NDEV, 2, DN_CONV, conv_cols), (1, 2, 0, 3)).reshape(2, DN_CONV, -1)
    b_cols = lax.dynamic_slice_in_dim(w["ada_b"], me * ada_cols, ada_cols, axis=1).reshape(DEPTH, 1, ada_cols)
    mod_part = _ada_fwd(c16, w["ada_w"], b_cols, "ada_fwd")[:, :NDEV, :]
    g2 = _all_gather(mod_part.reshape(DEPTH * NDEV, ada_cols), "gather_mod", True)
    g2 = g2.reshape(NDEV, DEPTH, NDEV, ada_cols)
    mod = lax.dynamic_index_in_dim(g2, me, axis=2, keepdims=False)
    mod = jnp.transpose(mod, (1, 0, 2)).reshape(DEPTH, N_MOD * D)
    return c16, conv_full, mod


def _reduce_small(gr, c16):
    me = _device_index()
    ada_cols = N_MOD * D // NDEV
    conv_cols = 3 * DN_H * DN_D // NDEV
    grads = {}
    small = jnp.concatenate([gr["dn_a_log"].reshape(-1), gr["dn_dt_bias"].reshape(-1), gr["dn_onorm_g"].reshape(-1),
                             gr["sb_q_norm_g"].reshape(-1), gr["sb_k_norm_g"].reshape(-1)])
    small = jnp.pad(small, (0, D - small.size)).reshape(1, D)
    blk3 = jnp.concatenate([gr["mod"].reshape(-1, D), gr["norm1_g"], gr["norm2_g"], gr["dn_conv_w"].reshape(-1, D),
                            small], axis=0)
    blk3 = jnp.pad(blk3, ((0, 64 - blk3.shape[0]), (0, 0)))
    g3 = _all_gather(blk3, "gather_small_grads", True)
    tot = _sum_sources(g3, "sum_small_grads")
    grads["ada_b"] = tot[:24].reshape(DEPTH, N_MOD * D)
    grads["norm1_g"] = tot[24:28]
    grads["norm2_g"] = tot[28:32]
    conv_g = tot[32:56].reshape(2, DN_CONV, NDEV * conv_cols)
    grads["dn_conv_w"] = lax.dynamic_slice_in_dim(conv_g, me * conv_cols, conv_cols, axis=2)
    sm = tot[56]
    grads["dn_a_log"] = sm[0:16].reshape(2, DN_H)
    grads["dn_dt_bias"] = sm[16:32].reshape(2, DN_H)
    grads["dn_onorm_g"] = sm[32:288].reshape(2, DN_D)
    grads["sb_q_norm_g"] = sm[288:416].reshape(2, SB_D)
    grads["sb_k_norm_g"] = sm[416:544].reshape(2, SB_D)
    dmod_all = g3[:, :24, :].reshape(NDEV, DEPTH, N_MOD * D)
    dmod_cols = lax.dynamic_slice_in_dim(dmod_all, me * ada_cols, ada_cols, axis=2)
    dmod16 = jnp.pad(jnp.transpose(dmod_cols, (1, 0, 2)), ((0, 0), (0, 8), (0, 0)))
    grads["ada_w"] = _ada_bwd(c16, dmod16, "ada_bwd")
    return grads


def _reduce_big(recv):
    parts = {}
    for st in range(DEPTH + 1):
        parts.update(_unpack_stage(_sum_sources(recv[st], f"sum_grads_{st}"), st))
    out = {}
    for (n, j) in sorted(parts):
        out.setdefault(n, []).append(parts[(n, j)])
    return {n: jnp.stack(v) for n, v in out.items()}


def kernel(x, c, ada_w, ada_b, norm1_g, norm2_g, dn_w_in, dn_conv_w, dn_a_log, dn_dt_bias, dn_onorm_g, dn_w_out, sb_w_qkv, sb_q_norm_g, sb_k_norm_g, sb_w_out, ffn_w_in, ffn_w_out, loss_target, m_ada_w, m_ada_b, m_norm1_g, m_norm2_g, m_dn_w_in, m_dn_conv_w, m_dn_a_log, m_dn_dt_bias, m_dn_onorm_g, m_dn_w_out, m_sb_w_qkv, m_sb_q_norm_g, m_sb_k_norm_g, m_sb_w_out, m_ffn_w_in, m_ffn_w_out, v_ada_w, v_ada_b, v_norm1_g, v_norm2_g, v_dn_w_in, v_dn_conv_w, v_dn_a_log, v_dn_dt_bias, v_dn_onorm_g, v_dn_w_out, v_sb_w_qkv, v_sb_q_norm_g, v_sb_k_norm_g, v_sb_w_out, v_ffn_w_in, v_ffn_w_out):
    w = dict(ada_w=ada_w, ada_b=ada_b, norm1_g=norm1_g, norm2_g=norm2_g, dn_w_in=dn_w_in, dn_conv_w=dn_conv_w,
             dn_a_log=dn_a_log, dn_dt_bias=dn_dt_bias, dn_onorm_g=dn_onorm_g, dn_w_out=dn_w_out, sb_w_qkv=sb_w_qkv,
             sb_q_norm_g=sb_q_norm_g, sb_k_norm_g=sb_k_norm_g, sb_w_out=sb_w_out, ffn_w_in=ffn_w_in, ffn_w_out=ffn_w_out)
    mom = dict(ada_w=m_ada_w, ada_b=m_ada_b, norm1_g=m_norm1_g, norm2_g=m_norm2_g, dn_w_in=m_dn_w_in,
               dn_conv_w=m_dn_conv_w, dn_a_log=m_dn_a_log, dn_dt_bias=m_dn_dt_bias, dn_onorm_g=m_dn_onorm_g,
               dn_w_out=m_dn_w_out, sb_w_qkv=m_sb_w_qkv, sb_q_norm_g=m_sb_q_norm_g, sb_k_norm_g=m_sb_k_norm_g,
               sb_w_out=m_sb_w_out, ffn_w_in=m_ffn_w_in, ffn_w_out=m_ffn_w_out)
    var = dict(ada_w=v_ada_w, ada_b=v_ada_b, norm1_g=v_norm1_g, norm2_g=v_norm2_g, dn_w_in=v_dn_w_in,
               dn_conv_w=v_dn_conv_w, dn_a_log=v_dn_a_log, dn_dt_bias=v_dn_dt_bias, dn_onorm_g=v_dn_onorm_g,
               dn_w_out=v_dn_w_out, sb_w_qkv=v_sb_w_qkv, sb_q_norm_g=v_sb_q_norm_g, sb_k_norm_g=v_sb_k_norm_g,
               sb_w_out=v_sb_w_out, ffn_w_in=v_ffn_w_in, ffn_w_out=v_ffn_w_out)
    names = list(w)
    c16, conv_full, mod = _gather_small(w, c)
    packs = [_pack_stage(w, st) for st in range(DEPTH + 1)]
    w0 = _full_stage(_all_gather(packs[0], "gather_weights_0", False), 0)
    small = dict(norm1_g=norm1_g, norm2_g=norm2_g, dn_conv_w=conv_full, dn_a_log=dn_a_log, dn_dt_bias=dn_dt_bias,
                 dn_onorm_g=dn_onorm_g, sb_q_norm_g=sb_q_norm_g, sb_k_norm_g=sb_k_norm_g)
    sq, grad_x, sgr, recv = _local_step(x[0], loss_target[0], mod, small, w0, ("packed", packs[1:]))
    loss = lax.psum(sq[0, 0] * (0.5 / D), ("x", "y", "c"))
    grads = {**_reduce_big(recv), **_reduce_small(sgr, c16)}
    delta, new_m, new_v = {}, {}, {}
    for n in names:
        delta[n], new_m[n], new_v[n] = _adamw(w[n], grads[n], mom[n], var[n], f"adamw_{n}")
    return (loss, grad_x[None], *[grads[n] for n in names], *[delta[n] for n in names],
            *[new_m[n] for n in names], *[new_v[n] for n in names])
```

```python
import jax
import jax.numpy as jnp
from jax import lax
from jax.experimental import pallas as pl
from jax.experimental.pallas import tpu as pltpu

F32, BF16 = jnp.float32, jnp.bfloat16

D = 1024
DEPTH = 4
N_MOD = 6
DN_H, DN_D, DN_C, DN_CONV = 8, 128, 64, 4
DN_UNROLL_FWD, DN_UNROLL_BWD = 16, 8
DN_COLS = 4 * DN_H * DN_D + 2 * DN_H
DN_COLS_PAD = 33 * 128
SB_H, SB_D = 16, 64
SB_TILE = 256
SB_HPB = 4
SB_W = SB_HPB * SB_D
DFF = 2816
EPS = 1e-6
NDEV = 8
LANES = 128
VMEM_LIMIT = 56 * 1024 * 1024

ADAM_LR, ADAM_B1, ADAM_B2, ADAM_EPS, ADAM_WD, ADAM_STEP = 0.001, 0.9, 0.999, 1e-08, 0.01, 10

NN = ((1,), (0,))
NT = ((1,), (1,))
TN = ((0,), (0,))
MESH = pl.DeviceIdType.MESH


def _dot(a, b, dims=NN):
    a, b = a.astype(BF16), b.astype(BF16)
    if a.ndim == 2 and b.ndim == 2:
        return lax.dot_general(a, b, (dims, ((), ())), preferred_element_type=F32)
    n = a.shape[0] if a.ndim == 3 else b.shape[0]
    if a.ndim == 2:
        a = jnp.broadcast_to(a, (n,) + a.shape)
    if b.ndim == 2:
        b = jnp.broadcast_to(b, (n,) + b.shape)
    (ca,), (cb,) = dims
    return lax.dot_general(a, b, (((ca + 1,), (cb + 1,)), ((0,), (0,))), preferred_element_type=F32)


def _split(a):
    hi = a.astype(BF16)
    lo = (a - hi.astype(F32)).astype(BF16)
    return hi, lo


def _dot2r(a, m, dims=NN):
    ah, al = _split(a)
    return _dot(ah, m, dims) + _dot(al, m, dims)


def _dot2l(m, b, dims=NN):
    bh, bl = _split(b)
    return _dot(m, bh, dims) + _dot(m, bl, dims)


def _sigmoid(x):
    return 1.0 / (1.0 + jnp.exp(-x))


def _silu(x):
    return x * _sigmoid(x)


def _dsilu(x):
    s = _sigmoid(x)
    return s * (1.0 + x * (1.0 - s))


def _softplus(x):
    return jnp.maximum(x, 0.0) + jnp.log(1.0 + jnp.exp(-jnp.abs(x)))


def _iota(shape, dim):
    return lax.broadcasted_iota(jnp.int32, shape, dim)


def _rowsum(x):
    return jnp.sum(x, axis=-1, keepdims=True)


def _colsum(x):
    return jnp.sum(x, axis=-2, keepdims=True)


def _tile(n, pref):
    if n <= pref:
        return n
    best = None
    for t in range(LANES, pref + 1, LANES):
        if n % t == 0:
            best = t
    assert best is not None, (n, pref)
    return best


def _rtile(r, pref=512):
    best = None
    for t in range(8, min(r, pref) + 1, 8):
        if r % t == 0:
            best = t
    return best if best is not None else r


def _params(sem):
    return pltpu.CompilerParams(dimension_semantics=sem, vmem_limit_bytes=VMEM_LIMIT)


MM_TILE = 1408


def _mm(a, b, mode, out_dtype, name):
    if mode == "nn":
        (M, K), (K2, N) = a.shape, b.shape
        dims = NN
    elif mode == "nt":
        (M, K), (N, K2) = a.shape, b.shape
        dims = NT
    else:
        (K, M), (K2, N) = a.shape, b.shape
        dims = TN
    assert K == K2, (a.shape, b.shape, mode)
    tm, tn, tk = _tile(M, MM_TILE), _tile(N, MM_TILE), _tile(K, MM_TILE)
    nk = K // tk

    def body_single(a_ref, b_ref, o_ref):
        o_ref[...] = _dot(a_ref[...], b_ref[...], dims).astype(o_ref.dtype)

    def body_acc(a_ref, b_ref, o_ref, acc_ref):
        k = pl.program_id(2)
        p = _dot(a_ref[...], b_ref[...], dims)

        @pl.when(k == 0)
        def _():
            acc_ref[...] = p

        @pl.when(k > 0)
        def _():
            acc_ref[...] += p

        @pl.when(k == nk - 1)
        def _():
            o_ref[...] = acc_ref[...].astype(o_ref.dtype)

    if mode == "nn":
        a_spec = pl.BlockSpec((tm, tk), lambda i, j, k: (i, k))
        b_spec = pl.BlockSpec((tk, tn), lambda i, j, k: (k, j))
    elif mode == "nt":
        a_spec = pl.BlockSpec((tm, tk), lambda i, j, k: (i, k))
        b_spec = pl.BlockSpec((tn, tk), lambda i, j, k: (j, k))
    else:
        a_spec = pl.BlockSpec((tk, tm), lambda i, j, k: (k, i))
        b_spec = pl.BlockSpec((tk, tn), lambda i, j, k: (k, j))
    return pl.pallas_call(
        body_single if nk == 1 else body_acc, grid=(M // tm, N // tn, nk), in_specs=[a_spec, b_spec],
        out_specs=pl.BlockSpec((tm, tn), lambda i, j, k: (i, j)),
        out_shape=jax.ShapeDtypeStruct((M, N), out_dtype),
        scratch_shapes=[] if nk == 1 else [pltpu.VMEM((tm, tn), F32)], name=name,
        compiler_params=_params(("parallel", "parallel", "arbitrary")),
    )(a, b)


def _rowwise(fn, name, rows, bcasts, out_rows, out_reds=(), tile=256):
    T = rows[0].shape[0]
    tile = min(tile, T)
    nr, nb, no = len(rows), len(bcasts), len(out_rows)

    def body(*refs):
        rv = [r[...] for r in refs[:nr]]
        bv = [r[...] for r in refs[nr:nr + nb]]
        outs, reds = fn(rv, bv)
        for r, o in zip(refs[nr + nb:nr + nb + no], outs):
            r[...] = o.astype(r.dtype)
        red_refs = refs[nr + nb + no:]
        if red_refs:
            @pl.when(pl.program_id(0) == 0)
            def _():
                for r in red_refs:
                    r[...] = jnp.zeros(r.shape, F32)

            for r, v in zip(red_refs, reds):
                r[...] += v

    in_specs = [pl.BlockSpec((tile, a.shape[1]), lambda i: (i, 0)) for a in rows]
    in_specs += [pl.BlockSpec(b.shape, lambda i: (0, 0)) for b in bcasts]
    out_specs = [pl.BlockSpec((tile, c), lambda i: (i, 0)) for c, _ in out_rows]
    out_specs += [pl.BlockSpec(s, lambda i: (0, 0)) for s in out_reds]
    out_shape = [jax.ShapeDtypeStruct((T, c), dt) for c, dt in out_rows]
    out_shape += [jax.ShapeDtypeStruct(s, F32) for s in out_reds]
    return pl.pallas_call(
        body, grid=(T // tile,), in_specs=in_specs, out_specs=out_specs, out_shape=out_shape, name=name,
        compiler_params=_params(("arbitrary",)),
    )(*rows, *bcasts)


def _norm_mod(x, g, sc, sh, name):
    def fn(rv, bv):
        (xv,), (gv, scv, shv) = rv, bv
        r = lax.rsqrt(jnp.mean(xv * xv, axis=1, keepdims=True) + EPS)
        return [(xv * r * gv) * (1.0 + scv) + shv], []
    return _rowwise(fn, name, [x], [g, sc, sh], [(D, BF16)])[0]


def _norm_mod_bwd(dh, x, dres, g, sc, name):
    def fn(rv, bv):
        (dhv, xv, drv), (gv, scv) = rv, bv
        r = lax.rsqrt(jnp.mean(xv * xv, axis=1, keepdims=True) + EPS)
        xhat = xv * r
        dxhat = dhv * (gv * (1.0 + scv))
        dx = r * (dxhat - xhat * jnp.mean(dxhat * xhat, axis=1, keepdims=True)) + drv
        return [dx], [_colsum(dhv * xhat), _colsum(dhv)]
    return _rowwise(fn, name, [dh, x, dres], [g, sc], [(D, F32)], [(1, D), (1, D)])


def _gate_res(x, y, gt, name):
    def fn(rv, bv):
        return [rv[0] + bv[0] * rv[1]], []
    return _rowwise(fn, name, [x, y], [gt], [(D, F32)])[0]


def _gate_res_norm(x, y, gt, g, sc, sh, name):
    def fn(rv, bv):
        (xv, yv), (gtv, gv, scv, shv) = rv, bv
        xn = xv + gtv * yv
        r = lax.rsqrt(jnp.mean(xn * xn, axis=1, keepdims=True) + EPS)
        return [xn, (xn * r * gv) * (1.0 + scv) + shv], []
    return _rowwise(fn, name, [x, y], [gt, g, sc, sh], [(D, F32), (D, BF16)])


def _gate_res_bwd(dxn, y, gt, name):
    def fn(rv, bv):
        return [rv[0] * bv[0]], [_colsum(rv[0] * rv[1])]
    return _rowwise(fn, name, [dxn, y], [gt], [(D, BF16)], [(1, D)])


def _swiglu_act(u, name):
    def fn(rv, bv):
        uv = rv[0].astype(F32)
        return [_silu(uv[:, :DFF]) * uv[:, DFF:]], []
    return _rowwise(fn, name, [u], [], [(DFF, BF16)])[0]


def _swiglu_act_bwd(da, u, name):
    def fn(rv, bv):
        dav, uv = rv[0].astype(F32), rv[1].astype(F32)
        gate, up = uv[:, :DFF], uv[:, DFF:]
        return [jnp.concatenate([dav * up * _dsilu(gate), dav * _silu(gate)], axis=1)], []
    return _rowwise(fn, name, [da, u], [], [(2 * DFF, BF16)])[0]


def _loss_head(y, target, name):
    def fn(rv, bv):
        err = rv[0] - rv[1]
        return [err * (1.0 / D)], [_colsum(_rowsum(err * err))]
    return _rowwise(fn, name, [y, target], [], [(D, F32)], [(1, 1)])


def _shift_rows(x, s):
    if s == 0:
        return x
    T = x.shape[0]
    r = pltpu.roll(x, s % T, axis=0)
    t = _iota(x.shape, 0)
    keep = (t >= s) if s > 0 else (t < T + s)
    return jnp.where(keep, r, 0.0)


def _dn_prep(pq, pk, pv, pab, cq_ref, ck_ref, cv_ref, alog, dtb, h):
    lane = _iota(pab.shape, 1)
    a_col = _rowsum(jnp.where(lane == h, pab, 0.0))
    b_col = _rowsum(jnp.where(lane == DN_H + h, pab, 0.0))
    lane1 = _iota(alog.shape, 1)
    alog_h = _rowsum(jnp.where(lane1 == h, alog, 0.0))
    dtb_h = _rowsum(jnp.where(lane1 == h, dtb, 0.0))
    pre = a_col + dtb_h
    neg_ea = -jnp.exp(alog_h)
    g = neg_ea * _softplus(pre)
    beta = _sigmoid(b_col)

    def conv(x, w_ref):
        acc = x * w_ref[DN_CONV - 1:DN_CONV, :]
        for i in range(DN_CONV - 1):
            acc = acc + _shift_rows(x, DN_CONV - 1 - i) * w_ref[i:i + 1, :]
        return acc

    xq, xk, xv = conv(pq, cq_ref), conv(pk, ck_ref), conv(pv, cv_ref)
    sq, sk, v = _silu(xq), _silu(xk), _silu(xv)
    rq = lax.rsqrt(_rowsum(sq * sq) + EPS)
    rk = lax.rsqrt(_rowsum(sk * sk) + EPS)
    return dict(g=g, beta=beta, pre=pre, neg_ea=neg_ea, xq=xq, xk=xk, xv=xv, rq=rq, rk=rk,
                qn=sq * rq, kn=sk * rk, v=v)


def _dn_masks():
    C = DN_C
    r, c = _iota((C, C), 0), _iota((C, C), 1)
    incl = r >= c
    strict = r > c
    blk16 = jnp.right_shift(r, 4) == jnp.right_shift(c, 4)
    blk32 = jnp.right_shift(r, 5) == jnp.right_shift(c, 5)
    return dict(incl=incl, strict=strict, upper=r <= c, blk16=blk16, blk32=blk32,
                tri=incl.astype(BF16), triT=(r <= c).astype(BF16), ones=jnp.ones((C, C), BF16),
                eye=(r == c).astype(F32), last=_iota((C, 1), 0) == C - 1)


def _tri_inverse(A, mk):
    P = -jnp.where(mk["blk16"], A, 0.0)
    X = mk["eye"] + P
    for _ in range(3):
        P = _dot(P, P)
        X = X + _dot(X, P)
    off1 = jnp.where(mk["blk32"] & (~mk["blk16"]), A, 0.0)
    X = X - _dot(_dot(X, off1), X)
    off2 = jnp.where(mk["blk32"], 0.0, A)
    X = X - _dot(_dot(X, off2), X)
    return X


def _dn_local(qc, kc, vc, gc, bc, mk):
    C = DN_C
    gm = jnp.broadcast_to(gc, gc.shape[:-1] + (C,))
    Gc = _dot2l(mk["tri"], gm)
    Gr = _dot2l(mk["ones"], jnp.where(mk["upper"], gm, 0.0))
    Dm = jnp.where(mk["incl"], jnp.exp(jnp.where(mk["incl"], Gc - Gr, 0.0)), 0.0)
    Gcol = jnp.max(Gc, axis=-1, keepdims=True)
    Gl = _colsum(jnp.where(mk["last"], Gcol, 0.0))
    eG = jnp.exp(Gcol)
    eT = jnp.exp(Gl - Gcol)
    gl = jnp.exp(Gl)
    kb = kc * bc
    vb = vc * bc
    KK = _dot(kb, kc, NT)
    Tinv = _tri_inverse(jnp.where(mk["strict"], KK * Dm, 0.0), mk)
    KBE = kb * eG
    QK = _dot(qc, kc, NT)
    return dict(Dm=Dm, eG=eG, eT=eT, gl=gl, kb=kb, vb=vb, KK=KK, Tinv=Tinv, KBE=KBE, U=_dot(Tinv, vb),
                W=_dot(Tinv, KBE), QK=QK, attn=QK * Dm, QD=qc * eG, KT=kc * eT)


def _dn_recur(f, S):
    vnew = f["U"] - _dot(f["W"], S)
    o = _dot(f["QD"], S) + _dot(f["attn"], vnew)
    return o, S * f["gl"] + _dot(f["KT"], vnew, TN), vnew


def _dn_bwd_chain(f, do, dS):
    dvnew = _dot(f["KT"], dS) + _dot(f["attn"], do, TN)
    return dvnew, dS * f["gl"] + _dot(f["QD"], do, TN) - _dot(f["W"], dvnew, TN)


def _dn_bwd_rest(qc, kc, vc, bc, f, S, vnew, do, dS, dvnew, mk):
    C = DN_C
    Dm, eG, eT, gl, kb, vb, KK, Tinv, KBE, QK = (
        f[n] for n in ("Dm", "eG", "eT", "gl", "kb", "vb", "KK", "Tinv", "KBE", "QK"))
    dKT = _dot(vnew, dS, NT)
    dgl = _colsum(_rowsum(dS * S))
    dQD = _dot(do, S, NT)
    dattn = _dot(do, vnew, NT)
    dU = dvnew
    dW = -_dot(dvnew, S, NT)
    dQK = dattn * Dm
    dD = dattn * QK
    dq = _dot(dQK, kc)
    dk = _dot(dQK, qc, TN)
    dTinv = _dot(dU, vb, NT) + _dot(dW, KBE, NT)
    dvb = _dot(Tinv, dU, TN)
    dKBE = _dot(Tinv, dW, TN)
    dA = -_dot(_dot(Tinv, dTinv, TN), Tinv, NT)
    dA = jnp.where(mk["strict"], dA, 0.0)
    dKK = dA * Dm
    dD = dD + dA * KK
    dkb = _dot(dKK, kc) + dKBE * eG
    dk = dk + _dot(dKK, kb, TN)
    deG = _rowsum(dKBE * kb)
    dk = dk + dkb * bc
    dbeta = _rowsum(dkb * kc) + _rowsum(dvb * vc)
    dv = dvb * bc
    dq = dq + dQD * eG
    deG = deG + _rowsum(dQD * qc)
    dk = dk + dKT * eT
    deT = _rowsum(dKT * kc)
    dGcol = deG * eG - deT * eT
    dGl = _colsum(deT * eT) + dgl * gl
    Y = dD * Dm
    ycol = jnp.max(_dot2r(Y, mk["ones"], TN), axis=-1, keepdims=True)
    dGcol = dGcol + _rowsum(Y) - ycol
    dGcol = dGcol + jnp.where(mk["last"], dGl, 0.0)
    dg = jnp.max(_dot2l(mk["triT"], jnp.broadcast_to(dGcol, dGcol.shape[:-1] + (C,))), axis=-1, keepdims=True)
    return dq, dk, dv, dg, dbeta


def _dn_core_fwd(proj, conv_w, alog, dtb, og, name, comm=None):
    T = proj.shape[0]
    N = T // DN_C

    def body(pq_ref, pk_ref, pv_ref, pz_ref, pab_ref, cq_ref, ck_ref, cv_ref, alog_ref, dtb_ref, og_ref,
             out_ref, o_ref, st_ref, q_s, k_s, v_s, g_s, b_s, S_s):
        h = pl.program_id(0)
        p = _dn_prep(pq_ref[...], pk_ref[...], pv_ref[...], pab_ref[...], cq_ref, ck_ref, cv_ref,
                     alog_ref[...], dtb_ref[...], h)
        q_s[...] = p["qn"] * (DN_D ** -0.5)
        k_s[...] = p["kn"]
        v_s[...] = p["v"]
        g_s[...] = p["g"]
        b_s[...] = p["beta"]
        S_s[...] = jnp.zeros(S_s.shape, F32)
        mk = _dn_masks()

        nu = min(DN_UNROLL_FWD, N)

        def step(it, carry):
            rows = pl.ds(pl.multiple_of(it * (nu * DN_C), nu * DN_C), nu * DN_C)
            loc = _dn_local(*(r[rows, :].reshape(nu, DN_C, r.shape[1]) for r in (q_s, k_s, v_s, g_s, b_s)), mk)
            S = S_s[...]
            outs = []
            for u in range(nu):
                st_ref[0, it * nu + u] = S
                o, S, _ = _dn_recur({n: v[u] for n, v in loc.items()}, S)
                outs.append(o)
            o_ref[rows, :] = jnp.concatenate(outs, axis=0)
            S_s[...] = S
            return carry

        lax.fori_loop(0, N // nu, step, 0)
        o = o_ref[...]
        ro = lax.rsqrt(jnp.mean(o * o, axis=1, keepdims=True) + EPS)
        out_ref[...] = ((o * ro * og_ref[...]) * _silu(pz_ref[...])).astype(out_ref.dtype)

    col = lambda k: pl.BlockSpec((T, DN_D), lambda h: (0, k * DN_H + h))
    cw = lambda k: pl.BlockSpec((DN_CONV, DN_D), lambda h: (0, k * DN_H + h))
    small = pl.BlockSpec((1, LANES), lambda h: (0, 0))
    return _grid_call(
        body, comm, grid=(DN_H,),
        in_specs=[col(0), col(1), col(2), col(3), pl.BlockSpec((T, LANES), lambda h: (0, 4 * DN_H)),
                  cw(0), cw(1), cw(2), small, small, small],
        out_specs=[pl.BlockSpec((T, DN_D), lambda h: (0, h)), pl.BlockSpec((T, DN_D), lambda h: (0, h)),
                   pl.BlockSpec((1, N, DN_D, DN_D), lambda h: (h, 0, 0, 0))],
        out_shape=[jax.ShapeDtypeStruct((T, D), BF16), jax.ShapeDtypeStruct((T, D), F32),
                   jax.ShapeDtypeStruct((DN_H, N, DN_D, DN_D), F32)],
        scratch_shapes=[pltpu.VMEM((T, DN_D), F32)] * 3 + [pltpu.VMEM((T, 1), F32)] * 2 + [pltpu.VMEM((DN_D, DN_D), F32)],
        name=name, args=(proj, proj, proj, proj, proj, conv_w, conv_w, conv_w, alog, dtb, og))


def _dn_core_bwd(proj, conv_w, alog, dtb, og, o, states, dout, name, comm=None):
    T = proj.shape[0]
    N = T // DN_C

    def body(pq_ref, pk_ref, pv_ref, pz_ref, pab_ref, cq_ref, ck_ref, cv_ref, alog_ref, dtb_ref, og_ref,
             o_ref, st_ref, dout_ref,
             dpq_ref, dpk_ref, dpv_ref, dpz_ref, dpab_ref, dcq_ref, dck_ref, dcv_ref, dalog_ref, ddtb_ref, dog_ref,
             q_s, k_s, v_s, g_s, b_s, do_s, dS_s):
        h = pl.program_id(0)
        scale = DN_D ** -0.5

        def prep():
            return _dn_prep(pq_ref[...], pk_ref[...], pv_ref[...], pab_ref[...], cq_ref, ck_ref, cv_ref,
                            alog_ref[...], dtb_ref[...], h)

        p = prep()
        q_s[...] = p["qn"] * scale
        k_s[...] = p["kn"]
        v_s[...] = p["v"]
        g_s[...] = p["g"]
        b_s[...] = p["beta"]
        del p

        o = o_ref[...]
        z = pz_ref[...]
        dout = dout_ref[...]
        ogv = og_ref[...]
        ro = lax.rsqrt(jnp.mean(o * o, axis=1, keepdims=True) + EPS)
        on = o * ro
        dy = dout * _silu(z)
        dpz_ref[...] = (dout * (on * ogv) * _dsilu(z)).astype(dpz_ref.dtype)
        dyg = dy * ogv
        do_s[...] = ro * (dyg - on * jnp.mean(dyg * on, axis=1, keepdims=True))
        dog_h = _colsum(dy * on)

        dS_s[...] = jnp.zeros(dS_s.shape, F32)
        mk = _dn_masks()

        nu = min(DN_UNROLL_BWD, N)

        def step(it, carry):
            c0 = (N // nu - 1 - it) * nu
            rows = pl.ds(pl.multiple_of(c0 * DN_C, nu * DN_C), nu * DN_C)
            q, k, v, g, b, do = (r[rows, :].reshape(nu, DN_C, r.shape[1]) for r in (q_s, k_s, v_s, g_s, b_s, do_s))
            loc = _dn_local(q, k, v, g, b, mk)
            Ss = st_ref[0, pl.ds(c0, nu)]
            vnew = loc["U"] - _dot(loc["W"], Ss)
            dS = dS_s[...]
            dS_in, dvnew = [None] * nu, [None] * nu
            for u in reversed(range(nu)):
                dS_in[u] = dS
                dvnew[u], dS = _dn_bwd_chain({n: x[u] for n, x in loc.items()}, do[u], dS)
            dS_s[...] = dS
            grads = _dn_bwd_rest(q, k, v, b, loc, Ss, vnew, do, jnp.stack(dS_in), jnp.stack(dvnew), mk)
            for r, d in zip((q_s, k_s, v_s, g_s, b_s), grads):
                r[rows, :] = d.reshape(nu * DN_C, r.shape[1])
            return carry

        lax.fori_loop(0, N // nu, step, 0)

        p = prep()
        pq, pk, pv = pq_ref[...], pk_ref[...], pv_ref[...]
        dqn = q_s[...] * scale
        dkn = k_s[...]
        qn, kn = p["qn"], p["kn"]
        dsq = p["rq"] * (dqn - qn * _rowsum(dqn * qn))
        dsk = p["rk"] * (dkn - kn * _rowsum(dkn * kn))
        dxq = dsq * _dsilu(p["xq"])
        dxk = dsk * _dsilu(p["xk"])
        dxv = v_s[...] * _dsilu(p["xv"])

        def conv_bwd(dx, x, w_ref, dp_ref, dc_ref):
            acc = dx * w_ref[DN_CONV - 1:DN_CONV, :]
            dc_ref[DN_CONV - 1:DN_CONV, :] = _colsum(dx * x)
            for i in range(DN_CONV - 1):
                s = DN_CONV - 1 - i
                acc = acc + _shift_rows(dx, -s) * w_ref[i:i + 1, :]
                dc_ref[i:i + 1, :] = _colsum(dx * _shift_rows(x, s))
            dp_ref[...] = acc.astype(dp_ref.dtype)

        conv_bwd(dxq, pq, cq_ref, dpq_ref, dcq_ref)
        conv_bwd(dxk, pk, ck_ref, dpk_ref, dck_ref)
        conv_bwd(dxv, pv, cv_ref, dpv_ref, dcv_ref)

        dg = g_s[...]
        beta = p["beta"]
        da_raw = dg * p["neg_ea"] * _sigmoid(p["pre"])
        db_raw = b_s[...] * beta * (1.0 - beta)
        lane = _iota((T, LANES), 1)
        contrib = jnp.where(lane == h, da_raw, 0.0) + jnp.where(lane == DN_H + h, db_raw, 0.0)
        lane1 = _iota((1, LANES), 1)
        dalog_h = jnp.where(lane1 == h, _colsum(dg * p["g"]), 0.0)
        ddtb_h = jnp.where(lane1 == h, _colsum(da_raw), 0.0)

        @pl.when(h == 0)
        def _():
            dpab_ref[...] = contrib.astype(dpab_ref.dtype)
            dalog_ref[...] = dalog_h
            ddtb_ref[...] = ddtb_h
            dog_ref[...] = dog_h

        @pl.when(h > 0)
        def _():
            dpab_ref[...] += contrib.astype(dpab_ref.dtype)
            dalog_ref[...] += dalog_h
            ddtb_ref[...] += ddtb_h
            dog_ref[...] += dog_h

    col = lambda k: pl.BlockSpec((T, DN_D), lambda h: (0, k * DN_H + h))
    cw = lambda k: pl.BlockSpec((DN_CONV, DN_D), lambda h: (0, k * DN_H + h))
    small = pl.BlockSpec((1, LANES), lambda h: (0, 0))
    ab = pl.BlockSpec((T, LANES), lambda h: (0, 4 * DN_H))
    hcol = pl.BlockSpec((T, DN_D), lambda h: (0, h))
    outs = _grid_call(
        body, comm, grid=(DN_H,),
        in_specs=[col(0), col(1), col(2), col(3), ab, cw(0), cw(1), cw(2), small, small, small,
                  hcol, pl.BlockSpec((1, N, DN_D, DN_D), lambda h: (h, 0, 0, 0)), hcol],
        out_specs=[hcol, hcol, hcol, hcol, pl.BlockSpec((T, LANES), lambda h: (0, 0)),
                   pl.BlockSpec((DN_CONV, DN_D), lambda h: (0, h)), pl.BlockSpec((DN_CONV, DN_D), lambda h: (0, h)),
                   pl.BlockSpec((DN_CONV, DN_D), lambda h: (0, h)), small, small, small],
        out_shape=[jax.ShapeDtypeStruct((T, D), BF16)] * 4 + [jax.ShapeDtypeStruct((T, LANES), BF16)]
                  + [jax.ShapeDtypeStruct((DN_CONV, D), F32)] * 3 + [jax.ShapeDtypeStruct((1, LANES), F32)] * 3,
        scratch_shapes=[pltpu.VMEM((T, DN_D), F32)] * 3 + [pltpu.VMEM((T, 1), F32)] * 2
                       + [pltpu.VMEM((T, DN_D), F32), pltpu.VMEM((DN_D, DN_D), F32)],
        name=name, args=(proj, proj, proj, proj, proj, conv_w, conv_w, conv_w, alog, dtb, og, o, states, dout))
    dpq, dpk, dpv, dpz, dpab, dcq, dck, dcv, dalog, ddtb, dog = outs[:11]
    dproj = jnp.concatenate([dpq, dpk, dpv, dpz, dpab], axis=1)
    dconv = jnp.concatenate([dcq, dck, dcv], axis=1)
    return dproj, dconv, dalog, ddtb, dog, (outs[11] if comm else None)


def _sb_head_of(shape):
    return jnp.right_shift(_iota(shape, 1), 6)


def _sb_head_sums(x, head):
    out = jnp.zeros_like(x)
    for hh in range(SB_HPB):
        out = jnp.where(head == hh, _rowsum(jnp.where(head == hh, x, 0.0)), out)
    return out


def _sb_head_norm(x, head):
    r = lax.rsqrt(_sb_head_sums(x * x, head) * (1.0 / SB_D) + EPS)
    return x * r, r


def _sb_fill(q_ref, k_ref, v_ref, qg_ref, kg_ref, qs_s, kn_s, v_s):
    head = _sb_head_of(q_ref.shape)
    qs_s[...] = (_sb_head_norm(q_ref[...], head)[0] * qg_ref[...] * (SB_D ** -0.5)).astype(BF16)
    kn_s[...] = (_sb_head_norm(k_ref[...], head)[0] * kg_ref[...]).astype(BF16)
    v_s[...] = v_ref[...].astype(BF16)


def _sb_head_masks(dtype):
    return jnp.stack([(_sb_head_of((1, SB_W)) == hh).astype(dtype) for hh in range(SB_HPB)])


def _sb_core_fwd(qkv, qg, kg, name, comm=None):
    T = qkv.shape[0]
    B = min(SB_TILE, T)
    NB = T // B
    NP = SB_H // SB_HPB

    def body(q_ref, k_ref, v_ref, qg_ref, kg_ref, o_ref, tot_ref, qs_s, kn_s, v_s):
        _sb_fill(q_ref, k_ref, v_ref, qg_ref, kg_ref, qs_s, kn_s, v_s)
        r, c = _iota((B, B), 0), _iota((B, B), 1)
        causal = c < r
        m_after = (r > c).astype(BF16)
        head_b = _sb_head_of((B, SB_W))
        hm = _sb_head_masks(BF16)

        def tile(qb, kj, vj, R, acc, diag):
            z = _dot(qb, kj, NT)
            sp = _softplus(z)
            ls = z - sp
            lm = jnp.where(causal, -sp, 0.0) if diag else -sp
            cs = _dot2r(lm, m_after) + R
            a = jnp.exp(ls + cs)
            if diag:
                a = jnp.where(causal, a, 0.0)
            return R + _rowsum(lm), acc + jnp.sum(_dot(a, vj), axis=0)

        def qblock(i, carry):
            rows_i = pl.ds(pl.multiple_of(i * B, B), B)
            qb = qs_s[rows_i, :][None] * hm

            def step(rows_j, st, diag):
                return tile(qb, kn_s[rows_j, :], v_s[rows_j, :][None] * hm, st[0], st[1], diag)

            def keys(j):
                return pl.ds(pl.multiple_of(j * B, B), B)

            st = step(rows_i, (jnp.zeros((SB_HPB, B, 1), F32), jnp.zeros((B, SB_W), F32)), True)
            st = lax.fori_loop(
                0, i // 2, lambda s, st: step(keys(i - 2 - 2 * s), step(keys(i - 1 - 2 * s), st, False), False), st)
            st = lax.cond(i % 2 == 1, lambda st: step(keys(0), st, False), lambda st: st, st)
            o_ref[rows_i, :] = st[1].astype(o_ref.dtype)
            tot = jnp.zeros((B, SB_W), F32)
            for hh in range(SB_HPB):
                tot = jnp.where(head_b == hh, st[0][hh], tot)
            tot_ref[rows_i, :] = tot
            return carry

        lax.fori_loop(0, NB, qblock, 0)

    blk = lambda k: pl.BlockSpec((T, SB_W), lambda p: (0, k * NP + p))
    small = pl.BlockSpec((1, SB_W), lambda p: (0, 0))
    return _grid_call(
        body, comm, grid=(NP,), in_specs=[blk(0), blk(1), blk(2), small, small],
        out_specs=[pl.BlockSpec((T, SB_W), lambda p: (0, p))] * 2,
        out_shape=[jax.ShapeDtypeStruct((T, D), BF16), jax.ShapeDtypeStruct((T, D), F32)],
        scratch_shapes=[pltpu.VMEM((T, SB_W), BF16)] * 3, name=name, args=(qkv, qkv, qkv, qg, kg))


def _sb_core_bwd(qkv, qg, kg, tot, dout, name, comm=None):
    T = qkv.shape[0]
    B = min(SB_TILE, T)
    NB = T // B
    NP = SB_H // SB_HPB

    def body(q_ref, k_ref, v_ref, qg_ref, kg_ref, tot_ref, do_ref, dq_ref, dk_ref, dv_ref, dqg_ref, dkg_ref,
             qs_s, kn_s, v_s, dqn_s, dkn_s, dvv_s):
        p = pl.program_id(0)
        _sb_fill(q_ref, k_ref, v_ref, qg_ref, kg_ref, qs_s, kn_s, v_s)
        dkn_s[...] = jnp.zeros(dkn_s.shape, F32)
        dvv_s[...] = jnp.zeros(dvv_s.shape, F32)
        r, c = _iota((B, B), 0), _iota((B, B), 1)
        causal = c < r
        m_upto = (r <= c).astype(BF16)
        m_before = (r < c).astype(BF16)
        lane_b = _iota((B, SB_W), 1)
        hm, hmf = _sb_head_masks(BF16), _sb_head_masks(F32)

        def tile(qb, dob, tot_h, kj, vj, PL, P, dq, diag):
            z = _dot(qb, kj, NT)
            sp = _softplus(z)
            ls = z - sp
            lm = jnp.where(causal, -sp, 0.0) if diag else -sp
            cs = tot_h - PL - _dot2r(lm, m_upto)
            a = jnp.exp(ls + cs)
            if diag:
                a = jnp.where(causal, a, 0.0)
            e = _dot(dob, vj, NT) * a
            E = _dot(e, m_before) + P
            sig = jnp.exp(ls)
            dz = e * (1.0 - sig) - E * sig
            if diag:
                dz = jnp.where(causal, dz, 0.0)
            dq = dq + jnp.sum(_dot(dz, kj) * hmf, axis=0)
            return (PL + _rowsum(lm), P + _rowsum(e), dq), jnp.sum(_dot(dz, qb, TN), axis=0), jnp.sum(_dot(a, dob, TN), axis=0)

        def qblock(i, carry):
            rows_i = pl.ds(pl.multiple_of(i * B, B), B)
            totb = tot_ref[rows_i, :]
            tot_h = jnp.stack([_rowsum(jnp.where(lane_b == hh * SB_D, totb, 0.0)) for hh in range(SB_HPB)])
            qb = qs_s[rows_i, :][None] * hm
            dob = do_ref[rows_i, :][None] * hm

            def step(rows_j, st, diag):
                st, dk, dv = tile(qb, dob, tot_h, kn_s[rows_j, :], v_s[rows_j, :][None] * hm, st[0], st[1], st[2], diag)
                dkn_s[rows_j, :] += dk
                dvv_s[rows_j, :] += dv
                return st

            def keys(j):
                return pl.ds(pl.multiple_of(j * B, B), B)

            zero = (jnp.zeros((SB_HPB, B, 1), F32), jnp.zeros((SB_HPB, B, 1), F32), jnp.zeros((B, SB_W), F32))
            st = lax.fori_loop(0, i // 2, lambda p, st: step(keys(2 * p + 1), step(keys(2 * p), st, False), False), zero)
            st = lax.cond(i % 2 == 1, lambda st: step(keys(i - 1), st, False), lambda st: st, st)
            st = step(rows_i, st, True)
            dqn_s[rows_i, :] = st[2] * (SB_D ** -0.5)
            return carry

        lax.fori_loop(0, NB, qblock, 0)

        head = _sb_head_of((T, SB_W))

        def norm_bwd(dn, x_ref, g):
            xh, rr = _sb_head_norm(x_ref[...], head)
            t = dn * g
            return rr * (t - xh * (_sb_head_sums(t * xh, head) * (1.0 / SB_D))), _colsum(dn * xh)

        dq, dqg = norm_bwd(dqn_s[...], q_ref, qg_ref[...])
        dq_ref[...] = dq.astype(dq_ref.dtype)
        dk, dkg = norm_bwd(dkn_s[...], k_ref, kg_ref[...])
        dk_ref[...] = dk.astype(dk_ref.dtype)
        dv_ref[...] = dvv_s[...].astype(dv_ref.dtype)

        @pl.when(p == 0)
        def _():
            dqg_ref[...] = dqg
            dkg_ref[...] = dkg

        @pl.when(p > 0)
        def _():
            dqg_ref[...] += dqg
            dkg_ref[...] += dkg

    blk = lambda k: pl.BlockSpec((T, SB_W), lambda p: (0, k * NP + p))
    small = pl.BlockSpec((1, SB_W), lambda p: (0, 0))
    own = pl.BlockSpec((T, SB_W), lambda p: (0, p))
    outs = _grid_call(
        body, comm, grid=(NP,), in_specs=[blk(0), blk(1), blk(2), small, small, own, own],
        out_specs=[own, own, own, small, small],
        out_shape=[jax.ShapeDtypeStruct((T, D), BF16)] * 3 + [jax.ShapeDtypeStruct((1, SB_W), F32)] * 2,
        scratch_shapes=[pltpu.VMEM((T, SB_W), BF16)] * 3 + [pltpu.VMEM((T, SB_W), F32)] * 3,
        name=name, args=(qkv, qkv, qkv, qg, kg, tot, dout))
    dq, dk, dv, dqg, dkg = outs[:5]
    return jnp.concatenate([dq, dk, dv], axis=1), dqg, dkg, (outs[5] if comm else None)


def _ada_fwd(c16, ada_w, ada_b_cols, name):
    L, _, cols = ada_w.shape

    def body(c_ref, w_ref, b_ref, o_ref):
        o_ref[0] = _dot(_silu(c_ref[...]), w_ref[0]) + b_ref[0]

    return pl.pallas_call(
        body, grid=(L,),
        in_specs=[pl.BlockSpec((16, D), lambda i: (0, 0)), pl.BlockSpec((1, D, cols), lambda i: (i, 0, 0)),
                  pl.BlockSpec((1, 1, cols), lambda i: (i, 0, 0))],
        out_specs=pl.BlockSpec((1, 16, cols), lambda i: (i, 0, 0)),
        out_shape=jax.ShapeDtypeStruct((L, 16, cols), F32), name=name, compiler_params=_params(("arbitrary",)),
    )(c16, ada_w, ada_b_cols)


def _ada_bwd(c16, dmod16, name):
    L, _, cols = dmod16.shape

    def body(c_ref, d_ref, o_ref):
        o_ref[0] = _dot(_silu(c_ref[...]), d_ref[0], TN)

    return pl.pallas_call(
        body, grid=(L,),
        in_specs=[pl.BlockSpec((16, D), lambda i: (0, 0)), pl.BlockSpec((1, 16, cols), lambda i: (i, 0, 0))],
        out_specs=pl.BlockSpec((1, D, cols), lambda i: (i, 0, 0)),
        out_shape=jax.ShapeDtypeStruct((L, D, cols), F32), name=name, compiler_params=_params(("arbitrary",)),
    )(c16, dmod16)


def _sum_sources(x, name):
    n, R, C = x.shape
    tile = _sum_tile(R) or R

    def body(x_ref, o_ref):
        acc = x_ref[0].astype(F32)
        for k in range(1, n):
            acc = acc + x_ref[k].astype(F32)
        o_ref[...] = acc

    return pl.pallas_call(
        body, grid=(R // tile,), in_specs=[pl.BlockSpec((n, tile, C), lambda i: (0, i, 0))],
        out_specs=pl.BlockSpec((tile, C), lambda i: (i, 0)), out_shape=jax.ShapeDtypeStruct((R, C), F32),
        name=name, compiler_params=_params(("parallel",)),
    )(x)


def _adamw(w, g, m, v, name):
    shape = w.shape
    C = shape[-1]
    R = w.size // C
    w2, g2, m2, v2 = (a.reshape(R, C) for a in (w, g, m, v))
    tile = _rtile(R)
    c1 = 1.0 / (1.0 - ADAM_B1 ** ADAM_STEP)
    c2 = 1.0 / (1.0 - ADAM_B2 ** ADAM_STEP)

    def body(w_ref, g_ref, m_ref, v_ref, d_ref, nm_ref, nv_ref):
        gv = g_ref[...]
        nm = ADAM_B1 * m_ref[...] + (1.0 - ADAM_B1) * gv
        nv = ADAM_B2 * v_ref[...] + (1.0 - ADAM_B2) * (gv * gv)
        d_ref[...] = -ADAM_LR * ((nm * c1) / (jnp.sqrt(nv * c2) + ADAM_EPS) + ADAM_WD * w_ref[...])
        nm_ref[...] = nm
        nv_ref[...] = nv

    spec = pl.BlockSpec((tile, C), lambda i: (i, 0))
    outs = pl.pallas_call(
        body, grid=(R // tile,), in_specs=[spec] * 4, out_specs=[spec] * 3,
        out_shape=[jax.ShapeDtypeStruct((R, C), F32)] * 3, name=name, compiler_params=_params(("parallel",)),
    )(w2, g2, m2, v2)
    return tuple(o.reshape(shape) for o in outs)


def _gather_steps(x_ref, out_ref, send_sems, recv_sems, local_sem):
    mx, my, mc = lax.axis_index("x"), lax.axis_index("y"), lax.axis_index("c")
    me, sibling = (mx, my, mc), (mx, my, 1 - mc)
    chips = [(1 - mx, my), (mx, 1 - my), (1 - mx, 1 - my)]

    def slot(px, py, pc):
        return out_ref.at[4 * px + 2 * py + pc]

    def copy(k, block, to, src=None):
        return pltpu.make_async_remote_copy(
            src_ref=slot(*block) if src is None else src, dst_ref=slot(*block),
            send_sem=send_sems.at[k], recv_sem=recv_sems.at[k], device_id=to, device_id_type=MESH)

    mine = pltpu.make_async_copy(x_ref, slot(*me), local_sem)
    first = [copy(0, me, sibling, src=x_ref)] + [copy(1 + j, me, (*chip, mc), src=x_ref) for j, chip in enumerate(chips)]
    passed = [copy(4 + j, (*chip, mc), sibling) for j, chip in enumerate(chips)]

    def start():
        mine.start()
        for cp in first:
            cp.start()

    def forward():
        for j, chip in enumerate(chips):
            copy(1 + j, (*chip, mc), me).wait_recv()
            passed[j].start()

    def finish():
        copy(0, sibling, me).wait_recv()
        for j, chip in enumerate(chips):
            copy(4 + j, (*chip, 1 - mc), me).wait_recv()
        for cp in first + passed:
            cp.wait_send()
        mine.wait()

    return start, forward, finish


def _exchange_steps(x_ref, out_ref, send_sems, recv_sems, local_sem):
    mx, my, mc = lax.axis_index("x"), lax.axis_index("y"), lax.axis_index("c")
    me = 4 * mx + 2 * my + mc
    mine = pltpu.make_async_copy(x_ref.at[me], out_ref.at[me], local_sem)
    copies = []
    for k in range(1, NDEV):
        px, py, pc = mx ^ (k >> 2), my ^ ((k >> 1) & 1), mc ^ (k & 1)
        copies.append(pltpu.make_async_remote_copy(
            src_ref=x_ref.at[4 * px + 2 * py + pc], dst_ref=out_ref.at[me], send_sem=send_sems.at[k - 1],
            recv_sem=recv_sems.at[k - 1], device_id=(px, py, pc), device_id_type=MESH))

    def start():
        mine.start()
        for cp in copies:
            cp.start()

    def finish():
        for cp in copies:
            cp.wait_recv()
        for cp in copies:
            cp.wait_send()
        mine.wait()

    return start, finish


def _comm_sems():
    return [pltpu.SemaphoreType.DMA((7,)), pltpu.SemaphoreType.DMA((7,)), pltpu.SemaphoreType.DMA]


def _comm_out_shape(comm):
    kind, x = comm
    return jax.ShapeDtypeStruct(((NDEV,) + x.shape) if kind == "gather" else x.shape, x.dtype)


def _ride_along(comm, step, n_steps, refs, at_end):
    if comm[0] == "gather":
        start, forward, finish = _gather_steps(*refs)
        todo = [(n_steps - 1, finish)] if at_end else [(0, start), (n_steps - 1, forward)]
    else:
        start, finish = _exchange_steps(*refs)
        todo = [(n_steps - 1, finish)] if at_end else [(0, start)]
    for at, fn in todo:
        pl.when(step == at)(fn)


def _grid_call(body, comm, *, grid, in_specs, out_specs, out_shape, scratch_shapes, name, args):
    if comm is None:
        return pl.pallas_call(body, grid=grid, in_specs=in_specs, out_specs=out_specs, out_shape=out_shape,
                              scratch_shapes=scratch_shapes, name=name, compiler_params=_params(("arbitrary",)))(*args)
    n_in, n_out, n_steps = len(in_specs), len(out_specs), grid[0]

    def with_comm(*refs):
        ins, outs, scr = refs[:n_in], refs[n_in + 1:n_in + 1 + n_out], refs[n_in + 2 + n_out:-3]
        comm_refs = (refs[n_in], refs[n_in + 1 + n_out]) + refs[-3:]
        step = pl.program_id(0)
        _ride_along(comm, step, n_steps, comm_refs, False)
        body(*ins, *outs, *scr)
        _ride_along(comm, step, n_steps, comm_refs, True)

    hbm = pl.BlockSpec(memory_space=pl.ANY)
    return pl.pallas_call(
        with_comm, grid=grid, in_specs=list(in_specs) + [hbm], out_specs=list(out_specs) + [hbm],
        out_shape=list(out_shape) + [_comm_out_shape(comm)], scratch_shapes=list(scratch_shapes) + _comm_sems(),
        name=name, compiler_params=_params(("arbitrary",)))(*args, comm[1])


def _all_gather(x, name, in_vmem):
    def body(x_ref, out_ref, send_sems, recv_sems, local_sem):
        start, forward, finish = _gather_steps(x_ref, out_ref, send_sems, recv_sems, local_sem)
        start()
        forward()
        finish()

    space = pltpu.VMEM if in_vmem else pl.ANY
    return pl.pallas_call(
        body, out_shape=jax.ShapeDtypeStruct((NDEV,) + x.shape, x.dtype),
        in_specs=[pl.BlockSpec(memory_space=space)], out_specs=pl.BlockSpec(memory_space=space),
        scratch_shapes=_comm_sems(), name=name, compiler_params=pltpu.CompilerParams(vmem_limit_bytes=VMEM_LIMIT),
    )(x)


def _all_to_all(x, name):
    def body(x_ref, out_ref, send_sems, recv_sems, local_sem):
        start, finish = _exchange_steps(x_ref, out_ref, send_sems, recv_sems, local_sem)
        start()
        finish()

    return pl.pallas_call(
        body, out_shape=jax.ShapeDtypeStruct(x.shape, x.dtype),
        in_specs=[pl.BlockSpec(memory_space=pl.ANY)], out_specs=pl.BlockSpec(memory_space=pl.ANY),
        scratch_shapes=_comm_sems(), name=name, compiler_params=pltpu.CompilerParams(vmem_limit_bytes=VMEM_LIMIT),
    )(x)


def _stage_parts(st):
    parts = []
    if st >= 1:
        parts += [("ffn_w_in", st - 1, (D, 2 * DFF // NDEV), "colsT"), ("ffn_w_out", st - 1, (DFF // NDEV, D), "rows")]
    if st < DEPTH:
        j = st // 2
        if st % 2 == 0:
            parts += [("dn_w_in", j, (D, DN_COLS // NDEV), "colsT"), ("dn_w_out", j, (D // NDEV, D), "rows")]
        else:
            parts += [("sb_w_qkv", j, (D, 3 * D // NDEV), "colsT"), ("sb_w_out", j, (D // NDEV, D), "rows")]
    return parts


def _sum_tile(rows):
    for t in range(512, 191, -16):
        if rows % t == 0:
            return t
    return None


def _stage_layout(st):
    out, r = [], 0
    for n, j, s, kind in _stage_parts(st):
        k = s[0] * s[1] // D
        kp = -(-k // 16) * 16
        out.append((n, j, s, kind, k, kp, r))
        r += kp
    while _sum_tile(r) is None:
        r += 16
    return out, r


def _pack_stage(w, st):
    parts, rows = _stage_layout(st)
    bufs = [jnp.pad((w[n][j].T if kind == "colsT" else w[n][j]).astype(BF16).reshape(k, D), ((0, kp - k), (0, 0)))
            for n, j, _, kind, k, kp, _ in parts]
    used = parts[-1][6] + parts[-1][5]
    if rows > used:
        bufs.append(jnp.zeros((rows - used, D), BF16))
    return jnp.concatenate(bufs, axis=0)


def _full_stage(g, st):
    out = {}
    for n, _, s, kind, k, _, r in _stage_layout(st)[0]:
        out[n] = g[:, r:r + k].reshape((NDEV * s[0], s[1]) if kind == "rows" else (NDEV * s[1], s[0]))
    if "dn_w_in" in out:
        out["dn_w_in"] = jnp.pad(out["dn_w_in"], ((0, DN_COLS_PAD - DN_COLS), (0, 0)))
    return out


def _shards_stage(full, st):
    parts, rows = _stage_layout(st)
    bufs = []
    for n, _, s, kind, k, kp, _ in parts:
        bufs.append(jnp.pad(full[n].astype(BF16).reshape(NDEV, k, D), ((0, 0), (0, kp - k), (0, 0))))
    used = parts[-1][6] + parts[-1][5]
    if rows > used:
        bufs.append(jnp.zeros((NDEV, rows - used, D), BF16))
    return jnp.concatenate(bufs, axis=1)


def _unpack_stage(buf, st):
    return {(n, j): buf[r:r + k].reshape(s[1], s[0]).T if kind == "colsT" else buf[r:r + k].reshape(s)
            for n, j, s, kind, k, _, r in _stage_layout(st)[0]}


def _local_step(x, target, mod, small, w0, rest):
    dist = isinstance(rest, tuple)
    packs = rest[1] if dist else None
    W = [dict(w0), {}, {}, {}] if dist else [w0] + list(rest)

    def arrived(k, g):
        for n, a in _full_stage(g, k + 1).items():
            W[k if n.startswith("ffn") else k + 1][n] = a

    row = lambda v: v.reshape(1, -1)
    pad128 = lambda v: jnp.pad(v.reshape(1, -1), ((0, 0), (0, LANES - v.size)))
    saved = []
    mods = [[row(mod[i, k * D:(k + 1) * D]) for k in range(N_MOD)] for i in range(DEPTH)]
    h1 = _norm_mod(x, row(small["norm1_g"][0]), mods[0][1], mods[0][0], "norm1_0")
    for i in range(DEPTH):
        j = i // 2
        w = W[i]
        m = mods[i]
        sh1, sc1, gt1, sh2, sc2, gt2 = m
        g1, g2 = row(small["norm1_g"][i]), row(small["norm2_g"][i])
        if i % 2 == 0:
            proj = _mm(h1, w["dn_w_in"], "nt", F32, f"dn_proj_{i}")
            alog, dtb, og = pad128(small["dn_a_log"][j]), pad128(small["dn_dt_bias"][j]), row(small["dn_onorm_g"][j])
            comm = ("gather", packs[i]) if dist else None
            res = _dn_core_fwd(proj, small["dn_conv_w"][j], alog, dtb, og, f"dn_core_{i}", comm)
            om, o_pre, states = res[:3]
            if comm:
                arrived(i, res[3])
            y1 = _mm(om, w["dn_w_out"], "nn", F32, f"dn_out_{i}")
            mix = (proj, alog, dtb, og, o_pre, states, om)
        else:
            qkv = _mm(h1, w["sb_w_qkv"], "nt", F32, f"sb_qkv_{i}")
            qg2 = jnp.tile(row(small["sb_q_norm_g"][j]), (1, SB_HPB))
            kg2 = jnp.tile(row(small["sb_k_norm_g"][j]), (1, SB_HPB))
            comm = ("gather", packs[i]) if dist else None
            res = _sb_core_fwd(qkv, qg2, kg2, f"sb_core_{i}", comm)
            om, tot = res[:2]
            if comm:
                arrived(i, res[2])
            y1 = _mm(om, w["sb_w_out"], "nn", F32, f"sb_out_{i}")
            mix = (qkv, qg2, kg2, tot, om)
        x_mid, h2 = _gate_res_norm(x, y1, gt1, g2, sc2, sh2, f"res1_{i}")
        u = _mm(h2, w["ffn_w_in"], "nt", BF16, f"ffn_in_{i}")
        a = _swiglu_act(u, f"ffn_act_{i}")
        y2 = _mm(a, w["ffn_w_out"], "nn", F32, f"ffn_out_{i}")
        saved.append((x, h1, mix, y1, x_mid, h2, u, a, y2, m, g1, g2))
        if i + 1 < DEPTH:
            x, h1 = _gate_res_norm(x_mid, y2, gt2, row(small["norm1_g"][i + 1]), mods[i + 1][1], mods[i + 1][0],
                                   f"res2_{i}")
        else:
            x = _gate_res(x_mid, y2, gt2, f"res2_{i}")

    dx, sq = _loss_head(x, target, "loss_head")

    sg = dict(mod=[None] * DEPTH, norm1_g=[None] * DEPTH, norm2_g=[None] * DEPTH, dn_conv_w=[None] * 2,
              dn_a_log=[None] * 2, dn_dt_bias=[None] * 2, dn_onorm_g=[None] * 2, sb_q_norm_g=[None] * 2,
              sb_k_norm_g=[None] * 2)
    big = [None] * (DEPTH + 1 if dist else DEPTH)
    mixer_gw = {}
    for i in reversed(range(DEPTH)):
        j = i // 2
        w = W[i]
        x_in, h1, mix, y1, x_mid, h2, u, a, y2, m, g1, g2 = saved[i]
        sh1, sc1, gt1, sh2, sc2, gt2 = m
        gw = {}
        dy2, dgt2 = _gate_res_bwd(dx, y2, gt2, f"res2_bwd_{i}")
        da = _mm(dy2, w["ffn_w_out"], "nt", BF16, f"ffn_out_bwd_{i}")
        gw["ffn_w_out"] = _mm(a, dy2, "tn", BF16, f"ffn_out_wg_{i}")
        du = _swiglu_act_bwd(da, u, f"ffn_act_bwd_{i}")
        dh2 = _mm(du, w["ffn_w_in"], "nn", F32, f"ffn_in_bwd_{i}")
        gw["ffn_w_in"] = _mm(du, h2, "tn", BF16, f"ffn_in_wg_{i}")
        dx_mid, s2, dsh2 = _norm_mod_bwd(dh2, x_mid, dx, g2, sc2, f"norm2_bwd_{i}")
        dy1, dgt1 = _gate_res_bwd(dx_mid, y1, gt1, f"res1_bwd_{i}")
        comm = ("exchange", _shards_stage({**gw, **mixer_gw}, i + 1)) if dist else None
        if i % 2 == 0:
            proj, alog, dtb, og, o_pre, states, om = mix
            dom = _mm(dy1, w["dn_w_out"], "nt", F32, f"dn_out_bwd_{i}")
            gw["dn_w_out"] = _mm(om, dy1, "tn", BF16, f"dn_out_wg_{i}")
            dproj, dconv, dalog, ddtb, dog, got = _dn_core_bwd(proj, small["dn_conv_w"][j], alog, dtb, og, o_pre, states,
                                                               dom, f"dn_core_bwd_{i}", comm)
            dh1 = _mm(dproj, w["dn_w_in"], "nn", F32, f"dn_proj_bwd_{i}")
            gw["dn_w_in"] = _mm(dproj, h1, "tn", BF16, f"dn_proj_wg_{i}")[:DN_COLS]
            sg["dn_conv_w"][j] = dconv
            sg["dn_a_log"][j] = dalog[0, :DN_H]
            sg["dn_dt_bias"][j] = ddtb[0, :DN_H]
            sg["dn_onorm_g"][j] = dog[0]
        else:
            qkv, qg2, kg2, tot, om = mix
            dom = _mm(dy1, w["sb_w_out"], "nt", BF16, f"sb_out_bwd_{i}")
            gw["sb_w_out"] = _mm(om, dy1, "tn", BF16, f"sb_out_wg_{i}")
            dqkv, dqg, dkg, got = _sb_core_bwd(qkv, qg2, kg2, tot, dom, f"sb_core_bwd_{i}", comm)
            dh1 = _mm(dqkv, w["sb_w_qkv"], "nn", F32, f"sb_qkv_bwd_{i}")
            gw["sb_w_qkv"] = _mm(dqkv, h1, "tn", BF16, f"sb_qkv_wg_{i}")
            sg["sb_q_norm_g"][j] = jnp.sum(dqg.reshape(SB_HPB, SB_D), axis=0)
            sg["sb_k_norm_g"][j] = jnp.sum(dkg.reshape(SB_HPB, SB_D), axis=0)
        if comm:
            big[i + 1] = got
        dx, s1, dsh1 = _norm_mod_bwd(dh1, x_in, dx_mid, g1, sc1, f"norm1_bwd_{i}")
        sg["mod"][i] = jnp.concatenate([dsh1, s1 * g1, dgt1, dsh2, s2 * g2, dgt2], axis=1)[0]
        sg["norm1_g"][i] = (s1 * (1.0 + sc1))[0]
        sg["norm2_g"][i] = (s2 * (1.0 + sc2))[0]
        if dist:
            mixer_gw = {n: a for n, a in gw.items() if not n.startswith("ffn")}
        else:
            big[i] = gw
    if dist:
        big[0] = _all_to_all(_shards_stage(mixer_gw, 0), "exchange_grads_0")
    return sq, dx, {k: jnp.stack(v) for k, v in sg.items()}, big


def _device_index():
    return 4 * lax.axis_index("x") + 2 * lax.axis_index("y") + lax.axis_index("c")


def _gather_small(w, c):
    me = _device_index()
    ada_cols = w["ada_w"].shape[-1]
    conv_cols = w["dn_conv_w"].shape[-1]
    blk = jnp.concatenate([c.reshape(1, D), w["dn_conv_w"].reshape(-1, D)], axis=0)
    g1 = _all_gather(blk, "gather_cond", True)
    c16 = jnp.pad(g1[:, 0, :], ((0, 8), (0, 0)))
    conv_full = jnp.transpose(g1[:, 1:, :].reshape(NDEV, 2, DN_CONV, conv_cols), (1, 2, 0, 3)).reshape(2, DN_CONV, -1)
    b_cols = lax.dynamic_slice_in_dim(w["ada_b"], me * ada_cols, ada_cols, axis=1).reshape(DEPTH, 1, ada_cols)
    mod_part = _ada_fwd(c16, w["ada_w"], b_cols, "ada_fwd")[:, :NDEV, :]
    g2 = _all_gather(mod_part.reshape(DEPTH * NDEV, ada_cols), "gather_mod", True)
    g2 = g2.reshape(NDEV, DEPTH, NDEV, ada_cols)
    mod = lax.dynamic_index_in_dim(g2, me, axis=2, keepdims=False)
    mod = jnp.transpose(mod, (1, 0, 2)).reshape(DEPTH, N_MOD * D)
    return c16, conv_full, mod


def _reduce_small(gr, c16):
    me = _device_index()
    ada_cols = N_MOD * D // NDEV
    conv_cols = 3 * DN_H * DN_D // NDEV
    grads = {}
    small = jnp.concatenate([gr["dn_a_log"].reshape(-1), gr["dn_dt_bias"].reshape(-1), gr["dn_onorm_g"].reshape(-1),
                             gr["sb_q_norm_g"].reshape(-1), gr["sb_k_norm_g"].reshape(-1)])
    small = jnp.pad(small, (0, D - small.size)).reshape(1, D)
    blk3 = jnp.concatenate([gr["mod"].reshape(-1, D), gr["norm1_g"], gr["norm2_g"], gr["dn_conv_w"].reshape(-1, D),
                            small], axis=0)
    blk3 = jnp.pad(blk3, ((0, 64 - blk3.shape[0]), (0, 0)))
    g3 = _all_gather(blk3, "gather_small_grads", True)
    tot = _sum_sources(g3, "sum_small_grads")
    grads["ada_b"] = tot[:24].reshape(DEPTH, N_MOD * D)
    grads["norm1_g"] = tot[24:28]
    grads["norm2_g"] = tot[28:32]
    conv_g = tot[32:56].reshape(2, DN_CONV, NDEV * conv_cols)
    grads["dn_conv_w"] = lax.dynamic_slice_in_dim(conv_g, me * conv_cols, conv_cols, axis=2)
    sm = tot[56]
    grads["dn_a_log"] = sm[0:16].reshape(2, DN_H)
    grads["dn_dt_bias"] = sm[16:32].reshape(2, DN_H)
    grads["dn_onorm_g"] = sm[32:288].reshape(2, DN_D)
    grads["sb_q_norm_g"] = sm[288:416].reshape(2, SB_D)
    grads["sb_k_norm_g"] = sm[416:544].reshape(2, SB_D)
    dmod_all = g3[:, :24, :].reshape(NDEV, DEPTH, N_MOD * D)
    dmod_cols = lax.dynamic_slice_in_dim(dmod_all, me * ada_cols, ada_cols, axis=2)
    dmod16 = jnp.pad(jnp.transpose(dmod_cols, (1, 0, 2)), ((0, 0), (0, 8), (0, 0)))
    grads["ada_w"] = _ada_bwd(c16, dmod16, "ada_bwd")
    return grads


def _reduce_big(recv):
    parts = {}
    for st in range(DEPTH + 1):
        parts.update(_unpack_stage(_sum_sources(recv[st], f"sum_grads_{st}"), st))
    out = {}
    for (n, j) in sorted(parts):
        out.setdefault(n, []).append(parts[(n, j)])
    return {n: jnp.stack(v) for n, v in out.items()}


def kernel(x, c, ada_w, ada_b, norm1_g, norm2_g, dn_w_in, dn_conv_w, dn_a_log, dn_dt_bias, dn_onorm_g, dn_w_out, sb_w_qkv, sb_q_norm_g, sb_k_norm_g, sb_w_out, ffn_w_in, ffn_w_out, loss_target, m_ada_w, m_ada_b, m_norm1_g, m_norm2_g, m_dn_w_in, m_dn_conv_w, m_dn_a_log, m_dn_dt_bias, m_dn_onorm_g, m_dn_w_out, m_sb_w_qkv, m_sb_q_norm_g, m_sb_k_norm_g, m_sb_w_out, m_ffn_w_in, m_ffn_w_out, v_ada_w, v_ada_b, v_norm1_g, v_norm2_g, v_dn_w_in, v_dn_conv_w, v_dn_a_log, v_dn_dt_bias, v_dn_onorm_g, v_dn_w_out, v_sb_w_qkv, v_sb_q_norm_g, v_sb_k_norm_g, v_sb_w_out, v_ffn_w_in, v_ffn_w_out):
    w = dict(ada_w=ada_w, ada_b=ada_b, norm1_g=norm1_g, norm2_g=norm2_g, dn_w_in=dn_w_in, dn_conv_w=dn_conv_w,
             dn_a_log=dn_a_log, dn_dt_bias=dn_dt_bias, dn_onorm_g=dn_onorm_g, dn_w_out=dn_w_out, sb_w_qkv=sb_w_qkv,
             sb_q_norm_g=sb_q_norm_g, sb_k_norm_g=sb_k_norm_g, sb_w_out=sb_w_out, ffn_w_in=ffn_w_in, ffn_w_out=ffn_w_out)
    mom = dict(ada_w=m_ada_w, ada_b=m_ada_b, norm1_g=m_norm1_g, norm2_g=m_norm2_g, dn_w_in=m_dn_w_in,
               dn_conv_w=m_dn_conv_w, dn_a_log=m_dn_a_log, dn_dt_bias=m_dn_dt_bias, dn_onorm_g=m_dn_onorm_g,
               dn_w_out=m_dn_w_out, sb_w_qkv=m_sb_w_qkv, sb_q_norm_g=m_sb_q_norm_g, sb_k_norm_g=m_sb_k_norm_g,
               sb_w_out=m_sb_w_out, ffn_w_in=m_ffn_w_in, ffn_w_out=m_ffn_w_out)
    var = dict(ada_w=v_ada_w, ada_b=v_ada_b, norm1_g=v_norm1_g, norm2_g=v_norm2_g, dn_w_in=v_dn_w_in,
               dn_conv_w=v_dn_conv_w, dn_a_log=v_dn_a_log, dn_dt_bias=v_dn_dt_bias, dn_onorm_g=v_dn_onorm_g,
               dn_w_out=v_dn_w_out, sb_w_qkv=v_sb_w_qkv, sb_q_norm_g=v_sb_q_norm_g, sb_k_norm_g=v_sb_k_norm_g,
               sb_w_out=v_sb_w_out, ffn_w_in=v_ffn_w_in, ffn_w_out=v_ffn_w_out)
    names = list(w)
    c16, conv_full, mod = _gather_small(w, c)
    packs = [_pack_stage(w, st) for st in range(DEPTH + 1)]
    w0 = _full_stage(_all_gather(packs[0], "gather_weights_0", False), 0)
    small = dict(norm1_g=norm1_g, norm2_g=norm2_g, dn_conv_w=conv_full, dn_a_log=dn_a_log, dn_dt_bias=dn_dt_bias,
                 dn_onorm_g=dn_onorm_g, sb_q_norm_g=sb_q_norm_g, sb_k_norm_g=sb_k_norm_g)
    sq, grad_x, sgr, recv = _local_step(x[0], loss_target[0], mod, small, w0, ("packed", packs[1:]))
    loss = lax.psum(sq[0, 0] * (0.5 / D), ("x", "y", "c"))
    grads = {**_reduce_big(recv), **_reduce_small(sgr, c16)}
    delta, new_m, new_v = {}, {}, {}
    for n in names:
        delta[n], new_m[n], new_v[n] = _adamw(w[n], grads[n], mom[n], var[n], f"adamw_{n}")
    return (loss, grad_x[None], *[grads[n] for n in names], *[delta[n] for n in names],
            *[new_m[n] for n in names], *[new_v[n] for n in names])
```

```python
import jax
import jax.numpy as jnp
from jax import lax
from jax.experimental import pallas as pl
from jax.experimental.pallas import tpu as pltpu

F32, BF16 = jnp.float32, jnp.bfloat16

D = 1024
DEPTH = 4
N_MOD = 6
DN_H, DN_D, DN_C, DN_CONV = 8, 128, 64, 4
DN_UNROLL_FWD, DN_UNROLL_BWD = 16, 8
DN_COLS = 4 * DN_H * DN_D + 2 * DN_H
DN_COLS_PAD = 33 * 128
SB_H, SB_D = 16, 64
SB_TILE = 256
SB_HPB = 4
SB_W = SB_HPB * SB_D
DFF = 2816
EPS = 1e-6
NDEV = 8
LANES = 128
VMEM_LIMIT = 56 * 1024 * 1024

ADAM_LR, ADAM_B1, ADAM_B2, ADAM_EPS, ADAM_WD, ADAM_STEP = 0.001, 0.9, 0.999, 1e-08, 0.01, 10

NN = ((1,), (0,))
NT = ((1,), (1,))
TN = ((0,), (0,))
MESH = pl.DeviceIdType.MESH


def _dot(a, b, dims=NN):
    a, b = a.astype(BF16), b.astype(BF16)
    if a.ndim == 2 and b.ndim == 2:
        return lax.dot_general(a, b, (dims, ((), ())), preferred_element_type=F32)
    n = a.shape[0] if a.ndim == 3 else b.shape[0]
    if a.ndim == 2:
        a = jnp.broadcast_to(a, (n,) + a.shape)
    if b.ndim == 2:
        b = jnp.broadcast_to(b, (n,) + b.shape)
    (ca,), (cb,) = dims
    return lax.dot_general(a, b, (((ca + 1,), (cb + 1,)), ((0,), (0,))), preferred_element_type=F32)


def _split(a):
    hi = a.astype(BF16)
    lo = (a - hi.astype(F32)).astype(BF16)
    return hi, lo


def _dot2r(a, m, dims=NN):
    ah, al = _split(a)
    return _dot(ah, m, dims) + _dot(al, m, dims)


def _dot2l(m, b, dims=NN):
    bh, bl = _split(b)
    return _dot(m, bh, dims) + _dot(m, bl, dims)


def _sigmoid(x):
    return 1.0 / (1.0 + jnp.exp(-x))


def _silu(x):
    return x * _sigmoid(x)


def _dsilu(x):
    s = _sigmoid(x)
    return s * (1.0 + x * (1.0 - s))


def _softplus(x):
    return jnp.maximum(x, 0.0) + jnp.log(1.0 + jnp.exp(-jnp.abs(x)))


def _iota(shape, dim):
    return lax.broadcasted_iota(jnp.int32, shape, dim)


def _rowsum(x):
    return jnp.sum(x, axis=-1, keepdims=True)


def _colsum(x):
    return jnp.sum(x, axis=-2, keepdims=True)


def _tile(n, pref):
    if n <= pref:
        return n
    best = None
    for t in range(LANES, pref + 1, LANES):
        if n % t == 0:
            best = t
    assert best is not None, (n, pref)
    return best


def _rtile(r, pref=512):
    best = None
    for t in range(8, min(r, pref) + 1, 8):
        if r % t == 0:
            best = t
    return best if best is not None else r


def _params(sem):
    return pltpu.CompilerParams(dimension_semantics=sem, vmem_limit_bytes=VMEM_LIMIT)


MM_TILE = 1408


def _mm(a, b, mode, out_dtype, name):
    if mode == "nn":
        (M, K), (K2, N) = a.shape, b.shape
        dims = NN
    elif mode == "nt":
        (M, K), (N, K2) = a.shape, b.shape
        dims = NT
    else:
        (K, M), (K2, N) = a.shape, b.shape
        dims = TN
    assert K == K2, (a.shape, b.shape, mode)
    tm, tn, tk = _tile(M, MM_TILE), _tile(N, MM_TILE), _tile(K, MM_TILE)
    nk = K // tk

    def body_single(a_ref, b_ref, o_ref):
        o_ref[...] = _dot(a_ref[...], b_ref[...], dims).astype(o_ref.dtype)

    def body_acc(a_ref, b_ref, o_ref, acc_ref):
        k = pl.program_id(2)
        p = _dot(a_ref[...], b_ref[...], dims)

        @pl.when(k == 0)
        def _():
            acc_ref[...] = p

        @pl.when(k > 0)
        def _():
            acc_ref[...] += p

        @pl.when(k == nk - 1)
        def _():
            o_ref[...] = acc_ref[...].astype(o_ref.dtype)

    if mode == "nn":
        a_spec = pl.BlockSpec((tm, tk), lambda i, j, k: (i, k))
        b_spec = pl.BlockSpec((tk, tn), lambda i, j, k: (k, j))
    elif mode == "nt":
        a_spec = pl.BlockSpec((tm, tk), lambda i, j, k: (i, k))
        b_spec = pl.BlockSpec((tn, tk), lambda i, j, k: (j, k))
    else:
        a_spec = pl.BlockSpec((tk, tm), lambda i, j, k: (k, i))
        b_spec = pl.BlockSpec((tk, tn), lambda i, j, k: (k, j))
    return pl.pallas_call(
        body_single if nk == 1 else body_acc, grid=(M // tm, N // tn, nk), in_specs=[a_spec, b_spec],
        out_specs=pl.BlockSpec((tm, tn), lambda i, j, k: (i, j)),
        out_shape=jax.ShapeDtypeStruct((M, N), out_dtype),
        scratch_shapes=[] if nk == 1 else [pltpu.VMEM((tm, tn), F32)], name=name,
        compiler_params=_params(("parallel", "parallel", "arbitrary")),
    )(a, b)


def _rowwise(fn, name, rows, bcasts, out_rows, out_reds=(), tile=256):
    T = rows[0].shape[0]
    tile = min(tile, T)
    nr, nb, no = len(rows), len(bcasts), len(out_rows)

    def body(*refs):
        rv = [r[...] for r in refs[:nr]]
        bv = [r[...] for r in refs[nr:nr + nb]]
        outs, reds = fn(rv, bv)
        for r, o in zip(refs[nr + nb:nr + nb + no], outs):
            r[...] = o.astype(r.dtype)
        red_refs = refs[nr + nb + no:]
        if red_refs:
            @pl.when(pl.program_id(0) == 0)
            def _():
                for r in red_refs:
                    r[...] = jnp.zeros(r.shape, F32)

            for r, v in zip(red_refs, reds):
                r[...] += v

    in_specs = [pl.BlockSpec((tile, a.shape[1]), lambda i: (i, 0)) for a in rows]
    in_specs += [pl.BlockSpec(b.shape, lambda i: (0, 0)) for b in bcasts]
    out_specs = [pl.BlockSpec((tile, c), lambda i: (i, 0)) for c, _ in out_rows]
    out_specs += [pl.BlockSpec(s, lambda i: (0, 0)) for s in out_reds]
    out_shape = [jax.ShapeDtypeStruct((T, c), dt) for c, dt in out_rows]
    out_shape += [jax.ShapeDtypeStruct(s, F32) for s in out_reds]
    return pl.pallas_call(
        body, grid=(T // tile,), in_specs=in_specs, out_specs=out_specs, out_shape=out_shape, name=name,
        compiler_params=_params(("arbitrary",)),
    )(*rows, *bcasts)


def _norm_mod(x, g, sc, sh, name):
    def fn(rv, bv):
        (xv,), (gv, scv, shv) = rv, bv
        r = lax.rsqrt(jnp.mean(xv * xv, axis=1, keepdims=True) + EPS)
        return [(xv * r * gv) * (1.0 + scv) + shv], []
    return _rowwise(fn, name, [x], [g, sc, sh], [(D, BF16)])[0]


def _norm_mod_bwd(dh, x, dres, g, sc, name):
    def fn(rv, bv):
        (dhv, xv, drv), (gv, scv) = rv, bv
        r = lax.rsqrt(jnp.mean(xv * xv, axis=1, keepdims=True) + EPS)
        xhat = xv * r
        dxhat = dhv * (gv * (1.0 + scv))
        dx = r * (dxhat - xhat * jnp.mean(dxhat * xhat, axis=1, keepdims=True)) + drv
        return [dx], [_colsum(dhv * xhat), _colsum(dhv)]
    return _rowwise(fn, name, [dh, x, dres], [g, sc], [(D, F32)], [(1, D), (1, D)])


def _gate_res(x, y, gt, name):
    def fn(rv, bv):
        return [rv[0] + bv[0] * rv[1]], []
    return _rowwise(fn, name, [x, y], [gt], [(D, F32)])[0]


def _gate_res_norm(x, y, gt, g, sc, sh, name):
    def fn(rv, bv):
        (xv, yv), (gtv, gv, scv, shv) = rv, bv
        xn = xv + gtv * yv
        r = lax.rsqrt(jnp.mean(xn * xn, axis=1, keepdims=True) + EPS)
        return [xn, (xn * r * gv) * (1.0 + scv) + shv], []
    return _rowwise(fn, name, [x, y], [gt, g, sc, sh], [(D, F32), (D, BF16)])


def _gate_res_bwd(dxn, y, gt, name):
    def fn(rv, bv):
        return [rv[0] * bv[0]], [_colsum(rv[0] * rv[1])]
    return _rowwise(fn, name, [dxn, y], [gt], [(D, BF16)], [(1, D)])


def _swiglu_act(u, name):
    def fn(rv, bv):
        uv = rv[0].astype(F32)
        return [_silu(uv[:, :DFF]) * uv[:, DFF:]], []
    return _rowwise(fn, name, [u], [], [(DFF, BF16)])[0]


def _swiglu_act_bwd(da, u, name):
    def fn(rv, bv):
        dav, uv = rv[0].astype(F32), rv[1].astype(F32)
        gate, up = uv[:, :DFF], uv[:, DFF:]
        return [jnp.concatenate([dav * up * _dsilu(gate), dav * _silu(gate)], axis=1)], []
    return _rowwise(fn, name, [da, u], [], [(2 * DFF, BF16)])[0]


def _loss_head(y, target, name):
    def fn(rv, bv):
        err = rv[0] - rv[1]
        return [err * (1.0 / D)], [_colsum(_rowsum(err * err))]
    return _rowwise(fn, name, [y, target], [], [(D, F32)], [(1, 1)])


def _shift_rows(x, s):
    if s == 0:
        return x
    T = x.shape[0]
    r = pltpu.roll(x, s % T, axis=0)
    t = _iota(x.shape, 0)
    keep = (t >= s) if s > 0 else (t < T + s)
    return jnp.where(keep, r, 0.0)


def _dn_prep(pq, pk, pv, pab, cq_ref, ck_ref, cv_ref, alog, dtb, h):
    lane = _iota(pab.shape, 1)
    a_col = _rowsum(jnp.where(lane == h, pab, 0.0))
    b_col = _rowsum(jnp.where(lane == DN_H + h, pab, 0.0))
    lane1 = _iota(alog.shape, 1)
    alog_h = _rowsum(jnp.where(lane1 == h, alog, 0.0))
    dtb_h = _rowsum(jnp.where(lane1 == h, dtb, 0.0))
    pre = a_col + dtb_h
    neg_ea = -jnp.exp(alog_h)
    g = neg_ea * _softplus(pre)
    beta = _sigmoid(b_col)

    def conv(x, w_ref):
        acc = x * w_ref[DN_CONV - 1:DN_CONV, :]
        for i in range(DN_CONV - 1):
            acc = acc + _shift_rows(x, DN_CONV - 1 - i) * w_ref[i:i + 1, :]
        return acc

    xq, xk, xv = conv(pq, cq_ref), conv(pk, ck_ref), conv(pv, cv_ref)
    sq, sk, v = _silu(xq), _silu(xk), _silu(xv)
    rq = lax.rsqrt(_rowsum(sq * sq) + EPS)
    rk = lax.rsqrt(_rowsum(sk * sk) + EPS)
    return dict(g=g, beta=beta, pre=pre, neg_ea=neg_ea, xq=xq, xk=xk, xv=xv, rq=rq, rk=rk,
                qn=sq * rq, kn=sk * rk, v=v)


def _dn_masks():
    C = DN_C
    r, c = _iota((C, C), 0), _iota((C, C), 1)
    incl = r >= c
    strict = r > c
    blk16 = jnp.right_shift(r, 4) == jnp.right_shift(c, 4)
    blk32 = jnp.right_shift(r, 5) == jnp.right_shift(c, 5)
    return dict(incl=incl, strict=strict, upper=r <= c, blk16=blk16, blk32=blk32,
                tri=incl.astype(BF16), triT=(r <= c).astype(BF16), ones=jnp.ones((C, C), BF16),
                eye=(r == c).astype(F32), last=_iota((C, 1), 0) == C - 1)


def _tri_inverse(A, mk):
    P = -jnp.where(mk["blk16"], A, 0.0)
    X = mk["eye"] + P
    for _ in range(3):
        P = _dot(P, P)
        X = X + _dot(X, P)
    off1 = jnp.where(mk["blk32"] & (~mk["blk16"]), A, 0.0)
    X = X - _dot(_dot(X, off1), X)
    off2 = jnp.where(mk["blk32"], 0.0, A)
    X = X - _dot(_dot(X, off2), X)
    return X


def _dn_local(qc, kc, vc, gc, bc, mk):
    C = DN_C
    gm = jnp.broadcast_to(gc, gc.shape[:-1] + (C,))
    Gc = _dot2l(mk["tri"], gm)
    Gr = _dot2l(mk["ones"], jnp.where(mk["upper"], gm, 0.0))
    Dm = jnp.where(mk["incl"], jnp.exp(jnp.where(mk["incl"], Gc - Gr, 0.0)), 0.0)
    Gcol = jnp.max(Gc, axis=-1, keepdims=True)
    Gl = _colsum(jnp.where(mk["last"], Gcol, 0.0))
    eG = jnp.exp(Gcol)
    eT = jnp.exp(Gl - Gcol)
    gl = jnp.exp(Gl)
    kb = kc * bc
    vb = vc * bc
    KK = _dot(kb, kc, NT)
    Tinv = _tri_inverse(jnp.where(mk["strict"], KK * Dm, 0.0), mk)
    KBE = kb * eG
    QK = _dot(qc, kc, NT)
    return dict(Dm=Dm, eG=eG, eT=eT, gl=gl, kb=kb, vb=vb, KK=KK, Tinv=Tinv, KBE=KBE, U=_dot(Tinv, vb),
                W=_dot(Tinv, KBE), QK=QK, attn=QK * Dm, QD=qc * eG, KT=kc * eT)


def _dn_recur(f, S):
    vnew = f["U"] - _dot(f["W"], S)
    o = _dot(f["QD"], S) + _dot(f["attn"], vnew)
    return o, S * f["gl"] + _dot(f["KT"], vnew, TN), vnew


def _dn_bwd_chain(f, do, dS):
    dvnew = _dot(f["KT"], dS) + _dot(f["attn"], do, TN)
    return dvnew, dS * f["gl"] + _dot(f["QD"], do, TN) - _dot(f["W"], dvnew, TN)


def _dn_bwd_rest(qc, kc, vc, bc, f, S, vnew, do, dS, dvnew, mk):
    C = DN_C
    Dm, eG, eT, gl, kb, vb, KK, Tinv, KBE, QK = (
        f[n] for n in ("Dm", "eG", "eT", "gl", "kb", "vb", "KK", "Tinv", "KBE", "QK"))
    dKT = _dot(vnew, dS, NT)
    dgl = _colsum(_rowsum(dS * S))
    dQD = _dot(do, S, NT)
    dattn = _dot(do, vnew, NT)
    dU = dvnew
    dW = -_dot(dvnew, S, NT)
    dQK = dattn * Dm
    dD = dattn * QK
    dq = _dot(dQK, kc)
    dk = _dot(dQK, qc, TN)
    dTinv = _dot(dU, vb, NT) + _dot(dW, KBE, NT)
    dvb = _dot(Tinv, dU, TN)
    dKBE = _dot(Tinv, dW, TN)
    dA = -_dot(_dot(Tinv, dTinv, TN), Tinv, NT)
    dA = jnp.where(mk["strict"], dA, 0.0)
    dKK = dA * Dm
    dD = dD + dA * KK
    dkb = _dot(dKK, kc) + dKBE * eG
    dk = dk + _dot(dKK, kb, TN)
    deG = _rowsum(dKBE * kb)
    dk = dk + dkb * bc
    dbeta = _rowsum(dkb * kc) + _rowsum(dvb * vc)
    dv = dvb * bc
    dq = dq + dQD * eG
    deG = deG + _rowsum(dQD * qc)
    dk = dk + dKT * eT
    deT = _rowsum(dKT * kc)
    dGcol = deG * eG - deT * eT
    dGl = _colsum(deT * eT) + dgl * gl
    Y = dD * Dm
    ycol = jnp.max(_dot2r(Y, mk["ones"], TN), axis=-1, keepdims=True)
    dGcol = dGcol + _rowsum(Y) - ycol
    dGcol = dGcol + jnp.where(mk["last"], dGl, 0.0)
    dg = jnp.max(_dot2l(mk["triT"], jnp.broadcast_to(dGcol, dGcol.shape[:-1] + (C,))), axis=-1, keepdims=True)
    return dq, dk, dv, dg, dbeta


def _dn_core_fwd(proj, conv_w, alog, dtb, og, name, comm=None):
    T = proj.shape[0]
    N = T // DN_C

    def body(pq_ref, pk_ref, pv_ref, pz_ref, pab_ref, cq_ref, ck_ref, cv_ref, alog_ref, dtb_ref, og_ref,
             out_ref, o_ref, st_ref, q_s, k_s, v_s, g_s, b_s, S_s):
        h = pl.program_id(0)
        p = _dn_prep(pq_ref[...], pk_ref[...], pv_ref[...], pab_ref[...], cq_ref, ck_ref, cv_ref,
                     alog_ref[...], dtb_ref[...], h)
        q_s[...] = p["qn"] * (DN_D ** -0.5)
        k_s[...] = p["kn"]
        v_s[...] = p["v"]
        g_s[...] = p["g"]
        b_s[...] = p["beta"]
        S_s[...] = jnp.zeros(S_s.shape, F32)
        mk = _dn_masks()

        nu = min(DN_UNROLL_FWD, N)

        def step(it, carry):
            rows = pl.ds(pl.multiple_of(it * (nu * DN_C), nu * DN_C), nu * DN_C)
            loc = _dn_local(*(r[rows, :].reshape(nu, DN_C, r.shape[1]) for r in (q_s, k_s, v_s, g_s, b_s)), mk)
            S = S_s[...]
            outs = []
            for u in range(nu):
                st_ref[0, it * nu + u] = S
                o, S, _ = _dn_recur({n: v[u] for n, v in loc.items()}, S)
                outs.append(o)
            o_ref[rows, :] = jnp.concatenate(outs, axis=0)
            S_s[...] = S
            return carry

        lax.fori_loop(0, N // nu, step, 0)
        o = o_ref[...]
        ro = lax.rsqrt(jnp.mean(o * o, axis=1, keepdims=True) + EPS)
        out_ref[...] = ((o * ro * og_ref[...]) * _silu(pz_ref[...])).astype(out_ref.dtype)

    col = lambda k: pl.BlockSpec((T, DN_D), lambda h: (0, k * DN_H + h))
    cw = lambda k: pl.BlockSpec((DN_CONV, DN_D), lambda h: (0, k * DN_H + h))
    small = pl.BlockSpec((1, LANES), lambda h: (0, 0))
    return _grid_call(
        body, comm, grid=(DN_H,),
        in_specs=[col(0), col(1), col(2), col(3), pl.BlockSpec((T, LANES), lambda h: (0, 4 * DN_H)),
                  cw(0), cw(1), cw(2), small, small, small],
        out_specs=[pl.BlockSpec((T, DN_D), lambda h: (0, h)), pl.BlockSpec((T, DN_D), lambda h: (0, h)),
                   pl.BlockSpec((1, N, DN_D, DN_D), lambda h: (h, 0, 0, 0))],
        out_shape=[jax.ShapeDtypeStruct((T, D), BF16), jax.ShapeDtypeStruct((T, D), F32),
                   jax.ShapeDtypeStruct((DN_H, N, DN_D, DN_D), F32)],
        scratch_shapes=[pltpu.VMEM((T, DN_D), F32)] * 3 + [pltpu.VMEM((T, 1), F32)] * 2 + [pltpu.VMEM((DN_D, DN_D), F32)],
        name=name, args=(proj, proj, proj, proj, proj, conv_w, conv_w, conv_w, alog, dtb, og))


def _dn_core_bwd(proj, conv_w, alog, dtb, og, o, states, dout, name, comm=None):
    T = proj.shape[0]
    N = T // DN_C

    def body(pq_ref, pk_ref, pv_ref, pz_ref, pab_ref, cq_ref, ck_ref, cv_ref, alog_ref, dtb_ref, og_ref,
             o_ref, st_ref, dout_ref,
             dpq_ref, dpk_ref, dpv_ref, dpz_ref, dpab_ref, dcq_ref, dck_ref, dcv_ref, dalog_ref, ddtb_ref, dog_ref,
             q_s, k_s, v_s, g_s, b_s, do_s, dS_s):
        h = pl.program_id(0)
        scale = DN_D ** -0.5

        def prep():
            return _dn_prep(pq_ref[...], pk_ref[...], pv_ref[...], pab_ref[...], cq_ref, ck_ref, cv_ref,
                            alog_ref[...], dtb_ref[...], h)

        p = prep()
        q_s[...] = p["qn"] * scale
        k_s[...] = p["kn"]
        v_s[...] = p["v"]
        g_s[...] = p["g"]
        b_s[...] = p["beta"]
        del p

        o = o_ref[...]
        z = pz_ref[...]
        dout = dout_ref[...]
        ogv = og_ref[...]
        ro = lax.rsqrt(jnp.mean(o * o, axis=1, keepdims=True) + EPS)
        on = o * ro
        dy = dout * _silu(z)
        dpz_ref[...] = (dout * (on * ogv) * _dsilu(z)).astype(dpz_ref.dtype)
        dyg = dy * ogv
        do_s[...] = ro * (dyg - on * jnp.mean(dyg * on, axis=1, keepdims=True))
        dog_h = _colsum(dy * on)

        dS_s[...] = jnp.zeros(dS_s.shape, F32)
        mk = _dn_masks()

        nu = min(DN_UNROLL_BWD, N)

        def step(it, carry):
            c0 = (N // nu - 1 - it) * nu
            rows = pl.ds(pl.multiple_of(c0 * DN_C, nu * DN_C), nu * DN_C)
            q, k, v, g, b, do = (r[rows, :].reshape(nu, DN_C, r.shape[1]) for r in (q_s, k_s, v_s, g_s, b_s, do_s))
            loc = _dn_local(q, k, v, g, b, mk)
            Ss = st_ref[0, pl.ds(c0, nu)]
            vnew = loc["U"] - _dot(loc["W"], Ss)
            dS = dS_s[...]
            dS_in, dvnew = [None] * nu, [None] * nu
            for u in reversed(range(nu)):
                dS_in[u] = dS
                dvnew[u], dS = _dn_bwd_chain({n: x[u] for n, x in loc.items()}, do[u], dS)
            dS_s[...] = dS
            grads = _dn_bwd_rest(q, k, v, b, loc, Ss, vnew, do, jnp.stack(dS_in), jnp.stack(dvnew), mk)
            for r, d in zip((q_s, k_s, v_s, g_s, b_s), grads):
                r[rows, :] = d.reshape(nu * DN_C, r.shape[1])
            return carry

        lax.fori_loop(0, N // nu, step, 0)

        p = prep()
        pq, pk, pv = pq_ref[...], pk_ref[...], pv_ref[...]
        dqn = q_s[...] * scale
        dkn = k_s[...]
        qn, kn = p["qn"], p["kn"]
        dsq = p["rq"] * (dqn - qn * _rowsum(dqn * qn))
        dsk = p["rk"] * (dkn - kn * _rowsum(dkn * kn))
        dxq = dsq * _dsilu(p["xq"])
        dxk = dsk * _dsilu(p["xk"])
        dxv = v_s[...] * _dsilu(p["xv"])

        def conv_bwd(dx, x, w_ref, dp_ref, dc_ref):
            acc = dx * w_ref[DN_CONV - 1:DN_CONV, :]
            dc_ref[DN_CONV - 1:DN_CONV, :] = _colsum(dx * x)
            for i in range(DN_CONV - 1):
                s = DN_CONV - 1 - i
                acc = acc + _shift_rows(dx, -s) * w_ref[i:i + 1, :]
                dc_ref[i:i + 1, :] = _colsum(dx * _shift_rows(x, s))
            dp_ref[...] = acc.astype(dp_ref.dtype)

        conv_bwd(dxq, pq, cq_ref, dpq_ref, dcq_ref)
        conv_bwd(dxk, pk, ck_ref, dpk_ref, dck_ref)
        conv_bwd(dxv, pv, cv_ref, dpv_ref, dcv_ref)

        dg = g_s[...]
        beta = p["beta"]
        da_raw = dg * p["neg_ea"] * _sigmoid(p["pre"])
        db_raw = b_s[...] * beta * (1.0 - beta)
        lane = _iota((T, LANES), 1)
        contrib = jnp.where(lane == h, da_raw, 0.0) + jnp.where(lane == DN_H + h, db_raw, 0.0)
        lane1 = _iota((1, LANES), 1)
        dalog_h = jnp.where(lane1 == h, _colsum(dg * p["g"]), 0.0)
        ddtb_h = jnp.where(lane1 == h, _colsum(da_raw), 0.0)

        @pl.when(h == 0)
        def _():
            dpab_ref[...] = contrib.astype(dpab_ref.dtype)
            dalog_ref[...] = dalog_h
            ddtb_ref[...] = ddtb_h
            dog_ref[...] = dog_h

        @pl.when(h > 0)
        def _():
            dpab_ref[...] += contrib.astype(dpab_ref.dtype)
            dalog_ref[...] += dalog_h
            ddtb_ref[...] += ddtb_h
            dog_ref[...] += dog_h

    col = lambda k: pl.BlockSpec((T, DN_D), lambda h: (0, k * DN_H + h))
    cw = lambda k: pl.BlockSpec((DN_CONV, DN_D), lambda h: (0, k * DN_H + h))
    small = pl.BlockSpec((1, LANES), lambda h: (0, 0))
    ab = pl.BlockSpec((T, LANES), lambda h: (0, 4 * DN_H))
    hcol = pl.BlockSpec((T, DN_D), lambda h: (0, h))
    outs = _grid_call(
        body, comm, grid=(DN_H,),
        in_specs=[col(0), col(1), col(2), col(3), ab, cw(0), cw(1), cw(2), small, small, small,
                  hcol, pl.BlockSpec((1, N, DN_D, DN_D), lambda h: (h, 0, 0, 0)), hcol],
        out_specs=[hcol, hcol, hcol, hcol, pl.BlockSpec((T, LANES), lambda h: (0, 0)),
                   pl.BlockSpec((DN_CONV, DN_D), lambda h: (0, h)), pl.BlockSpec((DN_CONV, DN_D), lambda h: (0, h)),
                   pl.BlockSpec((DN_CONV, DN_D), lambda h: (0, h)), small, small, small],
        out_shape=[jax.ShapeDtypeStruct((T, D), BF16)] * 4 + [jax.ShapeDtypeStruct((T, LANES), BF16)]
                  + [jax.ShapeDtypeStruct((DN_CONV, D), F32)] * 3 + [jax.ShapeDtypeStruct((1, LANES), F32)] * 3,
        scratch_shapes=[pltpu.VMEM((T, DN_D), F32)] * 3 + [pltpu.VMEM((T, 1), F32)] * 2
                       + [pltpu.VMEM((T, DN_D), F32), pltpu.VMEM((DN_D, DN_D), F32)],
        name=name, args=(proj, proj, proj, proj, proj, conv_w, conv_w, conv_w, alog, dtb, og, o, states, dout))
    dpq, dpk, dpv, dpz, dpab, dcq, dck, dcv, dalog, ddtb, dog = outs[:11]
    dproj = jnp.concatenate([dpq, dpk, dpv, dpz, dpab], axis=1)
    dconv = jnp.concatenate([dcq, dck, dcv], axis=1)
    return dproj, dconv, dalog, ddtb, dog, (outs[11] if comm else None)


def _sb_head_of(shape):
    return jnp.right_shift(_iota(shape, 1), 6)


def _sb_head_sums(x, head):
    out = jnp.zeros_like(x)
    for hh in range(SB_HPB):
        out = jnp.where(head == hh, _rowsum(jnp.where(head == hh, x, 0.0)), out)
    return out


def _sb_head_norm(x, head):
    r = lax.rsqrt(_sb_head_sums(x * x, head) * (1.0 / SB_D) + EPS)
    return x * r, r


def _sb_fill(q_ref, k_ref, v_ref, qg_ref, kg_ref, qs_s, kn_s, v_s):
    head = _sb_head_of(q_ref.shape)
    qs_s[...] = (_sb_head_norm(q_ref[...], head)[0] * qg_ref[...] * (SB_D ** -0.5)).astype(BF16)
    kn_s[...] = (_sb_head_norm(k_ref[...], head)[0] * kg_ref[...]).astype(BF16)
    v_s[...] = v_ref[...].astype(BF16)


def _sb_head_masks(dtype):
    return jnp.stack([(_sb_head_of((1, SB_W)) == hh).astype(dtype) for hh in range(SB_HPB)])


def _sb_core_fwd(qkv, qg, kg, name, comm=None):
    T = qkv.shape[0]
    B = min(SB_TILE, T)
    NB = T // B
    NP = SB_H // SB_HPB

    def body(q_ref, k_ref, v_ref, qg_ref, kg_ref, o_ref, tot_ref, qs_s, kn_s, v_s):
        _sb_fill(q_ref, k_ref, v_ref, qg_ref, kg_ref, qs_s, kn_s, v_s)
        r, c = _iota((B, B), 0), _iota((B, B), 1)
        causal = c < r
        m_after = (r > c).astype(BF16)
        head_b = _sb_head_of((B, SB_W))
        hm = _sb_head_masks(BF16)

        def tile(qb, kj, vj, R, acc, diag):
            z = _dot(qb, kj, NT)
            sp = _softplus(z)
            ls = z - sp
            lm = jnp.where(causal, -sp, 0.0) if diag else -sp
            cs = _dot2r(lm, m_after) + R
            a = jnp.exp(ls + cs)
            if diag:
                a = jnp.where(causal, a, 0.0)
            return R + _rowsum(lm), acc + jnp.sum(_dot(a, vj), axis=0)

        def qblock(i, carry):
            rows_i = pl.ds(pl.multiple_of(i * B, B), B)
            qb = qs_s[rows_i, :][None] * hm

            def step(rows_j, st, diag):
                return tile(qb, kn_s[rows_j, :], v_s[rows_j, :][None] * hm, st[0], st[1], diag)

            def keys(j):
                return pl.ds(pl.multiple_of(j * B, B), B)

            st = step(rows_i, (jnp.zeros((SB_HPB, B, 1), F32), jnp.zeros((B, SB_W), F32)), True)
            st = lax.fori_loop(
                0, i // 2, lambda s, st: step(keys(i - 2 - 2 * s), step(keys(i - 1 - 2 * s), st, False), False), st)
            st = lax.cond(i % 2 == 1, lambda st: step(keys(0), st, False), lambda st: st, st)
            o_ref[rows_i, :] = st[1].astype(o_ref.dtype)
            tot = jnp.zeros((B, SB_W), F32)
            for hh in range(SB_HPB):
                tot = jnp.where(head_b == hh, st[0][hh], tot)
            tot_ref[rows_i, :] = tot
            return carry

        lax.fori_loop(0, NB, qblock, 0)

    blk = lambda k: pl.BlockSpec((T, SB_W), lambda p: (0, k * NP + p))
    small = pl.BlockSpec((1, SB_W), lambda p: (0, 0))
    return _grid_call(
        body, comm, grid=(NP,), in_specs=[blk(0), blk(1), blk(2), small, small],
        out_specs=[pl.BlockSpec((T, SB_W), lambda p: (0, p))] * 2,
        out_shape=[jax.ShapeDtypeStruct((T, D), BF16), jax.ShapeDtypeStruct((T, D), F32)],
        scratch_shapes=[pltpu.VMEM((T, SB_W), BF16)] * 3, name=name, args=(qkv, qkv, qkv, qg, kg))


def _sb_core_bwd(qkv, qg, kg, tot, dout, name, comm=None):
    T = qkv.shape[0]
    B = min(SB_TILE, T)
    NB = T // B
    NP = SB_H // SB_HPB

    def body(q_ref, k_ref, v_ref, qg_ref, kg_ref, tot_ref, do_ref, dq_ref, dk_ref, dv_ref, dqg_ref, dkg_ref,
             qs_s, kn_s, v_s, dqn_s, dkn_s, dvv_s):
        p = pl.program_id(0)
        _sb_fill(q_ref, k_ref, v_ref, qg_ref, kg_ref, qs_s, kn_s, v_s)
        dkn_s[...] = jnp.zeros(dkn_s.shape, F32)
        dvv_s[...] = jnp.zeros(dvv_s.shape, F32)
        r, c = _iota((B, B), 0), _iota((B, B), 1)
        causal = c < r
        m_upto = (r <= c).astype(BF16)
        m_before = (r < c).astype(BF16)
        lane_b = _iota((B, SB_W), 1)
        hm, hmf = _sb_head_masks(BF16), _sb_head_masks(F32)

        def tile(qb, dob, tot_h, kj, vj, PL, P, dq, diag):
            z = _dot(qb, kj, NT)
            sp = _softplus(z)
            ls = z - sp
            lm = jnp.where(causal, -sp, 0.0) if diag else -sp
            cs = tot_h - PL - _dot2r(lm, m_upto)
            a = jnp.exp(ls + cs)
            if diag:
                a = jnp.where(causal, a, 0.0)
            e = _dot(dob, vj, NT) * a
            E = _dot(e, m_before) + P
            sig = jnp.exp(ls)
            dz = e * (1.0 - sig) - E * sig
            if diag:
                dz = jnp.where(causal, dz, 0.0)
            dq = dq + jnp.sum(_dot(dz, kj) * hmf, axis=0)
            return (PL + _rowsum(lm), P + _rowsum(e), dq), jnp.sum(_dot(dz, qb, TN), axis=0), jnp.sum(_dot(a, dob, TN), axis=0)

        def qblock(i, carry):
            rows_i = pl.ds(pl.multiple_of(i * B, B), B)
            totb = tot_ref[rows_i, :]
            tot_h = jnp.stack([_rowsum(jnp.where(lane_b == hh * SB_D, totb, 0.0)) for hh in range(SB_HPB)])
            qb = qs_s[rows_i, :][None] * hm
            dob = do_ref[rows_i, :][None] * hm

            def step(rows_j, st, diag):
                st, dk, dv = tile(qb, dob, tot_h, kn_s[rows_j, :], v_s[rows_j, :][None] * hm, st[0], st[1], st[2], diag)
                dkn_s[rows_j, :] += dk
                dvv_s[rows_j, :] += dv
                return st

            def keys(j):
                return pl.ds(pl.multiple_of(j * B, B), B)

            zero = (jnp.zeros((SB_HPB, B, 1), F32), jnp.zeros((SB_HPB, B, 1), F32), jnp.zeros((B, SB_W), F32))
            st = lax.fori_loop(0, i // 2, lambda p, st: step(keys(2 * p + 1), step(keys(2 * p), st, False), False), zero)
            st = lax.cond(i % 2 == 1, lambda st: step(keys(i - 1), st, False), lambda st: st, st)
            st = step(rows_i, st, True)
            dqn_s[rows_i, :] = st[2] * (SB_D ** -0.5)
            return carry

        lax.fori_loop(0, NB, qblock, 0)

        head = _sb_head_of((T, SB_W))

        def norm_bwd(dn, x_ref, g):
            xh, rr = _sb_head_norm(x_ref[...], head)
            t = dn * g
            return rr * (t - xh * (_sb_head_sums(t * xh, head) * (1.0 / SB_D))), _colsum(dn * xh)

        dq, dqg = norm_bwd(dqn_s[...], q_ref, qg_ref[...])
        dq_ref[...] = dq.astype(dq_ref.dtype)
        dk, dkg = norm_bwd(dkn_s[...], k_ref, kg_ref[...])
        dk_ref[...] = dk.astype(dk_ref.dtype)
        dv_ref[...] = dvv_s[...].astype(dv_ref.dtype)

        @pl.when(p == 0)
        def _():
            dqg_ref[...] = dqg
            dkg_ref[...] = dkg

        @pl.when(p > 0)
        def _():
            dqg_ref[...] += dqg
            dkg_ref[...] += dkg

    blk = lambda k: pl.BlockSpec((T, SB_W), lambda p: (0, k * NP + p))
    small = pl.BlockSpec((1, SB_W), lambda p: (0, 0))
    own = pl.BlockSpec((T, SB_W), lambda p: (0, p))
    outs = _grid_call(
        body, comm, grid=(NP,), in_specs=[blk(0), blk(1), blk(2), small, small, own, own],
        out_specs=[own, own, own, small, small],
        out_shape=[jax.ShapeDtypeStruct((T, D), BF16)] * 3 + [jax.ShapeDtypeStruct((1, SB_W), F32)] * 2,
        scratch_shapes=[pltpu.VMEM((T, SB_W), BF16)] * 3 + [pltpu.VMEM((T, SB_W), F32)] * 3,
        name=name, args=(qkv, qkv, qkv, qg, kg, tot, dout))
    dq, dk, dv, dqg, dkg = outs[:5]
    return jnp.concatenate([dq, dk, dv], axis=1), dqg, dkg, (outs[5] if comm else None)


def _ada_fwd(c16, ada_w, ada_b_cols, name):
    L, _, cols = ada_w.shape

    def body(c_ref, w_ref, b_ref, o_ref):
        o_ref[0] = _dot(_silu(c_ref[...]), w_ref[0]) + b_ref[0]

    return pl.pallas_call(
        body, grid=(L,),
        in_specs=[pl.BlockSpec((16, D), lambda i: (0, 0)), pl.BlockSpec((1, D, cols), lambda i: (i, 0, 0)),
                  pl.BlockSpec((1, 1, cols), lambda i: (i, 0, 0))],
        out_specs=pl.BlockSpec((1, 16, cols), lambda i: (i, 0, 0)),
        out_shape=jax.ShapeDtypeStruct((L, 16, cols), F32), name=name, compiler_params=_params(("arbitrary",)),
    )(c16, ada_w, ada_b_cols)


def _ada_bwd(c16, dmod16, name):
    L, _, cols = dmod16.shape

    def body(c_ref, d_ref, o_ref):
        o_ref[0] = _dot(_silu(c_ref[...]), d_ref[0], TN)

    return pl.pallas_call(
        body, grid=(L,),
        in_specs=[pl.BlockSpec((16, D), lambda i: (0, 0)), pl.BlockSpec((1, 16, cols), lambda i: (i, 0, 0))],
        out_specs=pl.BlockSpec((1, D, cols), lambda i: (i, 0, 0)),
        out_shape=jax.ShapeDtypeStruct((L, D, cols), F32), name=name, compiler_params=_params(("arbitrary",)),
    )(c16, dmod16)


def _sum_sources(x, name):
    n, R, C = x.shape
    tile = _sum_tile(R) or R

    def body(x_ref, o_ref):
        acc = x_ref[0].astype(F32)
        for k in range(1, n):
            acc = acc + x_ref[k].astype(F32)
        o_ref[...] = acc

    return pl.pallas_call(
        body, grid=(R // tile,), in_specs=[pl.BlockSpec((n, tile, C), lambda i: (0, i, 0))],
        out_specs=pl.BlockSpec((tile, C), lambda i: (i, 0)), out_shape=jax.ShapeDtypeStruct((R, C), F32),
        name=name, compiler_params=_params(("parallel",)),
    )(x)


def _adamw(w, g, m, v, name):
    shape = w.shape
    C = shape[-1]
    R = w.size // C
    w2, g2, m2, v2 = (a.reshape(R, C) for a in (w, g, m, v))
    tile = _rtile(R)
    c1 = 1.0 / (1.0 - ADAM_B1 ** ADAM_STEP)
    c2 = 1.0 / (1.0 - ADAM_B2 ** ADAM_STEP)

    def body(w_ref, g_ref, m_ref, v_ref, d_ref, nm_ref, nv_ref):
        gv = g_ref[...]
        nm = ADAM_B1 * m_ref[...] + (1.0 - ADAM_B1) * gv
        nv = ADAM_B2 * v_ref[...] + (1.0 - ADAM_B2) * (gv * gv)
        d_ref[...] = -ADAM_LR * ((nm * c1) / (jnp.sqrt(nv * c2) + ADAM_EPS) + ADAM_WD * w_ref[...])
        nm_ref[...] = nm
        nv_ref[...] = nv

    spec = pl.BlockSpec((tile, C), lambda i: (i, 0))
    outs = pl.pallas_call(
        body, grid=(R // tile,), in_specs=[spec] * 4, out_specs=[spec] * 3,
        out_shape=[jax.ShapeDtypeStruct((R, C), F32)] * 3, name=name, compiler_params=_params(("parallel",)),
    )(w2, g2, m2, v2)
    return tuple(o.reshape(shape) for o in outs)


def _gather_steps(x_ref, out_ref, send_sems, recv_sems, local_sem):
    mx, my, mc = lax.axis_index("x"), lax.axis_index("y"), lax.axis_index("c")
    me, sibling = (mx, my, mc), (mx, my, 1 - mc)
    chips = [(1 - mx, my), (mx, 1 - my), (1 - mx, 1 - my)]

    def slot(px, py, pc):
        return out_ref.at[4 * px + 2 * py + pc]

    def copy(k, block, to, src=None):
        return pltpu.make_async_remote_copy(
            src_ref=slot(*block) if src is None else src, dst_ref=slot(*block),
            send_sem=send_sems.at[k], recv_sem=recv_sems.at[k], device_id=to, device_id_type=MESH)

    mine = pltpu.make_async_copy(x_ref, slot(*me), local_sem)
    first = [copy(0, me, sibling, src=x_ref)] + [copy(1 + j, me, (*chip, mc), src=x_ref) for j, chip in enumerate(chips)]
    passed = [copy(4 + j, (*chip, mc), sibling) for j, chip in enumerate(chips)]

    def start():
        mine.start()
        for cp in first:
            cp.start()

    def forward():
        for j, chip in enumerate(chips):
            copy(1 + j, (*chip, mc), me).wait_recv()
            passed[j].start()

    def finish():
        copy(0, sibling, me).wait_recv()
        for j, chip in enumerate(chips):
            copy(4 + j, (*chip, 1 - mc), me).wait_recv()
        for cp in first + passed:
            cp.wait_send()
        mine.wait()

    return start, forward, finish


def _exchange_steps(x_ref, out_ref, send_sems, recv_sems, local_sem):
    mx, my, mc = lax.axis_index("x"), lax.axis_index("y"), lax.axis_index("c")
    me = 4 * mx + 2 * my + mc
    mine = pltpu.make_async_copy(x_ref.at[me], out_ref.at[me], local_sem)
    copies = []
    for k in range(1, NDEV):
        px, py, pc = mx ^ (k >> 2), my ^ ((k >> 1) & 1), mc ^ (k & 1)
        copies.append(pltpu.make_async_remote_copy(
            src_ref=x_ref.at[4 * px + 2 * py + pc], dst_ref=out_ref.at[me], send_sem=send_sems.at[k - 1],
            recv_sem=recv_sems.at[k - 1], device_id=(px, py, pc), device_id_type=MESH))

    def start():
        mine.start()
        for cp in copies:
            cp.start()

    def finish():
        for cp in copies:
            cp.wait_recv()
        for cp in copies:
            cp.wait_send()
        mine.wait()

    return start, finish


def _comm_sems():
    return [pltpu.SemaphoreType.DMA((7,)), pltpu.SemaphoreType.DMA((7,)), pltpu.SemaphoreType.DMA]


def _comm_out_shapes(comm):
    kind, xs = comm
    return [jax.ShapeDtypeStruct(((NDEV,) + x.shape) if kind == "gather" else x.shape, x.dtype) for x in xs]


def _ride_along(comm, step, n_steps, refs, at_end):
    if comm[0] == "gather":
        start, forward, finish = _gather_steps(*refs)
        todo = [(n_steps - 1, finish)] if at_end else [(0, start), (n_steps - 1, forward)]
    else:
        start, finish = _exchange_steps(*refs)
        todo = [(n_steps - 1, finish)] if at_end else [(0, start)]
    for at, fn in todo:
        pl.when(step == at)(fn)


def _grid_call(body, comm, *, grid, in_specs, out_specs, out_shape, scratch_shapes, name, args):
    if comm is None:
        return pl.pallas_call(body, grid=grid, in_specs=in_specs, out_specs=out_specs, out_shape=out_shape,
                              scratch_shapes=scratch_shapes, name=name, compiler_params=_params(("arbitrary",)))(*args)
    n_in, n_out, n_scr, n_steps, n_parts = len(in_specs), len(out_specs), len(scratch_shapes), grid[0], len(comm[1])

    def with_comm(*refs):
        ins, xs = refs[:n_in], refs[n_in:n_in + n_parts]
        outs, gots = refs[n_in + n_parts:n_in + n_parts + n_out], refs[n_in + n_parts + n_out:n_in + 2 * n_parts + n_out]
        scr, sems = refs[n_in + 2 * n_parts + n_out:][:n_scr], refs[n_in + 2 * n_parts + n_out + n_scr:]
        part_refs = [(xs[p], gots[p]) + tuple(sems[3 * p:3 * p + 3]) for p in range(n_parts)]
        step = pl.program_id(0)
        for pr in part_refs:
            _ride_along(comm, step, n_steps, pr, False)
        body(*ins, *outs, *scr)
        for pr in part_refs:
            _ride_along(comm, step, n_steps, pr, True)

    hbm = pl.BlockSpec(memory_space=pl.ANY)
    res = pl.pallas_call(
        with_comm, grid=grid, in_specs=list(in_specs) + [hbm] * n_parts, out_specs=list(out_specs) + [hbm] * n_parts,
        out_shape=list(out_shape) + _comm_out_shapes(comm), scratch_shapes=list(scratch_shapes) + _comm_sems() * n_parts,
        name=name, compiler_params=_params(("arbitrary",)))(*args, *comm[1])
    return list(res[:n_out]) + [list(res[n_out:])]


def _all_gather(x, name, in_vmem):
    def body(x_ref, out_ref, send_sems, recv_sems, local_sem):
        start, forward, finish = _gather_steps(x_ref, out_ref, send_sems, recv_sems, local_sem)
        start()
        forward()
        finish()

    space = pltpu.VMEM if in_vmem else pl.ANY
    return pl.pallas_call(
        body, out_shape=jax.ShapeDtypeStruct((NDEV,) + x.shape, x.dtype),
        in_specs=[pl.BlockSpec(memory_space=space)], out_specs=pl.BlockSpec(memory_space=space),
        scratch_shapes=_comm_sems(), name=name, compiler_params=pltpu.CompilerParams(vmem_limit_bytes=VMEM_LIMIT),
    )(x)


def _all_to_all(x, name):
    def body(x_ref, out_ref, send_sems, recv_sems, local_sem):
        start, finish = _exchange_steps(x_ref, out_ref, send_sems, recv_sems, local_sem)
        start()
        finish()

    return pl.pallas_call(
        body, out_shape=jax.ShapeDtypeStruct(x.shape, x.dtype),
        in_specs=[pl.BlockSpec(memory_space=pl.ANY)], out_specs=pl.BlockSpec(memory_space=pl.ANY),
        scratch_shapes=_comm_sems(), name=name, compiler_params=pltpu.CompilerParams(vmem_limit_bytes=VMEM_LIMIT),
    )(x)


def _stage_parts(st):
    parts = []
    if st >= 1:
        parts += [("ffn_w_in", st - 1, (D, 2 * DFF // NDEV), "colsT"), ("ffn_w_out", st - 1, (DFF // NDEV, D), "rows")]
    if st < DEPTH:
        j = st // 2
        if st % 2 == 0:
            parts += [("dn_w_in", j, (D, DN_COLS // NDEV), "colsT"), ("dn_w_out", j, (D // NDEV, D), "rows")]
        else:
            parts += [("sb_w_qkv", j, (D, 3 * D // NDEV), "colsT"), ("sb_w_out", j, (D // NDEV, D), "rows")]
    return parts


def _sum_tile(rows):
    for t in range(512, 191, -16):
        if rows % t == 0:
            return t
    return None


def _part_rows(s):
    return s[0] * s[1] // D


def _pack_stage(w, st):
    return [(w[n][j].T if kind == "colsT" else w[n][j]).astype(BF16).reshape(_part_rows(s), D)
            for n, j, s, kind in _stage_parts(st)]


def _full_stage(gs, st):
    out = {n: g.reshape((NDEV * s[0], s[1]) if kind == "rows" else (NDEV * s[1], s[0]))
           for g, (n, _, s, kind) in zip(gs, _stage_parts(st))}
    if "dn_w_in" in out:
        out["dn_w_in"] = jnp.pad(out["dn_w_in"], ((0, DN_COLS_PAD - DN_COLS), (0, 0)))
    return out


def _shards_stage(full, st):
    return [full[n].astype(BF16).reshape(NDEV, _part_rows(s), D) for n, _, s, _ in _stage_parts(st)]


def _unpack_stage(bufs, st):
    return {(n, j): b.reshape(s[1], s[0]).T if kind == "colsT" else b.reshape(s)
            for b, (n, j, s, kind) in zip(bufs, _stage_parts(st))}


def _local_step(x, target, mod, small, w0, rest):
    dist = isinstance(rest, tuple)
    packs = rest[1] if dist else None
    W = [dict(w0), {}, {}, {}] if dist else [w0] + list(rest)

    def arrived(k, g):
        for n, a in _full_stage(g, k + 1).items():
            W[k if n.startswith("ffn") else k + 1][n] = a

    row = lambda v: v.reshape(1, -1)
    pad128 = lambda v: jnp.pad(v.reshape(1, -1), ((0, 0), (0, LANES - v.size)))
    saved = []
    mods = [[row(mod[i, k * D:(k + 1) * D]) for k in range(N_MOD)] for i in range(DEPTH)]
    h1 = _norm_mod(x, row(small["norm1_g"][0]), mods[0][1], mods[0][0], "norm1_0")
    for i in range(DEPTH):
        j = i // 2
        w = W[i]
        m = mods[i]
        sh1, sc1, gt1, sh2, sc2, gt2 = m
        g1, g2 = row(small["norm1_g"][i]), row(small["norm2_g"][i])
        if i % 2 == 0:
            proj = _mm(h1, w["dn_w_in"], "nt", F32, f"dn_proj_{i}")
            alog, dtb, og = pad128(small["dn_a_log"][j]), pad128(small["dn_dt_bias"][j]), row(small["dn_onorm_g"][j])
            comm = ("gather", packs[i]) if dist else None
            res = _dn_core_fwd(proj, small["dn_conv_w"][j], alog, dtb, og, f"dn_core_{i}", comm)
            om, o_pre, states = res[:3]
            if comm:
                arrived(i, res[3])
            y1 = _mm(om, w["dn_w_out"], "nn", F32, f"dn_out_{i}")
            mix = (proj, alog, dtb, og, o_pre, states, om)
        else:
            qkv = _mm(h1, w["sb_w_qkv"], "nt", F32, f"sb_qkv_{i}")
            qg2 = jnp.tile(row(small["sb_q_norm_g"][j]), (1, SB_HPB))
            kg2 = jnp.tile(row(small["sb_k_norm_g"][j]), (1, SB_HPB))
            comm = ("gather", packs[i]) if dist else None
            res = _sb_core_fwd(qkv, qg2, kg2, f"sb_core_{i}", comm)
            om, tot = res[:2]
            if comm:
                arrived(i, res[2])
            y1 = _mm(om, w["sb_w_out"], "nn", F32, f"sb_out_{i}")
            mix = (qkv, qg2, kg2, tot, om)
        x_mid, h2 = _gate_res_norm(x, y1, gt1, g2, sc2, sh2, f"res1_{i}")
        u = _mm(h2, w["ffn_w_in"], "nt", BF16, f"ffn_in_{i}")
        a = _swiglu_act(u, f"ffn_act_{i}")
        y2 = _mm(a, w["ffn_w_out"], "nn", F32, f"ffn_out_{i}")
        saved.append((x, h1, mix, y1, x_mid, h2, u, a, y2, m, g1, g2))
        if i + 1 < DEPTH:
            x, h1 = _gate_res_norm(x_mid, y2, gt2, row(small["norm1_g"][i + 1]), mods[i + 1][1], mods[i + 1][0],
                                   f"res2_{i}")
        else:
            x = _gate_res(x_mid, y2, gt2, f"res2_{i}")

    dx, sq = _loss_head(x, target, "loss_head")

    sg = dict(mod=[None] * DEPTH, norm1_g=[None] * DEPTH, norm2_g=[None] * DEPTH, dn_conv_w=[None] * 2,
              dn_a_log=[None] * 2, dn_dt_bias=[None] * 2, dn_onorm_g=[None] * 2, sb_q_norm_g=[None] * 2,
              sb_k_norm_g=[None] * 2)
    big = [None] * (DEPTH + 1 if dist else DEPTH)
    mixer_gw = {}
    for i in reversed(range(DEPTH)):
        j = i // 2
        w = W[i]
        x_in, h1, mix, y1, x_mid, h2, u, a, y2, m, g1, g2 = saved[i]
        sh1, sc1, gt1, sh2, sc2, gt2 = m
        gw = {}
        dy2, dgt2 = _gate_res_bwd(dx, y2, gt2, f"res2_bwd_{i}")
        da = _mm(dy2, w["ffn_w_out"], "nt", BF16, f"ffn_out_bwd_{i}")
        gw["ffn_w_out"] = _mm(a, dy2, "tn", BF16, f"ffn_out_wg_{i}")
        du = _swiglu_act_bwd(da, u, f"ffn_act_bwd_{i}")
        dh2 = _mm(du, w["ffn_w_in"], "nn", F32, f"ffn_in_bwd_{i}")
        gw["ffn_w_in"] = _mm(du, h2, "tn", BF16, f"ffn_in_wg_{i}")
        dx_mid, s2, dsh2 = _norm_mod_bwd(dh2, x_mid, dx, g2, sc2, f"norm2_bwd_{i}")
        dy1, dgt1 = _gate_res_bwd(dx_mid, y1, gt1, f"res1_bwd_{i}")
        comm = ("exchange", _shards_stage({**gw, **mixer_gw}, i + 1)) if dist else None
        if i % 2 == 0:
            proj, alog, dtb, og, o_pre, states, om = mix
            dom = _mm(dy1, w["dn_w_out"], "nt", F32, f"dn_out_bwd_{i}")
            gw["dn_w_out"] = _mm(om, dy1, "tn", BF16, f"dn_out_wg_{i}")
            dproj, dconv, dalog, ddtb, dog, got = _dn_core_bwd(proj, small["dn_conv_w"][j], alog, dtb, og, o_pre, states,
                                                               dom, f"dn_core_bwd_{i}", comm)
            dh1 = _mm(dproj, w["dn_w_in"], "nn", F32, f"dn_proj_bwd_{i}")
            gw["dn_w_in"] = _mm(dproj, h1, "tn", BF16, f"dn_proj_wg_{i}")[:DN_COLS]
            sg["dn_conv_w"][j] = dconv
            sg["dn_a_log"][j] = dalog[0, :DN_H]
            sg["dn_dt_bias"][j] = ddtb[0, :DN_H]
            sg["dn_onorm_g"][j] = dog[0]
        else:
            qkv, qg2, kg2, tot, om = mix
            dom = _mm(dy1, w["sb_w_out"], "nt", BF16, f"sb_out_bwd_{i}")
            gw["sb_w_out"] = _mm(om, dy1, "tn", BF16, f"sb_out_wg_{i}")
            dqkv, dqg, dkg, got = _sb_core_bwd(qkv, qg2, kg2, tot, dom, f"sb_core_bwd_{i}", comm)
            dh1 = _mm(dqkv, w["sb_w_qkv"], "nn", F32, f"sb_qkv_bwd_{i}")
            gw["sb_w_qkv"] = _mm(dqkv, h1, "tn", BF16, f"sb_qkv_wg_{i}")
            sg["sb_q_norm_g"][j] = jnp.sum(dqg.reshape(SB_HPB, SB_D), axis=0)
            sg["sb_k_norm_g"][j] = jnp.sum(dkg.reshape(SB_HPB, SB_D), axis=0)
        if comm:
            big[i + 1] = got
        dx, s1, dsh1 = _norm_mod_bwd(dh1, x_in, dx_mid, g1, sc1, f"norm1_bwd_{i}")
        sg["mod"][i] = jnp.concatenate([dsh1, s1 * g1, dgt1, dsh2, s2 * g2, dgt2], axis=1)[0]
        sg["norm1_g"][i] = (s1 * (1.0 + sc1))[0]
        sg["norm2_g"][i] = (s2 * (1.0 + sc2))[0]
        if dist:
            mixer_gw = {n: a for n, a in gw.items() if not n.startswith("ffn")}
        else:
            big[i] = gw
    if dist:
        big[0] = [_all_to_all(a, f"exchange_grads_0_{p}") for p, a in enumerate(_shards_stage(mixer_gw, 0))]
    return sq, dx, {k: jnp.stack(v) for k, v in sg.items()}, big


def _device_index():
    return 4 * lax.axis_index("x") + 2 * lax.axis_index("y") + lax.axis_index("c")


def _gather_small(w, c):
    me = _device_index()
    ada_cols = w["ada_w"].shape[-1]
    conv_cols = w["dn_conv_w"].shape[-1]
    blk = jnp.concatenate([c.reshape(1, D), w["dn_conv_w"].reshape(-1, D)], axis=0)
    g1 = _all_gather(blk, "gather_cond", True)
    c16 = jnp.pad(g1[:, 0, :], ((0, 8), (0, 0)))
    conv_full = jnp.transpose(g1[:, 1:, :].reshape(NDEV, 2, DN_CONV, conv_cols), (1, 2, 0, 3)).reshape(2, DN_CONV, -1)
    b_cols = lax.dynamic_slice_in_dim(w["ada_b"], me * ada_cols, ada_cols, axis=1).reshape(DEPTH, 1, ada_cols)
    mod_part = _ada_fwd(c16, w["ada_w"], b_cols, "ada_fwd")[:, :NDEV, :]
    g2 = _all_gather(mod_part.reshape(DEPTH * NDEV, ada_cols), "gather_mod", True)
    g2 = g2.reshape(NDEV, DEPTH, NDEV, ada_cols)
    mod = lax.dynamic_index_in_dim(g2, me, axis=2, keepdims=False)
    mod = jnp.transpose(mod, (1, 0, 2)).reshape(DEPTH, N_MOD * D)
    return c16, conv_full, mod


def _reduce_small(gr, c16):
    me = _device_index()
    ada_cols = N_MOD * D // NDEV
    conv_cols = 3 * DN_H * DN_D // NDEV
    grads = {}
    small = jnp.concatenate([gr["dn_a_log"].reshape(-1), gr["dn_dt_bias"].reshape(-1), gr["dn_onorm_g"].reshape(-1),
                             gr["sb_q_norm_g"].reshape(-1), gr["sb_k_norm_g"].reshape(-1)])
    small = jnp.pad(small, (0, D - small.size)).reshape(1, D)
    blk3 = jnp.concatenate([gr["mod"].reshape(-1, D), gr["norm1_g"], gr["norm2_g"], gr["dn_conv_w"].reshape(-1, D),
                            small], axis=0)
    blk3 = jnp.pad(blk3, ((0, 64 - blk3.shape[0]), (0, 0)))
    g3 = _all_gather(blk3, "gather_small_grads", True)
    tot = _sum_sources(g3, "sum_small_grads")
    grads["ada_b"] = tot[:24].reshape(DEPTH, N_MOD * D)
    grads["norm1_g"] = tot[24:28]
    grads["norm2_g"] = tot[28:32]
    conv_g = tot[32:56].reshape(2, DN_CONV, NDEV * conv_cols)
    grads["dn_conv_w"] = lax.dynamic_slice_in_dim(conv_g, me * conv_cols, conv_cols, axis=2)
    sm = tot[56]
    grads["dn_a_log"] = sm[0:16].reshape(2, DN_H)
    grads["dn_dt_bias"] = sm[16:32].reshape(2, DN_H)
    grads["dn_onorm_g"] = sm[32:288].reshape(2, DN_D)
    grads["sb_q_norm_g"] = sm[288:416].reshape(2, SB_D)
    grads["sb_k_norm_g"] = sm[416:544].reshape(2, SB_D)
    dmod_all = g3[:, :24, :].reshape(NDEV, DEPTH, N_MOD * D)
    dmod_cols = lax.dynamic_slice_in_dim(dmod_all, me * ada_cols, ada_cols, axis=2)
    dmod16 = jnp.pad(jnp.transpose(dmod_cols, (1, 0, 2)), ((0, 0), (0, 8), (0, 0)))
    grads["ada_w"] = _ada_bwd(c16, dmod16, "ada_bwd")
    return grads


def _reduce_big(recv):
    parts = {}
    for st in range(DEPTH + 1):
        parts.update(_unpack_stage([_sum_sources(r, f"sum_grads_{st}_{p}") for p, r in enumerate(recv[st])], st))
    out = {}
    for (n, j) in sorted(parts):
        out.setdefault(n, []).append(parts[(n, j)])
    return {n: jnp.stack(v) for n, v in out.items()}


def kernel(x, c, ada_w, ada_b, norm1_g, norm2_g, dn_w_in, dn_conv_w, dn_a_log, dn_dt_bias, dn_onorm_g, dn_w_out, sb_w_qkv, sb_q_norm_g, sb_k_norm_g, sb_w_out, ffn_w_in, ffn_w_out, loss_target, m_ada_w, m_ada_b, m_norm1_g, m_norm2_g, m_dn_w_in, m_dn_conv_w, m_dn_a_log, m_dn_dt_bias, m_dn_onorm_g, m_dn_w_out, m_sb_w_qkv, m_sb_q_norm_g, m_sb_k_norm_g, m_sb_w_out, m_ffn_w_in, m_ffn_w_out, v_ada_w, v_ada_b, v_norm1_g, v_norm2_g, v_dn_w_in, v_dn_conv_w, v_dn_a_log, v_dn_dt_bias, v_dn_onorm_g, v_dn_w_out, v_sb_w_qkv, v_sb_q_norm_g, v_sb_k_norm_g, v_sb_w_out, v_ffn_w_in, v_ffn_w_out):
    w = dict(ada_w=ada_w, ada_b=ada_b, norm1_g=norm1_g, norm2_g=norm2_g, dn_w_in=dn_w_in, dn_conv_w=dn_conv_w,
             dn_a_log=dn_a_log, dn_dt_bias=dn_dt_bias, dn_onorm_g=dn_onorm_g, dn_w_out=dn_w_out, sb_w_qkv=sb_w_qkv,
             sb_q_norm_g=sb_q_norm_g, sb_k_norm_g=sb_k_norm_g, sb_w_out=sb_w_out, ffn_w_in=ffn_w_in, ffn_w_out=ffn_w_out)
    mom = dict(ada_w=m_ada_w, ada_b=m_ada_b, norm1_g=m_norm1_g, norm2_g=m_norm2_g, dn_w_in=m_dn_w_in,
               dn_conv_w=m_dn_conv_w, dn_a_log=m_dn_a_log, dn_dt_bias=m_dn_dt_bias, dn_onorm_g=m_dn_onorm_g,
               dn_w_out=m_dn_w_out, sb_w_qkv=m_sb_w_qkv, sb_q_norm_g=m_sb_q_norm_g, sb_k_norm_g=m_sb_k_norm_g,
               sb_w_out=m_sb_w_out, ffn_w_in=m_ffn_w_in, ffn_w_out=m_ffn_w_out)
    var = dict(ada_w=v_ada_w, ada_b=v_ada_b, norm1_g=v_norm1_g, norm2_g=v_norm2_g, dn_w_in=v_dn_w_in,
               dn_conv_w=v_dn_conv_w, dn_a_log=v_dn_a_log, dn_dt_bias=v_dn_dt_bias, dn_onorm_g=v_dn_onorm_g,
               dn_w_out=v_dn_w_out, sb_w_qkv=v_sb_w_qkv, sb_q_norm_g=v_sb_q_norm_g, sb_k_norm_g=v_sb_k_norm_g,
               sb_w_out=v_sb_w_out, ffn_w_in=v_ffn_w_in, ffn_w_out=v_ffn_w_out)
    names = list(w)
    c16, conv_full, mod = _gather_small(w, c)
    packs = [_pack_stage(w, st) for st in range(DEPTH + 1)]
    w0 = _full_stage([_all_gather(a, f"gather_weights_0_{p}", False) for p, a in enumerate(packs[0])], 0)
    small = dict(norm1_g=norm1_g, norm2_g=norm2_g, dn_conv_w=conv_full, dn_a_log=dn_a_log, dn_dt_bias=dn_dt_bias,
                 dn_onorm_g=dn_onorm_g, sb_q_norm_g=sb_q_norm_g, sb_k_norm_g=sb_k_norm_g)
    sq, grad_x, sgr, recv = _local_step(x[0], loss_target[0], mod, small, w0, ("packed", packs[1:]))
    loss = lax.psum(sq[0, 0] * (0.5 / D), ("x", "y", "c"))
    grads = {**_reduce_big(recv), **_reduce_small(sgr, c16)}
    delta, new_m, new_v = {}, {}, {}
    for n in names:
        delta[n], new_m[n], new_v[n] = _adamw(w[n], grads[n], mom[n], var[n], f"adamw_{n}")
    return (loss, grad_x[None], *[grads[n] for n in names], *[delta[n] for n in names],
            *[new_m[n] for n in names], *[new_v[n] for n in names])
```

```python
import jax
import jax.numpy as jnp
from jax import lax
from jax.experimental import pallas as pl
from jax.experimental.pallas import tpu as pltpu

F32, BF16 = jnp.float32, jnp.bfloat16

D = 1024
DEPTH = 4
N_MOD = 6
DN_H, DN_D, DN_C, DN_CONV = 8, 128, 64, 4
DN_UNROLL_FWD, DN_UNROLL_BWD = 16, 8
DN_COLS = 4 * DN_H * DN_D + 2 * DN_H
DN_COLS_PAD = 33 * 128
SB_H, SB_D = 16, 64
SB_TILE = 256
SB_HPB = 4
SB_W = SB_HPB * SB_D
DFF = 2816
EPS = 1e-6
NDEV = 8
LANES = 128
VMEM_LIMIT = 56 * 1024 * 1024

ADAM_LR, ADAM_B1, ADAM_B2, ADAM_EPS, ADAM_WD, ADAM_STEP = 0.001, 0.9, 0.999, 1e-08, 0.01, 10

NN = ((1,), (0,))
NT = ((1,), (1,))
TN = ((0,), (0,))
MESH = pl.DeviceIdType.MESH


def _dot(a, b, dims=NN):
    a, b = a.astype(BF16), b.astype(BF16)
    if a.ndim == 2 and b.ndim == 2:
        return lax.dot_general(a, b, (dims, ((), ())), preferred_element_type=F32)
    n = a.shape[0] if a.ndim == 3 else b.shape[0]
    if a.ndim == 2:
        a = jnp.broadcast_to(a, (n,) + a.shape)
    if b.ndim == 2:
        b = jnp.broadcast_to(b, (n,) + b.shape)
    (ca,), (cb,) = dims
    return lax.dot_general(a, b, (((ca + 1,), (cb + 1,)), ((0,), (0,))), preferred_element_type=F32)


def _split(a):
    hi = a.astype(BF16)
    lo = (a - hi.astype(F32)).astype(BF16)
    return hi, lo


def _dot2r(a, m, dims=NN):
    ah, al = _split(a)
    return _dot(ah, m, dims) + _dot(al, m, dims)


def _dot2l(m, b, dims=NN):
    bh, bl = _split(b)
    return _dot(m, bh, dims) + _dot(m, bl, dims)


def _sigmoid(x):
    return 1.0 / (1.0 + jnp.exp(-x))


def _silu(x):
    return x * _sigmoid(x)


def _dsilu(x):
    s = _sigmoid(x)
    return s * (1.0 + x * (1.0 - s))


def _softplus(x):
    return jnp.maximum(x, 0.0) + jnp.log(1.0 + jnp.exp(-jnp.abs(x)))


def _iota(shape, dim):
    return lax.broadcasted_iota(jnp.int32, shape, dim)


def _rowsum(x):
    return jnp.sum(x, axis=-1, keepdims=True)


def _colsum(x):
    return jnp.sum(x, axis=-2, keepdims=True)


def _tile(n, pref):
    if n <= pref:
        return n
    best = None
    for t in range(LANES, pref + 1, LANES):
        if n % t == 0:
            best = t
    assert best is not None, (n, pref)
    return best


def _rtile(r, pref=512):
    best = None
    for t in range(8, min(r, pref) + 1, 8):
        if r % t == 0:
            best = t
    return best if best is not None else r


def _params(sem):
    return pltpu.CompilerParams(dimension_semantics=sem, vmem_limit_bytes=VMEM_LIMIT)


MM_TILE = 1408


def _mm(a, b, mode, out_dtype, name):
    if mode == "nn":
        (M, K), (K2, N) = a.shape, b.shape
        dims = NN
    elif mode == "nt":
        (M, K), (N, K2) = a.shape, b.shape
        dims = NT
    else:
        (K, M), (K2, N) = a.shape, b.shape
        dims = TN
    assert K == K2, (a.shape, b.shape, mode)
    tm, tn, tk = _tile(M, MM_TILE), _tile(N, MM_TILE), _tile(K, MM_TILE)
    nk = K // tk

    def body_single(a_ref, b_ref, o_ref):
        o_ref[...] = _dot(a_ref[...], b_ref[...], dims).astype(o_ref.dtype)

    def body_acc(a_ref, b_ref, o_ref, acc_ref):
        k = pl.program_id(2)
        p = _dot(a_ref[...], b_ref[...], dims)

        @pl.when(k == 0)
        def _():
            acc_ref[...] = p

        @pl.when(k > 0)
        def _():
            acc_ref[...] += p

        @pl.when(k == nk - 1)
        def _():
            o_ref[...] = acc_ref[...].astype(o_ref.dtype)

    if mode == "nn":
        a_spec = pl.BlockSpec((tm, tk), lambda i, j, k: (i, k))
        b_spec = pl.BlockSpec((tk, tn), lambda i, j, k: (k, j))
    elif mode == "nt":
        a_spec = pl.BlockSpec((tm, tk), lambda i, j, k: (i, k))
        b_spec = pl.BlockSpec((tn, tk), lambda i, j, k: (j, k))
    else:
        a_spec = pl.BlockSpec((tk, tm), lambda i, j, k: (k, i))
        b_spec = pl.BlockSpec((tk, tn), lambda i, j, k: (k, j))
    return pl.pallas_call(
        body_single if nk == 1 else body_acc, grid=(M // tm, N // tn, nk), in_specs=[a_spec, b_spec],
        out_specs=pl.BlockSpec((tm, tn), lambda i, j, k: (i, j)),
        out_shape=jax.ShapeDtypeStruct((M, N), out_dtype),
        scratch_shapes=[] if nk == 1 else [pltpu.VMEM((tm, tn), F32)], name=name,
        compiler_params=_params(("parallel", "parallel", "arbitrary")),
    )(a, b)


def _rowwise(fn, name, rows, bcasts, out_rows, out_reds=(), tile=256):
    T = rows[0].shape[0]
    tile = min(tile, T)
    nr, nb, no = len(rows), len(bcasts), len(out_rows)

    def body(*refs):
        rv = [r[...] for r in refs[:nr]]
        bv = [r[...] for r in refs[nr:nr + nb]]
        outs, reds = fn(rv, bv)
        for r, o in zip(refs[nr + nb:nr + nb + no], outs):
            r[...] = o.astype(r.dtype)
        red_refs = refs[nr + nb + no:]
        if red_refs:
            @pl.when(pl.program_id(0) == 0)
            def _():
                for r in red_refs:
                    r[...] = jnp.zeros(r.shape, F32)

            for r, v in zip(red_refs, reds):
                r[...] += v

    in_specs = [pl.BlockSpec((tile, a.shape[1]), lambda i: (i, 0)) for a in rows]
    in_specs += [pl.BlockSpec(b.shape, lambda i: (0, 0)) for b in bcasts]
    out_specs = [pl.BlockSpec((tile, c), lambda i: (i, 0)) for c, _ in out_rows]
    out_specs += [pl.BlockSpec(s, lambda i: (0, 0)) for s in out_reds]
    out_shape = [jax.ShapeDtypeStruct((T, c), dt) for c, dt in out_rows]
    out_shape += [jax.ShapeDtypeStruct(s, F32) for s in out_reds]
    return pl.pallas_call(
        body, grid=(T // tile,), in_specs=in_specs, out_specs=out_specs, out_shape=out_shape, name=name,
        compiler_params=_params(("arbitrary",)),
    )(*rows, *bcasts)


def _norm_mod(x, g, sc, sh, name):
    def fn(rv, bv):
        (xv,), (gv, scv, shv) = rv, bv
        r = lax.rsqrt(jnp.mean(xv * xv, axis=1, keepdims=True) + EPS)
        return [(xv * r * gv) * (1.0 + scv) + shv], []
    return _rowwise(fn, name, [x], [g, sc, sh], [(D, BF16)])[0]


def _norm_mod_bwd(dh, x, dres, g, sc, name):
    def fn(rv, bv):
        (dhv, xv, drv), (gv, scv) = rv, bv
        r = lax.rsqrt(jnp.mean(xv * xv, axis=1, keepdims=True) + EPS)
        xhat = xv * r
        dxhat = dhv * (gv * (1.0 + scv))
        dx = r * (dxhat - xhat * jnp.mean(dxhat * xhat, axis=1, keepdims=True)) + drv
        return [dx], [_colsum(dhv * xhat), _colsum(dhv)]
    return _rowwise(fn, name, [dh, x, dres], [g, sc], [(D, F32)], [(1, D), (1, D)])


def _gate_res(x, y, gt, name):
    def fn(rv, bv):
        return [rv[0] + bv[0] * rv[1]], []
    return _rowwise(fn, name, [x, y], [gt], [(D, F32)])[0]


def _gate_res_norm(x, y, gt, g, sc, sh, name):
    def fn(rv, bv):
        (xv, yv), (gtv, gv, scv, shv) = rv, bv
        xn = xv + gtv * yv
        r = lax.rsqrt(jnp.mean(xn * xn, axis=1, keepdims=True) + EPS)
        return [xn, (xn * r * gv) * (1.0 + scv) + shv], []
    return _rowwise(fn, name, [x, y], [gt, g, sc, sh], [(D, F32), (D, BF16)])


def _gate_res_bwd(dxn, y, gt, name):
    def fn(rv, bv):
        return [rv[0] * bv[0]], [_colsum(rv[0] * rv[1])]
    return _rowwise(fn, name, [dxn, y], [gt], [(D, BF16)], [(1, D)])


def _swiglu_act(u, name):
    def fn(rv, bv):
        uv = rv[0].astype(F32)
        return [_silu(uv[:, :DFF]) * uv[:, DFF:]], []
    return _rowwise(fn, name, [u], [], [(DFF, BF16)])[0]


def _swiglu_act_bwd(da, u, name):
    def fn(rv, bv):
        dav, uv = rv[0].astype(F32), rv[1].astype(F32)
        gate, up = uv[:, :DFF], uv[:, DFF:]
        return [jnp.concatenate([dav * up * _dsilu(gate), dav * _silu(gate)], axis=1)], []
    return _rowwise(fn, name, [da, u], [], [(2 * DFF, BF16)])[0]


def _loss_head(y, target, name):
    def fn(rv, bv):
        err = rv[0] - rv[1]
        return [err * (1.0 / D)], [_colsum(_rowsum(err * err))]
    return _rowwise(fn, name, [y, target], [], [(D, F32)], [(1, 1)])


def _shift_rows(x, s):
    if s == 0:
        return x
    T = x.shape[0]
    r = pltpu.roll(x, s % T, axis=0)
    t = _iota(x.shape, 0)
    keep = (t >= s) if s > 0 else (t < T + s)
    return jnp.where(keep, r, 0.0)


def _dn_prep(pq, pk, pv, pab, cq_ref, ck_ref, cv_ref, alog, dtb, h):
    lane = _iota(pab.shape, 1)
    a_col = _rowsum(jnp.where(lane == h, pab, 0.0))
    b_col = _rowsum(jnp.where(lane == DN_H + h, pab, 0.0))
    lane1 = _iota(alog.shape, 1)
    alog_h = _rowsum(jnp.where(lane1 == h, alog, 0.0))
    dtb_h = _rowsum(jnp.where(lane1 == h, dtb, 0.0))
    pre = a_col + dtb_h
    neg_ea = -jnp.exp(alog_h)
    g = neg_ea * _softplus(pre)
    beta = _sigmoid(b_col)

    def conv(x, w_ref):
        acc = x * w_ref[DN_CONV - 1:DN_CONV, :]
        for i in range(DN_CONV - 1):
            acc = acc + _shift_rows(x, DN_CONV - 1 - i) * w_ref[i:i + 1, :]
        return acc

    xq, xk, xv = conv(pq, cq_ref), conv(pk, ck_ref), conv(pv, cv_ref)
    sq, sk, v = _silu(xq), _silu(xk), _silu(xv)
    rq = lax.rsqrt(_rowsum(sq * sq) + EPS)
    rk = lax.rsqrt(_rowsum(sk * sk) + EPS)
    return dict(g=g, beta=beta, pre=pre, neg_ea=neg_ea, xq=xq, xk=xk, xv=xv, rq=rq, rk=rk,
                qn=sq * rq, kn=sk * rk, v=v)


def _dn_masks():
    C = DN_C
    r, c = _iota((C, C), 0), _iota((C, C), 1)
    incl = r >= c
    strict = r > c
    blk16 = jnp.right_shift(r, 4) == jnp.right_shift(c, 4)
    blk32 = jnp.right_shift(r, 5) == jnp.right_shift(c, 5)
    return dict(incl=incl, strict=strict, upper=r <= c, blk16=blk16, blk32=blk32,
                tri=incl.astype(BF16), triT=(r <= c).astype(BF16), ones=jnp.ones((C, C), BF16),
                eye=(r == c).astype(F32), last=_iota((C, 1), 0) == C - 1)


def _tri_inverse(A, mk):
    P = -jnp.where(mk["blk16"], A, 0.0)
    X = mk["eye"] + P
    for _ in range(3):
        P = _dot(P, P)
        X = X + _dot(X, P)
    off1 = jnp.where(mk["blk32"] & (~mk["blk16"]), A, 0.0)
    X = X - _dot(_dot(X, off1), X)
    off2 = jnp.where(mk["blk32"], 0.0, A)
    X = X - _dot(_dot(X, off2), X)
    return X


def _dn_local(qc, kc, vc, gc, bc, mk):
    C = DN_C
    gm = jnp.broadcast_to(gc, gc.shape[:-1] + (C,))
    Gc = _dot2l(mk["tri"], gm)
    Gr = _dot2l(mk["ones"], jnp.where(mk["upper"], gm, 0.0))
    Dm = jnp.where(mk["incl"], jnp.exp(jnp.where(mk["incl"], Gc - Gr, 0.0)), 0.0)
    Gcol = jnp.max(Gc, axis=-1, keepdims=True)
    Gl = _colsum(jnp.where(mk["last"], Gcol, 0.0))
    eG = jnp.exp(Gcol)
    eT = jnp.exp(Gl - Gcol)
    gl = jnp.exp(Gl)
    kb = kc * bc
    vb = vc * bc
    KK = _dot(kb, kc, NT)
    Tinv = _tri_inverse(jnp.where(mk["strict"], KK * Dm, 0.0), mk)
    KBE = kb * eG
    QK = _dot(qc, kc, NT)
    return dict(Dm=Dm, eG=eG, eT=eT, gl=gl, kb=kb, vb=vb, KK=KK, Tinv=Tinv, KBE=KBE, U=_dot(Tinv, vb),
                W=_dot(Tinv, KBE), QK=QK, attn=QK * Dm, QD=qc * eG, KT=kc * eT)


def _dn_recur(f, S):
    vnew = f["U"] - _dot(f["W"], S)
    o = _dot(f["QD"], S) + _dot(f["attn"], vnew)
    return o, S * f["gl"] + _dot(f["KT"], vnew, TN), vnew


def _dn_bwd_chain(f, do, dS):
    dvnew = _dot(f["KT"], dS) + _dot(f["attn"], do, TN)
    return dvnew, dS * f["gl"] + _dot(f["QD"], do, TN) - _dot(f["W"], dvnew, TN)


def _dn_bwd_rest(qc, kc, vc, bc, f, S, vnew, do, dS, dvnew, mk):
    C = DN_C
    Dm, eG, eT, gl, kb, vb, KK, Tinv, KBE, QK = (
        f[n] for n in ("Dm", "eG", "eT", "gl", "kb", "vb", "KK", "Tinv", "KBE", "QK"))
    dKT = _dot(vnew, dS, NT)
    dgl = _colsum(_rowsum(dS * S))
    dQD = _dot(do, S, NT)
    dattn = _dot(do, vnew, NT)
    dU = dvnew
    dW = -_dot(dvnew, S, NT)
    dQK = dattn * Dm
    dD = dattn * QK
    dq = _dot(dQK, kc)
    dk = _dot(dQK, qc, TN)
    dTinv = _dot(dU, vb, NT) + _dot(dW, KBE, NT)
    dvb = _dot(Tinv, dU, TN)
    dKBE = _dot(Tinv, dW, TN)
    dA = -_dot(_dot(Tinv, dTinv, TN), Tinv, NT)
    dA = jnp.where(mk["strict"], dA, 0.0)
    dKK = dA * Dm
    dD = dD + dA * KK
    dkb = _dot(dKK, kc) + dKBE * eG
    dk = dk + _dot(dKK, kb, TN)
    deG = _rowsum(dKBE * kb)
    dk = dk + dkb * bc
    dbeta = _rowsum(dkb * kc) + _rowsum(dvb * vc)
    dv = dvb * bc
    dq = dq + dQD * eG
    deG = deG + _rowsum(dQD * qc)
    dk = dk + dKT * eT
    deT = _rowsum(dKT * kc)
    dGcol = deG * eG - deT * eT
    dGl = _colsum(deT * eT) + dgl * gl
    Y = dD * Dm
    ycol = jnp.max(_dot2r(Y, mk["ones"], TN), axis=-1, keepdims=True)
    dGcol = dGcol + _rowsum(Y) - ycol
    dGcol = dGcol + jnp.where(mk["last"], dGl, 0.0)
    dg = jnp.max(_dot2l(mk["triT"], jnp.broadcast_to(dGcol, dGcol.shape[:-1] + (C,))), axis=-1, keepdims=True)
    return dq, dk, dv, dg, dbeta


def _dn_core_fwd(proj, conv_w, alog, dtb, og, name, comm=None):
    T = proj.shape[0]
    N = T // DN_C

    def body(pq_ref, pk_ref, pv_ref, pz_ref, pab_ref, cq_ref, ck_ref, cv_ref, alog_ref, dtb_ref, og_ref,
             out_ref, o_ref, st_ref, q_s, k_s, v_s, g_s, b_s, S_s):
        h = pl.program_id(0)
        p = _dn_prep(pq_ref[...], pk_ref[...], pv_ref[...], pab_ref[...], cq_ref, ck_ref, cv_ref,
                     alog_ref[...], dtb_ref[...], h)
        q_s[...] = p["qn"] * (DN_D ** -0.5)
        k_s[...] = p["kn"]
        v_s[...] = p["v"]
        g_s[...] = p["g"]
        b_s[...] = p["beta"]
        S_s[...] = jnp.zeros(S_s.shape, F32)
        mk = _dn_masks()

        nu = min(DN_UNROLL_FWD, N)

        def step(it, carry):
            rows = pl.ds(pl.multiple_of(it * (nu * DN_C), nu * DN_C), nu * DN_C)
            loc = _dn_local(*(r[rows, :].reshape(nu, DN_C, r.shape[1]) for r in (q_s, k_s, v_s, g_s, b_s)), mk)
            S = S_s[...]
            outs = []
            for u in range(nu):
                st_ref[0, it * nu + u] = S
                o, S, _ = _dn_recur({n: v[u] for n, v in loc.items()}, S)
                outs.append(o)
            o_ref[rows, :] = jnp.concatenate(outs, axis=0)
            S_s[...] = S
            return carry

        lax.fori_loop(0, N // nu, step, 0)
        o = o_ref[...]
        ro = lax.rsqrt(jnp.mean(o * o, axis=1, keepdims=True) + EPS)
        out_ref[...] = ((o * ro * og_ref[...]) * _silu(pz_ref[...])).astype(out_ref.dtype)

    col = lambda k: pl.BlockSpec((T, DN_D), lambda h: (0, k * DN_H + h))
    cw = lambda k: pl.BlockSpec((DN_CONV, DN_D), lambda h: (0, k * DN_H + h))
    small = pl.BlockSpec((1, LANES), lambda h: (0, 0))
    return _grid_call(
        body, comm, grid=(DN_H,),
        in_specs=[col(0), col(1), col(2), col(3), pl.BlockSpec((T, LANES), lambda h: (0, 4 * DN_H)),
                  cw(0), cw(1), cw(2), small, small, small],
        out_specs=[pl.BlockSpec((T, DN_D), lambda h: (0, h)), pl.BlockSpec((T, DN_D), lambda h: (0, h)),
                   pl.BlockSpec((1, N, DN_D, DN_D), lambda h: (h, 0, 0, 0))],
        out_shape=[jax.ShapeDtypeStruct((T, D), BF16), jax.ShapeDtypeStruct((T, D), F32),
                   jax.ShapeDtypeStruct((DN_H, N, DN_D, DN_D), F32)],
        scratch_shapes=[pltpu.VMEM((T, DN_D), F32)] * 3 + [pltpu.VMEM((T, 1), F32)] * 2 + [pltpu.VMEM((DN_D, DN_D), F32)],
        name=name, args=(proj, proj, proj, proj, proj, conv_w, conv_w, conv_w, alog, dtb, og))


def _dn_core_bwd(proj, conv_w, alog, dtb, og, o, states, dout, name, comm=None):
    T = proj.shape[0]
    N = T // DN_C

    def body(pq_ref, pk_ref, pv_ref, pz_ref, pab_ref, cq_ref, ck_ref, cv_ref, alog_ref, dtb_ref, og_ref,
             o_ref, st_ref, dout_ref,
             dpq_ref, dpk_ref, dpv_ref, dpz_ref, dpab_ref, dcq_ref, dck_ref, dcv_ref, dalog_ref, ddtb_ref, dog_ref,
             q_s, k_s, v_s, g_s, b_s, do_s, dS_s):
        h = pl.program_id(0)
        scale = DN_D ** -0.5

        def prep():
            return _dn_prep(pq_ref[...], pk_ref[...], pv_ref[...], pab_ref[...], cq_ref, ck_ref, cv_ref,
                            alog_ref[...], dtb_ref[...], h)

        p = prep()
        q_s[...] = p["qn"] * scale
        k_s[...] = p["kn"]
        v_s[...] = p["v"]
        g_s[...] = p["g"]
        b_s[...] = p["beta"]
        del p

        o = o_ref[...]
        z = pz_ref[...]
        dout = dout_ref[...]
        ogv = og_ref[...]
        ro = lax.rsqrt(jnp.mean(o * o, axis=1, keepdims=True) + EPS)
        on = o * ro
        dy = dout * _silu(z)
        dpz_ref[...] = (dout * (on * ogv) * _dsilu(z)).astype(dpz_ref.dtype)
        dyg = dy * ogv
        do_s[...] = ro * (dyg - on * jnp.mean(dyg * on, axis=1, keepdims=True))
        dog_h = _colsum(dy * on)

        dS_s[...] = jnp.zeros(dS_s.shape, F32)
        mk = _dn_masks()

        nu = min(DN_UNROLL_BWD, N)

        def step(it, carry):
            c0 = (N // nu - 1 - it) * nu
            rows = pl.ds(pl.multiple_of(c0 * DN_C, nu * DN_C), nu * DN_C)
            q, k, v, g, b, do = (r[rows, :].reshape(nu, DN_C, r.shape[1]) for r in (q_s, k_s, v_s, g_s, b_s, do_s))
            loc = _dn_local(q, k, v, g, b, mk)
            Ss = st_ref[0, pl.ds(c0, nu)]
            vnew = loc["U"] - _dot(loc["W"], Ss)
            dS = dS_s[...]
            dS_in, dvnew = [None] * nu, [None] * nu
            for u in reversed(range(nu)):
                dS_in[u] = dS
                dvnew[u], dS = _dn_bwd_chain({n: x[u] for n, x in loc.items()}, do[u], dS)
            dS_s[...] = dS
            grads = _dn_bwd_rest(q, k, v, b, loc, Ss, vnew, do, jnp.stack(dS_in), jnp.stack(dvnew), mk)
            for r, d in zip((q_s, k_s, v_s, g_s, b_s), grads):
                r[rows, :] = d.reshape(nu * DN_C, r.shape[1])
            return carry

        lax.fori_loop(0, N // nu, step, 0)

        p = prep()
        pq, pk, pv = pq_ref[...], pk_ref[...], pv_ref[...]
        dqn = q_s[...] * scale
        dkn = k_s[...]
        qn, kn = p["qn"], p["kn"]
        dsq = p["rq"] * (dqn - qn * _rowsum(dqn * qn))
        dsk = p["rk"] * (dkn - kn * _rowsum(dkn * kn))
        dxq = dsq * _dsilu(p["xq"])
        dxk = dsk * _dsilu(p["xk"])
        dxv = v_s[...] * _dsilu(p["xv"])

        def conv_bwd(dx, x, w_ref, dp_ref, dc_ref):
            acc = dx * w_ref[DN_CONV - 1:DN_CONV, :]
            dc_ref[DN_CONV - 1:DN_CONV, :] = _colsum(dx * x)
            for i in range(DN_CONV - 1):
                s = DN_CONV - 1 - i
                acc = acc + _shift_rows(dx, -s) * w_ref[i:i + 1, :]
                dc_ref[i:i + 1, :] = _colsum(dx * _shift_rows(x, s))
            dp_ref[...] = acc.astype(dp_ref.dtype)

        conv_bwd(dxq, pq, cq_ref, dpq_ref, dcq_ref)
        conv_bwd(dxk, pk, ck_ref, dpk_ref, dck_ref)
        conv_bwd(dxv, pv, cv_ref, dpv_ref, dcv_ref)

        dg = g_s[...]
        beta = p["beta"]
        da_raw = dg * p["neg_ea"] * _sigmoid(p["pre"])
        db_raw = b_s[...] * beta * (1.0 - beta)
        lane = _iota((T, LANES), 1)
        contrib = jnp.where(lane == h, da_raw, 0.0) + jnp.where(lane == DN_H + h, db_raw, 0.0)
        lane1 = _iota((1, LANES), 1)
        dalog_h = jnp.where(lane1 == h, _colsum(dg * p["g"]), 0.0)
        ddtb_h = jnp.where(lane1 == h, _colsum(da_raw), 0.0)

        @pl.when(h == 0)
        def _():
            dpab_ref[...] = contrib.astype(dpab_ref.dtype)
            dalog_ref[...] = dalog_h
            ddtb_ref[...] = ddtb_h
            dog_ref[...] = dog_h

        @pl.when(h > 0)
        def _():
            dpab_ref[...] += contrib.astype(dpab_ref.dtype)
            dalog_ref[...] += dalog_h
            ddtb_ref[...] += ddtb_h
            dog_ref[...] += dog_h

    col = lambda k: pl.BlockSpec((T, DN_D), lambda h: (0, k * DN_H + h))
    cw = lambda k: pl.BlockSpec((DN_CONV, DN_D), lambda h: (0, k * DN_H + h))
    small = pl.BlockSpec((1, LANES), lambda h: (0, 0))
    ab = pl.BlockSpec((T, LANES), lambda h: (0, 4 * DN_H))
    hcol = pl.BlockSpec((T, DN_D), lambda h: (0, h))
    outs = _grid_call(
        body, comm, grid=(DN_H,),
        in_specs=[col(0), col(1), col(2), col(3), ab, cw(0), cw(1), cw(2), small, small, small,
                  hcol, pl.BlockSpec((1, N, DN_D, DN_D), lambda h: (h, 0, 0, 0)), hcol],
        out_specs=[hcol, hcol, hcol, hcol, pl.BlockSpec((T, LANES), lambda h: (0, 0)),
                   pl.BlockSpec((DN_CONV, DN_D), lambda h: (0, h)), pl.BlockSpec((DN_CONV, DN_D), lambda h: (0, h)),
                   pl.BlockSpec((DN_CONV, DN_D), lambda h: (0, h)), small, small, small],
        out_shape=[jax.ShapeDtypeStruct((T, D), BF16)] * 4 + [jax.ShapeDtypeStruct((T, LANES), BF16)]
                  + [jax.ShapeDtypeStruct((DN_CONV, D), F32)] * 3 + [jax.ShapeDtypeStruct((1, LANES), F32)] * 3,
        scratch_shapes=[pltpu.VMEM((T, DN_D), F32)] * 3 + [pltpu.VMEM((T, 1), F32)] * 2
                       + [pltpu.VMEM((T, DN_D), F32), pltpu.VMEM((DN_D, DN_D), F32)],
        name=name, args=(proj, proj, proj, proj, proj, conv_w, conv_w, conv_w, alog, dtb, og, o, states, dout))
    dpq, dpk, dpv, dpz, dpab, dcq, dck, dcv, dalog, ddtb, dog = outs[:11]
    dproj = jnp.concatenate([dpq, dpk, dpv, dpz, dpab], axis=1)
    dconv = jnp.concatenate([dcq, dck, dcv], axis=1)
    return dproj, dconv, dalog, ddtb, dog, (outs[11] if comm else None)


def _sb_head_of(shape):
    return jnp.right_shift(_iota(shape, 1), 6)


def _sb_head_sums(x, head):
    out = jnp.zeros_like(x)
    for hh in range(SB_HPB):
        out = jnp.where(head == hh, _rowsum(jnp.where(head == hh, x, 0.0)), out)
    return out


def _sb_head_norm(x, head):
    r = lax.rsqrt(_sb_head_sums(x * x, head) * (1.0 / SB_D) + EPS)
    return x * r, r


def _sb_fill(q_ref, k_ref, v_ref, qg_ref, kg_ref, qs_s, kn_s, v_s):
    head = _sb_head_of(q_ref.shape)
    qs_s[...] = (_sb_head_norm(q_ref[...], head)[0] * qg_ref[...] * (SB_D ** -0.5)).astype(BF16)
    kn_s[...] = (_sb_head_norm(k_ref[...], head)[0] * kg_ref[...]).astype(BF16)
    v_s[...] = v_ref[...].astype(BF16)


def _sb_head_masks(dtype):
    return jnp.stack([(_sb_head_of((1, SB_W)) == hh).astype(dtype) for hh in range(SB_HPB)])


def _sb_core_fwd(qkv, qg, kg, name, comm=None):
    T = qkv.shape[0]
    B = min(SB_TILE, T)
    NB = T // B
    NP = SB_H // SB_HPB

    def body(q_ref, k_ref, v_ref, qg_ref, kg_ref, o_ref, tot_ref, qs_s, kn_s, v_s):
        _sb_fill(q_ref, k_ref, v_ref, qg_ref, kg_ref, qs_s, kn_s, v_s)
        r, c = _iota((B, B), 0), _iota((B, B), 1)
        causal = c < r
        m_after = (r > c).astype(BF16)
        head_b = _sb_head_of((B, SB_W))
        hm = _sb_head_masks(BF16)

        def tile(qb, kj, vj, R, acc, diag):
            z = _dot(qb, kj, NT)
            sp = _softplus(z)
            ls = z - sp
            lm = jnp.where(causal, -sp, 0.0) if diag else -sp
            cs = _dot2r(lm, m_after) + R
            a = jnp.exp(ls + cs)
            if diag:
                a = jnp.where(causal, a, 0.0)
            return R + _rowsum(lm), acc + jnp.sum(_dot(a, vj), axis=0)

        def qblock(i, carry):
            rows_i = pl.ds(pl.multiple_of(i * B, B), B)
            qb = qs_s[rows_i, :][None] * hm

            def step(rows_j, st, diag):
                return tile(qb, kn_s[rows_j, :], v_s[rows_j, :][None] * hm, st[0], st[1], diag)

            def keys(j):
                return pl.ds(pl.multiple_of(j * B, B), B)

            st = step(rows_i, (jnp.zeros((SB_HPB, B, 1), F32), jnp.zeros((B, SB_W), F32)), True)
            st = lax.fori_loop(
                0, i // 2, lambda s, st: step(keys(i - 2 - 2 * s), step(keys(i - 1 - 2 * s), st, False), False), st)
            st = lax.cond(i % 2 == 1, lambda st: step(keys(0), st, False), lambda st: st, st)
            o_ref[rows_i, :] = st[1].astype(o_ref.dtype)
            tot = jnp.zeros((B, SB_W), F32)
            for hh in range(SB_HPB):
                tot = jnp.where(head_b == hh, st[0][hh], tot)
            tot_ref[rows_i, :] = tot
            return carry

        lax.fori_loop(0, NB, qblock, 0)

    blk = lambda k: pl.BlockSpec((T, SB_W), lambda p: (0, k * NP + p))
    small = pl.BlockSpec((1, SB_W), lambda p: (0, 0))
    return _grid_call(
        body, comm, grid=(NP,), in_specs=[blk(0), blk(1), blk(2), small, small],
        out_specs=[pl.BlockSpec((T, SB_W), lambda p: (0, p))] * 2,
        out_shape=[jax.ShapeDtypeStruct((T, D), BF16), jax.ShapeDtypeStruct((T, D), F32)],
        scratch_shapes=[pltpu.VMEM((T, SB_W), BF16)] * 3, name=name, args=(qkv, qkv, qkv, qg, kg))


def _sb_core_bwd(qkv, qg, kg, tot, dout, name, comm=None):
    T = qkv.shape[0]
    B = min(SB_TILE, T)
    NB = T // B
    NP = SB_H // SB_HPB

    def body(q_ref, k_ref, v_ref, qg_ref, kg_ref, tot_ref, do_ref, dq_ref, dk_ref, dv_ref, dqg_ref, dkg_ref,
             qs_s, kn_s, v_s, dqn_s, dkn_s, dvv_s):
        p = pl.program_id(0)
        _sb_fill(q_ref, k_ref, v_ref, qg_ref, kg_ref, qs_s, kn_s, v_s)
        dkn_s[...] = jnp.zeros(dkn_s.shape, F32)
        dvv_s[...] = jnp.zeros(dvv_s.shape, F32)
        r, c = _iota((B, B), 0), _iota((B, B), 1)
        causal = c < r
        m_upto = (r <= c).astype(BF16)
        m_before = (r < c).astype(BF16)
        lane_b = _iota((B, SB_W), 1)
        hm, hmf = _sb_head_masks(BF16), _sb_head_masks(F32)

        def tile(qb, dob, tot_h, kj, vj, PL, P, dq, diag):
            z = _dot(qb, kj, NT)
            sp = _softplus(z)
            ls = z - sp
            lm = jnp.where(causal, -sp, 0.0) if diag else -sp
            cs = tot_h - PL - _dot2r(lm, m_upto)
            a = jnp.exp(ls + cs)
            if diag:
                a = jnp.where(causal, a, 0.0)
            e = _dot(dob, vj, NT) * a
            E = _dot(e, m_before) + P
            sig = jnp.exp(ls)
            dz = e * (1.0 - sig) - E * sig
            if diag:
                dz = jnp.where(causal, dz, 0.0)
            dq = dq + jnp.sum(_dot(dz, kj) * hmf, axis=0)
            return (PL + _rowsum(lm), P + _rowsum(e), dq), jnp.sum(_dot(dz, qb, TN), axis=0), jnp.sum(_dot(a, dob, TN), axis=0)

        def qblock(i, carry):
            rows_i = pl.ds(pl.multiple_of(i * B, B), B)
            totb = tot_ref[rows_i, :]
            tot_h = jnp.stack([_rowsum(jnp.where(lane_b == hh * SB_D, totb, 0.0)) for hh in range(SB_HPB)])
            qb = qs_s[rows_i, :][None] * hm
            dob = do_ref[rows_i, :][None] * hm

            def step(rows_j, st, diag):
                st, dk, dv = tile(qb, dob, tot_h, kn_s[rows_j, :], v_s[rows_j, :][None] * hm, st[0], st[1], st[2], diag)
                dkn_s[rows_j, :] += dk
                dvv_s[rows_j, :] += dv
                return st

            def keys(j):
                return pl.ds(pl.multiple_of(j * B, B), B)

            zero = (jnp.zeros((SB_HPB, B, 1), F32), jnp.zeros((SB_HPB, B, 1), F32), jnp.zeros((B, SB_W), F32))
            st = lax.fori_loop(0, i // 2, lambda p, st: step(keys(2 * p + 1), step(keys(2 * p), st, False), False), zero)
            st = lax.cond(i % 2 == 1, lambda st: step(keys(i - 1), st, False), lambda st: st, st)
            st = step(rows_i, st, True)
            dqn_s[rows_i, :] = st[2] * (SB_D ** -0.5)
            return carry

        lax.fori_loop(0, NB, qblock, 0)

        head = _sb_head_of((T, SB_W))

        def norm_bwd(dn, x_ref, g):
            xh, rr = _sb_head_norm(x_ref[...], head)
            t = dn * g
            return rr * (t - xh * (_sb_head_sums(t * xh, head) * (1.0 / SB_D))), _colsum(dn * xh)

        dq, dqg = norm_bwd(dqn_s[...], q_ref, qg_ref[...])
        dq_ref[...] = dq.astype(dq_ref.dtype)
        dk, dkg = norm_bwd(dkn_s[...], k_ref, kg_ref[...])
        dk_ref[...] = dk.astype(dk_ref.dtype)
        dv_ref[...] = dvv_s[...].astype(dv_ref.dtype)

        @pl.when(p == 0)
        def _():
            dqg_ref[...] = dqg
            dkg_ref[...] = dkg

        @pl.when(p > 0)
        def _():
            dqg_ref[...] += dqg
            dkg_ref[...] += dkg

    blk = lambda k: pl.BlockSpec((T, SB_W), lambda p: (0, k * NP + p))
    small = pl.BlockSpec((1, SB_W), lambda p: (0, 0))
    own = pl.BlockSpec((T, SB_W), lambda p: (0, p))
    outs = _grid_call(
        body, comm, grid=(NP,), in_specs=[blk(0), blk(1), blk(2), small, small, own, own],
        out_specs=[own, own, own, small, small],
        out_shape=[jax.ShapeDtypeStruct((T, D), BF16)] * 3 + [jax.ShapeDtypeStruct((1, SB_W), F32)] * 2,
        scratch_shapes=[pltpu.VMEM((T, SB_W), BF16)] * 3 + [pltpu.VMEM((T, SB_W), F32)] * 3,
        name=name, args=(qkv, qkv, qkv, qg, kg, tot, dout))
    dq, dk, dv, dqg, dkg = outs[:5]
    return jnp.concatenate([dq, dk, dv], axis=1), dqg, dkg, (outs[5] if comm else None)


def _ada_fwd(c16, ada_w, ada_b_cols, name):
    L, _, cols = ada_w.shape

    def body(c_ref, w_ref, b_ref, o_ref):
        o_ref[0] = _dot(_silu(c_ref[...]), w_ref[0]) + b_ref[0]

    return pl.pallas_call(
        body, grid=(L,),
        in_specs=[pl.BlockSpec((16, D), lambda i: (0, 0)), pl.BlockSpec((1, D, cols), lambda i: (i, 0, 0)),
                  pl.BlockSpec((1, 1, cols), lambda i: (i, 0, 0))],
        out_specs=pl.BlockSpec((1, 16, cols), lambda i: (i, 0, 0)),
        out_shape=jax.ShapeDtypeStruct((L, 16, cols), F32), name=name, compiler_params=_params(("arbitrary",)),
    )(c16, ada_w, ada_b_cols)


def _ada_bwd(c16, dmod16, name):
    L, _, cols = dmod16.shape

    def body(c_ref, d_ref, o_ref):
        o_ref[0] = _dot(_silu(c_ref[...]), d_ref[0], TN)

    return pl.pallas_call(
        body, grid=(L,),
        in_specs=[pl.BlockSpec((16, D), lambda i: (0, 0)), pl.BlockSpec((1, 16, cols), lambda i: (i, 0, 0))],
        out_specs=pl.BlockSpec((1, D, cols), lambda i: (i, 0, 0)),
        out_shape=jax.ShapeDtypeStruct((L, D, cols), F32), name=name, compiler_params=_params(("arbitrary",)),
    )(c16, dmod16)


def _sum_sources(x, name):
    n, R, C = x.shape
    tile = _sum_tile(R) or R

    def body(x_ref, o_ref):
        acc = x_ref[0].astype(F32)
        for k in range(1, n):
            acc = acc + x_ref[k].astype(F32)
        o_ref[...] = acc

    return pl.pallas_call(
        body, grid=(R // tile,), in_specs=[pl.BlockSpec((n, tile, C), lambda i: (0, i, 0))],
        out_specs=pl.BlockSpec((tile, C), lambda i: (i, 0)), out_shape=jax.ShapeDtypeStruct((R, C), F32),
        name=name, compiler_params=_params(("parallel",)),
    )(x)


def _adamw(w, g, m, v, name):
    shape = w.shape
    C = shape[-1]
    R = w.size // C
    w2, g2, m2, v2 = (a.reshape(R, C) for a in (w, g, m, v))
    tile = _rtile(R)
    c1 = 1.0 / (1.0 - ADAM_B1 ** ADAM_STEP)
    c2 = 1.0 / (1.0 - ADAM_B2 ** ADAM_STEP)

    def body(w_ref, g_ref, m_ref, v_ref, d_ref, nm_ref, nv_ref):
        gv = g_ref[...]
        nm = ADAM_B1 * m_ref[...] + (1.0 - ADAM_B1) * gv
        nv = ADAM_B2 * v_ref[...] + (1.0 - ADAM_B2) * (gv * gv)
        d_ref[...] = -ADAM_LR * ((nm * c1) / (jnp.sqrt(nv * c2) + ADAM_EPS) + ADAM_WD * w_ref[...])
        nm_ref[...] = nm
        nv_ref[...] = nv

    spec = pl.BlockSpec((tile, C), lambda i: (i, 0))
    outs = pl.pallas_call(
        body, grid=(R // tile,), in_specs=[spec] * 4, out_specs=[spec] * 3,
        out_shape=[jax.ShapeDtypeStruct((R, C), F32)] * 3, name=name, compiler_params=_params(("parallel",)),
    )(w2, g2, m2, v2)
    return tuple(o.reshape(shape) for o in outs)


def _gather_steps(x_ref, out_ref, send_sems, recv_sems, local_sem):
    mx, my, mc = lax.axis_index("x"), lax.axis_index("y"), lax.axis_index("c")
    me, sibling = (mx, my, mc), (mx, my, 1 - mc)
    chips = [(1 - mx, my), (mx, 1 - my), (1 - mx, 1 - my)]

    def slot(px, py, pc):
        return out_ref.at[4 * px + 2 * py + pc]

    def copy(k, block, to, src=None):
        return pltpu.make_async_remote_copy(
            src_ref=slot(*block) if src is None else src, dst_ref=slot(*block),
            send_sem=send_sems.at[k], recv_sem=recv_sems.at[k], device_id=to, device_id_type=MESH)

    mine = pltpu.make_async_copy(x_ref, slot(*me), local_sem)
    first = [copy(0, me, sibling, src=x_ref)] + [copy(1 + j, me, (*chip, mc), src=x_ref) for j, chip in enumerate(chips)]
    passed = [copy(4 + j, (*chip, mc), sibling) for j, chip in enumerate(chips)]

    def start():
        mine.start()
        for cp in first:
            cp.start()

    def forward():
        for j, chip in enumerate(chips):
            copy(1 + j, (*chip, mc), me).wait_recv()
            passed[j].start()

    def finish():
        copy(0, sibling, me).wait_recv()
        for j, chip in enumerate(chips):
            copy(4 + j, (*chip, 1 - mc), me).wait_recv()
        for cp in first + passed:
            cp.wait_send()
        mine.wait()

    return start, forward, finish


def _exchange_steps(x_ref, out_ref, send_sems, recv_sems, local_sem):
    mx, my, mc = lax.axis_index("x"), lax.axis_index("y"), lax.axis_index("c")
    me = 4 * mx + 2 * my + mc
    mine = pltpu.make_async_copy(x_ref.at[me], out_ref.at[me], local_sem)
    copies = []
    for k in range(1, NDEV):
        px, py, pc = mx ^ (k >> 2), my ^ ((k >> 1) & 1), mc ^ (k & 1)
        copies.append(pltpu.make_async_remote_copy(
            src_ref=x_ref.at[4 * px + 2 * py + pc], dst_ref=out_ref.at[me], send_sem=send_sems.at[k - 1],
            recv_sem=recv_sems.at[k - 1], device_id=(px, py, pc), device_id_type=MESH))

    def start():
        mine.start()
        for cp in copies:
            cp.start()

    def finish():
        for cp in copies:
            cp.wait_recv()
        for cp in copies:
            cp.wait_send()
        mine.wait()

    return start, finish


def _comm_sems():
    return [pltpu.SemaphoreType.DMA((7,)), pltpu.SemaphoreType.DMA((7,)), pltpu.SemaphoreType.DMA]


def _comm_out_shapes(comm):
    kind, xs = comm
    return [jax.ShapeDtypeStruct(((NDEV,) + x.shape) if kind == "gather" else x.shape, x.dtype) for x in xs]


def _ride_along(comm, step, n_steps, refs, at_end):
    if comm[0] == "gather":
        start, forward, finish = _gather_steps(*refs)
        todo = [(n_steps - 1, finish)] if at_end else [(0, start), (n_steps - 1, forward)]
    else:
        start, finish = _exchange_steps(*refs)
        todo = [(n_steps - 1, finish)] if at_end else [(0, start)]
    for at, fn in todo:
        pl.when(step == at)(fn)


def _grid_call(body, comm, *, grid, in_specs, out_specs, out_shape, scratch_shapes, name, args):
    if comm is None:
        return pl.pallas_call(body, grid=grid, in_specs=in_specs, out_specs=out_specs, out_shape=out_shape,
                              scratch_shapes=scratch_shapes, name=name, compiler_params=_params(("arbitrary",)))(*args)
    n_in, n_out, n_scr, n_steps, n_parts = len(in_specs), len(out_specs), len(scratch_shapes), grid[0], len(comm[1])

    def with_comm(*refs):
        ins, xs = refs[:n_in], refs[n_in:n_in + n_parts]
        outs, gots = refs[n_in + n_parts:n_in + n_parts + n_out], refs[n_in + n_parts + n_out:n_in + 2 * n_parts + n_out]
        scr, sems = refs[n_in + 2 * n_parts + n_out:][:n_scr], refs[n_in + 2 * n_parts + n_out + n_scr:]
        part_refs = [(xs[p], gots[p]) + tuple(sems[3 * p:3 * p + 3]) for p in range(n_parts)]
        step = pl.program_id(0)
        for pr in part_refs:
            _ride_along(comm, step, n_steps, pr, False)
        body(*ins, *outs, *scr)
        for pr in part_refs:
            _ride_along(comm, step, n_steps, pr, True)

    hbm = pl.BlockSpec(memory_space=pl.ANY)
    res = pl.pallas_call(
        with_comm, grid=grid, in_specs=list(in_specs) + [hbm] * n_parts, out_specs=list(out_specs) + [hbm] * n_parts,
        out_shape=list(out_shape) + _comm_out_shapes(comm), scratch_shapes=list(scratch_shapes) + _comm_sems() * n_parts,
        name=name, compiler_params=_params(("arbitrary",)))(*args, *comm[1])
    return list(res[:n_out]) + [list(res[n_out:])]


def _all_gather(x, name, in_vmem):
    def body(x_ref, out_ref, send_sems, recv_sems, local_sem):
        start, forward, finish = _gather_steps(x_ref, out_ref, send_sems, recv_sems, local_sem)
        start()
        forward()
        finish()

    space = pltpu.VMEM if in_vmem else pl.ANY
    return pl.pallas_call(
        body, out_shape=jax.ShapeDtypeStruct((NDEV,) + x.shape, x.dtype),
        in_specs=[pl.BlockSpec(memory_space=space)], out_specs=pl.BlockSpec(memory_space=space),
        scratch_shapes=_comm_sems(), name=name, compiler_params=pltpu.CompilerParams(vmem_limit_bytes=VMEM_LIMIT),
    )(x)


def _all_to_all(x, name):
    def body(x_ref, out_ref, send_sems, recv_sems, local_sem):
        start, finish = _exchange_steps(x_ref, out_ref, send_sems, recv_sems, local_sem)
        start()
        finish()

    return pl.pallas_call(
        body, out_shape=jax.ShapeDtypeStruct(x.shape, x.dtype),
        in_specs=[pl.BlockSpec(memory_space=pl.ANY)], out_specs=pl.BlockSpec(memory_space=pl.ANY),
        scratch_shapes=_comm_sems(), name=name, compiler_params=pltpu.CompilerParams(vmem_limit_bytes=VMEM_LIMIT),
    )(x)


def _stage_parts(st):
    parts = []
    if st >= 1:
        parts += [("dn_w_out" if (st - 1) % 2 == 0 else "sb_w_out", (st - 1) // 2, (D // NDEV, D), "rows"),
                  ("ffn_w_in", st - 1, (D, 2 * DFF // NDEV), "colsT"), ("ffn_w_out", st - 1, (DFF // NDEV, D), "rows")]
    if st < DEPTH:
        parts += [("dn_w_in", st // 2, (D, DN_COLS // NDEV), "colsT") if st % 2 == 0 else
                  ("sb_w_qkv", st // 2, (D, 3 * D // NDEV), "colsT")]
    return parts


_IN_PROJ = ("dn_w_in", "sb_w_qkv")


def _sum_tile(rows):
    for t in range(512, 191, -16):
        if rows % t == 0:
            return t
    return None


def _part_rows(s):
    return s[0] * s[1] // D


def _pack_stage(w, st):
    return [(w[n][j].T if kind == "colsT" else w[n][j]).astype(BF16).reshape(_part_rows(s), D)
            for n, j, s, kind in _stage_parts(st)]


def _full_stage(gs, st):
    out = {n: g.reshape((NDEV * s[0], s[1]) if kind == "rows" else (NDEV * s[1], s[0]))
           for g, (n, _, s, kind) in zip(gs, _stage_parts(st))}
    if "dn_w_in" in out:
        out["dn_w_in"] = jnp.pad(out["dn_w_in"], ((0, DN_COLS_PAD - DN_COLS), (0, 0)))
    return out


def _shards_stage(full, st):
    return [full[n].astype(BF16).reshape(NDEV, _part_rows(s), D) for n, _, s, _ in _stage_parts(st)]


def _unpack_stage(bufs, st):
    return {(n, j): b.reshape(s[1], s[0]).T if kind == "colsT" else b.reshape(s)
            for b, (n, j, s, kind) in zip(bufs, _stage_parts(st))}


def _local_step(x, target, mod, small, w0, rest):
    dist = isinstance(rest, tuple)
    packs = rest[1] if dist else None
    W = [dict(w0), {}, {}, {}] if dist else [w0] + list(rest)

    def arrived(k, g):
        for n, a in _full_stage(g, k + 1).items():
            W[k + 1 if n in _IN_PROJ else k][n] = a

    row = lambda v: v.reshape(1, -1)
    pad128 = lambda v: jnp.pad(v.reshape(1, -1), ((0, 0), (0, LANES - v.size)))
    saved = []
    mods = [[row(mod[i, k * D:(k + 1) * D]) for k in range(N_MOD)] for i in range(DEPTH)]
    h1 = _norm_mod(x, row(small["norm1_g"][0]), mods[0][1], mods[0][0], "norm1_0")
    for i in range(DEPTH):
        j = i // 2
        w = W[i]
        m = mods[i]
        sh1, sc1, gt1, sh2, sc2, gt2 = m
        g1, g2 = row(small["norm1_g"][i]), row(small["norm2_g"][i])
        if i % 2 == 0:
            proj = _mm(h1, w["dn_w_in"], "nt", F32, f"dn_proj_{i}")
            alog, dtb, og = pad128(small["dn_a_log"][j]), pad128(small["dn_dt_bias"][j]), row(small["dn_onorm_g"][j])
            comm = ("gather", packs[i]) if dist else None
            res = _dn_core_fwd(proj, small["dn_conv_w"][j], alog, dtb, og, f"dn_core_{i}", comm)
            om, o_pre, states = res[:3]
            if comm:
                arrived(i, res[3])
            y1 = _mm(om, w["dn_w_out"], "nn", F32, f"dn_out_{i}")
            mix = (proj, alog, dtb, og, o_pre, states, om)
        else:
            qkv = _mm(h1, w["sb_w_qkv"], "nt", F32, f"sb_qkv_{i}")
            qg2 = jnp.tile(row(small["sb_q_norm_g"][j]), (1, SB_HPB))
            kg2 = jnp.tile(row(small["sb_k_norm_g"][j]), (1, SB_HPB))
            comm = ("gather", packs[i]) if dist else None
            res = _sb_core_fwd(qkv, qg2, kg2, f"sb_core_{i}", comm)
            om, tot = res[:2]
            if comm:
                arrived(i, res[2])
            y1 = _mm(om, w["sb_w_out"], "nn", F32, f"sb_out_{i}")
            mix = (qkv, qg2, kg2, tot, om)
        x_mid, h2 = _gate_res_norm(x, y1, gt1, g2, sc2, sh2, f"res1_{i}")
        u = _mm(h2, w["ffn_w_in"], "nt", BF16, f"ffn_in_{i}")
        a = _swiglu_act(u, f"ffn_act_{i}")
        y2 = _mm(a, w["ffn_w_out"], "nn", F32, f"ffn_out_{i}")
        saved.append((x, h1, mix, y1, x_mid, h2, u, a, y2, m, g1, g2))
        if i + 1 < DEPTH:
            x, h1 = _gate_res_norm(x_mid, y2, gt2, row(small["norm1_g"][i + 1]), mods[i + 1][1], mods[i + 1][0],
                                   f"res2_{i}")
        else:
            x = _gate_res(x_mid, y2, gt2, f"res2_{i}")

    dx, sq = _loss_head(x, target, "loss_head")

    sg = dict(mod=[None] * DEPTH, norm1_g=[None] * DEPTH, norm2_g=[None] * DEPTH, dn_conv_w=[None] * 2,
              dn_a_log=[None] * 2, dn_dt_bias=[None] * 2, dn_onorm_g=[None] * 2, sb_q_norm_g=[None] * 2,
              sb_k_norm_g=[None] * 2)
    big = [None] * (DEPTH + 1 if dist else DEPTH)
    mixer_gw = {}
    for i in reversed(range(DEPTH)):
        j = i // 2
        w = W[i]
        x_in, h1, mix, y1, x_mid, h2, u, a, y2, m, g1, g2 = saved[i]
        sh1, sc1, gt1, sh2, sc2, gt2 = m
        gw = {}
        dy2, dgt2 = _gate_res_bwd(dx, y2, gt2, f"res2_bwd_{i}")
        da = _mm(dy2, w["ffn_w_out"], "nt", BF16, f"ffn_out_bwd_{i}")
        gw["ffn_w_out"] = _mm(a, dy2, "tn", BF16, f"ffn_out_wg_{i}")
        du = _swiglu_act_bwd(da, u, f"ffn_act_bwd_{i}")
        dh2 = _mm(du, w["ffn_w_in"], "nn", F32, f"ffn_in_bwd_{i}")
        gw["ffn_w_in"] = _mm(du, h2, "tn", BF16, f"ffn_in_wg_{i}")
        dx_mid, s2, dsh2 = _norm_mod_bwd(dh2, x_mid, dx, g2, sc2, f"norm2_bwd_{i}")
        dy1, dgt1 = _gate_res_bwd(dx_mid, y1, gt1, f"res1_bwd_{i}")
        if i % 2 == 0:
            proj, alog, dtb, og, o_pre, states, om = mix
            dom = _mm(dy1, w["dn_w_out"], "nt", F32, f"dn_out_bwd_{i}")
            gw["dn_w_out"] = _mm(om, dy1, "tn", BF16, f"dn_out_wg_{i}")
            comm = ("exchange", _shards_stage({**gw, **mixer_gw}, i + 1)) if dist else None
            dproj, dconv, dalog, ddtb, dog, got = _dn_core_bwd(proj, small["dn_conv_w"][j], alog, dtb, og, o_pre, states,
                                                               dom, f"dn_core_bwd_{i}", comm)
            dh1 = _mm(dproj, w["dn_w_in"], "nn", F32, f"dn_proj_bwd_{i}")
            gw["dn_w_in"] = _mm(dproj, h1, "tn", BF16, f"dn_proj_wg_{i}")[:DN_COLS]
            sg["dn_conv_w"][j] = dconv
            sg["dn_a_log"][j] = dalog[0, :DN_H]
            sg["dn_dt_bias"][j] = ddtb[0, :DN_H]
            sg["dn_onorm_g"][j] = dog[0]
        else:
            qkv, qg2, kg2, tot, om = mix
            dom = _mm(dy1, w["sb_w_out"], "nt", BF16, f"sb_out_bwd_{i}")
            gw["sb_w_out"] = _mm(om, dy1, "tn", BF16, f"sb_out_wg_{i}")
            comm = ("exchange", _shards_stage({**gw, **mixer_gw}, i + 1)) if dist else None
            dqkv, dqg, dkg, got = _sb_core_bwd(qkv, qg2, kg2, tot, dom, f"sb_core_bwd_{i}", comm)
            dh1 = _mm(dqkv, w["sb_w_qkv"], "nn", F32, f"sb_qkv_bwd_{i}")
            gw["sb_w_qkv"] = _mm(dqkv, h1, "tn", BF16, f"sb_qkv_wg_{i}")
            sg["sb_q_norm_g"][j] = jnp.sum(dqg.reshape(SB_HPB, SB_D), axis=0)
            sg["sb_k_norm_g"][j] = jnp.sum(dkg.reshape(SB_HPB, SB_D), axis=0)
        if comm:
            big[i + 1] = got
        dx, s1, dsh1 = _norm_mod_bwd(dh1, x_in, dx_mid, g1, sc1, f"norm1_bwd_{i}")
        sg["mod"][i] = jnp.concatenate([dsh1, s1 * g1, dgt1, dsh2, s2 * g2, dgt2], axis=1)[0]
        sg["norm1_g"][i] = (s1 * (1.0 + sc1))[0]
        sg["norm2_g"][i] = (s2 * (1.0 + sc2))[0]
        if dist:
            mixer_gw = {n: a for n, a in gw.items() if n in _IN_PROJ}
        else:
            big[i] = gw
    if dist:
        big[0] = [_all_to_all(a, f"exchange_grads_0_{p}") for p, a in enumerate(_shards_stage(mixer_gw, 0))]
    return sq, dx, {k: jnp.stack(v) for k, v in sg.items()}, big


def _device_index():
    return 4 * lax.axis_index("x") + 2 * lax.axis_index("y") + lax.axis_index("c")


def _gather_small(w, c):
    me = _device_index()
    ada_cols = w["ada_w"].shape[-1]
    conv_cols = w["dn_conv_w"].shape[-1]
    blk = jnp.concatenate([c.reshape(1, D), w["dn_conv_w"].reshape(-1, D)], axis=0)
    g1 = _all_gather(blk, "gather_cond", True)
    c16 = jnp.pad(g1[:, 0, :], ((0, 8), (0, 0)))
    conv_full = jnp.transpose(g1[:, 1:, :].reshape(NDEV, 2, DN_CONV, conv_cols), (1, 2, 0, 3)).reshape(2, DN_CONV, -1)
    b_cols = lax.dynamic_slice_in_dim(w["ada_b"], me * ada_cols, ada_cols, axis=1).reshape(DEPTH, 1, ada_cols)
    mod_part = _ada_fwd(c16, w["ada_w"], b_cols, "ada_fwd")[:, :NDEV, :]
    g2 = _all_gather(mod_part.reshape(DEPTH * NDEV, ada_cols), "gather_mod", True)
    g2 = g2.reshape(NDEV, DEPTH, NDEV, ada_cols)
    mod = lax.dynamic_index_in_dim(g2, me, axis=2, keepdims=False)
    mod = jnp.transpose(mod, (1, 0, 2)).reshape(DEPTH, N_MOD * D)
    return c16, conv_full, mod


def _reduce_small(gr, c16):
    me = _device_index()
    ada_cols = N_MOD * D // NDEV
    conv_cols = 3 * DN_H * DN_D // NDEV
    grads = {}
    small = jnp.concatenate([gr["dn_a_log"].reshape(-1), gr["dn_dt_bias"].reshape(-1), gr["dn_onorm_g"].reshape(-1),
                             gr["sb_q_norm_g"].reshape(-1), gr["sb_k_norm_g"].reshape(-1)])
    small = jnp.pad(small, (0, D - small.size)).reshape(1, D)
    blk3 = jnp.concatenate([gr["mod"].reshape(-1, D), gr["norm1_g"], gr["norm2_g"], gr["dn_conv_w"].reshape(-1, D),
                            small], axis=0)
    blk3 = jnp.pad(blk3, ((0, 64 - blk3.shape[0]), (0, 0)))
    g3 = _all_gather(blk3, "gather_small_grads", True)
    tot = _sum_sources(g3, "sum_small_grads")
    grads["ada_b"] = tot[:24].reshape(DEPTH, N_MOD * D)
    grads["norm1_g"] = tot[24:28]
    grads["norm2_g"] = tot[28:32]
    conv_g = tot[32:56].reshape(2, DN_CONV, NDEV * conv_cols)
    grads["dn_conv_w"] = lax.dynamic_slice_in_dim(conv_g, me * conv_cols, conv_cols, axis=2)
    sm = tot[56]
    grads["dn_a_log"] = sm[0:16].reshape(2, DN_H)
    grads["dn_dt_bias"] = sm[16:32].reshape(2, DN_H)
    grads["dn_onorm_g"] = sm[32:288].reshape(2, DN_D)
    grads["sb_q_norm_g"] = sm[288:416].reshape(2, SB_D)
    grads["sb_k_norm_g"] = sm[416:544].reshape(2, SB_D)
    dmod_all = g3[:, :24, :].reshape(NDEV, DEPTH, N_MOD * D)
    dmod_cols = lax.dynamic_slice_in_dim(dmod_all, me * ada_cols, ada_cols, axis=2)
    dmod16 = jnp.pad(jnp.transpose(dmod_cols, (1, 0, 2)), ((0, 0), (0, 8), (0, 0)))
    grads["ada_w"] = _ada_bwd(c16, dmod16, "ada_bwd")
    return grads


def _reduce_big(recv):
    parts = {}
    for st in range(DEPTH + 1):
        parts.update(_unpack_stage([_sum_sources(r, f"sum_grads_{st}_{p}") for p, r in enumerate(recv[st])], st))
    out = {}
    for (n, j) in sorted(parts):
        out.setdefault(n, []).append(parts[(n, j)])
    return {n: jnp.stack(v) for n, v in out.items()}


def kernel(x, c, ada_w, ada_b, norm1_g, norm2_g, dn_w_in, dn_conv_w, dn_a_log, dn_dt_bias, dn_onorm_g, dn_w_out, sb_w_qkv, sb_q_norm_g, sb_k_norm_g, sb_w_out, ffn_w_in, ffn_w_out, loss_target, m_ada_w, m_ada_b, m_norm1_g, m_norm2_g, m_dn_w_in, m_dn_conv_w, m_dn_a_log, m_dn_dt_bias, m_dn_onorm_g, m_dn_w_out, m_sb_w_qkv, m_sb_q_norm_g, m_sb_k_norm_g, m_sb_w_out, m_ffn_w_in, m_ffn_w_out, v_ada_w, v_ada_b, v_norm1_g, v_norm2_g, v_dn_w_in, v_dn_conv_w, v_dn_a_log, v_dn_dt_bias, v_dn_onorm_g, v_dn_w_out, v_sb_w_qkv, v_sb_q_norm_g, v_sb_k_norm_g, v_sb_w_out, v_ffn_w_in, v_ffn_w_out):
    w = dict(ada_w=ada_w, ada_b=ada_b, norm1_g=norm1_g, norm2_g=norm2_g, dn_w_in=dn_w_in, dn_conv_w=dn_conv_w,
             dn_a_log=dn_a_log, dn_dt_bias=dn_dt_bias, dn_onorm_g=dn_onorm_g, dn_w_out=dn_w_out, sb_w_qkv=sb_w_qkv,
             sb_q_norm_g=sb_q_norm_g, sb_k_norm_g=sb_k_norm_g, sb_w_out=sb_w_out, ffn_w_in=ffn_w_in, ffn_w_out=ffn_w_out)
    mom = dict(ada_w=m_ada_w, ada_b=m_ada_b, norm1_g=m_norm1_g, norm2_g=m_norm2_g, dn_w_in=m_dn_w_in,
               dn_conv_w=m_dn_conv_w, dn_a_log=m_dn_a_log, dn_dt_bias=m_dn_dt_bias, dn_onorm_g=m_dn_onorm_g,
               dn_w_out=m_dn_w_out, sb_w_qkv=m_sb_w_qkv, sb_q_norm_g=m_sb_q_norm_g, sb_k_norm_g=m_sb_k_norm_g,
               sb_w_out=m_sb_w_out, ffn_w_in=m_ffn_w_in, ffn_w_out=m_ffn_w_out)
    var = dict(ada_w=v_ada_w, ada_b=v_ada_b, norm1_g=v_norm1_g, norm2_g=v_norm2_g, dn_w_in=v_dn_w_in,
               dn_conv_w=v_dn_conv_w, dn_a_log=v_dn_a_log, dn_dt_bias=v_dn_dt_bias, dn_onorm_g=v_dn_onorm_g,
               dn_w_out=v_dn_w_out, sb_w_qkv=v_sb_w_qkv, sb_q_norm_g=v_sb_q_norm_g, sb_k_norm_g=v_sb_k_norm_g,
               sb_w_out=v_sb_w_out, ffn_w_in=v_ffn_w_in, ffn_w_out=v_ffn_w_out)
    names = list(w)
    c16, conv_full, mod = _gather_small(w, c)
    packs = [_pack_stage(w, st) for st in range(DEPTH + 1)]
    w0 = _full_stage([_all_gather(a, f"gather_weights_0_{p}", False) for p, a in enumerate(packs[0])], 0)
    small = dict(norm1_g=norm1_g, norm2_g=norm2_g, dn_conv_w=conv_full, dn_a_log=dn_a_log, dn_dt_bias=dn_dt_bias,
                 dn_onorm_g=dn_onorm_g, sb_q_norm_g=sb_q_norm_g, sb_k_norm_g=sb_k_norm_g)
    sq, grad_x, sgr, recv = _local_step(x[0], loss_target[0], mod, small, w0, ("packed", packs[1:]))
    loss = lax.psum(sq[0, 0] * (0.5 / D), ("x", "y", "c"))
    grads = {**_reduce_big(recv), **_reduce_small(sgr, c16)}
    delta, new_m, new_v = {}, {}, {}
    for n in names:
        delta[n], new_m[n], new_v[n] = _adamw(w[n], grads[n], mom[n], var[n], f"adamw_{n}")
    return (loss, grad_x[None], *[grads[n] for n in names], *[delta[n] for n in names],
            *[new_m[n] for n in names], *[new_v[n] for n in names])
```

```python
import jax
import jax.numpy as jnp
from jax import lax
from jax.experimental import pallas as pl
from jax.experimental.pallas import tpu as pltpu

F32, BF16 = jnp.float32, jnp.bfloat16

D = 1024
DEPTH = 4
N_MOD = 6
DN_H, DN_D, DN_C, DN_CONV = 8, 128, 64, 4
DN_UNROLL_FWD, DN_UNROLL_BWD = 16, 16
DN_COLS = 4 * DN_H * DN_D + 2 * DN_H
DN_COLS_PAD = 33 * 128
SB_H, SB_D = 16, 64
SB_TILE = 256
SB_HPB = 4
SB_W = SB_HPB * SB_D
DFF = 2816
EPS = 1e-6
NDEV = 8
LANES = 128
VMEM_LIMIT = 56 * 1024 * 1024

ADAM_LR, ADAM_B1, ADAM_B2, ADAM_EPS, ADAM_WD, ADAM_STEP = 0.001, 0.9, 0.999, 1e-08, 0.01, 10

NN = ((1,), (0,))
NT = ((1,), (1,))
TN = ((0,), (0,))
MESH = pl.DeviceIdType.MESH


def _dot(a, b, dims=NN):
    a, b = a.astype(BF16), b.astype(BF16)
    if a.ndim == 2 and b.ndim == 2:
        return lax.dot_general(a, b, (dims, ((), ())), preferred_element_type=F32)
    n = a.shape[0] if a.ndim == 3 else b.shape[0]
    if a.ndim == 2:
        a = jnp.broadcast_to(a, (n,) + a.shape)
    if b.ndim == 2:
        b = jnp.broadcast_to(b, (n,) + b.shape)
    (ca,), (cb,) = dims
    return lax.dot_general(a, b, (((ca + 1,), (cb + 1,)), ((0,), (0,))), preferred_element_type=F32)


def _split(a):
    hi = a.astype(BF16)
    lo = (a - hi.astype(F32)).astype(BF16)
    return hi, lo


def _dot2r(a, m, dims=NN):
    ah, al = _split(a)
    return _dot(ah, m, dims) + _dot(al, m, dims)


def _dot2l(m, b, dims=NN):
    bh, bl = _split(b)
    return _dot(m, bh, dims) + _dot(m, bl, dims)


def _sigmoid(x):
    return 1.0 / (1.0 + jnp.exp(-x))


def _silu(x):
    return x * _sigmoid(x)


def _dsilu(x):
    s = _sigmoid(x)
    return s * (1.0 + x * (1.0 - s))


def _softplus(x):
    return jnp.maximum(x, 0.0) + jnp.log(1.0 + jnp.exp(-jnp.abs(x)))


def _iota(shape, dim):
    return lax.broadcasted_iota(jnp.int32, shape, dim)


def _rowsum(x):
    return jnp.sum(x, axis=-1, keepdims=True)


def _colsum(x):
    return jnp.sum(x, axis=-2, keepdims=True)


def _tile(n, pref):
    if n <= pref:
        return n
    best = None
    for t in range(LANES, pref + 1, LANES):
        if n % t == 0:
            best = t
    assert best is not None, (n, pref)
    return best


def _rtile(r, pref=512):
    best = None
    for t in range(8, min(r, pref) + 1, 8):
        if r % t == 0:
            best = t
    return best if best is not None else r


def _params(sem):
    return pltpu.CompilerParams(dimension_semantics=sem, vmem_limit_bytes=VMEM_LIMIT)


MM_TILE = 1408


def _mm(a, b, mode, out_dtype, name):
    if mode == "nn":
        (M, K), (K2, N) = a.shape, b.shape
        dims = NN
    elif mode == "nt":
        (M, K), (N, K2) = a.shape, b.shape
        dims = NT
    else:
        (K, M), (K2, N) = a.shape, b.shape
        dims = TN
    assert K == K2, (a.shape, b.shape, mode)
    tm, tn, tk = _tile(M, MM_TILE), _tile(N, MM_TILE), _tile(K, MM_TILE)
    nk = K // tk

    def body_single(a_ref, b_ref, o_ref):
        o_ref[...] = _dot(a_ref[...], b_ref[...], dims).astype(o_ref.dtype)

    def body_acc(a_ref, b_ref, o_ref, acc_ref):
        k = pl.program_id(2)
        p = _dot(a_ref[...], b_ref[...], dims)

        @pl.when(k == 0)
        def _():
            acc_ref[...] = p

        @pl.when(k > 0)
        def _():
            acc_ref[...] += p

        @pl.when(k == nk - 1)
        def _():
            o_ref[...] = acc_ref[...].astype(o_ref.dtype)

    if mode == "nn":
        a_spec = pl.BlockSpec((tm, tk), lambda i, j, k: (i, k))
        b_spec = pl.BlockSpec((tk, tn), lambda i, j, k: (k, j))
    elif mode == "nt":
        a_spec = pl.BlockSpec((tm, tk), lambda i, j, k: (i, k))
        b_spec = pl.BlockSpec((tn, tk), lambda i, j, k: (j, k))
    else:
        a_spec = pl.BlockSpec((tk, tm), lambda i, j, k: (k, i))
        b_spec = pl.BlockSpec((tk, tn), lambda i, j, k: (k, j))
    return pl.pallas_call(
        body_single if nk == 1 else body_acc, grid=(M // tm, N // tn, nk), in_specs=[a_spec, b_spec],
        out_specs=pl.BlockSpec((tm, tn), lambda i, j, k: (i, j)),
        out_shape=jax.ShapeDtypeStruct((M, N), out_dtype),
        scratch_shapes=[] if nk == 1 else [pltpu.VMEM((tm, tn), F32)], name=name,
        compiler_params=_params(("parallel", "parallel", "arbitrary")),
    )(a, b)


def _rowwise(fn, name, rows, bcasts, out_rows, out_reds=(), tile=256):
    T = rows[0].shape[0]
    tile = min(tile, T)
    nr, nb, no = len(rows), len(bcasts), len(out_rows)

    def body(*refs):
        rv = [r[...] for r in refs[:nr]]
        bv = [r[...] for r in refs[nr:nr + nb]]
        outs, reds = fn(rv, bv)
        for r, o in zip(refs[nr + nb:nr + nb + no], outs):
            r[...] = o.astype(r.dtype)
        red_refs = refs[nr + nb + no:]
        if red_refs:
            @pl.when(pl.program_id(0) == 0)
            def _():
                for r in red_refs:
                    r[...] = jnp.zeros(r.shape, F32)

            for r, v in zip(red_refs, reds):
                r[...] += v

    in_specs = [pl.BlockSpec((tile, a.shape[1]), lambda i: (i, 0)) for a in rows]
    in_specs += [pl.BlockSpec(b.shape, lambda i: (0, 0)) for b in bcasts]
    out_specs = [pl.BlockSpec((tile, c), lambda i: (i, 0)) for c, _ in out_rows]
    out_specs += [pl.BlockSpec(s, lambda i: (0, 0)) for s in out_reds]
    out_shape = [jax.ShapeDtypeStruct((T, c), dt) for c, dt in out_rows]
    out_shape += [jax.ShapeDtypeStruct(s, F32) for s in out_reds]
    return pl.pallas_call(
        body, grid=(T // tile,), in_specs=in_specs, out_specs=out_specs, out_shape=out_shape, name=name,
        compiler_params=_params(("arbitrary",)),
    )(*rows, *bcasts)


def _norm_mod(x, g, sc, sh, name):
    def fn(rv, bv):
        (xv,), (gv, scv, shv) = rv, bv
        r = lax.rsqrt(jnp.mean(xv * xv, axis=1, keepdims=True) + EPS)
        return [(xv * r * gv) * (1.0 + scv) + shv], []
    return _rowwise(fn, name, [x], [g, sc, sh], [(D, BF16)])[0]


def _norm_mod_bwd(dh, x, dres, g, sc, name):
    def fn(rv, bv):
        (dhv, xv, drv), (gv, scv) = rv, bv
        r = lax.rsqrt(jnp.mean(xv * xv, axis=1, keepdims=True) + EPS)
        xhat = xv * r
        dxhat = dhv * (gv * (1.0 + scv))
        dx = r * (dxhat - xhat * jnp.mean(dxhat * xhat, axis=1, keepdims=True)) + drv
        return [dx], [_colsum(dhv * xhat), _colsum(dhv)]
    return _rowwise(fn, name, [dh, x, dres], [g, sc], [(D, F32)], [(1, D), (1, D)])


def _gate_res(x, y, gt, name):
    def fn(rv, bv):
        return [rv[0] + bv[0] * rv[1]], []
    return _rowwise(fn, name, [x, y], [gt], [(D, F32)])[0]


def _gate_res_norm(x, y, gt, g, sc, sh, name):
    def fn(rv, bv):
        (xv, yv), (gtv, gv, scv, shv) = rv, bv
        xn = xv + gtv * yv
        r = lax.rsqrt(jnp.mean(xn * xn, axis=1, keepdims=True) + EPS)
        return [xn, (xn * r * gv) * (1.0 + scv) + shv], []
    return _rowwise(fn, name, [x, y], [gt, g, sc, sh], [(D, F32), (D, BF16)])


def _gate_res_bwd(dxn, y, gt, name):
    def fn(rv, bv):
        return [rv[0] * bv[0]], [_colsum(rv[0] * rv[1])]
    return _rowwise(fn, name, [dxn, y], [gt], [(D, BF16)], [(1, D)])


def _swiglu_act(u, name):
    def fn(rv, bv):
        uv = rv[0].astype(F32)
        return [_silu(uv[:, :DFF]) * uv[:, DFF:]], []
    return _rowwise(fn, name, [u], [], [(DFF, BF16)])[0]


def _swiglu_act_bwd(da, u, name):
    def fn(rv, bv):
        dav, uv = rv[0].astype(F32), rv[1].astype(F32)
        gate, up = uv[:, :DFF], uv[:, DFF:]
        return [jnp.concatenate([dav * up * _dsilu(gate), dav * _silu(gate)], axis=1)], []
    return _rowwise(fn, name, [da, u], [], [(2 * DFF, BF16)])[0]


def _loss_head(y, target, name):
    def fn(rv, bv):
        err = rv[0] - rv[1]
        return [err * (1.0 / D)], [_colsum(_rowsum(err * err))]
    return _rowwise(fn, name, [y, target], [], [(D, F32)], [(1, 1)])


def _shift_rows(x, s):
    if s == 0:
        return x
    T = x.shape[0]
    r = pltpu.roll(x, s % T, axis=0)
    t = _iota(x.shape, 0)
    keep = (t >= s) if s > 0 else (t < T + s)
    return jnp.where(keep, r, 0.0)


def _dn_prep(pq, pk, pv, pab, cq_ref, ck_ref, cv_ref, alog, dtb, h):
    lane = _iota(pab.shape, 1)
    a_col = _rowsum(jnp.where(lane == h, pab, 0.0))
    b_col = _rowsum(jnp.where(lane == DN_H + h, pab, 0.0))
    lane1 = _iota(alog.shape, 1)
    alog_h = _rowsum(jnp.where(lane1 == h, alog, 0.0))
    dtb_h = _rowsum(jnp.where(lane1 == h, dtb, 0.0))
    pre = a_col + dtb_h
    neg_ea = -jnp.exp(alog_h)
    g = neg_ea * _softplus(pre)
    beta = _sigmoid(b_col)

    def conv(x, w_ref):
        acc = x * w_ref[DN_CONV - 1:DN_CONV, :]
        for i in range(DN_CONV - 1):
            acc = acc + _shift_rows(x, DN_CONV - 1 - i) * w_ref[i:i + 1, :]
        return acc

    xq, xk, xv = conv(pq, cq_ref), conv(pk, ck_ref), conv(pv, cv_ref)
    sq, sk, v = _silu(xq), _silu(xk), _silu(xv)
    rq = lax.rsqrt(_rowsum(sq * sq) + EPS)
    rk = lax.rsqrt(_rowsum(sk * sk) + EPS)
    return dict(g=g, beta=beta, pre=pre, neg_ea=neg_ea, xq=xq, xk=xk, xv=xv, rq=rq, rk=rk,
                qn=sq * rq, kn=sk * rk, v=v)


def _dn_masks():
    C = DN_C
    r, c = _iota((C, C), 0), _iota((C, C), 1)
    incl = r >= c
    strict = r > c
    blk16 = jnp.right_shift(r, 4) == jnp.right_shift(c, 4)
    blk32 = jnp.right_shift(r, 5) == jnp.right_shift(c, 5)
    return dict(incl=incl, strict=strict, upper=r <= c, blk16=blk16, blk32=blk32,
                tri=incl.astype(BF16), triT=(r <= c).astype(BF16), ones=jnp.ones((C, C), BF16),
                eye=(r == c).astype(F32), last=_iota((C, 1), 0) == C - 1)


def _tri_inverse(A, mk):
    P = -jnp.where(mk["blk16"], A, 0.0)
    X = mk["eye"] + P
    for _ in range(3):
        P = _dot(P, P)
        X = X + _dot(X, P)
    off1 = jnp.where(mk["blk32"] & (~mk["blk16"]), A, 0.0)
    X = X - _dot(_dot(X, off1), X)
    off2 = jnp.where(mk["blk32"], 0.0, A)
    X = X - _dot(_dot(X, off2), X)
    return X


def _dn_local(qc, kc, vc, gc, bc, mk):
    C = DN_C
    gm = jnp.broadcast_to(gc, gc.shape[:-1] + (C,))
    Gc = _dot2l(mk["tri"], gm)
    Gr = _dot2l(mk["ones"], jnp.where(mk["upper"], gm, 0.0))
    Dm = jnp.where(mk["incl"], jnp.exp(jnp.where(mk["incl"], Gc - Gr, 0.0)), 0.0)
    Gcol = jnp.max(Gc, axis=-1, keepdims=True)
    Gl = _colsum(jnp.where(mk["last"], Gcol, 0.0))
    eG = jnp.exp(Gcol)
    eT = jnp.exp(Gl - Gcol)
    gl = jnp.exp(Gl)
    kb = kc * bc
    vb = vc * bc
    KK = _dot(kb, kc, NT)
    Tinv = _tri_inverse(jnp.where(mk["strict"], KK * Dm, 0.0), mk)
    KBE = kb * eG
    QK = _dot(qc, kc, NT)
    return dict(Dm=Dm, eG=eG, eT=eT, gl=gl, kb=kb, vb=vb, KK=KK, Tinv=Tinv, KBE=KBE, U=_dot(Tinv, vb),
                W=_dot(Tinv, KBE), QK=QK, attn=QK * Dm, QD=qc * eG, KT=kc * eT)


def _dn_recur(f, S):
    vnew = f["U"] - _dot(f["W"], S)
    o = _dot(f["QD"], S) + _dot(f["attn"], vnew)
    return o, S * f["gl"] + _dot(f["KT"], vnew, TN), vnew


def _dn_bwd_chain(f, do, dS):
    dvnew = _dot(f["KT"], dS) + _dot(f["attn"], do, TN)
    return dvnew, dS * f["gl"] + _dot(f["QD"], do, TN) - _dot(f["W"], dvnew, TN)


def _dn_bwd_rest(qc, kc, vc, bc, f, S, vnew, do, dS, dvnew, mk):
    C = DN_C
    Dm, eG, eT, gl, kb, vb, KK, Tinv, KBE, QK = (
        f[n] for n in ("Dm", "eG", "eT", "gl", "kb", "vb", "KK", "Tinv", "KBE", "QK"))
    dKT = _dot(vnew, dS, NT)
    dgl = _colsum(_rowsum(dS * S))
    dQD = _dot(do, S, NT)
    dattn = _dot(do, vnew, NT)
    dU = dvnew
    dW = -_dot(dvnew, S, NT)
    dQK = dattn * Dm
    dD = dattn * QK
    dq = _dot(dQK, kc)
    dk = _dot(dQK, qc, TN)
    dTinv = _dot(dU, vb, NT) + _dot(dW, KBE, NT)
    dvb = _dot(Tinv, dU, TN)
    dKBE = _dot(Tinv, dW, TN)
    dA = -_dot(_dot(Tinv, dTinv, TN), Tinv, NT)
    dA = jnp.where(mk["strict"], dA, 0.0)
    dKK = dA * Dm
    dD = dD + dA * KK
    dkb = _dot(dKK, kc) + dKBE * eG
    dk = dk + _dot(dKK, kb, TN)
    deG = _rowsum(dKBE * kb)
    dk = dk + dkb * bc
    dbeta = _rowsum(dkb * kc) + _rowsum(dvb * vc)
    dv = dvb * bc
    dq = dq + dQD * eG
    deG = deG + _rowsum(dQD * qc)
    dk = dk + dKT * eT
    deT = _rowsum(dKT * kc)
    dGcol = deG * eG - deT * eT
    dGl = _colsum(deT * eT) + dgl * gl
    Y = dD * Dm
    ycol = jnp.max(_dot2r(Y, mk["ones"], TN), axis=-1, keepdims=True)
    dGcol = dGcol + _rowsum(Y) - ycol
    dGcol = dGcol + jnp.where(mk["last"], dGl, 0.0)
    dg = jnp.max(_dot2l(mk["triT"], jnp.broadcast_to(dGcol, dGcol.shape[:-1] + (C,))), axis=-1, keepdims=True)
    return dq, dk, dv, dg, dbeta


def _dn_core_fwd(proj, conv_w, alog, dtb, og, name, comm=None):
    T = proj.shape[0]
    N = T // DN_C

    def body(pq_ref, pk_ref, pv_ref, pz_ref, pab_ref, cq_ref, ck_ref, cv_ref, alog_ref, dtb_ref, og_ref,
             out_ref, o_ref, st_ref, q_s, k_s, v_s, g_s, b_s, S_s):
        h = pl.program_id(0)
        p = _dn_prep(pq_ref[...], pk_ref[...], pv_ref[...], pab_ref[...], cq_ref, ck_ref, cv_ref,
                     alog_ref[...], dtb_ref[...], h)
        q_s[...] = p["qn"] * (DN_D ** -0.5)
        k_s[...] = p["kn"]
        v_s[...] = p["v"]
        g_s[...] = p["g"]
        b_s[...] = p["beta"]
        S_s[...] = jnp.zeros(S_s.shape, F32)
        mk = _dn_masks()

        nu = min(DN_UNROLL_FWD, N)

        def step(it, carry):
            rows = pl.ds(pl.multiple_of(it * (nu * DN_C), nu * DN_C), nu * DN_C)
            loc = _dn_local(*(r[rows, :].reshape(nu, DN_C, r.shape[1]) for r in (q_s, k_s, v_s, g_s, b_s)), mk)
            S = S_s[...]
            outs = []
            for u in range(nu):
                st_ref[0, it * nu + u] = S
                o, S, _ = _dn_recur({n: v[u] for n, v in loc.items()}, S)
                outs.append(o)
            o_ref[rows, :] = jnp.concatenate(outs, axis=0)
            S_s[...] = S
            return carry

        lax.fori_loop(0, N // nu, step, 0)
        o = o_ref[...]
        ro = lax.rsqrt(jnp.mean(o * o, axis=1, keepdims=True) + EPS)
        out_ref[...] = ((o * ro * og_ref[...]) * _silu(pz_ref[...])).astype(out_ref.dtype)

    col = lambda k: pl.BlockSpec((T, DN_D), lambda h: (0, k * DN_H + h))
    cw = lambda k: pl.BlockSpec((DN_CONV, DN_D), lambda h: (0, k * DN_H + h))
    small = pl.BlockSpec((1, LANES), lambda h: (0, 0))
    return _grid_call(
        body, comm, grid=(DN_H,),
        in_specs=[col(0), col(1), col(2), col(3), pl.BlockSpec((T, LANES), lambda h: (0, 4 * DN_H)),
                  cw(0), cw(1), cw(2), small, small, small],
        out_specs=[pl.BlockSpec((T, DN_D), lambda h: (0, h)), pl.BlockSpec((T, DN_D), lambda h: (0, h)),
                   pl.BlockSpec((1, N, DN_D, DN_D), lambda h: (h, 0, 0, 0))],
        out_shape=[jax.ShapeDtypeStruct((T, D), BF16), jax.ShapeDtypeStruct((T, D), F32),
                   jax.ShapeDtypeStruct((DN_H, N, DN_D, DN_D), F32)],
        scratch_shapes=[pltpu.VMEM((T, DN_D), F32)] * 3 + [pltpu.VMEM((T, 1), F32)] * 2 + [pltpu.VMEM((DN_D, DN_D), F32)],
        name=name, args=(proj, proj, proj, proj, proj, conv_w, conv_w, conv_w, alog, dtb, og))


def _dn_core_bwd(proj, conv_w, alog, dtb, og, o, states, dout, name, comm=None):
    T = proj.shape[0]
    N = T // DN_C

    def body(pq_ref, pk_ref, pv_ref, pz_ref, pab_ref, cq_ref, ck_ref, cv_ref, alog_ref, dtb_ref, og_ref,
             o_ref, st_ref, dout_ref,
             dpq_ref, dpk_ref, dpv_ref, dpz_ref, dpab_ref, dcq_ref, dck_ref, dcv_ref, dalog_ref, ddtb_ref, dog_ref,
             q_s, k_s, v_s, g_s, b_s, do_s, dS_s):
        h = pl.program_id(0)
        scale = DN_D ** -0.5

        def prep():
            return _dn_prep(pq_ref[...], pk_ref[...], pv_ref[...], pab_ref[...], cq_ref, ck_ref, cv_ref,
                            alog_ref[...], dtb_ref[...], h)

        p = prep()
        q_s[...] = p["qn"] * scale
        k_s[...] = p["kn"]
        v_s[...] = p["v"]
        g_s[...] = p["g"]
        b_s[...] = p["beta"]
        del p

        o = o_ref[...]
        z = pz_ref[...]
        dout = dout_ref[...]
        ogv = og_ref[...]
        ro = lax.rsqrt(jnp.mean(o * o, axis=1, keepdims=True) + EPS)
        on = o * ro
        dy = dout * _silu(z)
        dpz_ref[...] = (dout * (on * ogv) * _dsilu(z)).astype(dpz_ref.dtype)
        dyg = dy * ogv
        do_s[...] = ro * (dyg - on * jnp.mean(dyg * on, axis=1, keepdims=True))
        dog_h = _colsum(dy * on)

        dS_s[...] = jnp.zeros(dS_s.shape, F32)
        mk = _dn_masks()

        nu = min(DN_UNROLL_BWD, N)

        def step(it, carry):
            c0 = (N // nu - 1 - it) * nu
            rows = pl.ds(pl.multiple_of(c0 * DN_C, nu * DN_C), nu * DN_C)
            q, k, v, g, b, do = (r[rows, :].reshape(nu, DN_C, r.shape[1]) for r in (q_s, k_s, v_s, g_s, b_s, do_s))
            loc = _dn_local(q, k, v, g, b, mk)
            Ss = st_ref[0, pl.ds(c0, nu)]
            vnew = loc["U"] - _dot(loc["W"], Ss)
            dS = dS_s[...]
            dS_in, dvnew = [None] * nu, [None] * nu
            for u in reversed(range(nu)):
                dS_in[u] = dS
                dvnew[u], dS = _dn_bwd_chain({n: x[u] for n, x in loc.items()}, do[u], dS)
            dS_s[...] = dS
            grads = _dn_bwd_rest(q, k, v, b, loc, Ss, vnew, do, jnp.stack(dS_in), jnp.stack(dvnew), mk)
            for r, d in zip((q_s, k_s, v_s, g_s, b_s), grads):
                r[rows, :] = d.reshape(nu * DN_C, r.shape[1])
            return carry

        lax.fori_loop(0, N // nu, step, 0)

        p = prep()
        pq, pk, pv = pq_ref[...], pk_ref[...], pv_ref[...]
        dqn = q_s[...] * scale
        dkn = k_s[...]
        qn, kn = p["qn"], p["kn"]
        dsq = p["rq"] * (dqn - qn * _rowsum(dqn * qn))
        dsk = p["rk"] * (dkn - kn * _rowsum(dkn * kn))
        dxq = dsq * _dsilu(p["xq"])
        dxk = dsk * _dsilu(p["xk"])
        dxv = v_s[...] * _dsilu(p["xv"])

        def conv_bwd(dx, x, w_ref, dp_ref, dc_ref):
            acc = dx * w_ref[DN_CONV - 1:DN_CONV, :]
            dc_ref[DN_CONV - 1:DN_CONV, :] = _colsum(dx * x)
            for i in range(DN_CONV - 1):
                s = DN_CONV - 1 - i
                acc = acc + _shift_rows(dx, -s) * w_ref[i:i + 1, :]
                dc_ref[i:i + 1, :] = _colsum(dx * _shift_rows(x, s))
            dp_ref[...] = acc.astype(dp_ref.dtype)

        conv_bwd(dxq, pq, cq_ref, dpq_ref, dcq_ref)
        conv_bwd(dxk, pk, ck_ref, dpk_ref, dck_ref)
        conv_bwd(dxv, pv, cv_ref, dpv_ref, dcv_ref)

        dg = g_s[...]
        beta = p["beta"]
        da_raw = dg * p["neg_ea"] * _sigmoid(p["pre"])
        db_raw = b_s[...] * beta * (1.0 - beta)
        lane = _iota((T, LANES), 1)
        contrib = jnp.where(lane == h, da_raw, 0.0) + jnp.where(lane == DN_H + h, db_raw, 0.0)
        lane1 = _iota((1, LANES), 1)
        dalog_h = jnp.where(lane1 == h, _colsum(dg * p["g"]), 0.0)
        ddtb_h = jnp.where(lane1 == h, _colsum(da_raw), 0.0)

        @pl.when(h == 0)
        def _():
            dpab_ref[...] = contrib.astype(dpab_ref.dtype)
            dalog_ref[...] = dalog_h
            ddtb_ref[...] = ddtb_h
            dog_ref[...] = dog_h

        @pl.when(h > 0)
        def _():
            dpab_ref[...] += contrib.astype(dpab_ref.dtype)
            dalog_ref[...] += dalog_h
            ddtb_ref[...] += ddtb_h
            dog_ref[...] += dog_h

    col = lambda k: pl.BlockSpec((T, DN_D), lambda h: (0, k * DN_H + h))
    cw = lambda k: pl.BlockSpec((DN_CONV, DN_D), lambda h: (0, k * DN_H + h))
    small = pl.BlockSpec((1, LANES), lambda h: (0, 0))
    ab = pl.BlockSpec((T, LANES), lambda h: (0, 4 * DN_H))
    hcol = pl.BlockSpec((T, DN_D), lambda h: (0, h))
    outs = _grid_call(
        body, comm, grid=(DN_H,),
        in_specs=[col(0), col(1), col(2), col(3), ab, cw(0), cw(1), cw(2), small, small, small,
                  hcol, pl.BlockSpec((1, N, DN_D, DN_D), lambda h: (h, 0, 0, 0)), hcol],
        out_specs=[hcol, hcol, hcol, hcol, pl.BlockSpec((T, LANES), lambda h: (0, 0)),
                   pl.BlockSpec((DN_CONV, DN_D), lambda h: (0, h)), pl.BlockSpec((DN_CONV, DN_D), lambda h: (0, h)),
                   pl.BlockSpec((DN_CONV, DN_D), lambda h: (0, h)), small, small, small],
        out_shape=[jax.ShapeDtypeStruct((T, D), BF16)] * 4 + [jax.ShapeDtypeStruct((T, LANES), BF16)]
                  + [jax.ShapeDtypeStruct((DN_CONV, D), F32)] * 3 + [jax.ShapeDtypeStruct((1, LANES), F32)] * 3,
        scratch_shapes=[pltpu.VMEM((T, DN_D), F32)] * 3 + [pltpu.VMEM((T, 1), F32)] * 2
                       + [pltpu.VMEM((T, DN_D), F32), pltpu.VMEM((DN_D, DN_D), F32)],
        name=name, args=(proj, proj, proj, proj, proj, conv_w, conv_w, conv_w, alog, dtb, og, o, states, dout))
    dpq, dpk, dpv, dpz, dpab, dcq, dck, dcv, dalog, ddtb, dog = outs[:11]
    dproj = jnp.concatenate([dpq, dpk, dpv, dpz, dpab], axis=1)
    dconv = jnp.concatenate([dcq, dck, dcv], axis=1)
    return dproj, dconv, dalog, ddtb, dog, (outs[11] if comm else None)


def _sb_head_of(shape):
    return jnp.right_shift(_iota(shape, 1), 6)


def _sb_head_sums(x, head):
    out = jnp.zeros_like(x)
    for hh in range(SB_HPB):
        out = jnp.where(head == hh, _rowsum(jnp.where(head == hh, x, 0.0)), out)
    return out


def _sb_head_norm(x, head):
    r = lax.rsqrt(_sb_head_sums(x * x, head) * (1.0 / SB_D) + EPS)
    return x * r, r


def _sb_fill(q_ref, k_ref, v_ref, qg_ref, kg_ref, qs_s, kn_s, v_s):
    head = _sb_head_of(q_ref.shape)
    qs_s[...] = (_sb_head_norm(q_ref[...], head)[0] * qg_ref[...] * (SB_D ** -0.5)).astype(BF16)
    kn_s[...] = (_sb_head_norm(k_ref[...], head)[0] * kg_ref[...]).astype(BF16)
    v_s[...] = v_ref[...].astype(BF16)


def _sb_head_masks(dtype):
    return jnp.stack([(_sb_head_of((1, SB_W)) == hh).astype(dtype) for hh in range(SB_HPB)])


def _sb_core_fwd(qkv, qg, kg, name, comm=None):
    T = qkv.shape[0]
    B = min(SB_TILE, T)
    NB = T // B
    NP = SB_H // SB_HPB

    def body(q_ref, k_ref, v_ref, qg_ref, kg_ref, o_ref, tot_ref, qs_s, kn_s, v_s):
        _sb_fill(q_ref, k_ref, v_ref, qg_ref, kg_ref, qs_s, kn_s, v_s)
        r, c = _iota((B, B), 0), _iota((B, B), 1)
        causal = c < r
        m_after = (r > c).astype(BF16)
        head_b = _sb_head_of((B, SB_W))
        hm = _sb_head_masks(BF16)

        def tile(qb, kj, vj, R, acc, diag):
            z = _dot(qb, kj, NT)
            sp = _softplus(z)
            ls = z - sp
            lm = jnp.where(causal, -sp, 0.0) if diag else -sp
            cs = _dot2r(lm, m_after) + R
            a = jnp.exp(ls + cs)
            if diag:
                a = jnp.where(causal, a, 0.0)
            return R + _rowsum(lm), acc + jnp.sum(_dot(a, vj), axis=0)

        def qblock(i, carry):
            rows_i = pl.ds(pl.multiple_of(i * B, B), B)
            qb = qs_s[rows_i, :][None] * hm

            def step(rows_j, st, diag):
                return tile(qb, kn_s[rows_j, :], v_s[rows_j, :][None] * hm, st[0], st[1], diag)

            def keys(j):
                return pl.ds(pl.multiple_of(j * B, B), B)

            st = step(rows_i, (jnp.zeros((SB_HPB, B, 1), F32), jnp.zeros((B, SB_W), F32)), True)
            st = lax.fori_loop(
                0, i // 2, lambda s, st: step(keys(i - 2 - 2 * s), step(keys(i - 1 - 2 * s), st, False), False), st)
            st = lax.cond(i % 2 == 1, lambda st: step(keys(0), st, False), lambda st: st, st)
            o_ref[rows_i, :] = st[1].astype(o_ref.dtype)
            tot = jnp.zeros((B, SB_W), F32)
            for hh in range(SB_HPB):
                tot = jnp.where(head_b == hh, st[0][hh], tot)
            tot_ref[rows_i, :] = tot
            return carry

        lax.fori_loop(0, NB, qblock, 0)

    blk = lambda k: pl.BlockSpec((T, SB_W), lambda p: (0, k * NP + p))
    small = pl.BlockSpec((1, SB_W), lambda p: (0, 0))
    return _grid_call(
        body, comm, grid=(NP,), in_specs=[blk(0), blk(1), blk(2), small, small],
        out_specs=[pl.BlockSpec((T, SB_W), lambda p: (0, p))] * 2,
        out_shape=[jax.ShapeDtypeStruct((T, D), BF16), jax.ShapeDtypeStruct((T, D), F32)],
        scratch_shapes=[pltpu.VMEM((T, SB_W), BF16)] * 3, name=name, args=(qkv, qkv, qkv, qg, kg))


def _sb_core_bwd(qkv, qg, kg, tot, dout, name, comm=None):
    T = qkv.shape[0]
    B = min(SB_TILE, T)
    NB = T // B
    NP = SB_H // SB_HPB

    def body(q_ref, k_ref, v_ref, qg_ref, kg_ref, tot_ref, do_ref, dq_ref, dk_ref, dv_ref, dqg_ref, dkg_ref,
             qs_s, kn_s, v_s, dqn_s, dkn_s, dvv_s):
        p = pl.program_id(0)
        _sb_fill(q_ref, k_ref, v_ref, qg_ref, kg_ref, qs_s, kn_s, v_s)
        dkn_s[...] = jnp.zeros(dkn_s.shape, F32)
        dvv_s[...] = jnp.zeros(dvv_s.shape, F32)
        r, c = _iota((B, B), 0), _iota((B, B), 1)
        causal = c < r
        m_upto = (r <= c).astype(BF16)
        m_before = (r < c).astype(BF16)
        lane_b = _iota((B, SB_W), 1)
        hm, hmf = _sb_head_masks(BF16), _sb_head_masks(F32)

        def tile(qb, dob, tot_h, kj, vj, PL, P, dq, diag):
            z = _dot(qb, kj, NT)
            sp = _softplus(z)
            ls = z - sp
            lm = jnp.where(causal, -sp, 0.0) if diag else -sp
            cs = tot_h - PL - _dot2r(lm, m_upto)
            a = jnp.exp(ls + cs)
            if diag:
                a = jnp.where(causal, a, 0.0)
            e = _dot(dob, vj, NT) * a
            E = _dot(e, m_before) + P
            sig = jnp.exp(ls)
            dz = e * (1.0 - sig) - E * sig
            if diag:
                dz = jnp.where(causal, dz, 0.0)
            dq = dq + jnp.sum(_dot(dz, kj) * hmf, axis=0)
            return (PL + _rowsum(lm), P + _rowsum(e), dq), jnp.sum(_dot(dz, qb, TN), axis=0), jnp.sum(_dot(a, dob, TN), axis=0)

        def qblock(i, carry):
            rows_i = pl.ds(pl.multiple_of(i * B, B), B)
            totb = tot_ref[rows_i, :]
            tot_h = jnp.stack([_rowsum(jnp.where(lane_b == hh * SB_D, totb, 0.0)) for hh in range(SB_HPB)])
            qb = qs_s[rows_i, :][None] * hm
            dob = do_ref[rows_i, :][None] * hm

            def step(rows_j, st, diag):
                st, dk, dv = tile(qb, dob, tot_h, kn_s[rows_j, :], v_s[rows_j, :][None] * hm, st[0], st[1], st[2], diag)
                dkn_s[rows_j, :] += dk
                dvv_s[rows_j, :] += dv
                return st

            def keys(j):
                return pl.ds(pl.multiple_of(j * B, B), B)

            zero = (jnp.zeros((SB_HPB, B, 1), F32), jnp.zeros((SB_HPB, B, 1), F32), jnp.zeros((B, SB_W), F32))
            st = lax.fori_loop(0, i // 2, lambda p, st: step(keys(2 * p + 1), step(keys(2 * p), st, False), False), zero)
            st = lax.cond(i % 2 == 1, lambda st: step(keys(i - 1), st, False), lambda st: st, st)
            st = step(rows_i, st, True)
            dqn_s[rows_i, :] = st[2] * (SB_D ** -0.5)
            return carry

        lax.fori_loop(0, NB, qblock, 0)

        head = _sb_head_of((T, SB_W))

        def norm_bwd(dn, x_ref, g):
            xh, rr = _sb_head_norm(x_ref[...], head)
            t = dn * g
            return rr * (t - xh * (_sb_head_sums(t * xh, head) * (1.0 / SB_D))), _colsum(dn * xh)

        dq, dqg = norm_bwd(dqn_s[...], q_ref, qg_ref[...])
        dq_ref[...] = dq.astype(dq_ref.dtype)
        dk, dkg = norm_bwd(dkn_s[...], k_ref, kg_ref[...])
        dk_ref[...] = dk.astype(dk_ref.dtype)
        dv_ref[...] = dvv_s[...].astype(dv_ref.dtype)

        @pl.when(p == 0)
        def _():
            dqg_ref[...] = dqg
            dkg_ref[...] = dkg

        @pl.when(p > 0)
        def _():
            dqg_ref[...] += dqg
            dkg_ref[...] += dkg

    blk = lambda k: pl.BlockSpec((T, SB_W), lambda p: (0, k * NP + p))
    small = pl.BlockSpec((1, SB_W), lambda p: (0, 0))
    own = pl.BlockSpec((T, SB_W), lambda p: (0, p))
    outs = _grid_call(
        body, comm, grid=(NP,), in_specs=[blk(0), blk(1), blk(2), small, small, own, own],
        out_specs=[own, own, own, small, small],
        out_shape=[jax.ShapeDtypeStruct((T, D), BF16)] * 3 + [jax.ShapeDtypeStruct((1, SB_W), F32)] * 2,
        scratch_shapes=[pltpu.VMEM((T, SB_W), BF16)] * 3 + [pltpu.VMEM((T, SB_W), F32)] * 3,
        name=name, args=(qkv, qkv, qkv, qg, kg, tot, dout))
    dq, dk, dv, dqg, dkg = outs[:5]
    return jnp.concatenate([dq, dk, dv], axis=1), dqg, dkg, (outs[5] if comm else None)


def _ada_fwd(c16, ada_w, ada_b_cols, name):
    L, _, cols = ada_w.shape

    def body(c_ref, w_ref, b_ref, o_ref):
        o_ref[0] = _dot(_silu(c_ref[...]), w_ref[0]) + b_ref[0]

    return pl.pallas_call(
        body, grid=(L,),
        in_specs=[pl.BlockSpec((16, D), lambda i: (0, 0)), pl.BlockSpec((1, D, cols), lambda i: (i, 0, 0)),
                  pl.BlockSpec((1, 1, cols), lambda i: (i, 0, 0))],
        out_specs=pl.BlockSpec((1, 16, cols), lambda i: (i, 0, 0)),
        out_shape=jax.ShapeDtypeStruct((L, 16, cols), F32), name=name, compiler_params=_params(("arbitrary",)),
    )(c16, ada_w, ada_b_cols)


def _ada_bwd(c16, dmod16, name):
    L, _, cols = dmod16.shape

    def body(c_ref, d_ref, o_ref):
        o_ref[0] = _dot(_silu(c_ref[...]), d_ref[0], TN)

    return pl.pallas_call(
        body, grid=(L,),
        in_specs=[pl.BlockSpec((16, D), lambda i: (0, 0)), pl.BlockSpec((1, 16, cols), lambda i: (i, 0, 0))],
        out_specs=pl.BlockSpec((1, D, cols), lambda i: (i, 0, 0)),
        out_shape=jax.ShapeDtypeStruct((L, D, cols), F32), name=name, compiler_params=_params(("arbitrary",)),
    )(c16, dmod16)


def _sum_sources(x, name):
    n, R, C = x.shape
    tile = _sum_tile(R) or R

    def body(x_ref, o_ref):
        acc = x_ref[0].astype(F32)
        for k in range(1, n):
            acc = acc + x_ref[k].astype(F32)
        o_ref[...] = acc

    return pl.pallas_call(
        body, grid=(R // tile,), in_specs=[pl.BlockSpec((n, tile, C), lambda i: (0, i, 0))],
        out_specs=pl.BlockSpec((tile, C), lambda i: (i, 0)), out_shape=jax.ShapeDtypeStruct((R, C), F32),
        name=name, compiler_params=_params(("parallel",)),
    )(x)


def _adamw(w, g, m, v, name):
    shape = w.shape
    C = shape[-1]
    R = w.size // C
    w2, g2, m2, v2 = (a.reshape(R, C) for a in (w, g, m, v))
    tile = _rtile(R)
    c1 = 1.0 / (1.0 - ADAM_B1 ** ADAM_STEP)
    c2 = 1.0 / (1.0 - ADAM_B2 ** ADAM_STEP)

    def body(w_ref, g_ref, m_ref, v_ref, d_ref, nm_ref, nv_ref):
        gv = g_ref[...]
        nm = ADAM_B1 * m_ref[...] + (1.0 - ADAM_B1) * gv
        nv = ADAM_B2 * v_ref[...] + (1.0 - ADAM_B2) * (gv * gv)
        d_ref[...] = -ADAM_LR * ((nm * c1) / (jnp.sqrt(nv * c2) + ADAM_EPS) + ADAM_WD * w_ref[...])
        nm_ref[...] = nm
        nv_ref[...] = nv

    spec = pl.BlockSpec((tile, C), lambda i: (i, 0))
    outs = pl.pallas_call(
        body, grid=(R // tile,), in_specs=[spec] * 4, out_specs=[spec] * 3,
        out_shape=[jax.ShapeDtypeStruct((R, C), F32)] * 3, name=name, compiler_params=_params(("parallel",)),
    )(w2, g2, m2, v2)
    return tuple(o.reshape(shape) for o in outs)


def _gather_steps(x_ref, out_ref, send_sems, recv_sems, local_sem):
    mx, my, mc = lax.axis_index("x"), lax.axis_index("y"), lax.axis_index("c")
    me, sibling = (mx, my, mc), (mx, my, 1 - mc)
    chips = [(1 - mx, my), (mx, 1 - my), (1 - mx, 1 - my)]

    def slot(px, py, pc):
        return out_ref.at[4 * px + 2 * py + pc]

    def copy(k, block, to, src=None):
        return pltpu.make_async_remote_copy(
            src_ref=slot(*block) if src is None else src, dst_ref=slot(*block),
            send_sem=send_sems.at[k], recv_sem=recv_sems.at[k], device_id=to, device_id_type=MESH)

    mine = pltpu.make_async_copy(x_ref, slot(*me), local_sem)
    first = [copy(0, me, sibling, src=x_ref)] + [copy(1 + j, me, (*chip, mc), src=x_ref) for j, chip in enumerate(chips)]
    passed = [copy(4 + j, (*chip, mc), sibling) for j, chip in enumerate(chips)]

    def start():
        mine.start()
        for cp in first:
            cp.start()

    def forward():
        for j, chip in enumerate(chips):
            copy(1 + j, (*chip, mc), me).wait_recv()
            passed[j].start()

    def finish():
        copy(0, sibling, me).wait_recv()
        for j, chip in enumerate(chips):
            copy(4 + j, (*chip, 1 - mc), me).wait_recv()
        for cp in first + passed:
            cp.wait_send()
        mine.wait()

    return start, forward, finish


def _exchange_steps(x_ref, out_ref, send_sems, recv_sems, local_sem):
    mx, my, mc = lax.axis_index("x"), lax.axis_index("y"), lax.axis_index("c")
    me = 4 * mx + 2 * my + mc
    mine = pltpu.make_async_copy(x_ref.at[me], out_ref.at[me], local_sem)
    copies = []
    for k in range(1, NDEV):
        px, py, pc = mx ^ (k >> 2), my ^ ((k >> 1) & 1), mc ^ (k & 1)
        copies.append(pltpu.make_async_remote_copy(
            src_ref=x_ref.at[4 * px + 2 * py + pc], dst_ref=out_ref.at[me], send_sem=send_sems.at[k - 1],
            recv_sem=recv_sems.at[k - 1], device_id=(px, py, pc), device_id_type=MESH))

    def start():
        mine.start()
        for cp in copies:
            cp.start()

    def finish():
        for cp in copies:
            cp.wait_recv()
        for cp in copies:
            cp.wait_send()
        mine.wait()

    return start, finish


def _comm_sems():
    return [pltpu.SemaphoreType.DMA((7,)), pltpu.SemaphoreType.DMA((7,)), pltpu.SemaphoreType.DMA]


def _comm_out_shapes(comm):
    kind, xs = comm
    return [jax.ShapeDtypeStruct(((NDEV,) + x.shape) if kind == "gather" else x.shape, x.dtype) for x in xs]


def _ride_along(comm, step, n_steps, refs, at_end):
    if comm[0] == "gather":
        start, forward, finish = _gather_steps(*refs)
        todo = [(n_steps - 1, finish)] if at_end else [(0, start), (n_steps - 1, forward)]
    else:
        start, finish = _exchange_steps(*refs)
        todo = [(n_steps - 1, finish)] if at_end else [(0, start)]
    for at, fn in todo:
        pl.when(step == at)(fn)


def _grid_call(body, comm, *, grid, in_specs, out_specs, out_shape, scratch_shapes, name, args):
    if comm is None:
        return pl.pallas_call(body, grid=grid, in_specs=in_specs, out_specs=out_specs, out_shape=out_shape,
                              scratch_shapes=scratch_shapes, name=name, compiler_params=_params(("arbitrary",)))(*args)
    n_in, n_out, n_scr, n_steps, n_parts = len(in_specs), len(out_specs), len(scratch_shapes), grid[0], len(comm[1])

    def with_comm(*refs):
        ins, xs = refs[:n_in], refs[n_in:n_in + n_parts]
        outs, gots = refs[n_in + n_parts:n_in + n_parts + n_out], refs[n_in + n_parts + n_out:n_in + 2 * n_parts + n_out]
        scr, sems = refs[n_in + 2 * n_parts + n_out:][:n_scr], refs[n_in + 2 * n_parts + n_out + n_scr:]
        part_refs = [(xs[p], gots[p]) + tuple(sems[3 * p:3 * p + 3]) for p in range(n_parts)]
        step = pl.program_id(0)
        for pr in part_refs:
            _ride_along(comm, step, n_steps, pr, False)
        body(*ins, *outs, *scr)
        for pr in part_refs:
            _ride_along(comm, step, n_steps, pr, True)

    hbm = pl.BlockSpec(memory_space=pl.ANY)
    res = pl.pallas_call(
        with_comm, grid=grid, in_specs=list(in_specs) + [hbm] * n_parts, out_specs=list(out_specs) + [hbm] * n_parts,
        out_shape=list(out_shape) + _comm_out_shapes(comm), scratch_shapes=list(scratch_shapes) + _comm_sems() * n_parts,
        name=name, compiler_params=_params(("arbitrary",)))(*args, *comm[1])
    return list(res[:n_out]) + [list(res[n_out:])]


def _all_gather(x, name, in_vmem):
    def body(x_ref, out_ref, send_sems, recv_sems, local_sem):
        start, forward, finish = _gather_steps(x_ref, out_ref, send_sems, recv_sems, local_sem)
        start()
        forward()
        finish()

    space = pltpu.VMEM if in_vmem else pl.ANY
    return pl.pallas_call(
        body, out_shape=jax.ShapeDtypeStruct((NDEV,) + x.shape, x.dtype),
        in_specs=[pl.BlockSpec(memory_space=space)], out_specs=pl.BlockSpec(memory_space=space),
        scratch_shapes=_comm_sems(), name=name, compiler_params=pltpu.CompilerParams(vmem_limit_bytes=VMEM_LIMIT),
    )(x)


def _all_to_all(x, name):
    def body(x_ref, out_ref, send_sems, recv_sems, local_sem):
        start, finish = _exchange_steps(x_ref, out_ref, send_sems, recv_sems, local_sem)
        start()
        finish()

    return pl.pallas_call(
        body, out_shape=jax.ShapeDtypeStruct(x.shape, x.dtype),
        in_specs=[pl.BlockSpec(memory_space=pl.ANY)], out_specs=pl.BlockSpec(memory_space=pl.ANY),
        scratch_shapes=_comm_sems(), name=name, compiler_params=pltpu.CompilerParams(vmem_limit_bytes=VMEM_LIMIT),
    )(x)


def _stage_parts(st):
    parts = []
    if st >= 1:
        parts += [("dn_w_out" if (st - 1) % 2 == 0 else "sb_w_out", (st - 1) // 2, (D // NDEV, D), "rows"),
                  ("ffn_w_in", st - 1, (D, 2 * DFF // NDEV), "colsT"), ("ffn_w_out", st - 1, (DFF // NDEV, D), "rows")]
    if st < DEPTH:
        parts += [("dn_w_in", st // 2, (D, DN_COLS // NDEV), "colsT") if st % 2 == 0 else
                  ("sb_w_qkv", st // 2, (D, 3 * D // NDEV), "colsT")]
    return parts


_IN_PROJ = ("dn_w_in", "sb_w_qkv")


def _sum_tile(rows):
    for t in range(512, 191, -16):
        if rows % t == 0:
            return t
    return None


def _part_rows(s):
    return s[0] * s[1] // D


def _pack_stage(w, st):
    return [(w[n][j].T if kind == "colsT" else w[n][j]).astype(BF16).reshape(_part_rows(s), D)
            for n, j, s, kind in _stage_parts(st)]


def _full_stage(gs, st):
    out = {n: g.reshape((NDEV * s[0], s[1]) if kind == "rows" else (NDEV * s[1], s[0]))
           for g, (n, _, s, kind) in zip(gs, _stage_parts(st))}
    if "dn_w_in" in out:
        out["dn_w_in"] = jnp.pad(out["dn_w_in"], ((0, DN_COLS_PAD - DN_COLS), (0, 0)))
    return out


def _shards_stage(full, st):
    return [full[n].astype(BF16).reshape(NDEV, _part_rows(s), D) for n, _, s, _ in _stage_parts(st)]


def _unpack_stage(bufs, st):
    return {(n, j): b.reshape(s[1], s[0]).T if kind == "colsT" else b.reshape(s)
            for b, (n, j, s, kind) in zip(bufs, _stage_parts(st))}


def _local_step(x, target, mod, small, w0, rest):
    dist = isinstance(rest, tuple)
    packs = rest[1] if dist else None
    W = [dict(w0), {}, {}, {}] if dist else [w0] + list(rest)

    def arrived(k, g):
        for n, a in _full_stage(g, k + 1).items():
            W[k + 1 if n in _IN_PROJ else k][n] = a

    row = lambda v: v.reshape(1, -1)
    pad128 = lambda v: jnp.pad(v.reshape(1, -1), ((0, 0), (0, LANES - v.size)))
    saved = []
    mods = [[row(mod[i, k * D:(k + 1) * D]) for k in range(N_MOD)] for i in range(DEPTH)]
    h1 = _norm_mod(x, row(small["norm1_g"][0]), mods[0][1], mods[0][0], "norm1_0")
    for i in range(DEPTH):
        j = i // 2
        w = W[i]
        m = mods[i]
        sh1, sc1, gt1, sh2, sc2, gt2 = m
        g1, g2 = row(small["norm1_g"][i]), row(small["norm2_g"][i])
        if i % 2 == 0:
            proj = _mm(h1, w["dn_w_in"], "nt", F32, f"dn_proj_{i}")
            alog, dtb, og = pad128(small["dn_a_log"][j]), pad128(small["dn_dt_bias"][j]), row(small["dn_onorm_g"][j])
            comm = ("gather", packs[i]) if dist else None
            res = _dn_core_fwd(proj, small["dn_conv_w"][j], alog, dtb, og, f"dn_core_{i}", comm)
            om, o_pre, states = res[:3]
            if comm:
                arrived(i, res[3])
            y1 = _mm(om, w["dn_w_out"], "nn", F32, f"dn_out_{i}")
            mix = (proj, alog, dtb, og, o_pre, states, om)
        else:
            qkv = _mm(h1, w["sb_w_qkv"], "nt", F32, f"sb_qkv_{i}")
            qg2 = jnp.tile(row(small["sb_q_norm_g"][j]), (1, SB_HPB))
            kg2 = jnp.tile(row(small["sb_k_norm_g"][j]), (1, SB_HPB))
            comm = ("gather", packs[i]) if dist else None
            res = _sb_core_fwd(qkv, qg2, kg2, f"sb_core_{i}", comm)
            om, tot = res[:2]
            if comm:
                arrived(i, res[2])
            y1 = _mm(om, w["sb_w_out"], "nn", F32, f"sb_out_{i}")
            mix = (qkv, qg2, kg2, tot, om)
        x_mid, h2 = _gate_res_norm(x, y1, gt1, g2, sc2, sh2, f"res1_{i}")
        u = _mm(h2, w["ffn_w_in"], "nt", BF16, f"ffn_in_{i}")
        a = _swiglu_act(u, f"ffn_act_{i}")
        y2 = _mm(a, w["ffn_w_out"], "nn", F32, f"ffn_out_{i}")
        saved.append((x, h1, mix, y1, x_mid, h2, u, a, y2, m, g1, g2))
        if i + 1 < DEPTH:
            x, h1 = _gate_res_norm(x_mid, y2, gt2, row(small["norm1_g"][i + 1]), mods[i + 1][1], mods[i + 1][0],
                                   f"res2_{i}")
        else:
            x = _gate_res(x_mid, y2, gt2, f"res2_{i}")

    dx, sq = _loss_head(x, target, "loss_head")

    sg = dict(mod=[None] * DEPTH, norm1_g=[None] * DEPTH, norm2_g=[None] * DEPTH, dn_conv_w=[None] * 2,
              dn_a_log=[None] * 2, dn_dt_bias=[None] * 2, dn_onorm_g=[None] * 2, sb_q_norm_g=[None] * 2,
              sb_k_norm_g=[None] * 2)
    big = [None] * (DEPTH + 1 if dist else DEPTH)
    mixer_gw = {}
    for i in reversed(range(DEPTH)):
        j = i // 2
        w = W[i]
        x_in, h1, mix, y1, x_mid, h2, u, a, y2, m, g1, g2 = saved[i]
        sh1, sc1, gt1, sh2, sc2, gt2 = m
        gw = {}
        dy2, dgt2 = _gate_res_bwd(dx, y2, gt2, f"res2_bwd_{i}")
        da = _mm(dy2, w["ffn_w_out"], "nt", BF16, f"ffn_out_bwd_{i}")
        gw["ffn_w_out"] = _mm(a, dy2, "tn", BF16, f"ffn_out_wg_{i}")
        du = _swiglu_act_bwd(da, u, f"ffn_act_bwd_{i}")
        dh2 = _mm(du, w["ffn_w_in"], "nn", F32, f"ffn_in_bwd_{i}")
        gw["ffn_w_in"] = _mm(du, h2, "tn", BF16, f"ffn_in_wg_{i}")
        dx_mid, s2, dsh2 = _norm_mod_bwd(dh2, x_mid, dx, g2, sc2, f"norm2_bwd_{i}")
        dy1, dgt1 = _gate_res_bwd(dx_mid, y1, gt1, f"res1_bwd_{i}")
        if i % 2 == 0:
            proj, alog, dtb, og, o_pre, states, om = mix
            dom = _mm(dy1, w["dn_w_out"], "nt", F32, f"dn_out_bwd_{i}")
            gw["dn_w_out"] = _mm(om, dy1, "tn", BF16, f"dn_out_wg_{i}")
            comm = ("exchange", _shards_stage({**gw, **mixer_gw}, i + 1)) if dist else None
            dproj, dconv, dalog, ddtb, dog, got = _dn_core_bwd(proj, small["dn_conv_w"][j], alog, dtb, og, o_pre, states,
                                                               dom, f"dn_core_bwd_{i}", comm)
            dh1 = _mm(dproj, w["dn_w_in"], "nn", F32, f"dn_proj_bwd_{i}")
            gw["dn_w_in"] = _mm(dproj, h1, "tn", BF16, f"dn_proj_wg_{i}")[:DN_COLS]
            sg["dn_conv_w"][j] = dconv
            sg["dn_a_log"][j] = dalog[0, :DN_H]
            sg["dn_dt_bias"][j] = ddtb[0, :DN_H]
            sg["dn_onorm_g"][j] = dog[0]
        else:
            qkv, qg2, kg2, tot, om = mix
            dom = _mm(dy1, w["sb_w_out"], "nt", BF16, f"sb_out_bwd_{i}")
            gw["sb_w_out"] = _mm(om, dy1, "tn", BF16, f"sb_out_wg_{i}")
            comm = ("exchange", _shards_stage({**gw, **mixer_gw}, i + 1)) if dist else None
            dqkv, dqg, dkg, got = _sb_core_bwd(qkv, qg2, kg2, tot, dom, f"sb_core_bwd_{i}", comm)
            dh1 = _mm(dqkv, w["sb_w_qkv"], "nn", F32, f"sb_qkv_bwd_{i}")
            gw["sb_w_qkv"] = _mm(dqkv, h1, "tn", BF16, f"sb_qkv_wg_{i}")
            sg["sb_q_norm_g"][j] = jnp.sum(dqg.reshape(SB_HPB, SB_D), axis=0)
            sg["sb_k_norm_g"][j] = jnp.sum(dkg.reshape(SB_HPB, SB_D), axis=0)
        if comm:
            big[i + 1] = got
        dx, s1, dsh1 = _norm_mod_bwd(dh1, x_in, dx_mid, g1, sc1, f"norm1_bwd_{i}")
        sg["mod"][i] = jnp.concatenate([dsh1, s1 * g1, dgt1, dsh2, s2 * g2, dgt2], axis=1)[0]
        sg["norm1_g"][i] = (s1 * (1.0 + sc1))[0]
        sg["norm2_g"][i] = (s2 * (1.0 + sc2))[0]
        if dist:
            mixer_gw = {n: a for n, a in gw.items() if n in _IN_PROJ}
        else:
            big[i] = gw
    if dist:
        big[0] = [_all_to_all(a, f"exchange_grads_0_{p}") for p, a in enumerate(_shards_stage(mixer_gw, 0))]
    return sq, dx, {k: jnp.stack(v) for k, v in sg.items()}, big


def _device_index():
    return 4 * lax.axis_index("x") + 2 * lax.axis_index("y") + lax.axis_index("c")


def _gather_small(w, c):
    me = _device_index()
    ada_cols = w["ada_w"].shape[-1]
    conv_cols = w["dn_conv_w"].shape[-1]
    blk = jnp.concatenate([c.reshape(1, D), w["dn_conv_w"].reshape(-1, D)], axis=0)
    g1 = _all_gather(blk, "gather_cond", True)
    c16 = jnp.pad(g1[:, 0, :], ((0, 8), (0, 0)))
    conv_full = jnp.transpose(g1[:, 1:, :].reshape(NDEV, 2, DN_CONV, conv_cols), (1, 2, 0, 3)).reshape(2, DN_CONV, -1)
    b_cols = lax.dynamic_slice_in_dim(w["ada_b"], me * ada_cols, ada_cols, axis=1).reshape(DEPTH, 1, ada_cols)
    mod_part = _ada_fwd(c16, w["ada_w"], b_cols, "ada_fwd")[:, :NDEV, :]
    g2 = _all_gather(mod_part.reshape(DEPTH * NDEV, ada_cols), "gather_mod", True)
    g2 = g2.reshape(NDEV, DEPTH, NDEV, ada_cols)
    mod = lax.dynamic_index_in_dim(g2, me, axis=2, keepdims=False)
    mod = jnp.transpose(mod, (1, 0, 2)).reshape(DEPTH, N_MOD * D)
    return c16, conv_full, mod


def _reduce_small(gr, c16):
    me = _device_index()
    ada_cols = N_MOD * D // NDEV
    conv_cols = 3 * DN_H * DN_D // NDEV
    grads = {}
    small = jnp.concatenate([gr["dn_a_log"].reshape(-1), gr["dn_dt_bias"].reshape(-1), gr["dn_onorm_g"].reshape(-1),
                             gr["sb_q_norm_g"].reshape(-1), gr["sb_k_norm_g"].reshape(-1)])
    small = jnp.pad(small, (0, D - small.size)).reshape(1, D)
    blk3 = jnp.concatenate([gr["mod"].reshape(-1, D), gr["norm1_g"], gr["norm2_g"], gr["dn_conv_w"].reshape(-1, D),
                            small], axis=0)
    blk3 = jnp.pad(blk3, ((0, 64 - blk3.shape[0]), (0, 0)))
    g3 = _all_gather(blk3, "gather_small_grads", True)
    tot = _sum_sources(g3, "sum_small_grads")
    grads["ada_b"] = tot[:24].reshape(DEPTH, N_MOD * D)
    grads["norm1_g"] = tot[24:28]
    grads["norm2_g"] = tot[28:32]
    conv_g = tot[32:56].reshape(2, DN_CONV, NDEV * conv_cols)
    grads["dn_conv_w"] = lax.dynamic_slice_in_dim(conv_g, me * conv_cols, conv_cols, axis=2)
    sm = tot[56]
    grads["dn_a_log"] = sm[0:16].reshape(2, DN_H)
    grads["dn_dt_bias"] = sm[16:32].reshape(2, DN_H)
    grads["dn_onorm_g"] = sm[32:288].reshape(2, DN_D)
    grads["sb_q_norm_g"] = sm[288:416].reshape(2, SB_D)
    grads["sb_k_norm_g"] = sm[416:544].reshape(2, SB_D)
    dmod_all = g3[:, :24, :].reshape(NDEV, DEPTH, N_MOD * D)
    dmod_cols = lax.dynamic_slice_in_dim(dmod_all, me * ada_cols, ada_cols, axis=2)
    dmod16 = jnp.pad(jnp.transpose(dmod_cols, (1, 0, 2)), ((0, 0), (0, 8), (0, 0)))
    grads["ada_w"] = _ada_bwd(c16, dmod16, "ada_bwd")
    return grads


def _reduce_big(recv):
    parts = {}
    for st in range(DEPTH + 1):
        parts.update(_unpack_stage([_sum_sources(r, f"sum_grads_{st}_{p}") for p, r in enumerate(recv[st])], st))
    out = {}
    for (n, j) in sorted(parts):
        out.setdefault(n, []).append(parts[(n, j)])
    return {n: jnp.stack(v) for n, v in out.items()}


def kernel(x, c, ada_w, ada_b, norm1_g, norm2_g, dn_w_in, dn_conv_w, dn_a_log, dn_dt_bias, dn_onorm_g, dn_w_out, sb_w_qkv, sb_q_norm_g, sb_k_norm_g, sb_w_out, ffn_w_in, ffn_w_out, loss_target, m_ada_w, m_ada_b, m_norm1_g, m_norm2_g, m_dn_w_in, m_dn_conv_w, m_dn_a_log, m_dn_dt_bias, m_dn_onorm_g, m_dn_w_out, m_sb_w_qkv, m_sb_q_norm_g, m_sb_k_norm_g, m_sb_w_out, m_ffn_w_in, m_ffn_w_out, v_ada_w, v_ada_b, v_norm1_g, v_norm2_g, v_dn_w_in, v_dn_conv_w, v_dn_a_log, v_dn_dt_bias, v_dn_onorm_g, v_dn_w_out, v_sb_w_qkv, v_sb_q_norm_g, v_sb_k_norm_g, v_sb_w_out, v_ffn_w_in, v_ffn_w_out):
    w = dict(ada_w=ada_w, ada_b=ada_b, norm1_g=norm1_g, norm2_g=norm2_g, dn_w_in=dn_w_in, dn_conv_w=dn_conv_w,
             dn_a_log=dn_a_log, dn_dt_bias=dn_dt_bias, dn_onorm_g=dn_onorm_g, dn_w_out=dn_w_out, sb_w_qkv=sb_w_qkv,
             sb_q_norm_g=sb_q_norm_g, sb_k_norm_g=sb_k_norm_g, sb_w_out=sb_w_out, ffn_w_in=ffn_w_in, ffn_w_out=ffn_w_out)
    mom = dict(ada_w=m_ada_w, ada_b=m_ada_b, norm1_g=m_norm1_g, norm2_g=m_norm2_g, dn_w_in=m_dn_w_in,
               dn_conv_w=m_dn_conv_w, dn_a_log=m_dn_a_log, dn_dt_bias=m_dn_dt_bias, dn_onorm_g=m_dn_onorm_g,
               dn_w_out=m_dn_w_out, sb_w_qkv=m_sb_w_qkv, sb_q_norm_g=m_sb_q_norm_g, sb_k_norm_g=m_sb_k_norm_g,
               sb_w_out=m_sb_w_out, ffn_w_in=m_ffn_w_in, ffn_w_out=m_ffn_w_out)
    var = dict(ada_w=v_ada_w, ada_b=v_ada_b, norm1_g=v_norm1_g, norm2_g=v_norm2_g, dn_w_in=v_dn_w_in,
               dn_conv_w=v_dn_conv_w, dn_a_log=v_dn_a_log, dn_dt_bias=v_dn_dt_bias, dn_onorm_g=v_dn_onorm_g,
               dn_w_out=v_dn_w_out, sb_w_qkv=v_sb_w_qkv, sb_q_norm_g=v_sb_q_norm_g, sb_k_norm_g=v_sb_k_norm_g,
               sb_w_out=v_sb_w_out, ffn_w_in=v_ffn_w_in, ffn_w_out=v_ffn_w_out)
    names = list(w)
    c16, conv_full, mod = _gather_small(w, c)
    packs = [_pack_stage(w, st) for st in range(DEPTH + 1)]
    w0 = _full_stage([_all_gather(a, f"gather_weights_0_{p}", False) for p, a in enumerate(packs[0])], 0)
    small = dict(norm1_g=norm1_g, norm2_g=norm2_g, dn_conv_w=conv_full, dn_a_log=dn_a_log, dn_dt_bias=dn_dt_bias,
                 dn_onorm_g=dn_onorm_g, sb_q_norm_g=sb_q_norm_g, sb_k_norm_g=sb_k_norm_g)
    sq, grad_x, sgr, recv = _local_step(x[0], loss_target[0], mod, small, w0, ("packed", packs[1:]))
    loss = lax.psum(sq[0, 0] * (0.5 / D), ("x", "y", "c"))
    grads = {**_reduce_big(recv), **_reduce_small(sgr, c16)}
    delta, new_m, new_v = {}, {}, {}
    for n in names:
        delta[n], new_m[n], new_v[n] = _adamw(w[n], grads[n], mom[n], var[n], f"adamw_{n}")
    return (loss, grad_x[None], *[grads[n] for n in names], *[delta[n] for n in names],
            *[new_m[n] for n in names], *[new_v[n] for n in names])
```
